```python
import math
import functools
import jax
import jax.numpy as jnp
from jax import lax
import numpy as np

D_MODEL = 1024
BATCH = 2
SEQ = 8192
DEPTH = 2

EPS = 1e-6
D_FF = 4 * D_MODEL
N_EVEN = (DEPTH + 1) // 2
N_ODD = DEPTH // 2
CHUNK = 128
BLOCK_Q = 128

REL_BUCKETS = 32
REL_MAX_DIST = 128
N_BIAS_HEADS = 8

SSD_D_INNER = D_MODEL
SSD_HEAD_DIM = 64
SSD_HEADS = SSD_D_INNER // SSD_HEAD_DIM
SSD_GROUPS = 4
SSD_STATE = 128
SSD_CONV = 4
SSD_CONV_CH = SSD_D_INNER + 2 * SSD_GROUPS * SSD_STATE

DIFF_HEADS = N_BIAS_HEADS
DIFF_HEAD_DIM = D_MODEL // (2 * DIFF_HEADS)
DIFF_QK = DIFF_HEADS * 2 * DIFF_HEAD_DIM
DIFF_V = DIFF_HEADS * 2 * DIFF_HEAD_DIM

EVEN_SPLITS = (SSD_D_INNER, SSD_CONV_CH, SSD_HEADS, DIFF_QK, DIFF_QK, DIFF_V)
EVEN_IN = sum(EVEN_SPLITS)
EVEN_MIX = SSD_D_INNER + DIFF_V

RET_HEADS = 4
RET_QK_DIM = 128
RET_V_DIM = 256
ROPE_BASE = 10000.0

SWA_HEADS = N_BIAS_HEADS
SWA_KV_HEADS = 2
SWA_HEAD_DIM = 64
WINDOW = 128

ODD_SPLITS = (RET_HEADS * RET_QK_DIM, RET_HEADS * RET_QK_DIM, RET_HEADS * RET_V_DIM,
              RET_HEADS * RET_V_DIM, SWA_HEADS * SWA_HEAD_DIM,
              SWA_KV_HEADS * SWA_HEAD_DIM, SWA_KV_HEADS * SWA_HEAD_DIM)
ODD_IN = sum(ODD_SPLITS)
ODD_MIX = RET_HEADS * RET_V_DIM + SWA_HEADS * SWA_HEAD_DIM

kernel_name = "hybrid_ssd_diffattn_retention_swa_trunk"


def rms_norm(x, w=None):
    x32 = x.astype(jnp.float32)
    y = x32 * lax.rsqrt(jnp.mean(x32 * x32, axis=-1, keepdims=True) + EPS)
    if w is not None:
        y = y * w.astype(jnp.float32)
    return y.astype(x.dtype)


def split_cols(a, sizes):
    return jnp.split(a, np.cumsum(sizes)[:-1].tolist(), axis=-1)


def t5_bucket(dist):
    dist = jnp.maximum(dist, 0)
    max_exact = REL_BUCKETS // 2
    logd = jnp.log(jnp.maximum(dist, 1).astype(jnp.float32) / max_exact)
    large = max_exact + (logd / math.log(REL_MAX_DIST / max_exact)
                         * (REL_BUCKETS - max_exact)).astype(jnp.int32)
    large = jnp.minimum(large, REL_BUCKETS - 1)
    return jnp.where(dist < max_exact, dist, large)


def scan_chunk_states(chunk_states, chunk_decay):
    def step(carry, inp):
        st, dec = inp
        return carry * dec + st, carry
    init = jnp.zeros_like(chunk_states[:, 0])
    _, prev = lax.scan(step, init, (jnp.moveaxis(chunk_states, 1, 0), jnp.moveaxis(chunk_decay, 1, 0)))
    return jnp.moveaxis(prev, 0, 1)


def causal_depthwise_conv(x, w, b):
    ch = x.shape[-1]
    out = lax.conv_general_dilated(x, w[:, None, :].astype(x.dtype), window_strides=(1,),
                                   padding=[(SSD_CONV - 1, 0)],
                                   dimension_numbers=("NWC", "WIO", "NWC"),
                                   feature_group_count=ch)
    return out + b.astype(x.dtype)


def ssd_chunked(xh, dt, A, Bg, Cg):
    b, s, h, p = xh.shape
    g, n = Bg.shape[-2:]
    r = h // g
    nc = s // CHUNK
    X = (xh * dt[..., None]).reshape(b, nc, CHUNK, g, r, p)
    Bc = Bg.reshape(b, nc, CHUNK, g, n)
    Cc = Cg.reshape(b, nc, CHUNK, g, n)
    a_cs = jnp.cumsum((dt * A).reshape(b, nc, CHUNK, g, r), axis=2)
    causal = jnp.tril(jnp.ones((CHUNK, CHUNK), dtype=bool))
    seg = a_cs[:, :, :, None] - a_cs[:, :, None, :]
    decay_in = jnp.exp(jnp.where(causal[:, :, None, None], seg, -jnp.inf))
    cb = jnp.einsum("bclgn,bcsgn->bclsg", Cc, Bc)
    y_diag = jnp.einsum("bclsg,bclsgr,bcsgrp->bclgrp", cb, decay_in, X)
    decay_to_end = jnp.exp(a_cs[:, :, -1:] - a_cs)
    chunk_states = jnp.einsum("bcsgn,bcsgr,bcsgrp->bcgrpn", Bc, decay_to_end, X)
    state_prev = scan_chunk_states(chunk_states, jnp.exp(a_cs[:, :, -1])[..., None, None])
    y_off = jnp.einsum("bclgn,bcgrpn,bclgr->bclgrp", Cc, state_prev, jnp.exp(a_cs))
    return (y_diag + y_off).reshape(b, s, h, p)


def ssd_mixer(z, xbc, dt_raw, conv_w, conv_b, dt_bias, A_log, D_skip, norm_w):
    b, s, _ = z.shape
    xbc = jax.nn.silu(causal_depthwise_conv(xbc, conv_w, conv_b))
    xs, Bs, Cs = split_cols(xbc, (SSD_D_INNER, SSD_GROUPS * SSD_STATE, SSD_GROUPS * SSD_STATE))
    xh = xs.astype(jnp.float32).reshape(b, s, SSD_HEADS, SSD_HEAD_DIM)
    dt = jax.nn.softplus(dt_raw.astype(jnp.float32) + dt_bias.astype(jnp.float32))
    A = -jnp.exp(A_log.astype(jnp.float32))
    y = ssd_chunked(xh, dt, A,
                    Bs.astype(jnp.float32).reshape(b, s, SSD_GROUPS, SSD_STATE),
                    Cs.astype(jnp.float32).reshape(b, s, SSD_GROUPS, SSD_STATE))
    y = y + xh * D_skip.astype(jnp.float32)[:, None]
    y = y.reshape(b, s, SSD_D_INNER) * jax.nn.silu(z.astype(jnp.float32))
    return rms_norm(y, norm_w).astype(z.dtype)


def diff_attention(q, k, v, lam, subln_w, rel_bias, layer_idx):
    b, s = q.shape[:2]
    lam_init = 0.8 - 0.6 * math.exp(-0.3 * layer_idx)
    lam32 = lam.astype(jnp.float32)
    lam_full = (jnp.exp(jnp.sum(lam32[0] * lam32[1])) - jnp.exp(jnp.sum(lam32[2] * lam32[3])) + lam_init)
    q32 = q.astype(jnp.float32) * DIFF_HEAD_DIM ** -0.5
    k32 = k.astype(jnp.float32)
    v32 = v.astype(jnp.float32)
    nb = s // BLOCK_Q
    q_blocks = jnp.moveaxis(q32.reshape(b, nb, BLOCK_Q, DIFF_HEADS, 2, DIFF_HEAD_DIM), 1, 0)
    key_pos = jnp.arange(s)

    def one_block(args):
        q_blk, blk = args
        dist = (blk * BLOCK_Q + jnp.arange(BLOCK_Q))[:, None] - key_pos[None, :]
        bias = jnp.transpose(rel_bias[t5_bucket(dist)], (2, 0, 1)).astype(jnp.float32)
        logits = jnp.einsum("bqhmd,bkhmd->bhmqk", q_blk, k32) + bias[None, :, None]
        logits = jnp.where(dist >= 0, logits, -jnp.inf)
        p = jax.nn.softmax(logits, axis=-1)
        attn = p[:, :, 0] - lam_full * p[:, :, 1]
        return jnp.einsum("bhqk,bkhe->bqhe", attn, v32)

    out = lax.map(one_block, (q_blocks, jnp.arange(nb)))
    out = jnp.moveaxis(out, 0, 1).reshape(b, s, DIFF_HEADS, 2 * DIFF_HEAD_DIM)
    out = rms_norm(out, subln_w) * (1.0 - lam_init)
    return out.reshape(b, s, DIFF_V).astype(q.dtype)


def rope(t, pos):
    half = t.shape[-1] // 2
    inv = ROPE_BASE ** (-jnp.arange(half, dtype=jnp.float32) / half)
    ang = pos[:, None] * inv[None]
    cos = jnp.cos(ang)[None, :, None]
    sin = jnp.sin(ang)[None, :, None]
    t1, t2 = t[..., :half], t[..., half:]
    return jnp.concatenate([t1 * cos - t2 * sin, t1 * sin + t2 * cos], axis=-1)


def retention(q, k, v):
    b, s = q.shape[:2]
    nc = s // CHUNK
    pos = jnp.arange(s, dtype=jnp.float32)
    Q = rope(q.astype(jnp.float32), pos).reshape(b, nc, CHUNK, RET_HEADS, RET_QK_DIM)
    K = (rope(k.astype(jnp.float32), pos) * RET_QK_DIM ** -0.5).reshape(b, nc, CHUNK, RET_HEADS, RET_QK_DIM)
    V = v.astype(jnp.float32).reshape(b, nc, CHUNK, RET_HEADS, RET_V_DIM)
    log_gamma = jnp.log(1.0 - 2.0 ** (-5.0 - jnp.arange(RET_HEADS, dtype=jnp.float32)))
    idx = jnp.arange(CHUNK, dtype=jnp.float32)
    rel = idx[:, None] - idx[None, :]
    decay_mat = jnp.where(rel >= 0, jnp.exp(jnp.maximum(rel, 0.0)[None] * log_gamma[:, None, None]), 0.0)
    scores = jnp.einsum("bclhd,bcshd->bchls", Q, K) * decay_mat
    inner = jnp.einsum("bchls,bcshe->bclhe", scores, V)
    zeta = jnp.exp((CHUNK - 1 - idx)[None] * log_gamma[:, None])
    kv = jnp.einsum("bcshd,hs,bcshe->bchde", K, zeta, V)
    chunk_decay = jnp.broadcast_to(jnp.exp(CHUNK * log_gamma)[:, None, None], (b, nc, RET_HEADS, 1, 1))
    state_prev = scan_chunk_states(kv, chunk_decay)
    xi = jnp.exp((idx + 1.0)[None] * log_gamma[:, None])
    cross = jnp.einsum("bclhd,bchde,hl->bclhe", Q, state_prev, xi)
    out = (inner + cross).reshape(b, s, RET_HEADS, RET_V_DIM)
    return rms_norm(out).reshape(b, s, RET_HEADS * RET_V_DIM)


def sliding_window_attention(q, k, v, sinks, rel_bias):
    b, s = q.shape[:2]
    rep = SWA_HEADS // SWA_KV_HEADS
    nb = s // BLOCK_Q
    q32 = q.astype(jnp.float32).reshape(b, nb, BLOCK_Q, SWA_KV_HEADS, rep, SWA_HEAD_DIM) * SWA_HEAD_DIM ** -0.5

    def band(t):
        t = jnp.pad(t.astype(jnp.float32), ((0, 0), (BLOCK_Q, 0), (0, 0), (0, 0)))
        t = t.reshape(b, nb + 1, BLOCK_Q, SWA_KV_HEADS, SWA_HEAD_DIM)
        return jnp.concatenate([t[:, :-1], t[:, 1:]], axis=2)

    kb, vb = band(k), band(v)
    q_off = jnp.arange(BLOCK_Q)
    k_off = jnp.arange(2 * BLOCK_Q) - BLOCK_Q
    dist = q_off[:, None] - k_off[None, :]
    k_pos = (jnp.arange(nb) * BLOCK_Q)[:, None] + k_off[None, :]
    valid = ((dist >= 0) & (dist < WINDOW))[None] & (k_pos >= 0)[:, None, :]
    bias = jnp.transpose(rel_bias[t5_bucket(dist)], (2, 0, 1)).astype(jnp.float32)
    bias = bias.reshape(SWA_KV_HEADS, rep, BLOCK_Q, 2 * BLOCK_Q)
    logits = jnp.einsum("bnqgrd,bnkgd->bngrqk", q32, kb) + bias
    logits = jnp.where(valid[None, :, None, None], logits, -jnp.inf)
    sink = jnp.broadcast_to(sinks.astype(jnp.float32).reshape(1, 1, SWA_KV_HEADS, rep, 1, 1),
                            logits.shape[:-1] + (1,))
    p = jax.nn.softmax(jnp.concatenate([logits, sink], axis=-1), axis=-1)[..., :-1]
    out = jnp.einsum("bngrqk,bnkgd->bnqgrd", p, vb)
    return out.reshape(b, s, SWA_HEADS * SWA_HEAD_DIM).astype(q.dtype)


def even_mixer(h, w_in, conv_w, conv_b, dt_bias, A_log, D_skip, ssd_norm, lam, diff_norm, w_out,
               rel_bias, layer_idx):
    b, s, _ = h.shape
    z, xbc, dt_raw, q, k, v = split_cols(h @ w_in, EVEN_SPLITS)
    y_ssd = ssd_mixer(z, xbc, dt_raw, conv_w, conv_b, dt_bias, A_log, D_skip, ssd_norm)
    y_diff = diff_attention(q.reshape(b, s, DIFF_HEADS, 2, DIFF_HEAD_DIM),
                            k.reshape(b, s, DIFF_HEADS, 2, DIFF_HEAD_DIM),
                            v.reshape(b, s, DIFF_HEADS, 2 * DIFF_HEAD_DIM),
                            lam, diff_norm, rel_bias, layer_idx)
    return jnp.concatenate([y_ssd, y_diff], axis=-1) @ w_out


def odd_mixer(h, w_in, sinks, w_out, rel_bias):
    b, s, _ = h.shape
    rq, rk, rv, rg, sq, sk, sv = split_cols(h @ w_in, ODD_SPLITS)
    y_ret = retention(rq.reshape(b, s, RET_HEADS, RET_QK_DIM),
                      rk.reshape(b, s, RET_HEADS, RET_QK_DIM),
                      rv.reshape(b, s, RET_HEADS, RET_V_DIM))
    y_ret = (y_ret * jax.nn.silu(rg.astype(jnp.float32))).astype(h.dtype)
    y_swa = sliding_window_attention(sq.reshape(b, s, SWA_HEADS, SWA_HEAD_DIM),
                                     sk.reshape(b, s, SWA_KV_HEADS, SWA_HEAD_DIM),
                                     sv.reshape(b, s, SWA_KV_HEADS, SWA_HEAD_DIM),
                                     sinks, rel_bias)
    return jnp.concatenate([y_ret, y_swa], axis=-1) @ w_out


def squared_relu_mlp(h, w1, w2):
    return jnp.square(jax.nn.relu(h @ w1)) @ w2


def sandwich(x, c_act, sublayer, g_pre, g_post, w_mod, b_mod):
    shift, scale, gate = jnp.split(c_act @ w_mod + b_mod, 3, axis=-1)
    h = rms_norm(x, g_pre) * (1 + scale[:, None]) + shift[:, None]
    return x + gate[:, None] * rms_norm(sublayer(h), g_post)


def setup_inputs(seed: int = 0) -> dict:
    key = jax.random.key(seed)
    ks = jax.random.split(key, 24)

    def nrm(k, shape, scale):
        return jax.random.normal(k, shape, jnp.float32) * scale

    dt0 = jnp.exp(jax.random.uniform(ks[11], (N_EVEN, SSD_HEADS), jnp.float32,
                                     minval=math.log(1e-3), maxval=math.log(1e-1)))
    return {
        "x": nrm(ks[0], (BATCH, SEQ, D_MODEL), 1.0),
        "c": nrm(ks[1], (BATCH, D_MODEL), 1.0),
        "rel_bias": nrm(ks[2], (REL_BUCKETS, N_BIAS_HEADS), 0.5),
        "norm_gains": 1.0 + nrm(ks[3], (DEPTH, 4, D_MODEL), 0.05),
        "mod_w": nrm(ks[4], (DEPTH, 2, D_MODEL, 3 * D_MODEL), D_MODEL ** -0.5),
        "mod_b": nrm(ks[5], (DEPTH, 2, 3 * D_MODEL), 0.02),
        "mlp_w1": nrm(ks[6], (DEPTH, D_MODEL, D_FF), D_MODEL ** -0.5),
        "mlp_w2": nrm(ks[7], (DEPTH, D_FF, D_MODEL), D_FF ** -0.5),
        "e_w_in": nrm(ks[8], (N_EVEN, D_MODEL, EVEN_IN), D_MODEL ** -0.5),
        "e_conv_w": nrm(ks[9], (N_EVEN, SSD_CONV, SSD_CONV_CH), SSD_CONV ** -0.5),
        "e_conv_b": nrm(ks[10], (N_EVEN, SSD_CONV_CH), 0.02),
        "e_dt_bias": dt0 + jnp.log(-jnp.expm1(-dt0)),
        "e_A_log": jnp.log(jax.random.uniform(ks[12], (N_EVEN, SSD_HEADS), jnp.float32, minval=1.0, maxval=16.0)),
        "e_D": 1.0 + nrm(ks[13], (N_EVEN, SSD_HEADS), 0.05),
        "e_ssd_norm": 1.0 + nrm(ks[14], (N_EVEN, SSD_D_INNER), 0.05),
        "e_lambda": nrm(ks[15], (N_EVEN, 4, DIFF_HEAD_DIM), 0.1),
        "e_diff_norm": 1.0 + nrm(ks[16], (N_EVEN, 2 * DIFF_HEAD_DIM), 0.05),
        "e_w_out": nrm(ks[17], (N_EVEN, EVEN_MIX, D_MODEL), EVEN_MIX ** -0.5),
        "o_w_in": nrm(ks[18], (N_ODD, D_MODEL, ODD_IN), D_MODEL ** -0.5),
        "o_sinks": nrm(ks[19], (N_ODD, SWA_HEADS), 0.5),
        "o_w_out": nrm(ks[20], (N_ODD, ODD_MIX, D_MODEL), ODD_MIX ** -0.5),
    }


def reference(x, c, rel_bias, norm_gains, mod_w, mod_b, mlp_w1, mlp_w2, e_w_in, e_conv_w, e_conv_b,
              e_dt_bias, e_A_log, e_D, e_ssd_norm, e_lambda, e_diff_norm, e_w_out, o_w_in, o_sinks,
              o_w_out):
    c_act = jax.nn.silu(c)
    for layer in range(DEPTH):
        j = layer // 2
        if layer % 2 == 0:
            mixer = functools.partial(even_mixer, w_in=e_w_in[j], conv_w=e_conv_w[j], conv_b=e_conv_b[j],
                                      dt_bias=e_dt_bias[j], A_log=e_A_log[j], D_skip=e_D[j],
                                      ssd_norm=e_ssd_norm[j], lam=e_lambda[j], diff_norm=e_diff_norm[j],
                                      w_out=e_w_out[j], rel_bias=rel_bias, layer_idx=layer)
        else:
            mixer = functools.partial(odd_mixer, w_in=o_w_in[j], sinks=o_sinks[j], w_out=o_w_out[j],
                                      rel_bias=rel_bias)
        x = sandwich(x, c_act, mixer, norm_gains[layer, 0], norm_gains[layer, 1],
                     mod_w[layer, 0], mod_b[layer, 0])
        mlp = functools.partial(squared_relu_mlp, w1=mlp_w1[layer], w2=mlp_w2[layer])
        x = sandwich(x, c_act, mlp, norm_gains[layer, 2], norm_gains[layer, 3],
                     mod_w[layer, 1], mod_b[layer, 1])
    return x
```

```python
import functools
import math

import jax
import jax.numpy as jnp
import numpy as np
from jax import lax
from jax.experimental import pallas as pl
from jax.experimental.pallas import tpu as pltpu

EPS = 1e-6
MASK_VALUE = -1e30
LANES = 128
SUBLANES = 8
VMEM_LIMIT = 56 * 1024 * 1024

CHUNK = 128
REL_BUCKETS = 32
REL_MAX_DIST = 128
SSD_HEAD_DIM = 64
SSD_GROUPS = 4
SSD_STATE = 128
SSD_CONV = 4
DIFF_HEAD_DIM = 64
RET_HEADS = 4
RET_QK_DIM = 128
RET_V_DIM = 256
ROPE_BASE = 10000.0
SWA_HEADS = 8
SWA_KV_HEADS = 2
SWA_HEAD_DIM = 64
SWA_BLOCK = 128
ATTN_BLOCK = 256

BF16 = jnp.bfloat16
F32 = jnp.float32
NT_DIMS = (((1,), (1,)), ((), ()))


def _params(semantics):
    return pltpu.CompilerParams(dimension_semantics=semantics, vmem_limit_bytes=VMEM_LIMIT)


def _full_spec(shape):
    return pl.BlockSpec(shape, lambda *_: (0,) * len(shape))


def _silu(x):
    return x * jax.nn.sigmoid(x)


def _rms(x, eps=EPS):
    return x * lax.rsqrt(jnp.mean(x * x, axis=-1, keepdims=True) + eps)


def _mod_kernel(c_ref, w_ref, b_ref, o_ref):
    c_act = _silu(c_ref[...])
    o_ref[...] = jnp.dot(c_act, w_ref[...], preferred_element_type=F32,
                         precision=lax.Precision.HIGHEST) + b_ref[...]


def modulation(c, mod_w, mod_b):
    b, d = c.shape
    n = mod_w.shape[0]
    rows = SUBLANES
    c_pad = jnp.zeros((rows, d), F32).at[:b].set(c)
    tn = d
    out = pl.pallas_call(
        _mod_kernel,
        grid=(n, 3 * d // tn),
        in_specs=[pl.BlockSpec((rows, d), lambda s, j: (0, 0)),
                  pl.BlockSpec((None, d, tn), lambda s, j: (s, 0, j)),
                  pl.BlockSpec((None, 1, tn), lambda s, j: (s, 0, j))],
        out_specs=pl.BlockSpec((None, rows, tn), lambda s, j: (s, 0, j)),
        out_shape=jax.ShapeDtypeStruct((n, rows, 3 * d), F32),
        compiler_params=_params(("parallel", "parallel")),
        name="modulation",
    )(c_pad, mod_w, mod_b.reshape(n, 1, 3 * d))
    return out[:, :b].reshape(n, b, 3, d)


def _t5_bucket_np(dist):
    max_exact = REL_BUCKETS // 2
    logd = np.log(np.maximum(dist, 1).astype(np.float32) / np.float32(max_exact))
    large = max_exact + (logd / np.float32(math.log(REL_MAX_DIST / max_exact))
                         * np.float32(REL_BUCKETS - max_exact)).astype(np.int32)
    large = np.minimum(large, REL_BUCKETS - 1)
    return np.where(dist < max_exact, dist, large).astype(np.int32)


def _bias_kernel(rb_ref, idx_ref, o_ref, *, shift_bucket):
    h = pl.program_id(0)
    idx = idx_ref[...]
    acc = jnp.full(idx.shape, MASK_VALUE, F32)
    shift = rb_ref[shift_bucket, h] if shift_bucket is not None else 0.0
    for bucket in range(REL_BUCKETS):
        acc = jnp.where(idx == bucket, rb_ref[bucket, h] - shift, acc)
    o_ref[...] = acc


def bias_tiles(rel_bias, bucket_idx, shift_bucket=None):
    n_heads = rel_bias.shape[1]
    r, c = bucket_idx.shape
    return pl.pallas_call(
        functools.partial(_bias_kernel, shift_bucket=shift_bucket),
        grid=(n_heads,),
        in_specs=[pl.BlockSpec(memory_space=pltpu.SMEM),
                  pl.BlockSpec((r, c), lambda h: (0, 0))],
        out_specs=pl.BlockSpec((None, r, c), lambda h: (h, 0, 0)),
        out_shape=jax.ShapeDtypeStruct((n_heads, r, c), F32),
        compiler_params=_params(("parallel",)),
        name="bias_tiles",
    )(rel_bias, jnp.asarray(bucket_idx))


def _diff_bucket_idx(t):
    q = np.arange(t)[:, None]
    k = np.arange(t)[None, :]
    diag = q - k
    idx_diag = np.where(diag >= 0, _t5_bucket_np(np.maximum(diag, 0)), -1)
    idx_prev = _t5_bucket_np(diag + t)
    return np.concatenate([idx_diag, idx_prev], axis=0).astype(np.int32)


def _swa_bucket_idx():
    q = np.arange(SWA_BLOCK)[:, None]
    k = np.arange(2 * SWA_BLOCK)[None, :] - SWA_BLOCK
    dist = q - k
    valid = (dist >= 0) & (dist < SWA_BLOCK)
    return np.where(valid, _t5_bucket_np(np.maximum(dist, 0)), -1).astype(np.int32)


def _norm_proj_kernel(x_ref, mod_ref, g_ref, *refs):
    n = len(refs) // 2
    w_refs, o_refs = refs[:n], refs[n:]
    mod = mod_ref[...]
    h = _rms(x_ref[...]) * g_ref[...] * (1.0 + mod[1:2]) + mod[0:1]
    hb = h.astype(BF16)
    for w_ref, o_ref in zip(w_refs, o_refs):
        o_ref[...] = jnp.dot(hb, w_ref[...], preferred_element_type=F32).astype(o_ref.dtype)


def norm_proj(x2, mod, gain, weights, out_dtypes, seq, tm=512):
    m, d = x2.shape
    per_batch = seq // tm
    in_specs = [pl.BlockSpec((tm, d), lambda i: (i, 0)),
                pl.BlockSpec((None, 3, d), lambda i: (i // per_batch, 0, 0)),
                _full_spec((1, d))]
    in_specs += [_full_spec(w.shape) for w in weights]
    out_specs = [pl.BlockSpec((tm, w.shape[1]), lambda i: (i, 0)) for w in weights]
    out_shape = [jax.ShapeDtypeStruct((m, w.shape[1]), dt) for w, dt in zip(weights, out_dtypes)]
    return pl.pallas_call(
        _norm_proj_kernel,
        grid=(m // tm,),
        in_specs=in_specs,
        out_specs=out_specs,
        out_shape=out_shape,
        compiler_params=_params(("parallel",)),
        name="norm_proj",
    )(x2, mod, gain.reshape(1, d), *weights)


def _out_proj_kernel(x_ref, mod_ref, g_ref, *refs):
    n = (len(refs) - 1) // 2
    y_refs, w_refs, o_ref = refs[:n], refs[n:2 * n], refs[2 * n]
    acc = None
    for y_ref, w_ref in zip(y_refs, w_refs):
        part = jnp.dot(y_ref[...], w_ref[...], preferred_element_type=F32)
        acc = part if acc is None else acc + part
    gate = mod_ref[...][2:3]
    o_ref[...] = x_ref[...] + gate * (_rms(acc) * g_ref[...])


def out_proj(x2, mod, gain, ys, weights, seq, tm=512):
    m, d = x2.shape
    per_batch = seq // tm
    in_specs = [pl.BlockSpec((tm, d), lambda i: (i, 0)),
                pl.BlockSpec((None, 3, d), lambda i: (i // per_batch, 0, 0)),
                _full_spec((1, d))]
    in_specs += [pl.BlockSpec((tm, y.shape[1]), lambda i: (i, 0)) for y in ys]
    in_specs += [_full_spec(w.shape) for w in weights]
    return pl.pallas_call(
        _out_proj_kernel,
        grid=(m // tm,),
        in_specs=in_specs,
        out_specs=pl.BlockSpec((tm, d), lambda i: (i, 0)),
        out_shape=jax.ShapeDtypeStruct((m, d), F32),
        compiler_params=_params(("parallel",)),
        name="out_proj",
    )(x2, mod, gain.reshape(1, d), *ys, *weights)


def _mlp_kernel(x_ref, mod_ref, g_pre_ref, g_post_ref, w1_ref, w2_ref, o_ref, *, ff_chunk):
    x = x_ref[...]
    mod = mod_ref[...]
    h = _rms(x) * g_pre_ref[...] * (1.0 + mod[1:2]) + mod[0:1]
    hb = h.astype(BF16)
    d_ff = w1_ref.shape[1]
    acc = None
    for c0 in range(0, d_ff, ff_chunk):
        a = jnp.dot(hb, w1_ref[:, c0:c0 + ff_chunk], preferred_element_type=F32)
        a = jnp.square(jnp.maximum(a, 0.0)).astype(BF16)
        part = jnp.dot(a, w2_ref[c0:c0 + ff_chunk, :], preferred_element_type=F32)
        acc = part if acc is None else acc + part
    o_ref[...] = x + mod[2:3] * (_rms(acc) * g_post_ref[...])


def mlp(x2, mod, g_pre, g_post, w1, w2, seq, tm=512, ff_chunk=1024):
    m, d = x2.shape
    per_batch = seq // tm
    return pl.pallas_call(
        functools.partial(_mlp_kernel, ff_chunk=ff_chunk),
        grid=(m // tm,),
        in_specs=[pl.BlockSpec((tm, d), lambda i: (i, 0)),
                  pl.BlockSpec((None, 3, d), lambda i: (i // per_batch, 0, 0)),
                  _full_spec((1, d)), _full_spec((1, d)),
                  _full_spec(w1.shape), _full_spec(w2.shape)],
        out_specs=pl.BlockSpec((tm, d), lambda i: (i, 0)),
        out_shape=jax.ShapeDtypeStruct((m, d), F32),
        compiler_params=_params(("parallel",)),
        name="mlp",
    )(x2, mod, g_pre.reshape(1, d), g_post.reshape(1, d), w1, w2)


def _ssd_kernel(z_ref, xbc_ref, dt_ref, cw_ref, cb_ref, dtb_ref, alog_ref, dskip_ref, nw_ref, expand_ref,
                o_ref, conv_buf, state_ref, *, d_inner):
    chunk = z_ref.shape[0]
    n_state = SSD_STATE
    gn = SSD_GROUPS * n_state
    pair = 2 * SSD_HEAD_DIM
    heads_per_group = d_inner // SSD_HEAD_DIM // SSD_GROUPS
    halo = SUBLANES

    @pl.when(pl.program_id(1) == 0)
    def _():
        state_ref[...] = jnp.zeros_like(state_ref)
        conv_buf[0:halo, :] = jnp.zeros((halo, conv_buf.shape[1]), F32)

    conv_buf[halo:halo + chunk, :] = xbc_ref[...].astype(F32)
    acc = cb_ref[...]
    for tap in range(SSD_CONV):
        off = halo - (SSD_CONV - 1) + tap
        acc = acc + cw_ref[tap:tap + 1, :] * conv_buf[off:off + chunk, :]
    conv_buf[0:halo, :] = conv_buf[chunk:chunk + halo, :]
    xbc = _silu(acc)
    xs = xbc[:, :d_inner]
    b_all = xbc[:, d_inner:d_inner + gn]
    c_all = xbc[:, d_inner + gn:]

    dt_in = dt_ref[...] + dtb_ref[...]
    dt = jnp.maximum(dt_in, 0.0) + jnp.log1p(jnp.exp(-jnp.abs(dt_in)))
    a = dt * (-jnp.exp(alog_ref[...]))
    row = lax.broadcasted_iota(jnp.int32, (chunk, chunk), 0)
    col = lax.broadcasted_iota(jnp.int32, (chunk, chunk), 1)
    causal = row >= col
    tril = causal.astype(F32)
    hi = lax.Precision.HIGHEST
    a_cs = jnp.dot(tril, a, preferred_element_type=F32, precision=hi)
    a_cs_t = a_cs.T
    expand = expand_ref[...]
    dt_e = jnp.dot(dt, expand, preferred_element_type=F32, precision=hi)
    acs_e = jnp.dot(a_cs, expand, preferred_element_type=F32, precision=hi)
    a_last_e = acs_e[chunk - 1:chunk, :]
    x_dt = xs * dt_e
    w_state = (x_dt * jnp.exp(a_last_e - acs_e)).astype(BF16)
    out_scale = jnp.exp(acs_e)
    chunk_decay = jnp.exp(a_last_e)
    x_dt_b = x_dt.astype(BF16)
    lane = lax.broadcasted_iota(jnp.int32, (chunk, pair), 1)
    first_head = lane < SSD_HEAD_DIM

    y_blocks = []
    for g in range(SSD_GROUPS):
        bg = b_all[:, g * n_state:(g + 1) * n_state]
        cg = c_all[:, g * n_state:(g + 1) * n_state].astype(BF16)
        cb = lax.dot_general(cg, bg.astype(BF16), NT_DIMS, preferred_element_type=F32)
        bg_t = bg.T.astype(BF16)
        for pi in range(heads_per_group // 2):
            p = g * (heads_per_group // 2) + pi
            sl = slice(p * pair, (p + 1) * pair)
            xp = x_dt_b[:, sl]
            y_diag = None
            for k in range(2):
                hh = 2 * p + k
                seg = a_cs[:, hh:hh + 1] - a_cs_t[hh:hh + 1, :]
                decay = jnp.exp(jnp.where(causal, seg, MASK_VALUE))
                m = (cb * decay).astype(BF16)
                xh = jnp.where(first_head if k == 0 else jnp.logical_not(first_head), xp, jnp.zeros_like(xp))
                part = jnp.dot(m, xh, preferred_element_type=F32)
                y_diag = part if y_diag is None else y_diag + part
            st = state_ref[p]
            y_off = jnp.dot(cg, st.astype(BF16), preferred_element_type=F32) * out_scale[:, sl]
            state_ref[p] = st * chunk_decay[:, sl] + jnp.dot(bg_t, w_state[:, sl], preferred_element_type=F32)
            y_blocks.append(y_diag + y_off)
    y = jnp.concatenate(y_blocks, axis=-1) + xs * dskip_ref[...]
    y = y * _silu(z_ref[...].astype(F32))
    o_ref[...] = (_rms(y) * nw_ref[...]).astype(o_ref.dtype)


def ssd_mixer(z, xbc, dt_raw, conv_w, conv_b, dt_bias, a_log, d_skip, norm_w, batch, seq):
    m, d_inner = z.shape
    conv_ch = xbc.shape[1]
    n_heads = d_inner // SSD_HEAD_DIM
    nc = seq // CHUNK
    pad = LANES - n_heads
    expand = np.zeros((LANES, d_inner), np.float32)
    for h in range(n_heads):
        expand[h, h * SSD_HEAD_DIM:(h + 1) * SSD_HEAD_DIM] = 1.0
    row_spec = lambda width: pl.BlockSpec((CHUNK, width), lambda b, c: (b * nc + c, 0))
    return pl.pallas_call(
        functools.partial(_ssd_kernel, d_inner=d_inner),
        grid=(batch, nc),
        in_specs=[row_spec(d_inner), row_spec(conv_ch), row_spec(LANES),
                  _full_spec((SSD_CONV, conv_ch)), _full_spec((1, conv_ch)),
                  _full_spec((1, LANES)), _full_spec((1, LANES)),
                  _full_spec((1, d_inner)), _full_spec((1, d_inner)), _full_spec((LANES, d_inner))],
        out_specs=row_spec(d_inner),
        out_shape=jax.ShapeDtypeStruct((m, d_inner), BF16),
        scratch_shapes=[pltpu.VMEM((CHUNK + SUBLANES, conv_ch), F32),
                        pltpu.VMEM((n_heads // 2, SSD_STATE, 2 * SSD_HEAD_DIM), F32)],
        compiler_params=_params(("parallel", "arbitrary")),
        name="ssd_mixer",
    )(z, xbc, dt_raw, conv_w, conv_b.reshape(1, conv_ch),
      jnp.pad(dt_bias, (0, pad)).reshape(1, LANES), jnp.pad(a_log, (0, pad)).reshape(1, LANES),
      jnp.repeat(d_skip, SSD_HEAD_DIM).reshape(1, d_inner), norm_w.reshape(1, d_inner), jnp.asarray(expand))


def _diff_attn_kernel(q_ref, k_ref, v_ref, bias_ref, lam_ref, nw_ref, o_ref, *, lam_init):
    t = q_ref.shape[0]
    i = pl.program_id(2)
    q = q_ref[...]
    lane = lax.broadcasted_iota(jnp.int32, q.shape, 1)
    zero = jnp.zeros_like(q)
    qq = jnp.concatenate([jnp.where(lane < DIFF_HEAD_DIM, q, zero),
                          jnp.where(lane >= DIFF_HEAD_DIM, q, zero)], axis=0)

    def step(start, bias, carry):
        m_prev, l_prev, acc = carry
        kb = k_ref[pl.ds(start, t), :]
        vb = v_ref[pl.ds(start, t), :]
        s = lax.dot_general(qq, kb, NT_DIMS, preferred_element_type=F32)
        if bias is not None:
            s = s + jnp.concatenate([bias, bias], axis=0)
        m_new = jnp.maximum(m_prev, jnp.max(s, axis=-1, keepdims=True))
        alpha = jnp.exp(m_prev - m_new)
        p = jnp.exp(s - m_new)
        l_new = alpha * l_prev + jnp.sum(p, axis=-1, keepdims=True)
        acc = alpha * acc + jnp.dot(p.astype(BF16), vb, preferred_element_type=F32)
        return m_new, l_new, acc

    carry = (jnp.full((2 * t, 1), MASK_VALUE, F32), jnp.zeros((2 * t, 1), F32),
             jnp.zeros((2 * t, v_ref.shape[1]), F32))
    carry = lax.fori_loop(0, jnp.maximum(i - 1, 0),
                          lambda j, c: step(pl.multiple_of(j * t, t), None, c), carry)
    carry = lax.cond(i > 0,
                     lambda c: step(pl.multiple_of((i - 1) * t, t), bias_ref[1], c),
                     lambda c: c, carry)
    m_f, l_f, acc = step(pl.multiple_of(i * t, t), bias_ref[0], carry)

    lam = lam_ref[...]
    lam_full = (jnp.exp(jnp.sum(lam[0:1] * lam[1:2], axis=-1, keepdims=True))
                - jnp.exp(jnp.sum(lam[2:3] * lam[3:4], axis=-1, keepdims=True)) + lam_init)
    out = acc / l_f
    out = out[:t] - lam_full * out[t:]
    o_ref[...] = (_rms(out) * nw_ref[...] * (1.0 - lam_init)).astype(o_ref.dtype)


def diff_attention(q, k, v, bias, lam, subln_w, batch, seq, lam_init):
    m, width = q.shape
    head_w = 2 * DIFF_HEAD_DIM
    n_heads = width // head_w
    t = ATTN_BLOCK
    nq = seq // t
    return pl.pallas_call(
        functools.partial(_diff_attn_kernel, lam_init=lam_init),
        grid=(batch, n_heads, nq),
        in_specs=[pl.BlockSpec((t, head_w), lambda b, h, i: (b * nq + i, h)),
                  pl.BlockSpec((seq, head_w), lambda b, h, i: (b, h)),
                  pl.BlockSpec((seq, head_w), lambda b, h, i: (b, h)),
                  pl.BlockSpec((None, 2, t, t), lambda b, h, i: (h, 0, 0, 0)),
                  _full_spec(lam.shape), _full_spec((1, head_w))],
        out_specs=pl.BlockSpec((t, head_w), lambda b, h, i: (b * nq + i, h)),
        out_shape=jax.ShapeDtypeStruct((m, width), BF16),
        compiler_params=_params(("parallel", "parallel", "arbitrary")),
        name="diff_attention",
    )(q, k, v, bias, lam, subln_w.reshape(1, head_w))


def _retention_kernel(q_ref, k_ref, v_ref, g_ref, cos_ref, sin_ref, decay_ref, zeta_ref, xi_ref, o_ref,
                      state_ref, *, chunk_decay):
    @pl.when(pl.program_id(1) == 0)
    def _():
        state_ref[...] = jnp.zeros_like(state_ref)

    cos = cos_ref[...]
    sin = sin_ref[...]
    half = RET_QK_DIM // 2

    def rope(t):
        return t * cos + pltpu.roll(t, half, 1) * sin

    for h in range(RET_HEADS):
        qs = slice(h * RET_QK_DIM, (h + 1) * RET_QK_DIM)
        vs = slice(h * RET_V_DIM, (h + 1) * RET_V_DIM)
        qr = rope(q_ref[:, qs].astype(F32))
        kr = rope(k_ref[:, qs].astype(F32)) * (RET_QK_DIM ** -0.5)
        vh = v_ref[:, vs]
        scores = lax.dot_general(qr.astype(BF16), kr.astype(BF16), NT_DIMS,
                                 preferred_element_type=F32) * decay_ref[h]
        inner = jnp.dot(scores.astype(BF16), vh, preferred_element_type=F32)
        st = state_ref[h]
        cross = jnp.dot((qr * xi_ref[h]).astype(BF16), st.astype(BF16), preferred_element_type=F32)
        kv = jnp.dot((kr.T * zeta_ref[h]).astype(BF16), vh, preferred_element_type=F32)
        state_ref[h] = st * chunk_decay[h] + kv
        out = _rms(inner + cross) * _silu(g_ref[:, vs].astype(F32))
        o_ref[:, vs] = out.astype(o_ref.dtype)


def retention(rq, rk, rv, rg, batch, seq):
    m = rq.shape[0]
    nc = seq // CHUNK
    half = RET_QK_DIM // 2
    pos = jnp.arange(seq, dtype=F32)
    inv = ROPE_BASE ** (-jnp.arange(half, dtype=F32) / half)
    ang = pos[:, None] * inv[None]
    cos = jnp.concatenate([jnp.cos(ang), jnp.cos(ang)], axis=-1)
    sin = jnp.concatenate([-jnp.sin(ang), jnp.sin(ang)], axis=-1)
    log_gamma = jnp.log(1.0 - 2.0 ** (-5.0 - jnp.arange(RET_HEADS, dtype=F32)))
    idx = jnp.arange(CHUNK, dtype=F32)
    rel = idx[:, None] - idx[None, :]
    decay = jnp.where(rel >= 0, jnp.exp(jnp.maximum(rel, 0.0)[None] * log_gamma[:, None, None]), 0.0)
    zeta = jnp.exp((CHUNK - 1 - idx)[None] * log_gamma[:, None])[:, None, :]
    xi = jnp.broadcast_to(jnp.exp((idx + 1.0)[None] * log_gamma[:, None])[:, :, None],
                          (RET_HEADS, CHUNK, RET_QK_DIM))
    gamma = 1.0 - 2.0 ** (-5.0 - np.arange(RET_HEADS, dtype=np.float64))
    chunk_decay = tuple(float(g ** CHUNK) for g in gamma)
    row_spec = lambda width: pl.BlockSpec((CHUNK, width), lambda b, c: (b * nc + c, 0))
    return pl.pallas_call(
        functools.partial(_retention_kernel, chunk_decay=chunk_decay),
        grid=(batch, nc),
        in_specs=[row_spec(rq.shape[1]), row_spec(rk.shape[1]), row_spec(rv.shape[1]), row_spec(rg.shape[1]),
                  pl.BlockSpec((CHUNK, RET_QK_DIM), lambda b, c: (c, 0)),
                  pl.BlockSpec((CHUNK, RET_QK_DIM), lambda b, c: (c, 0)),
                  _full_spec(decay.shape), _full_spec(zeta.shape), _full_spec(xi.shape)],
        out_specs=row_spec(rv.shape[1]),
        out_shape=jax.ShapeDtypeStruct((m, rv.shape[1]), BF16),
        scratch_shapes=[pltpu.VMEM((RET_HEADS, RET_QK_DIM, RET_V_DIM), F32)],
        compiler_params=_params(("parallel", "arbitrary")),
        name="retention",
    )(rq, rk, rv, rg, cos, sin, decay, zeta, xi)


def _swa_kernel(sink_ref, q_ref, kp_ref, kc_ref, vp_ref, vc_ref, bias_ref, o_ref):
    i = pl.program_id(1)
    blk = q_ref.shape[0]
    rep = SWA_HEADS // SWA_KV_HEADS
    kb = jnp.concatenate([kp_ref[...], kc_ref[...]], axis=0)
    vb = jnp.concatenate([vp_ref[...], vc_ref[...]], axis=0)
    col = lax.broadcasted_iota(jnp.int32, (blk, 2 * blk), 1)
    no_prev = jnp.logical_and(col < blk, i == 0)
    lane = lax.broadcasted_iota(jnp.int32, (blk, LANES), 1)
    lower = lane < SWA_HEAD_DIM
    outs = []
    for p in range(SWA_HEADS // 2):
        qp = q_ref[:, p * LANES:(p + 1) * LANES].astype(F32)
        pair_out = []
        for parity in range(2):
            h = 2 * p + parity
            g = h // rep
            qh = qp if parity == g else pltpu.roll(qp, SWA_HEAD_DIM, 1)
            qh = jnp.where(lower if g == 0 else jnp.logical_not(lower), qh, 0.0).astype(BF16)
            s = lax.dot_general(qh, kb, NT_DIMS, preferred_element_type=F32) + bias_ref[h]
            s = jnp.where(no_prev, MASK_VALUE, s)
            sink = sink_ref[h]
            m = jnp.maximum(jnp.max(s, axis=-1, keepdims=True), sink)
            e = jnp.exp(s - m)
            denom = jnp.sum(e, axis=-1, keepdims=True) + jnp.exp(sink - m)
            o = jnp.dot(e.astype(BF16), vb, preferred_element_type=F32) / denom
            pair_out.append(o if parity == g else pltpu.roll(o, SWA_HEAD_DIM, 1))
        outs.append(jnp.where(lower, pair_out[0], pair_out[1]))
    o_ref[...] = jnp.concatenate(outs, axis=-1).astype(o_ref.dtype)


def sliding_window_attention(sq, sk, sv, sinks, bias, batch, seq):
    m, qw = sq.shape
    kw = sk.shape[1]
    nb = seq // SWA_BLOCK
    cur = lambda b, i: (b * nb + i, 0)
    prev = lambda b, i: (b * nb + jnp.maximum(i - 1, 0), 0)
    return pl.pallas_call(
        _swa_kernel,
        grid=(batch, nb),
        in_specs=[pl.BlockSpec(memory_space=pltpu.SMEM),
                  pl.BlockSpec((SWA_BLOCK, qw), cur),
                  pl.BlockSpec((SWA_BLOCK, kw), prev), pl.BlockSpec((SWA_BLOCK, kw), cur),
                  pl.BlockSpec((SWA_BLOCK, kw), prev), pl.BlockSpec((SWA_BLOCK, kw), cur),
                  _full_spec(bias.shape)],
        out_specs=pl.BlockSpec((SWA_BLOCK, qw), cur),
        out_shape=jax.ShapeDtypeStruct((m, qw), BF16),
        compiler_params=_params(("parallel", "parallel")),
        name="sliding_window_attention",
    )(sinks, sq, sk, sk, sv, sv, bias)


def _split_cols(w, sizes):
    offs = np.cumsum((0,) + tuple(sizes))
    return [w[:, offs[j]:offs[j + 1]] for j in range(len(sizes))]


def even_layer_mixer(x2, mod, g_pre, g_post, w_in, conv_w, conv_b, dt_bias, a_log, d_skip, ssd_norm, lam,
                     diff_norm, w_out, diff_bias, layer_idx, batch, seq):
    d = x2.shape[1]
    d_inner = d
    n_ssd_heads = d_inner // SSD_HEAD_DIM
    conv_ch = d_inner + 2 * SSD_GROUPS * SSD_STATE
    wz, wxbc, wdt, wq, wk, wv = _split_cols(w_in, (d_inner, conv_ch, n_ssd_heads, d, d, d))
    wdt = jnp.pad(wdt, ((0, 0), (0, LANES - n_ssd_heads)))
    wq = wq * (DIFF_HEAD_DIM ** -0.5)
    weights = [w.astype(BF16) for w in (wz, wxbc, wdt, wq, wk, wv)]
    z, xbc, dt_raw, q, k, v = norm_proj(x2, mod, g_pre, weights, (BF16, BF16, F32, BF16, BF16, BF16), seq)
    y_ssd = ssd_mixer(z, xbc, dt_raw, conv_w, conv_b, dt_bias, a_log, d_skip, ssd_norm, batch, seq)
    lam_init = 0.8 - 0.6 * math.exp(-0.3 * layer_idx)
    y_diff = diff_attention(q, k, v, diff_bias, lam, diff_norm, batch, seq, lam_init)
    w_out = w_out.astype(BF16)
    return out_proj(x2, mod, g_post, [y_ssd, y_diff], [w_out[:d_inner], w_out[d_inner:]], seq)


def odd_layer_mixer(x2, mod, g_pre, g_post, w_in, sinks, w_out, swa_bias, batch, seq):
    ret_qk = RET_HEADS * RET_QK_DIM
    ret_v = RET_HEADS * RET_V_DIM
    swa_q = SWA_HEADS * SWA_HEAD_DIM
    swa_kv = SWA_KV_HEADS * SWA_HEAD_DIM
    wrq, wrk, wrv, wrg, wsq, wsk, wsv = _split_cols(w_in, (ret_qk, ret_qk, ret_v, ret_v, swa_q, swa_kv, swa_kv))
    wsq = wsq * (SWA_HEAD_DIM ** -0.5)
    weights = [w.astype(BF16) for w in (wrq, wrk, wrv, wrg, wsq, wsk, wsv)]
    rq, rk, rv, rg, sq, sk, sv = norm_proj(x2, mod, g_pre, weights, (BF16,) * 7, seq)
    y_ret = retention(rq, rk, rv, rg, batch, seq)
    y_swa = sliding_window_attention(sq, sk, sv, sinks, swa_bias, batch, seq)
    w_out = w_out.astype(BF16)
    return out_proj(x2, mod, g_post, [y_ret, y_swa], [w_out[:ret_v], w_out[ret_v:]], seq)


def kernel(x, c, rel_bias, norm_gains, mod_w, mod_b, mlp_w1, mlp_w2, e_w_in, e_conv_w, e_conv_b, e_dt_bias,
           e_A_log, e_D, e_ssd_norm, e_lambda, e_diff_norm, e_w_out, o_w_in, o_sinks, o_w_out):
    batch, seq, d = x.shape
    depth = norm_gains.shape[0]
    assert seq % 512 == 0 and seq % ATTN_BLOCK == 0 and d % LANES == 0
    mods = modulation(c, mod_w.reshape(depth * 2, d, 3 * d), mod_b.reshape(depth * 2, 3 * d))
    far_bucket = REL_BUCKETS - 1
    diff_bias = bias_tiles(rel_bias, _diff_bucket_idx(ATTN_BLOCK), shift_bucket=far_bucket)
    diff_bias = diff_bias.reshape(rel_bias.shape[1], 2, ATTN_BLOCK, ATTN_BLOCK)
    swa_bias = bias_tiles(rel_bias, _swa_bucket_idx())

    x2 = x.reshape(batch * seq, d)
    for layer in range(depth):
        j = layer // 2
        gains = norm_gains[layer]
        if layer % 2 == 0:
            x2 = even_layer_mixer(x2, mods[2 * layer], gains[0], gains[1], e_w_in[j], e_conv_w[j], e_conv_b[j],
                                  e_dt_bias[j], e_A_log[j], e_D[j], e_ssd_norm[j], e_lambda[j], e_diff_norm[j],
                                  e_w_out[j], diff_bias, layer, batch, seq)
        else:
            x2 = odd_layer_mixer(x2, mods[2 * layer], gains[0], gains[1], o_w_in[j], o_sinks[j], o_w_out[j],
                                 swa_bias, batch, seq)
        x2 = mlp(x2, mods[2 * layer + 1], gains[2], gains[3], mlp_w1[layer].astype(BF16),
                 mlp_w2[layer].astype(BF16), seq)
    return x2.reshape(batch, seq, d)
```

```python
import functools
import math

import jax
import jax.numpy as jnp
import numpy as np
from jax import lax
from jax.experimental import pallas as pl
from jax.experimental.pallas import tpu as pltpu

EPS = 1e-6
MASK_VALUE = -1e30
LANES = 128
SUBLANES = 8
VMEM_LIMIT = 56 * 1024 * 1024

CHUNK = 128
REL_BUCKETS = 32
REL_MAX_DIST = 128
SSD_HEAD_DIM = 64
SSD_GROUPS = 4
SSD_STATE = 128
SSD_CONV = 4
DIFF_HEAD_DIM = 64
RET_HEADS = 4
RET_QK_DIM = 128
RET_V_DIM = 256
ROPE_BASE = 10000.0
SWA_HEADS = 8
SWA_KV_HEADS = 2
SWA_HEAD_DIM = 64
SWA_BLOCK = 128
ATTN_BLOCK = 256

BF16 = jnp.bfloat16
F32 = jnp.float32
NT_DIMS = (((1,), (1,)), ((), ()))


def _params(semantics):
    return pltpu.CompilerParams(dimension_semantics=semantics, vmem_limit_bytes=VMEM_LIMIT)


def _full_spec(shape):
    return pl.BlockSpec(shape, lambda *_: (0,) * len(shape))


def _silu(x):
    return x * jax.nn.sigmoid(x)


def _rms(x, eps=EPS):
    return x * lax.rsqrt(jnp.mean(x * x, axis=-1, keepdims=True) + eps)


def _mod_kernel(c_ref, w_ref, b_ref, o_ref):
    c_act = _silu(c_ref[...])
    o_ref[...] = jnp.dot(c_act, w_ref[...], preferred_element_type=F32,
                         precision=lax.Precision.HIGHEST) + b_ref[...]


def modulation(c, mod_w, mod_b):
    b, d = c.shape
    n = mod_w.shape[0]
    rows = SUBLANES
    c_pad = jnp.zeros((rows, d), F32).at[:b].set(c)
    tn = d
    out = pl.pallas_call(
        _mod_kernel,
        grid=(n, 3 * d // tn),
        in_specs=[pl.BlockSpec((rows, d), lambda s, j: (0, 0)),
                  pl.BlockSpec((None, d, tn), lambda s, j: (s, 0, j)),
                  pl.BlockSpec((None, 1, tn), lambda s, j: (s, 0, j))],
        out_specs=pl.BlockSpec((None, rows, tn), lambda s, j: (s, 0, j)),
        out_shape=jax.ShapeDtypeStruct((n, rows, 3 * d), F32),
        compiler_params=_params(("parallel", "parallel")),
        name="modulation",
    )(c_pad, mod_w, mod_b.reshape(n, 1, 3 * d))
    return out[:, :b].reshape(n, b, 3, d)


def _t5_bucket_np(dist):
    max_exact = REL_BUCKETS // 2
    logd = np.log(np.maximum(dist, 1).astype(np.float32) / np.float32(max_exact))
    large = max_exact + (logd / np.float32(math.log(REL_MAX_DIST / max_exact))
                         * np.float32(REL_BUCKETS - max_exact)).astype(np.int32)
    large = np.minimum(large, REL_BUCKETS - 1)
    return np.where(dist < max_exact, dist, large).astype(np.int32)


def _bias_kernel(rb_ref, idx_ref, o_ref, *, shift_bucket):
    h = pl.program_id(0)
    idx = idx_ref[...]
    acc = jnp.full(idx.shape, MASK_VALUE, F32)
    shift = rb_ref[shift_bucket, h] if shift_bucket is not None else 0.0
    for bucket in range(REL_BUCKETS):
        acc = jnp.where(idx == bucket, rb_ref[bucket, h] - shift, acc)
    o_ref[...] = acc


def bias_tiles(rel_bias, bucket_idx, shift_bucket=None):
    n_heads = rel_bias.shape[1]
    r, c = bucket_idx.shape
    return pl.pallas_call(
        functools.partial(_bias_kernel, shift_bucket=shift_bucket),
        grid=(n_heads,),
        in_specs=[pl.BlockSpec(memory_space=pltpu.SMEM),
                  pl.BlockSpec((r, c), lambda h: (0, 0))],
        out_specs=pl.BlockSpec((None, r, c), lambda h: (h, 0, 0)),
        out_shape=jax.ShapeDtypeStruct((n_heads, r, c), F32),
        compiler_params=_params(("parallel",)),
        name="bias_tiles",
    )(rel_bias, jnp.asarray(bucket_idx))


def _diff_bucket_idx(t):
    q = np.arange(t)[:, None]
    k = np.arange(t)[None, :]
    diag = q - k
    idx_diag = np.where(diag >= 0, _t5_bucket_np(np.maximum(diag, 0)), -1)
    idx_prev = _t5_bucket_np(diag + t)
    first = np.concatenate([idx_diag, np.full((t, t), -1)], axis=1)
    later = np.concatenate([idx_prev, idx_diag], axis=1)
    return np.concatenate([first, later], axis=0).astype(np.int32)


def _swa_bucket_idx():
    q = np.arange(SWA_BLOCK)[:, None]
    k = np.arange(2 * SWA_BLOCK)[None, :] - SWA_BLOCK
    dist = q - k
    valid = (dist >= 0) & (dist < SWA_BLOCK)
    return np.where(valid, _t5_bucket_np(np.maximum(dist, 0)), -1).astype(np.int32)


def _norm_proj_kernel(x_ref, mod_ref, g_ref, *refs):
    n = len(refs) // 2
    w_refs, o_refs = refs[:n], refs[n:]
    mod = mod_ref[...]
    h = _rms(x_ref[...]) * g_ref[...] * (1.0 + mod[1:2]) + mod[0:1]
    hb = h.astype(BF16)
    for w_ref, o_ref in zip(w_refs, o_refs):
        o_ref[...] = jnp.dot(hb, w_ref[...], preferred_element_type=F32).astype(o_ref.dtype)


def norm_proj(x2, mod, gain, weights, out_dtypes, seq, tm=512):
    m, d = x2.shape
    per_batch = seq // tm
    in_specs = [pl.BlockSpec((tm, d), lambda i: (i, 0)),
                pl.BlockSpec((None, 3, d), lambda i: (i // per_batch, 0, 0)),
                _full_spec((1, d))]
    in_specs += [_full_spec(w.shape) for w in weights]
    out_specs = [pl.BlockSpec((tm, w.shape[1]), lambda i: (i, 0)) for w in weights]
    out_shape = [jax.ShapeDtypeStruct((m, w.shape[1]), dt) for w, dt in zip(weights, out_dtypes)]
    return pl.pallas_call(
        _norm_proj_kernel,
        grid=(m // tm,),
        in_specs=in_specs,
        out_specs=out_specs,
        out_shape=out_shape,
        compiler_params=_params(("parallel",)),
        name="norm_proj",
    )(x2, mod, gain.reshape(1, d), *weights)


def _out_proj_kernel(x_ref, mod_ref, g_ref, *refs):
    n = (len(refs) - 1) // 2
    y_refs, w_refs, o_ref = refs[:n], refs[n:2 * n], refs[2 * n]
    acc = None
    for y_ref, w_ref in zip(y_refs, w_refs):
        part = jnp.dot(y_ref[...], w_ref[...], preferred_element_type=F32)
        acc = part if acc is None else acc + part
    gate = mod_ref[...][2:3]
    o_ref[...] = x_ref[...] + gate * (_rms(acc) * g_ref[...])


def out_proj(x2, mod, gain, ys, weights, seq, tm=512):
    m, d = x2.shape
    per_batch = seq // tm
    in_specs = [pl.BlockSpec((tm, d), lambda i: (i, 0)),
                pl.BlockSpec((None, 3, d), lambda i: (i // per_batch, 0, 0)),
                _full_spec((1, d))]
    in_specs += [pl.BlockSpec((tm, y.shape[1]), lambda i: (i, 0)) for y in ys]
    in_specs += [_full_spec(w.shape) for w in weights]
    return pl.pallas_call(
        _out_proj_kernel,
        grid=(m // tm,),
        in_specs=in_specs,
        out_specs=pl.BlockSpec((tm, d), lambda i: (i, 0)),
        out_shape=jax.ShapeDtypeStruct((m, d), F32),
        compiler_params=_params(("parallel",)),
        name="out_proj",
    )(x2, mod, gain.reshape(1, d), *ys, *weights)


def _mlp_kernel(x_ref, mod_ref, g_pre_ref, g_post_ref, w1_ref, w2_ref, o_ref, *, ff_chunk):
    x = x_ref[...]
    mod = mod_ref[...]
    h = _rms(x) * g_pre_ref[...] * (1.0 + mod[1:2]) + mod[0:1]
    hb = h.astype(BF16)
    d_ff = w1_ref.shape[1]
    acc = None
    for c0 in range(0, d_ff, ff_chunk):
        a = jnp.dot(hb, w1_ref[:, c0:c0 + ff_chunk], preferred_element_type=F32)
        a = jnp.square(jnp.maximum(a, 0.0)).astype(BF16)
        part = jnp.dot(a, w2_ref[c0:c0 + ff_chunk, :], preferred_element_type=F32)
        acc = part if acc is None else acc + part
    o_ref[...] = x + mod[2:3] * (_rms(acc) * g_post_ref[...])


def mlp(x2, mod, g_pre, g_post, w1, w2, seq, tm=512, ff_chunk=1024):
    m, d = x2.shape
    per_batch = seq // tm
    return pl.pallas_call(
        functools.partial(_mlp_kernel, ff_chunk=ff_chunk),
        grid=(m // tm,),
        in_specs=[pl.BlockSpec((tm, d), lambda i: (i, 0)),
                  pl.BlockSpec((None, 3, d), lambda i: (i // per_batch, 0, 0)),
                  _full_spec((1, d)), _full_spec((1, d)),
                  _full_spec(w1.shape), _full_spec(w2.shape)],
        out_specs=pl.BlockSpec((tm, d), lambda i: (i, 0)),
        out_shape=jax.ShapeDtypeStruct((m, d), F32),
        compiler_params=_params(("parallel",)),
        name="mlp",
    )(x2, mod, g_pre.reshape(1, d), g_post.reshape(1, d), w1, w2)


def _ssd_kernel(z_ref, xbc_ref, dt_ref, cw_ref, cb_ref, dtb_ref, alog_ref, dskip_ref, nw_ref, expand_ref,
                o_ref, conv_buf, state_ref, *, d_inner):
    chunk = z_ref.shape[0]
    n_state = SSD_STATE
    gn = SSD_GROUPS * n_state
    pair = 2 * SSD_HEAD_DIM
    heads_per_group = d_inner // SSD_HEAD_DIM // SSD_GROUPS
    halo = SUBLANES

    @pl.when(pl.program_id(1) == 0)
    def _():
        state_ref[...] = jnp.zeros_like(state_ref)
        conv_buf[0:halo, :] = jnp.zeros((halo, conv_buf.shape[1]), F32)

    conv_buf[halo:halo + chunk, :] = xbc_ref[...].astype(F32)
    acc = cb_ref[...]
    for tap in range(SSD_CONV):
        off = halo - (SSD_CONV - 1) + tap
        acc = acc + cw_ref[tap:tap + 1, :] * conv_buf[off:off + chunk, :]
    conv_buf[0:halo, :] = conv_buf[chunk:chunk + halo, :]
    xbc = _silu(acc)
    xs = xbc[:, :d_inner]
    b_all = xbc[:, d_inner:d_inner + gn]
    c_all = xbc[:, d_inner + gn:]

    dt_in = dt_ref[...] + dtb_ref[...]
    dt = jnp.maximum(dt_in, 0.0) + jnp.log1p(jnp.exp(-jnp.abs(dt_in)))
    a = dt * (-jnp.exp(alog_ref[...]))
    row = lax.broadcasted_iota(jnp.int32, (chunk, chunk), 0)
    col = lax.broadcasted_iota(jnp.int32, (chunk, chunk), 1)
    causal = row >= col
    tril = causal.astype(F32)
    hi = lax.Precision.HIGHEST
    a_cs = jnp.dot(tril, a, preferred_element_type=F32, precision=hi)
    a_cs_t = a_cs.T
    expand = expand_ref[...]
    dt_e = jnp.dot(dt, expand, preferred_element_type=F32, precision=hi)
    acs_e = jnp.dot(a_cs, expand, preferred_element_type=F32, precision=hi)
    a_last_e = acs_e[chunk - 1:chunk, :]
    x_dt = xs * dt_e
    w_state = (x_dt * jnp.exp(a_last_e - acs_e)).astype(BF16)
    out_scale = jnp.exp(acs_e)
    chunk_decay = jnp.exp(a_last_e)
    x_dt_b = x_dt.astype(BF16)
    lane = lax.broadcasted_iota(jnp.int32, (chunk, pair), 1)
    first_head = lane < SSD_HEAD_DIM

    y_blocks = []
    for g in range(SSD_GROUPS):
        bg = b_all[:, g * n_state:(g + 1) * n_state]
        cg = c_all[:, g * n_state:(g + 1) * n_state].astype(BF16)
        cb = lax.dot_general(cg, bg.astype(BF16), NT_DIMS, preferred_element_type=F32)
        bg_t = bg.T.astype(BF16)
        for pi in range(heads_per_group // 2):
            p = g * (heads_per_group // 2) + pi
            sl = slice(p * pair, (p + 1) * pair)
            xp = x_dt_b[:, sl]
            y_diag = None
            for k in range(2):
                hh = 2 * p + k
                seg = a_cs[:, hh:hh + 1] - a_cs_t[hh:hh + 1, :]
                decay = jnp.exp(jnp.where(causal, seg, MASK_VALUE))
                m = (cb * decay).astype(BF16)
                xh = jnp.where(first_head if k == 0 else jnp.logical_not(first_head), xp, jnp.zeros_like(xp))
                part = jnp.dot(m, xh, preferred_element_type=F32)
                y_diag = part if y_diag is None else y_diag + part
            st = state_ref[p]
            y_off = jnp.dot(cg, st.astype(BF16), preferred_element_type=F32) * out_scale[:, sl]
            state_ref[p] = st * chunk_decay[:, sl] + jnp.dot(bg_t, w_state[:, sl], preferred_element_type=F32)
            y_blocks.append(y_diag + y_off)
    y = jnp.concatenate(y_blocks, axis=-1) + xs * dskip_ref[...]
    y = y * _silu(z_ref[...].astype(F32))
    o_ref[...] = (_rms(y) * nw_ref[...]).astype(o_ref.dtype)


def ssd_mixer(z, xbc, dt_raw, conv_w, conv_b, dt_bias, a_log, d_skip, norm_w, batch, seq):
    m, d_inner = z.shape
    conv_ch = xbc.shape[1]
    n_heads = d_inner // SSD_HEAD_DIM
    nc = seq // CHUNK
    pad = LANES - n_heads
    expand = np.zeros((LANES, d_inner), np.float32)
    for h in range(n_heads):
        expand[h, h * SSD_HEAD_DIM:(h + 1) * SSD_HEAD_DIM] = 1.0
    row_spec = lambda width: pl.BlockSpec((CHUNK, width), lambda b, c: (b * nc + c, 0))
    return pl.pallas_call(
        functools.partial(_ssd_kernel, d_inner=d_inner),
        grid=(batch, nc),
        in_specs=[row_spec(d_inner), row_spec(conv_ch), row_spec(LANES),
                  _full_spec((SSD_CONV, conv_ch)), _full_spec((1, conv_ch)),
                  _full_spec((1, LANES)), _full_spec((1, LANES)),
                  _full_spec((1, d_inner)), _full_spec((1, d_inner)), _full_spec((LANES, d_inner))],
        out_specs=row_spec(d_inner),
        out_shape=jax.ShapeDtypeStruct((m, d_inner), BF16),
        scratch_shapes=[pltpu.VMEM((CHUNK + SUBLANES, conv_ch), F32),
                        pltpu.VMEM((n_heads // 2, SSD_STATE, 2 * SSD_HEAD_DIM), F32)],
        compiler_params=_params(("parallel", "arbitrary")),
        name="ssd_mixer",
    )(z, xbc, dt_raw, conv_w, conv_b.reshape(1, conv_ch),
      jnp.pad(dt_bias, (0, pad)).reshape(1, LANES), jnp.pad(a_log, (0, pad)).reshape(1, LANES),
      jnp.repeat(d_skip, SSD_HEAD_DIM).reshape(1, d_inner), norm_w.reshape(1, d_inner), jnp.asarray(expand))


def _diff_attn_kernel(q_ref, k_ref, v_ref, bias_ref, lam_ref, nw_ref, o_ref,
                      vext_ref, s_ref, p_ref, m_ref, alpha_ref, acc_ref, *, lam_init, row_chunk):
    t = q_ref.shape[0]
    i = pl.program_id(2)
    q = q_ref[...]
    lane = lax.broadcasted_iota(jnp.int32, q.shape, 1)
    zero = jnp.zeros_like(q)
    qq = jnp.concatenate([jnp.where(lane < DIFF_HEAD_DIM, q, zero),
                          jnp.where(lane >= DIFF_HEAD_DIM, q, zero)], axis=0)

    head_w = v_ref.shape[1]
    rows = 2 * t

    @pl.when(i == 0)
    def _():
        vext_ref[:, :head_w] = v_ref[...]
        vext_ref[:, head_w:] = jnp.ones((vext_ref.shape[0], head_w), BF16)

    m_ref[...] = jnp.full(m_ref.shape, MASK_VALUE, F32)
    acc_ref[...] = jnp.zeros(acc_ref.shape, F32)

    def step(start, bk, bias):
        kb = k_ref[pl.ds(start, bk), :]
        s = lax.dot_general(qq, kb, NT_DIMS, preferred_element_type=F32)
        if bias is not None:
            s = s + jnp.concatenate([bias, bias], axis=0)
        s_ref[:, :bk] = s

        def row_max(c, _):
            r = pl.ds(pl.multiple_of(c * row_chunk, row_chunk), row_chunk)
            m_old = m_ref[r, :]
            m_new = jnp.maximum(m_old, jnp.max(s_ref[r, :bk], axis=-1, keepdims=True))
            m_ref[r, :] = m_new
            alpha_ref[r, :] = jnp.exp(m_old - m_new)
            return 0

        lax.fori_loop(0, rows // row_chunk, row_max, 0, unroll=True)

        def probs(c, _):
            r = pl.ds(pl.multiple_of(c * row_chunk, row_chunk), row_chunk)
            m_rep = jnp.tile(m_ref[r, :], (1, bk // LANES))
            p_ref[r, :bk] = jnp.exp(s_ref[r, :bk] - m_rep).astype(BF16)
            return 0

        lax.fori_loop(0, rows // row_chunk, probs, 0, unroll=True)
        pv = jnp.dot(p_ref[:, :bk], vext_ref[pl.ds(start, bk), :], preferred_element_type=F32)
        acc_ref[...] = jnp.tile(alpha_ref[...], (1, 2)) * acc_ref[...] + pv

    n_far = jnp.maximum(i - 1, 0)
    far_bk = 2 * t

    def far_body(j, _):
        step(pl.multiple_of(j * far_bk, far_bk), far_bk, None)
        return 0

    lax.fori_loop(0, n_far // 2, far_body, 0)

    @pl.when(n_far % 2 == 1)
    def _():
        step(pl.multiple_of((n_far - 1) * t, t), t, None)

    step(pl.multiple_of(n_far * t, t), 2 * t, bias_ref[jnp.minimum(i, 1)])

    lam = lam_ref[...]
    lam_full = (jnp.exp(jnp.sum(lam[0:1] * lam[1:2], axis=-1, keepdims=True))
                - jnp.exp(jnp.sum(lam[2:3] * lam[3:4], axis=-1, keepdims=True)) + lam_init)
    acc = acc_ref[...]
    out = acc[:, :head_w] / acc[:, head_w:]
    out = out[:t] - lam_full * out[t:]
    o_ref[...] = (_rms(out) * nw_ref[...] * (1.0 - lam_init)).astype(o_ref.dtype)


def diff_attention(q, k, v, bias, lam, subln_w, batch, seq, lam_init, row_chunk=64):
    m, width = q.shape
    head_w = 2 * DIFF_HEAD_DIM
    n_heads = width // head_w
    t = ATTN_BLOCK
    nq = seq // t
    rows, max_bk = 2 * t, 2 * t
    return pl.pallas_call(
        functools.partial(_diff_attn_kernel, lam_init=lam_init, row_chunk=row_chunk),
        grid=(batch, n_heads, nq),
        in_specs=[pl.BlockSpec((t, head_w), lambda b, h, i: (b * nq + i, h)),
                  pl.BlockSpec((seq, head_w), lambda b, h, i: (b, h)),
                  pl.BlockSpec((seq, head_w), lambda b, h, i: (b, h)),
                  pl.BlockSpec((None, 2, t, max_bk), lambda b, h, i: (h, 0, 0, 0)),
                  _full_spec(lam.shape), _full_spec((1, head_w))],
        out_specs=pl.BlockSpec((t, head_w), lambda b, h, i: (b * nq + i, h)),
        out_shape=jax.ShapeDtypeStruct((m, width), BF16),
        scratch_shapes=[pltpu.VMEM((seq, 2 * head_w), BF16),
                        pltpu.VMEM((rows, max_bk), F32),
                        pltpu.VMEM((rows, max_bk), BF16),
                        pltpu.VMEM((rows, LANES), F32),
                        pltpu.VMEM((rows, LANES), F32),
                        pltpu.VMEM((rows, 2 * head_w), F32)],
        compiler_params=_params(("parallel", "parallel", "arbitrary")),
        name="diff_attention",
    )(q, k, v, bias, lam, subln_w.reshape(1, head_w))


def _retention_kernel(q_ref, k_ref, v_ref, g_ref, cos_ref, sin_ref, decay_ref, zeta_ref, xi_ref, o_ref,
                      state_ref, *, chunk_decay):
    @pl.when(pl.program_id(1) == 0)
    def _():
        state_ref[...] = jnp.zeros_like(state_ref)

    cos = cos_ref[...]
    sin = sin_ref[...]
    half = RET_QK_DIM // 2

    def rope(t):
        return t * cos + pltpu.roll(t, half, 1) * sin

    for h in range(RET_HEADS):
        qs = slice(h * RET_QK_DIM, (h + 1) * RET_QK_DIM)
        vs = slice(h * RET_V_DIM, (h + 1) * RET_V_DIM)
        qr = rope(q_ref[:, qs].astype(F32))
        kr = rope(k_ref[:, qs].astype(F32)) * (RET_QK_DIM ** -0.5)
        vh = v_ref[:, vs]
        scores = lax.dot_general(qr.astype(BF16), kr.astype(BF16), NT_DIMS,
                                 preferred_element_type=F32) * decay_ref[h]
        inner = jnp.dot(scores.astype(BF16), vh, preferred_element_type=F32)
        st = state_ref[h]
        cross = jnp.dot((qr * xi_ref[h]).astype(BF16), st.astype(BF16), preferred_element_type=F32)
        kv = jnp.dot((kr.T * zeta_ref[h]).astype(BF16), vh, preferred_element_type=F32)
        state_ref[h] = st * chunk_decay[h] + kv
        out = _rms(inner + cross) * _silu(g_ref[:, vs].astype(F32))
        o_ref[:, vs] = out.astype(o_ref.dtype)


def retention(rq, rk, rv, rg, batch, seq):
    m = rq.shape[0]
    nc = seq // CHUNK
    half = RET_QK_DIM // 2
    pos = jnp.arange(seq, dtype=F32)
    inv = ROPE_BASE ** (-jnp.arange(half, dtype=F32) / half)
    ang = pos[:, None] * inv[None]
    cos = jnp.concatenate([jnp.cos(ang), jnp.cos(ang)], axis=-1)
    sin = jnp.concatenate([-jnp.sin(ang), jnp.sin(ang)], axis=-1)
    log_gamma = jnp.log(1.0 - 2.0 ** (-5.0 - jnp.arange(RET_HEADS, dtype=F32)))
    idx = jnp.arange(CHUNK, dtype=F32)
    rel = idx[:, None] - idx[None, :]
    decay = jnp.where(rel >= 0, jnp.exp(jnp.maximum(rel, 0.0)[None] * log_gamma[:, None, None]), 0.0)
    zeta = jnp.exp((CHUNK - 1 - idx)[None] * log_gamma[:, None])[:, None, :]
    xi = jnp.broadcast_to(jnp.exp((idx + 1.0)[None] * log_gamma[:, None])[:, :, None],
                          (RET_HEADS, CHUNK, RET_QK_DIM))
    gamma = 1.0 - 2.0 ** (-5.0 - np.arange(RET_HEADS, dtype=np.float64))
    chunk_decay = tuple(float(g ** CHUNK) for g in gamma)
    row_spec = lambda width: pl.BlockSpec((CHUNK, width), lambda b, c: (b * nc + c, 0))
    return pl.pallas_call(
        functools.partial(_retention_kernel, chunk_decay=chunk_decay),
        grid=(batch, nc),
        in_specs=[row_spec(rq.shape[1]), row_spec(rk.shape[1]), row_spec(rv.shape[1]), row_spec(rg.shape[1]),
                  pl.BlockSpec((CHUNK, RET_QK_DIM), lambda b, c: (c, 0)),
                  pl.BlockSpec((CHUNK, RET_QK_DIM), lambda b, c: (c, 0)),
                  _full_spec(decay.shape), _full_spec(zeta.shape), _full_spec(xi.shape)],
        out_specs=row_spec(rv.shape[1]),
        out_shape=jax.ShapeDtypeStruct((m, rv.shape[1]), BF16),
        scratch_shapes=[pltpu.VMEM((RET_HEADS, RET_QK_DIM, RET_V_DIM), F32)],
        compiler_params=_params(("parallel", "arbitrary")),
        name="retention",
    )(rq, rk, rv, rg, cos, sin, decay, zeta, xi)


def _swa_kernel(sink_ref, q_ref, kp_ref, kc_ref, vp_ref, vc_ref, bias_ref, o_ref):
    i = pl.program_id(1)
    blk = q_ref.shape[0]
    rep = SWA_HEADS // SWA_KV_HEADS
    kb = jnp.concatenate([kp_ref[...], kc_ref[...]], axis=0)
    vb = jnp.concatenate([vp_ref[...], vc_ref[...]], axis=0)
    col = lax.broadcasted_iota(jnp.int32, (blk, 2 * blk), 1)
    no_prev = jnp.logical_and(col < blk, i == 0)
    lane = lax.broadcasted_iota(jnp.int32, (blk, LANES), 1)
    lower = lane < SWA_HEAD_DIM
    outs = []
    for p in range(SWA_HEADS // 2):
        qp = q_ref[:, p * LANES:(p + 1) * LANES].astype(F32)
        pair_out = []
        for parity in range(2):
            h = 2 * p + parity
            g = h // rep
            qh = qp if parity == g else pltpu.roll(qp, SWA_HEAD_DIM, 1)
            qh = jnp.where(lower if g == 0 else jnp.logical_not(lower), qh, 0.0).astype(BF16)
            s = lax.dot_general(qh, kb, NT_DIMS, preferred_element_type=F32) + bias_ref[h]
            s = jnp.where(no_prev, MASK_VALUE, s)
            sink = sink_ref[h]
            m = jnp.maximum(jnp.max(s, axis=-1, keepdims=True), sink)
            e = jnp.exp(s - m)
            denom = jnp.sum(e, axis=-1, keepdims=True) + jnp.exp(sink - m)
            o = jnp.dot(e.astype(BF16), vb, preferred_element_type=F32) / denom
            pair_out.append(o if parity == g else pltpu.roll(o, SWA_HEAD_DIM, 1))
        outs.append(jnp.where(lower, pair_out[0], pair_out[1]))
    o_ref[...] = jnp.concatenate(outs, axis=-1).astype(o_ref.dtype)


def sliding_window_attention(sq, sk, sv, sinks, bias, batch, seq):
    m, qw = sq.shape
    kw = sk.shape[1]
    nb = seq // SWA_BLOCK
    cur = lambda b, i: (b * nb + i, 0)
    prev = lambda b, i: (b * nb + jnp.maximum(i - 1, 0), 0)
    return pl.pallas_call(
        _swa_kernel,
        grid=(batch, nb),
        in_specs=[pl.BlockSpec(memory_space=pltpu.SMEM),
                  pl.BlockSpec((SWA_BLOCK, qw), cur),
                  pl.BlockSpec((SWA_BLOCK, kw), prev), pl.BlockSpec((SWA_BLOCK, kw), cur),
                  pl.BlockSpec((SWA_BLOCK, kw), prev), pl.BlockSpec((SWA_BLOCK, kw), cur),
                  _full_spec(bias.shape)],
        out_specs=pl.BlockSpec((SWA_BLOCK, qw), cur),
        out_shape=jax.ShapeDtypeStruct((m, qw), BF16),
        compiler_params=_params(("parallel", "parallel")),
        name="sliding_window_attention",
    )(sinks, sq, sk, sk, sv, sv, bias)


def _split_cols(w, sizes):
    offs = np.cumsum((0,) + tuple(sizes))
    return [w[:, offs[j]:offs[j + 1]] for j in range(len(sizes))]


def even_layer_mixer(x2, mod, g_pre, g_post, w_in, conv_w, conv_b, dt_bias, a_log, d_skip, ssd_norm, lam,
                     diff_norm, w_out, diff_bias, layer_idx, batch, seq):
    d = x2.shape[1]
    d_inner = d
    n_ssd_heads = d_inner // SSD_HEAD_DIM
    conv_ch = d_inner + 2 * SSD_GROUPS * SSD_STATE
    wz, wxbc, wdt, wq, wk, wv = _split_cols(w_in, (d_inner, conv_ch, n_ssd_heads, d, d, d))
    wdt = jnp.pad(wdt, ((0, 0), (0, LANES - n_ssd_heads)))
    wq = wq * (DIFF_HEAD_DIM ** -0.5)
    weights = [w.astype(BF16) for w in (wz, wxbc, wdt, wq, wk, wv)]
    z, xbc, dt_raw, q, k, v = norm_proj(x2, mod, g_pre, weights, (BF16, BF16, F32, BF16, BF16, BF16), seq)
    y_ssd = ssd_mixer(z, xbc, dt_raw, conv_w, conv_b, dt_bias, a_log, d_skip, ssd_norm, batch, seq)
    lam_init = 0.8 - 0.6 * math.exp(-0.3 * layer_idx)
    y_diff = diff_attention(q, k, v, diff_bias, lam, diff_norm, batch, seq, lam_init)
    w_out = w_out.astype(BF16)
    return out_proj(x2, mod, g_post, [y_ssd, y_diff], [w_out[:d_inner], w_out[d_inner:]], seq)


def odd_layer_mixer(x2, mod, g_pre, g_post, w_in, sinks, w_out, swa_bias, batch, seq):
    ret_qk = RET_HEADS * RET_QK_DIM
    ret_v = RET_HEADS * RET_V_DIM
    swa_q = SWA_HEADS * SWA_HEAD_DIM
    swa_kv = SWA_KV_HEADS * SWA_HEAD_DIM
    wrq, wrk, wrv, wrg, wsq, wsk, wsv = _split_cols(w_in, (ret_qk, ret_qk, ret_v, ret_v, swa_q, swa_kv, swa_kv))
    wsq = wsq * (SWA_HEAD_DIM ** -0.5)
    weights = [w.astype(BF16) for w in (wrq, wrk, wrv, wrg, wsq, wsk, wsv)]
    rq, rk, rv, rg, sq, sk, sv = norm_proj(x2, mod, g_pre, weights, (BF16,) * 7, seq)
    y_ret = retention(rq, rk, rv, rg, batch, seq)
    y_swa = sliding_window_attention(sq, sk, sv, sinks, swa_bias, batch, seq)
    w_out = w_out.astype(BF16)
    return out_proj(x2, mod, g_post, [y_ret, y_swa], [w_out[:ret_v], w_out[ret_v:]], seq)


def kernel(x, c, rel_bias, norm_gains, mod_w, mod_b, mlp_w1, mlp_w2, e_w_in, e_conv_w, e_conv_b, e_dt_bias,
           e_A_log, e_D, e_ssd_norm, e_lambda, e_diff_norm, e_w_out, o_w_in, o_sinks, o_w_out):
    batch, seq, d = x.shape
    depth = norm_gains.shape[0]
    assert seq % 512 == 0 and seq % ATTN_BLOCK == 0 and d % LANES == 0
    mods = modulation(c, mod_w.reshape(depth * 2, d, 3 * d), mod_b.reshape(depth * 2, 3 * d))
    far_bucket = REL_BUCKETS - 1
    diff_bias = bias_tiles(rel_bias, _diff_bucket_idx(ATTN_BLOCK), shift_bucket=far_bucket)
    diff_bias = diff_bias.reshape(rel_bias.shape[1], 2, ATTN_BLOCK, 2 * ATTN_BLOCK)
    swa_bias = bias_tiles(rel_bias, _swa_bucket_idx())

    x2 = x.reshape(batch * seq, d)
    for layer in range(depth):
        j = layer // 2
        gains = norm_gains[layer]
        if layer % 2 == 0:
            x2 = even_layer_mixer(x2, mods[2 * layer], gains[0], gains[1], e_w_in[j], e_conv_w[j], e_conv_b[j],
                                  e_dt_bias[j], e_A_log[j], e_D[j], e_ssd_norm[j], e_lambda[j], e_diff_norm[j],
                                  e_w_out[j], diff_bias, layer, batch, seq)
        else:
            x2 = odd_layer_mixer(x2, mods[2 * layer], gains[0], gains[1], o_w_in[j], o_sinks[j], o_w_out[j],
                                 swa_bias, batch, seq)
        x2 = mlp(x2, mods[2 * layer + 1], gains[2], gains[3], mlp_w1[layer].astype(BF16),
                 mlp_w2[layer].astype(BF16), seq)
    return x2.reshape(batch, seq, d)
```

```python
import functools
import math

import jax
import jax.numpy as jnp
import numpy as np
from jax import lax
from jax.experimental import pallas as pl
from jax.experimental.pallas import tpu as pltpu

EPS = 1e-6
MASK_VALUE = -1e30
LANES = 128
SUBLANES = 8
VMEM_LIMIT = 56 * 1024 * 1024

CHUNK = 128
REL_BUCKETS = 32
REL_MAX_DIST = 128
SSD_HEAD_DIM = 64
SSD_GROUPS = 4
SSD_STATE = 128
SSD_CONV = 4
DIFF_HEAD_DIM = 64
RET_HEADS = 4
RET_QK_DIM = 128
RET_V_DIM = 256
ROPE_BASE = 10000.0
SWA_HEADS = 8
SWA_KV_HEADS = 2
SWA_HEAD_DIM = 64
SWA_BLOCK = 128
ATTN_BLOCK = 512

BF16 = jnp.bfloat16
F32 = jnp.float32
NT_DIMS = (((1,), (1,)), ((), ()))


def _params(semantics):
    return pltpu.CompilerParams(dimension_semantics=semantics, vmem_limit_bytes=VMEM_LIMIT)


def _full_spec(shape):
    return pl.BlockSpec(shape, lambda *_: (0,) * len(shape))


def _silu(x):
    return x * jax.nn.sigmoid(x)


def _rms(x, eps=EPS):
    return x * lax.rsqrt(jnp.mean(x * x, axis=-1, keepdims=True) + eps)


def _mod_kernel(c_ref, w_ref, b_ref, o_ref):
    c_act = _silu(c_ref[...])
    o_ref[...] = jnp.dot(c_act, w_ref[...], preferred_element_type=F32,
                         precision=lax.Precision.HIGHEST) + b_ref[...]


def modulation(c, mod_w, mod_b):
    b, d = c.shape
    n = mod_w.shape[0]
    rows = SUBLANES
    c_pad = jnp.zeros((rows, d), F32).at[:b].set(c)
    tn = d
    out = pl.pallas_call(
        _mod_kernel,
        grid=(n, 3 * d // tn),
        in_specs=[pl.BlockSpec((rows, d), lambda s, j: (0, 0)),
                  pl.BlockSpec((None, d, tn), lambda s, j: (s, 0, j)),
                  pl.BlockSpec((None, 1, tn), lambda s, j: (s, 0, j))],
        out_specs=pl.BlockSpec((None, rows, tn), lambda s, j: (s, 0, j)),
        out_shape=jax.ShapeDtypeStruct((n, rows, 3 * d), F32),
        compiler_params=_params(("parallel", "parallel")),
        name="modulation",
    )(c_pad, mod_w, mod_b.reshape(n, 1, 3 * d))
    return out[:, :b].reshape(n, b, 3, d)


def _t5_bucket_np(dist):
    max_exact = REL_BUCKETS // 2
    logd = np.log(np.maximum(dist, 1).astype(np.float32) / np.float32(max_exact))
    large = max_exact + (logd / np.float32(math.log(REL_MAX_DIST / max_exact))
                         * np.float32(REL_BUCKETS - max_exact)).astype(np.int32)
    large = np.minimum(large, REL_BUCKETS - 1)
    return np.where(dist < max_exact, dist, large).astype(np.int32)


def _bias_kernel(rb_ref, idx_ref, o_ref, *, shift_bucket):
    h = pl.program_id(0)
    idx = idx_ref[...]
    acc = jnp.full(idx.shape, MASK_VALUE, F32)
    shift = rb_ref[shift_bucket, h] if shift_bucket is not None else 0.0
    for bucket in range(REL_BUCKETS):
        acc = jnp.where(idx == bucket, rb_ref[bucket, h] - shift, acc)
    o_ref[...] = acc


def bias_tiles(rel_bias, bucket_idx, shift_bucket=None):
    n_heads = rel_bias.shape[1]
    r, c = bucket_idx.shape
    return pl.pallas_call(
        functools.partial(_bias_kernel, shift_bucket=shift_bucket),
        grid=(n_heads,),
        in_specs=[pl.BlockSpec(memory_space=pltpu.SMEM),
                  pl.BlockSpec((r, c), lambda h: (0, 0))],
        out_specs=pl.BlockSpec((None, r, c), lambda h: (h, 0, 0)),
        out_shape=jax.ShapeDtypeStruct((n_heads, r, c), F32),
        compiler_params=_params(("parallel",)),
        name="bias_tiles",
    )(rel_bias, jnp.asarray(bucket_idx))


def _diff_bucket_idx(t):
    q = np.arange(t)[:, None]
    k = np.arange(t)[None, :]
    diag = q - k
    idx_diag = np.where(diag >= 0, _t5_bucket_np(np.maximum(diag, 0)), -1)
    idx_prev = _t5_bucket_np(diag + t)
    idx_far = _t5_bucket_np(diag + 2 * t)
    assert t >= REL_MAX_DIST and (idx_far == REL_BUCKETS - 1).all()
    return np.concatenate([idx_diag, idx_prev, idx_far], axis=0).astype(np.int32)


def _swa_bucket_idx():
    q = np.arange(SWA_BLOCK)[:, None]
    k = np.arange(2 * SWA_BLOCK)[None, :] - SWA_BLOCK
    dist = q - k
    valid = (dist >= 0) & (dist < SWA_BLOCK)
    return np.where(valid, _t5_bucket_np(np.maximum(dist, 0)), -1).astype(np.int32)


def _norm_proj_kernel(x_ref, mod_ref, g_ref, *refs):
    n = len(refs) // 2
    w_refs, o_refs = refs[:n], refs[n:]
    mod = mod_ref[...]
    h = _rms(x_ref[...]) * g_ref[...] * (1.0 + mod[1:2]) + mod[0:1]
    hb = h.astype(BF16)
    for w_ref, o_ref in zip(w_refs, o_refs):
        o_ref[...] = jnp.dot(hb, w_ref[...], preferred_element_type=F32).astype(o_ref.dtype)


def norm_proj(x2, mod, gain, weights, out_dtypes, seq, tm=512):
    m, d = x2.shape
    per_batch = seq // tm
    in_specs = [pl.BlockSpec((tm, d), lambda i: (i, 0)),
                pl.BlockSpec((None, 3, d), lambda i: (i // per_batch, 0, 0)),
                _full_spec((1, d))]
    in_specs += [_full_spec(w.shape) for w in weights]
    out_specs = [pl.BlockSpec((tm, w.shape[1]), lambda i: (i, 0)) for w in weights]
    out_shape = [jax.ShapeDtypeStruct((m, w.shape[1]), dt) for w, dt in zip(weights, out_dtypes)]
    return pl.pallas_call(
        _norm_proj_kernel,
        grid=(m // tm,),
        in_specs=in_specs,
        out_specs=out_specs,
        out_shape=out_shape,
        compiler_params=_params(("parallel",)),
        name="norm_proj",
    )(x2, mod, gain.reshape(1, d), *weights)


def _out_proj_kernel(x_ref, mod_ref, g_ref, *refs):
    n = (len(refs) - 1) // 2
    y_refs, w_refs, o_ref = refs[:n], refs[n:2 * n], refs[2 * n]
    acc = None
    for y_ref, w_ref in zip(y_refs, w_refs):
        part = jnp.dot(y_ref[...], w_ref[...], preferred_element_type=F32)
        acc = part if acc is None else acc + part
    gate = mod_ref[...][2:3]
    o_ref[...] = x_ref[...] + gate * (_rms(acc) * g_ref[...])


def out_proj(x2, mod, gain, ys, weights, seq, tm=512):
    m, d = x2.shape
    per_batch = seq // tm
    in_specs = [pl.BlockSpec((tm, d), lambda i: (i, 0)),
                pl.BlockSpec((None, 3, d), lambda i: (i // per_batch, 0, 0)),
                _full_spec((1, d))]
    in_specs += [pl.BlockSpec((tm, y.shape[1]), lambda i: (i, 0)) for y in ys]
    in_specs += [_full_spec(w.shape) for w in weights]
    return pl.pallas_call(
        _out_proj_kernel,
        grid=(m // tm,),
        in_specs=in_specs,
        out_specs=pl.BlockSpec((tm, d), lambda i: (i, 0)),
        out_shape=jax.ShapeDtypeStruct((m, d), F32),
        compiler_params=_params(("parallel",)),
        name="out_proj",
    )(x2, mod, gain.reshape(1, d), *ys, *weights)


def _mlp_kernel(x_ref, mod_ref, g_pre_ref, g_post_ref, w1_ref, w2_ref, o_ref, *, ff_chunk):
    x = x_ref[...]
    mod = mod_ref[...]
    h = _rms(x) * g_pre_ref[...] * (1.0 + mod[1:2]) + mod[0:1]
    hb = h.astype(BF16)
    d_ff = w1_ref.shape[1]
    acc = None
    for c0 in range(0, d_ff, ff_chunk):
        a = jnp.dot(hb, w1_ref[:, c0:c0 + ff_chunk], preferred_element_type=F32)
        a = jnp.square(jnp.maximum(a, 0.0)).astype(BF16)
        part = jnp.dot(a, w2_ref[c0:c0 + ff_chunk, :], preferred_element_type=F32)
        acc = part if acc is None else acc + part
    o_ref[...] = x + mod[2:3] * (_rms(acc) * g_post_ref[...])


def mlp(x2, mod, g_pre, g_post, w1, w2, seq, tm=512, ff_chunk=1024):
    m, d = x2.shape
    per_batch = seq // tm
    return pl.pallas_call(
        functools.partial(_mlp_kernel, ff_chunk=ff_chunk),
        grid=(m // tm,),
        in_specs=[pl.BlockSpec((tm, d), lambda i: (i, 0)),
                  pl.BlockSpec((None, 3, d), lambda i: (i // per_batch, 0, 0)),
                  _full_spec((1, d)), _full_spec((1, d)),
                  _full_spec(w1.shape), _full_spec(w2.shape)],
        out_specs=pl.BlockSpec((tm, d), lambda i: (i, 0)),
        out_shape=jax.ShapeDtypeStruct((m, d), F32),
        compiler_params=_params(("parallel",)),
        name="mlp",
    )(x2, mod, g_pre.reshape(1, d), g_post.reshape(1, d), w1, w2)


def _ssd_kernel(z_ref, xbc_ref, dt_ref, cw_ref, cb_ref, dtb_ref, alog_ref, dskip_ref, nw_ref, expand_ref,
                o_ref, conv_buf, state_ref, *, d_inner):
    chunk = z_ref.shape[0]
    n_state = SSD_STATE
    gn = SSD_GROUPS * n_state
    pair = 2 * SSD_HEAD_DIM
    heads_per_group = d_inner // SSD_HEAD_DIM // SSD_GROUPS
    halo = SUBLANES

    @pl.when(pl.program_id(1) == 0)
    def _():
        state_ref[...] = jnp.zeros_like(state_ref)
        conv_buf[0:halo, :] = jnp.zeros((halo, conv_buf.shape[1]), F32)

    conv_buf[halo:halo + chunk, :] = xbc_ref[...].astype(F32)
    acc = cb_ref[...]
    for tap in range(SSD_CONV):
        off = halo - (SSD_CONV - 1) + tap
        acc = acc + cw_ref[tap:tap + 1, :] * conv_buf[off:off + chunk, :]
    conv_buf[0:halo, :] = conv_buf[chunk:chunk + halo, :]
    xbc = _silu(acc)
    xs = xbc[:, :d_inner]
    b_all = xbc[:, d_inner:d_inner + gn]
    c_all = xbc[:, d_inner + gn:]

    dt_in = dt_ref[...] + dtb_ref[...]
    dt = jnp.maximum(dt_in, 0.0) + jnp.log1p(jnp.exp(-jnp.abs(dt_in)))
    a = dt * (-jnp.exp(alog_ref[...]))
    row = lax.broadcasted_iota(jnp.int32, (chunk, chunk), 0)
    col = lax.broadcasted_iota(jnp.int32, (chunk, chunk), 1)
    causal = row >= col
    tril = causal.astype(F32)
    hi = lax.Precision.HIGHEST
    a_cs = jnp.dot(tril, a, preferred_element_type=F32, precision=hi)
    a_cs_t = a_cs.T
    expand = expand_ref[...]
    dt_e = jnp.dot(dt, expand, preferred_element_type=F32, precision=hi)
    acs_e = jnp.dot(a_cs, expand, preferred_element_type=F32, precision=hi)
    a_last_e = acs_e[chunk - 1:chunk, :]
    x_dt = xs * dt_e
    w_state = (x_dt * jnp.exp(a_last_e - acs_e)).astype(BF16)
    out_scale = jnp.exp(acs_e)
    chunk_decay = jnp.exp(a_last_e)
    x_dt_b = x_dt.astype(BF16)
    lane = lax.broadcasted_iota(jnp.int32, (chunk, pair), 1)
    first_head = lane < SSD_HEAD_DIM

    y_blocks = []
    for g in range(SSD_GROUPS):
        bg = b_all[:, g * n_state:(g + 1) * n_state]
        cg = c_all[:, g * n_state:(g + 1) * n_state].astype(BF16)
        cb = lax.dot_general(cg, bg.astype(BF16), NT_DIMS, preferred_element_type=F32)
        bg_t = bg.T.astype(BF16)
        for pi in range(heads_per_group // 2):
            p = g * (heads_per_group // 2) + pi
            sl = slice(p * pair, (p + 1) * pair)
            xp = x_dt_b[:, sl]
            y_diag = None
            for k in range(2):
                hh = 2 * p + k
                seg = a_cs[:, hh:hh + 1] - a_cs_t[hh:hh + 1, :]
                decay = jnp.exp(jnp.where(causal, seg, MASK_VALUE))
                m = (cb * decay).astype(BF16)
                xh = jnp.where(first_head if k == 0 else jnp.logical_not(first_head), xp, jnp.zeros_like(xp))
                part = jnp.dot(m, xh, preferred_element_type=F32)
                y_diag = part if y_diag is None else y_diag + part
            st = state_ref[p]
            y_off = jnp.dot(cg, st.astype(BF16), preferred_element_type=F32) * out_scale[:, sl]
            state_ref[p] = st * chunk_decay[:, sl] + jnp.dot(bg_t, w_state[:, sl], preferred_element_type=F32)
            y_blocks.append(y_diag + y_off)
    y = jnp.concatenate(y_blocks, axis=-1) + xs * dskip_ref[...]
    y = y * _silu(z_ref[...].astype(F32))
    o_ref[...] = (_rms(y) * nw_ref[...]).astype(o_ref.dtype)


def ssd_mixer(z, xbc, dt_raw, conv_w, conv_b, dt_bias, a_log, d_skip, norm_w, batch, seq):
    m, d_inner = z.shape
    conv_ch = xbc.shape[1]
    n_heads = d_inner // SSD_HEAD_DIM
    nc = seq // CHUNK
    pad = LANES - n_heads
    expand = np.zeros((LANES, d_inner), np.float32)
    for h in range(n_heads):
        expand[h, h * SSD_HEAD_DIM:(h + 1) * SSD_HEAD_DIM] = 1.0
    row_spec = lambda width: pl.BlockSpec((CHUNK, width), lambda b, c: (b * nc + c, 0))
    return pl.pallas_call(
        functools.partial(_ssd_kernel, d_inner=d_inner),
        grid=(batch, nc),
        in_specs=[row_spec(d_inner), row_spec(conv_ch), row_spec(LANES),
                  _full_spec((SSD_CONV, conv_ch)), _full_spec((1, conv_ch)),
                  _full_spec((1, LANES)), _full_spec((1, LANES)),
                  _full_spec((1, d_inner)), _full_spec((1, d_inner)), _full_spec((LANES, d_inner))],
        out_specs=row_spec(d_inner),
        out_shape=jax.ShapeDtypeStruct((m, d_inner), BF16),
        scratch_shapes=[pltpu.VMEM((CHUNK + SUBLANES, conv_ch), F32),
                        pltpu.VMEM((n_heads // 2, SSD_STATE, 2 * SSD_HEAD_DIM), F32)],
        compiler_params=_params(("parallel", "arbitrary")),
        name="ssd_mixer",
    )(z, xbc, dt_raw, conv_w, conv_b.reshape(1, conv_ch),
      jnp.pad(dt_bias, (0, pad)).reshape(1, LANES), jnp.pad(a_log, (0, pad)).reshape(1, LANES),
      jnp.repeat(d_skip, SSD_HEAD_DIM).reshape(1, d_inner), norm_w.reshape(1, d_inner), jnp.asarray(expand))


def _diff_attn_kernel(q_ref, k_ref, v_ref, bias_ref, lam_ref, nw_ref, o_ref,
                      vext_ref, s0_ref, s1_ref, p0_ref, p1_ref, m_ref, alpha0_ref, alpha1_ref, acc_ref,
                      *, lam_init, row_chunk):
    s_refs, p_refs, alpha_refs = (s0_ref, s1_ref), (p0_ref, p1_ref), (alpha0_ref, alpha1_ref)
    t = q_ref.shape[0]
    head_w = v_ref.shape[1]
    rows = 2 * t
    i = pl.program_id(2)
    n_blocks = i + 1
    q = q_ref[...]
    lane = lax.broadcasted_iota(jnp.int32, q.shape, 1)
    zero = jnp.zeros_like(q)
    qq = jnp.concatenate([jnp.where(lane < DIFF_HEAD_DIM, q, zero),
                          jnp.where(lane >= DIFF_HEAD_DIM, q, zero)], axis=0)

    @pl.when(i == 0)
    def _():
        vext_ref[:, :head_w] = v_ref[...]
        vext_ref[:, head_w:] = jnp.ones((vext_ref.shape[0], head_w), BF16)

    m_ref[...] = jnp.full(m_ref.shape, MASK_VALUE, F32)
    acc_ref[...] = jnp.zeros(acc_ref.shape, F32)

    def logits(step, slot):
        j = jnp.minimum(step, n_blocks - 1)
        kind = jnp.minimum(n_blocks - 1 - j, 2)
        kb = k_ref[pl.ds(pl.multiple_of(j * t, t), t), :]
        s = lax.dot_general(qq, kb, NT_DIMS, preferred_element_type=F32)
        bias = bias_ref[kind]
        s_refs[slot][...] = s + jnp.concatenate([bias, bias], axis=0)

    def softmax(slot):
        s_ref, p_ref, alpha_ref = s_refs[slot], p_refs[slot], alpha_refs[slot]
        for c in range(rows // row_chunk):
            r = slice(c * row_chunk, (c + 1) * row_chunk)
            m_old = m_ref[r, :]
            m_new = jnp.maximum(m_old, jnp.max(s_ref[r, :], axis=-1, keepdims=True))
            m_ref[r, :] = m_new
            alpha_ref[r, :] = jnp.exp(m_old - m_new)
        for c in range(rows // row_chunk):
            r = slice(c * row_chunk, (c + 1) * row_chunk)
            m_rep = jnp.tile(m_ref[r, :], (1, t // LANES))
            p_ref[r, :] = jnp.exp(s_ref[r, :] - m_rep).astype(BF16)

    def accumulate(step, slot):
        vb = vext_ref[pl.ds(pl.multiple_of(step * t, t), t), :]
        pv = jnp.dot(p_refs[slot][...], vb, preferred_element_type=F32)
        acc_ref[...] = jnp.tile(alpha_refs[slot][...], (1, 2)) * acc_ref[...] + pv

    def pipeline_step(step, slot):
        logits(step, slot)
        softmax(1 - slot)
        accumulate(step - 2, slot)

    logits(0, 0)
    logits(1, 1)
    softmax(0)

    def pair(k, _):
        pipeline_step(2 + 2 * k, 0)
        pipeline_step(3 + 2 * k, 1)
        return 0

    lax.fori_loop(0, n_blocks // 2, pair, 0)

    @pl.when(n_blocks % 2 == 1)
    def _():
        pipeline_step(n_blocks + 1, 0)

    lam = lam_ref[...]
    lam_full = (jnp.exp(jnp.sum(lam[0:1] * lam[1:2], axis=-1, keepdims=True))
                - jnp.exp(jnp.sum(lam[2:3] * lam[3:4], axis=-1, keepdims=True)) + lam_init)
    acc = acc_ref[...]
    out = acc[:, :head_w] / acc[:, head_w:]
    out = out[:t] - lam_full * out[t:]
    o_ref[...] = (_rms(out) * nw_ref[...] * (1.0 - lam_init)).astype(o_ref.dtype)


def diff_attention(q, k, v, bias, lam, subln_w, batch, seq, lam_init, row_chunk=64):
    m, width = q.shape
    head_w = 2 * DIFF_HEAD_DIM
    n_heads = width // head_w
    t = ATTN_BLOCK
    nq = seq // t
    rows = 2 * t
    return pl.pallas_call(
        functools.partial(_diff_attn_kernel, lam_init=lam_init, row_chunk=row_chunk),
        grid=(batch, n_heads, nq),
        in_specs=[pl.BlockSpec((t, head_w), lambda b, h, i: (b * nq + i, h)),
                  pl.BlockSpec((seq, head_w), lambda b, h, i: (b, h)),
                  pl.BlockSpec((seq, head_w), lambda b, h, i: (b, h)),
                  pl.BlockSpec((None, 3, t, t), lambda b, h, i: (h, 0, 0, 0)),
                  _full_spec(lam.shape), _full_spec((1, head_w))],
        out_specs=pl.BlockSpec((t, head_w), lambda b, h, i: (b * nq + i, h)),
        out_shape=jax.ShapeDtypeStruct((m, width), BF16),
        scratch_shapes=[pltpu.VMEM((seq, 2 * head_w), BF16),
                        pltpu.VMEM((rows, t), F32), pltpu.VMEM((rows, t), F32),
                        pltpu.VMEM((rows, t), BF16), pltpu.VMEM((rows, t), BF16),
                        pltpu.VMEM((rows, LANES), F32),
                        pltpu.VMEM((rows, LANES), F32), pltpu.VMEM((rows, LANES), F32),
                        pltpu.VMEM((rows, 2 * head_w), F32)],
        compiler_params=_params(("parallel", "parallel", "arbitrary")),
        name="diff_attention",
    )(q, k, v, bias, lam, subln_w.reshape(1, head_w))


def _retention_kernel(q_ref, k_ref, v_ref, g_ref, cos_ref, sin_ref, decay_ref, zeta_ref, xi_ref, o_ref,
                      state_ref, *, chunk_decay):
    @pl.when(pl.program_id(1) == 0)
    def _():
        state_ref[...] = jnp.zeros_like(state_ref)

    cos = cos_ref[...]
    sin = sin_ref[...]
    half = RET_QK_DIM // 2

    def rope(t):
        return t * cos + pltpu.roll(t, half, 1) * sin

    for h in range(RET_HEADS):
        qs = slice(h * RET_QK_DIM, (h + 1) * RET_QK_DIM)
        vs = slice(h * RET_V_DIM, (h + 1) * RET_V_DIM)
        qr = rope(q_ref[:, qs].astype(F32))
        kr = rope(k_ref[:, qs].astype(F32)) * (RET_QK_DIM ** -0.5)
        vh = v_ref[:, vs]
        scores = lax.dot_general(qr.astype(BF16), kr.astype(BF16), NT_DIMS,
                                 preferred_element_type=F32) * decay_ref[h]
        inner = jnp.dot(scores.astype(BF16), vh, preferred_element_type=F32)
        st = state_ref[h]
        cross = jnp.dot((qr * xi_ref[h]).astype(BF16), st.astype(BF16), preferred_element_type=F32)
        kv = jnp.dot((kr.T * zeta_ref[h]).astype(BF16), vh, preferred_element_type=F32)
        state_ref[h] = st * chunk_decay[h] + kv
        out = _rms(inner + cross) * _silu(g_ref[:, vs].astype(F32))
        o_ref[:, vs] = out.astype(o_ref.dtype)


def retention(rq, rk, rv, rg, batch, seq):
    m = rq.shape[0]
    nc = seq // CHUNK
    half = RET_QK_DIM // 2
    pos = jnp.arange(seq, dtype=F32)
    inv = ROPE_BASE ** (-jnp.arange(half, dtype=F32) / half)
    ang = pos[:, None] * inv[None]
    cos = jnp.concatenate([jnp.cos(ang), jnp.cos(ang)], axis=-1)
    sin = jnp.concatenate([-jnp.sin(ang), jnp.sin(ang)], axis=-1)
    log_gamma = jnp.log(1.0 - 2.0 ** (-5.0 - jnp.arange(RET_HEADS, dtype=F32)))
    idx = jnp.arange(CHUNK, dtype=F32)
    rel = idx[:, None] - idx[None, :]
    decay = jnp.where(rel >= 0, jnp.exp(jnp.maximum(rel, 0.0)[None] * log_gamma[:, None, None]), 0.0)
    zeta = jnp.exp((CHUNK - 1 - idx)[None] * log_gamma[:, None])[:, None, :]
    xi = jnp.broadcast_to(jnp.exp((idx + 1.0)[None] * log_gamma[:, None])[:, :, None],
                          (RET_HEADS, CHUNK, RET_QK_DIM))
    gamma = 1.0 - 2.0 ** (-5.0 - np.arange(RET_HEADS, dtype=np.float64))
    chunk_decay = tuple(float(g ** CHUNK) for g in gamma)
    row_spec = lambda width: pl.BlockSpec((CHUNK, width), lambda b, c: (b * nc + c, 0))
    return pl.pallas_call(
        functools.partial(_retention_kernel, chunk_decay=chunk_decay),
        grid=(batch, nc),
        in_specs=[row_spec(rq.shape[1]), row_spec(rk.shape[1]), row_spec(rv.shape[1]), row_spec(rg.shape[1]),
                  pl.BlockSpec((CHUNK, RET_QK_DIM), lambda b, c: (c, 0)),
                  pl.BlockSpec((CHUNK, RET_QK_DIM), lambda b, c: (c, 0)),
                  _full_spec(decay.shape), _full_spec(zeta.shape), _full_spec(xi.shape)],
        out_specs=row_spec(rv.shape[1]),
        out_shape=jax.ShapeDtypeStruct((m, rv.shape[1]), BF16),
        scratch_shapes=[pltpu.VMEM((RET_HEADS, RET_QK_DIM, RET_V_DIM), F32)],
        compiler_params=_params(("parallel", "arbitrary")),
        name="retention",
    )(rq, rk, rv, rg, cos, sin, decay, zeta, xi)


def _swa_kernel(sink_ref, q_ref, kp_ref, kc_ref, vp_ref, vc_ref, bias_ref, o_ref):
    i = pl.program_id(1)
    blk = q_ref.shape[0]
    rep = SWA_HEADS // SWA_KV_HEADS
    kb = jnp.concatenate([kp_ref[...], kc_ref[...]], axis=0)
    vb = jnp.concatenate([vp_ref[...], vc_ref[...]], axis=0)
    col = lax.broadcasted_iota(jnp.int32, (blk, 2 * blk), 1)
    no_prev = jnp.logical_and(col < blk, i == 0)
    lane = lax.broadcasted_iota(jnp.int32, (blk, LANES), 1)
    lower = lane < SWA_HEAD_DIM
    outs = []
    for p in range(SWA_HEADS // 2):
        qp = q_ref[:, p * LANES:(p + 1) * LANES].astype(F32)
        pair_out = []
        for parity in range(2):
            h = 2 * p + parity
            g = h // rep
            qh = qp if parity == g else pltpu.roll(qp, SWA_HEAD_DIM, 1)
            qh = jnp.where(lower if g == 0 else jnp.logical_not(lower), qh, 0.0).astype(BF16)
            s = lax.dot_general(qh, kb, NT_DIMS, preferred_element_type=F32) + bias_ref[h]
            s = jnp.where(no_prev, MASK_VALUE, s)
            sink = sink_ref[h]
            m = jnp.maximum(jnp.max(s, axis=-1, keepdims=True), sink)
            e = jnp.exp(s - m)
            denom = jnp.sum(e, axis=-1, keepdims=True) + jnp.exp(sink - m)
            o = jnp.dot(e.astype(BF16), vb, preferred_element_type=F32) / denom
            pair_out.append(o if parity == g else pltpu.roll(o, SWA_HEAD_DIM, 1))
        outs.append(jnp.where(lower, pair_out[0], pair_out[1]))
    o_ref[...] = jnp.concatenate(outs, axis=-1).astype(o_ref.dtype)


def sliding_window_attention(sq, sk, sv, sinks, bias, batch, seq):
    m, qw = sq.shape
    kw = sk.shape[1]
    nb = seq // SWA_BLOCK
    cur = lambda b, i: (b * nb + i, 0)
    prev = lambda b, i: (b * nb + jnp.maximum(i - 1, 0), 0)
    return pl.pallas_call(
        _swa_kernel,
        grid=(batch, nb),
        in_specs=[pl.BlockSpec(memory_space=pltpu.SMEM),
                  pl.BlockSpec((SWA_BLOCK, qw), cur),
                  pl.BlockSpec((SWA_BLOCK, kw), prev), pl.BlockSpec((SWA_BLOCK, kw), cur),
                  pl.BlockSpec((SWA_BLOCK, kw), prev), pl.BlockSpec((SWA_BLOCK, kw), cur),
                  _full_spec(bias.shape)],
        out_specs=pl.BlockSpec((SWA_BLOCK, qw), cur),
        out_shape=jax.ShapeDtypeStruct((m, qw), BF16),
        compiler_params=_params(("parallel", "parallel")),
        name="sliding_window_attention",
    )(sinks, sq, sk, sk, sv, sv, bias)


def _split_cols(w, sizes):
    offs = np.cumsum((0,) + tuple(sizes))
    return [w[:, offs[j]:offs[j + 1]] for j in range(len(sizes))]


def even_layer_mixer(x2, mod, g_pre, g_post, w_in, conv_w, conv_b, dt_bias, a_log, d_skip, ssd_norm, lam,
                     diff_norm, w_out, diff_bias, layer_idx, batch, seq):
    d = x2.shape[1]
    d_inner = d
    n_ssd_heads = d_inner // SSD_HEAD_DIM
    conv_ch = d_inner + 2 * SSD_GROUPS * SSD_STATE
    wz, wxbc, wdt, wq, wk, wv = _split_cols(w_in, (d_inner, conv_ch, n_ssd_heads, d, d, d))
    wdt = jnp.pad(wdt, ((0, 0), (0, LANES - n_ssd_heads)))
    wq = wq * (DIFF_HEAD_DIM ** -0.5)
    weights = [w.astype(BF16) for w in (wz, wxbc, wdt, wq, wk, wv)]
    z, xbc, dt_raw, q, k, v = norm_proj(x2, mod, g_pre, weights, (BF16, BF16, F32, BF16, BF16, BF16), seq)
    y_ssd = ssd_mixer(z, xbc, dt_raw, conv_w, conv_b, dt_bias, a_log, d_skip, ssd_norm, batch, seq)
    lam_init = 0.8 - 0.6 * math.exp(-0.3 * layer_idx)
    y_diff = diff_attention(q, k, v, diff_bias, lam, diff_norm, batch, seq, lam_init)
    w_out = w_out.astype(BF16)
    return out_proj(x2, mod, g_post, [y_ssd, y_diff], [w_out[:d_inner], w_out[d_inner:]], seq)


def odd_layer_mixer(x2, mod, g_pre, g_post, w_in, sinks, w_out, swa_bias, batch, seq):
    ret_qk = RET_HEADS * RET_QK_DIM
    ret_v = RET_HEADS * RET_V_DIM
    swa_q = SWA_HEADS * SWA_HEAD_DIM
    swa_kv = SWA_KV_HEADS * SWA_HEAD_DIM
    wrq, wrk, wrv, wrg, wsq, wsk, wsv = _split_cols(w_in, (ret_qk, ret_qk, ret_v, ret_v, swa_q, swa_kv, swa_kv))
    wsq = wsq * (SWA_HEAD_DIM ** -0.5)
    weights = [w.astype(BF16) for w in (wrq, wrk, wrv, wrg, wsq, wsk, wsv)]
    rq, rk, rv, rg, sq, sk, sv = norm_proj(x2, mod, g_pre, weights, (BF16,) * 7, seq)
    y_ret = retention(rq, rk, rv, rg, batch, seq)
    y_swa = sliding_window_attention(sq, sk, sv, sinks, swa_bias, batch, seq)
    w_out = w_out.astype(BF16)
    return out_proj(x2, mod, g_post, [y_ret, y_swa], [w_out[:ret_v], w_out[ret_v:]], seq)


def kernel(x, c, rel_bias, norm_gains, mod_w, mod_b, mlp_w1, mlp_w2, e_w_in, e_conv_w, e_conv_b, e_dt_bias,
           e_A_log, e_D, e_ssd_norm, e_lambda, e_diff_norm, e_w_out, o_w_in, o_sinks, o_w_out):
    batch, seq, d = x.shape
    depth = norm_gains.shape[0]
    assert seq % 512 == 0 and seq % ATTN_BLOCK == 0 and d % LANES == 0
    mods = modulation(c, mod_w.reshape(depth * 2, d, 3 * d), mod_b.reshape(depth * 2, 3 * d))
    far_bucket = REL_BUCKETS - 1
    diff_bias = bias_tiles(rel_bias, _diff_bucket_idx(ATTN_BLOCK), shift_bucket=far_bucket)
    diff_bias = diff_bias.reshape(rel_bias.shape[1], 3, ATTN_BLOCK, ATTN_BLOCK)
    swa_bias = bias_tiles(rel_bias, _swa_bucket_idx())

    x2 = x.reshape(batch * seq, d)
    for layer in range(depth):
        j = layer // 2
        gains = norm_gains[layer]
        if layer % 2 == 0:
            x2 = even_layer_mixer(x2, mods[2 * layer], gains[0], gains[1], e_w_in[j], e_conv_w[j], e_conv_b[j],
                                  e_dt_bias[j], e_A_log[j], e_D[j], e_ssd_norm[j], e_lambda[j], e_diff_norm[j],
                                  e_w_out[j], diff_bias, layer, batch, seq)
        else:
            x2 = odd_layer_mixer(x2, mods[2 * layer], gains[0], gains[1], o_w_in[j], o_sinks[j], o_w_out[j],
                                 swa_bias, batch, seq)
        x2 = mlp(x2, mods[2 * layer + 1], gains[2], gains[3], mlp_w1[layer].astype(BF16),
                 mlp_w2[layer].astype(BF16), seq)
    return x2.reshape(batch, seq, d)
```

```python
import functools
import math

import jax
import jax.numpy as jnp
import numpy as np
from jax import lax
from jax.experimental import pallas as pl
from jax.experimental.pallas import tpu as pltpu

EPS = 1e-6
MASK_VALUE = -1e30
LOG2_E = math.log2(math.e)
LANES = 128
SUBLANES = 8
VMEM_LIMIT = 56 * 1024 * 1024

CHUNK = 128
REL_BUCKETS = 32
REL_MAX_DIST = 128
SSD_HEAD_DIM = 64
SSD_GROUPS = 4
SSD_STATE = 128
SSD_CONV = 4
DIFF_HEAD_DIM = 64
RET_HEADS = 4
RET_QK_DIM = 128
RET_V_DIM = 256
ROPE_BASE = 10000.0
SWA_HEADS = 8
SWA_KV_HEADS = 2
SWA_HEAD_DIM = 64
SWA_BLOCK = 128
ATTN_BLOCK = 512

BF16 = jnp.bfloat16
F32 = jnp.float32
NT_DIMS = (((1,), (1,)), ((), ()))


def _params(semantics, flags=None):
    return pltpu.CompilerParams(dimension_semantics=semantics, vmem_limit_bytes=VMEM_LIMIT, flags=flags)


def _full_spec(shape):
    return pl.BlockSpec(shape, lambda *_: (0,) * len(shape))


def _silu(x):
    return x * jax.nn.sigmoid(x)


def _rms(x, eps=EPS):
    return x * lax.rsqrt(jnp.mean(x * x, axis=-1, keepdims=True) + eps)


def _mod_kernel(c_ref, w_ref, b_ref, o_ref):
    c_act = _silu(c_ref[...])
    o_ref[...] = jnp.dot(c_act, w_ref[...], preferred_element_type=F32,
                         precision=lax.Precision.HIGHEST) + b_ref[...]


def modulation(c, mod_w, mod_b):
    b, d = c.shape
    n = mod_w.shape[0]
    rows = SUBLANES
    c_pad = jnp.zeros((rows, d), F32).at[:b].set(c)
    tn = d
    out = pl.pallas_call(
        _mod_kernel,
        grid=(n, 3 * d // tn),
        in_specs=[pl.BlockSpec((rows, d), lambda s, j: (0, 0)),
                  pl.BlockSpec((None, d, tn), lambda s, j: (s, 0, j)),
                  pl.BlockSpec((None, 1, tn), lambda s, j: (s, 0, j))],
        out_specs=pl.BlockSpec((None, rows, tn), lambda s, j: (s, 0, j)),
        out_shape=jax.ShapeDtypeStruct((n, rows, 3 * d), F32),
        compiler_params=_params(("parallel", "parallel")),
        name="modulation",
    )(c_pad, mod_w, mod_b.reshape(n, 1, 3 * d))
    return out[:, :b].reshape(n, b, 3, d)


def _t5_bucket_np(dist):
    max_exact = REL_BUCKETS // 2
    logd = np.log(np.maximum(dist, 1).astype(np.float32) / np.float32(max_exact))
    large = max_exact + (logd / np.float32(math.log(REL_MAX_DIST / max_exact))
                         * np.float32(REL_BUCKETS - max_exact)).astype(np.int32)
    large = np.minimum(large, REL_BUCKETS - 1)
    return np.where(dist < max_exact, dist, large).astype(np.int32)


def _bias_kernel(rb_ref, idx_ref, o_ref, *, shift_bucket, scale):
    h = pl.program_id(0)
    idx = idx_ref[...]
    acc = jnp.full(idx.shape, MASK_VALUE, F32)
    shift = rb_ref[shift_bucket, h] if shift_bucket is not None else 0.0
    for bucket in range(REL_BUCKETS):
        acc = jnp.where(idx == bucket, (rb_ref[bucket, h] - shift) * scale, acc)
    o_ref[...] = acc


def bias_tiles(rel_bias, bucket_idx, shift_bucket=None, scale=1.0):
    n_heads = rel_bias.shape[1]
    r, c = bucket_idx.shape
    return pl.pallas_call(
        functools.partial(_bias_kernel, shift_bucket=shift_bucket, scale=scale),
        grid=(n_heads,),
        in_specs=[pl.BlockSpec(memory_space=pltpu.SMEM),
                  pl.BlockSpec((r, c), lambda h: (0, 0))],
        out_specs=pl.BlockSpec((None, r, c), lambda h: (h, 0, 0)),
        out_shape=jax.ShapeDtypeStruct((n_heads, r, c), F32),
        compiler_params=_params(("parallel",)),
        name="bias_tiles",
    )(rel_bias, jnp.asarray(bucket_idx))


def _diff_bucket_idx(t):
    q = np.arange(t)[None, :]
    k = np.arange(t)[:, None]
    diag = q - k
    idx_diag = np.where(diag >= 0, _t5_bucket_np(np.maximum(diag, 0)), -1)
    idx_prev = _t5_bucket_np(diag + t)
    assert (_t5_bucket_np(diag + 2 * t) == REL_BUCKETS - 1).all()
    return np.concatenate([idx_diag, idx_prev], axis=0).astype(np.int32)


def _swa_bucket_idx():
    q = np.arange(SWA_BLOCK)[:, None]
    k = np.arange(2 * SWA_BLOCK)[None, :] - SWA_BLOCK
    dist = q - k
    valid = (dist >= 0) & (dist < SWA_BLOCK)
    return np.where(valid, _t5_bucket_np(np.maximum(dist, 0)), -1).astype(np.int32)


def _norm_proj_kernel(x_ref, mod_ref, g_ref, *refs):
    n = len(refs) // 2
    w_refs, o_refs = refs[:n], refs[n:]
    mod = mod_ref[...]
    h = _rms(x_ref[...]) * g_ref[...] * (1.0 + mod[1:2]) + mod[0:1]
    hb = h.astype(BF16)
    for w_ref, o_ref in zip(w_refs, o_refs):
        o_ref[...] = jnp.dot(hb, w_ref[...], preferred_element_type=F32).astype(o_ref.dtype)


def norm_proj(x2, mod, gain, weights, out_dtypes, seq, tm=512):
    m, d = x2.shape
    per_batch = seq // tm
    in_specs = [pl.BlockSpec((tm, d), lambda i: (i, 0)),
                pl.BlockSpec((None, 3, d), lambda i: (i // per_batch, 0, 0)),
                _full_spec((1, d))]
    in_specs += [_full_spec(w.shape) for w in weights]
    out_specs = [pl.BlockSpec((tm, w.shape[1]), lambda i: (i, 0)) for w in weights]
    out_shape = [jax.ShapeDtypeStruct((m, w.shape[1]), dt) for w, dt in zip(weights, out_dtypes)]
    return pl.pallas_call(
        _norm_proj_kernel,
        grid=(m // tm,),
        in_specs=in_specs,
        out_specs=out_specs,
        out_shape=out_shape,
        compiler_params=_params(("parallel",)),
        name="norm_proj",
    )(x2, mod, gain.reshape(1, d), *weights)


def _out_proj_kernel(x_ref, mod_ref, g_ref, *refs):
    n = (len(refs) - 1) // 2
    y_refs, w_refs, o_ref = refs[:n], refs[n:2 * n], refs[2 * n]
    acc = None
    for y_ref, w_ref in zip(y_refs, w_refs):
        part = jnp.dot(y_ref[...], w_ref[...], preferred_element_type=F32)
        acc = part if acc is None else acc + part
    gate = mod_ref[...][2:3]
    o_ref[...] = x_ref[...] + gate * (_rms(acc) * g_ref[...])


def out_proj(x2, mod, gain, ys, weights, seq, tm=512):
    m, d = x2.shape
    per_batch = seq // tm
    in_specs = [pl.BlockSpec((tm, d), lambda i: (i, 0)),
                pl.BlockSpec((None, 3, d), lambda i: (i // per_batch, 0, 0)),
                _full_spec((1, d))]
    in_specs += [pl.BlockSpec((tm, y.shape[1]), lambda i: (i, 0)) for y in ys]
    in_specs += [_full_spec(w.shape) for w in weights]
    return pl.pallas_call(
        _out_proj_kernel,
        grid=(m // tm,),
        in_specs=in_specs,
        out_specs=pl.BlockSpec((tm, d), lambda i: (i, 0)),
        out_shape=jax.ShapeDtypeStruct((m, d), F32),
        compiler_params=_params(("parallel",)),
        name="out_proj",
    )(x2, mod, gain.reshape(1, d), *ys, *weights)


def _mlp_kernel(x_ref, mod_ref, g_pre_ref, g_post_ref, w1_ref, w2_ref, o_ref, *, ff_chunk):
    x = x_ref[...]
    mod = mod_ref[...]
    h = _rms(x) * g_pre_ref[...] * (1.0 + mod[1:2]) + mod[0:1]
    hb = h.astype(BF16)
    d_ff = w1_ref.shape[1]
    acc = None
    for c0 in range(0, d_ff, ff_chunk):
        a = jnp.dot(hb, w1_ref[:, c0:c0 + ff_chunk], preferred_element_type=F32)
        a = jnp.square(jnp.maximum(a, 0.0)).astype(BF16)
        part = jnp.dot(a, w2_ref[c0:c0 + ff_chunk, :], preferred_element_type=F32)
        acc = part if acc is None else acc + part
    o_ref[...] = x + mod[2:3] * (_rms(acc) * g_post_ref[...])


def mlp(x2, mod, g_pre, g_post, w1, w2, seq, tm=512, ff_chunk=1024):
    m, d = x2.shape
    per_batch = seq // tm
    return pl.pallas_call(
        functools.partial(_mlp_kernel, ff_chunk=ff_chunk),
        grid=(m // tm,),
        in_specs=[pl.BlockSpec((tm, d), lambda i: (i, 0)),
                  pl.BlockSpec((None, 3, d), lambda i: (i // per_batch, 0, 0)),
                  _full_spec((1, d)), _full_spec((1, d)),
                  _full_spec(w1.shape), _full_spec(w2.shape)],
        out_specs=pl.BlockSpec((tm, d), lambda i: (i, 0)),
        out_shape=jax.ShapeDtypeStruct((m, d), F32),
        compiler_params=_params(("parallel",)),
        name="mlp",
    )(x2, mod, g_pre.reshape(1, d), g_post.reshape(1, d), w1, w2)


def _ssd_kernel(z_ref, xbc_ref, dt_ref, cw_ref, cb_ref, dtb_ref, alog_ref, dskip_ref, nw_ref, expand_ref,
                o_ref, conv_buf, state_ref, *, d_inner):
    chunk = z_ref.shape[0]
    n_state = SSD_STATE
    gn = SSD_GROUPS * n_state
    pair = 2 * SSD_HEAD_DIM
    heads_per_group = d_inner // SSD_HEAD_DIM // SSD_GROUPS
    halo = SUBLANES

    @pl.when(pl.program_id(1) == 0)
    def _():
        state_ref[...] = jnp.zeros_like(state_ref)
        conv_buf[0:halo, :] = jnp.zeros((halo, conv_buf.shape[1]), F32)

    conv_buf[halo:halo + chunk, :] = xbc_ref[...].astype(F32)
    acc = cb_ref[...]
    for tap in range(SSD_CONV):
        off = halo - (SSD_CONV - 1) + tap
        acc = acc + cw_ref[tap:tap + 1, :] * conv_buf[off:off + chunk, :]
    conv_buf[0:halo, :] = conv_buf[chunk:chunk + halo, :]
    xbc = _silu(acc)
    xs = xbc[:, :d_inner]
    b_all = xbc[:, d_inner:d_inner + gn]
    c_all = xbc[:, d_inner + gn:]

    dt_in = dt_ref[...] + dtb_ref[...]
    dt = jnp.maximum(dt_in, 0.0) + jnp.log1p(jnp.exp(-jnp.abs(dt_in)))
    a = dt * (-jnp.exp(alog_ref[...]))
    row = lax.broadcasted_iota(jnp.int32, (chunk, chunk), 0)
    col = lax.broadcasted_iota(jnp.int32, (chunk, chunk), 1)
    causal = row >= col
    tril = causal.astype(F32)
    hi = lax.Precision.HIGHEST
    a_cs = jnp.dot(tril, a, preferred_element_type=F32, precision=hi)
    a_cs_t = a_cs.T
    expand = expand_ref[...]
    dt_e = jnp.dot(dt, expand, preferred_element_type=F32, precision=hi)
    acs_e = jnp.dot(a_cs, expand, preferred_element_type=F32, precision=hi)
    a_last_e = acs_e[chunk - 1:chunk, :]
    x_dt = xs * dt_e
    w_state = (x_dt * jnp.exp(a_last_e - acs_e)).astype(BF16)
    out_scale = jnp.exp(acs_e)
    chunk_decay = jnp.exp(a_last_e)
    x_dt_b = x_dt.astype(BF16)
    lane = lax.broadcasted_iota(jnp.int32, (chunk, pair), 1)
    first_head = lane < SSD_HEAD_DIM

    y_blocks = []
    for g in range(SSD_GROUPS):
        bg = b_all[:, g * n_state:(g + 1) * n_state]
        cg = c_all[:, g * n_state:(g + 1) * n_state].astype(BF16)
        cb = lax.dot_general(cg, bg.astype(BF16), NT_DIMS, preferred_element_type=F32)
        bg_t = bg.T.astype(BF16)
        for pi in range(heads_per_group // 2):
            p = g * (heads_per_group // 2) + pi
            sl = slice(p * pair, (p + 1) * pair)
            xp = x_dt_b[:, sl]
            y_diag = None
            for k in range(2):
                hh = 2 * p + k
                seg = a_cs[:, hh:hh + 1] - a_cs_t[hh:hh + 1, :]
                decay = jnp.exp(jnp.where(causal, seg, MASK_VALUE))
                m = (cb * decay).astype(BF16)
                xh = jnp.where(first_head if k == 0 else jnp.logical_not(first_head), xp, jnp.zeros_like(xp))
                part = jnp.dot(m, xh, preferred_element_type=F32)
                y_diag = part if y_diag is None else y_diag + part
            st = state_ref[p]
            y_off = jnp.dot(cg, st.astype(BF16), preferred_element_type=F32) * out_scale[:, sl]
            state_ref[p] = st * chunk_decay[:, sl] + jnp.dot(bg_t, w_state[:, sl], preferred_element_type=F32)
            y_blocks.append(y_diag + y_off)
    y = jnp.concatenate(y_blocks, axis=-1) + xs * dskip_ref[...]
    y = y * _silu(z_ref[...].astype(F32))
    o_ref[...] = (_rms(y) * nw_ref[...]).astype(o_ref.dtype)


def ssd_mixer(z, xbc, dt_raw, conv_w, conv_b, dt_bias, a_log, d_skip, norm_w, batch, seq):
    m, d_inner = z.shape
    conv_ch = xbc.shape[1]
    n_heads = d_inner // SSD_HEAD_DIM
    nc = seq // CHUNK
    pad = LANES - n_heads
    expand = np.zeros((LANES, d_inner), np.float32)
    for h in range(n_heads):
        expand[h, h * SSD_HEAD_DIM:(h + 1) * SSD_HEAD_DIM] = 1.0
    row_spec = lambda width: pl.BlockSpec((CHUNK, width), lambda b, c: (b * nc + c, 0))
    return pl.pallas_call(
        functools.partial(_ssd_kernel, d_inner=d_inner),
        grid=(batch, nc),
        in_specs=[row_spec(d_inner), row_spec(conv_ch), row_spec(LANES),
                  _full_spec((SSD_CONV, conv_ch)), _full_spec((1, conv_ch)),
                  _full_spec((1, LANES)), _full_spec((1, LANES)),
                  _full_spec((1, d_inner)), _full_spec((1, d_inner)), _full_spec((LANES, d_inner))],
        out_specs=row_spec(d_inner),
        out_shape=jax.ShapeDtypeStruct((m, d_inner), BF16),
        scratch_shapes=[pltpu.VMEM((CHUNK + SUBLANES, conv_ch), F32),
                        pltpu.VMEM((n_heads // 2, SSD_STATE, 2 * SSD_HEAD_DIM), F32)],
        compiler_params=_params(("parallel", "arbitrary")),
        name="ssd_mixer",
    )(z, xbc, dt_raw, conv_w, conv_b.reshape(1, conv_ch),
      jnp.pad(dt_bias, (0, pad)).reshape(1, LANES), jnp.pad(a_log, (0, pad)).reshape(1, LANES),
      jnp.repeat(d_skip, SSD_HEAD_DIM).reshape(1, d_inner), norm_w.reshape(1, d_inner), jnp.asarray(expand))


def _diff_attn_kernel(q_ref, k_ref, v_ref, bias_ref, lam_ref, nw_ref, o_ref,
                      vt_ref, s0_ref, s1_ref, p_ref, m_ref, acc_ref, *, lam_init):
    s_refs = (s0_ref, s1_ref)
    t = q_ref.shape[0]
    head_w = v_ref.shape[1]
    ext_rows = vt_ref.shape[1]
    i = pl.program_id(2)
    n_blocks = i + 1
    q = q_ref[...]
    lane = lax.broadcasted_iota(jnp.int32, q.shape, 1)
    zero = jnp.zeros_like(q)
    qq = jnp.concatenate([jnp.where(lane < DIFF_HEAD_DIM, q, zero),
                          jnp.where(lane >= DIFF_HEAD_DIM, q, zero)], axis=0)

    @pl.when(i == 0)
    def _():
        for j in range(vt_ref.shape[0]):
            vt_ref[j, :head_w, :] = v_ref[j * t:(j + 1) * t, :].astype(F32).T.astype(BF16)
            vt_ref[j, head_w:, :] = jnp.ones((ext_rows - head_w, t), BF16)

    m_ref[...] = jnp.full(m_ref.shape, MASK_VALUE, F32)
    acc_ref[...] = jnp.zeros(acc_ref.shape, F32)

    def block_of(visit):
        return jnp.where(visit == 0, n_blocks - 1, jnp.where(visit == 1, jnp.maximum(n_blocks - 2, 0), visit - 2))

    def raw_logits(block):
        kb = k_ref[pl.ds(pl.multiple_of(block * t, t), t), :]
        return lax.dot_general(kb, qq, NT_DIMS, preferred_element_type=F32)

    def near_logits(visit, slot):
        bias = bias_ref[visit]
        s_refs[slot][...] = raw_logits(block_of(visit)) + jnp.concatenate([bias, bias], axis=1)

    def far_logits(visit, slot):
        s_refs[slot][...] = raw_logits(jnp.clip(visit - 2, 0, jnp.maximum(n_blocks - 3, 0)))

    def softmax_accumulate(visit, slot):
        s_ref = s_refs[slot]
        chains = [s_ref[0:SUBLANES, :], s_ref[SUBLANES:2 * SUBLANES, :]]
        for n, r in enumerate(range(2 * SUBLANES, t, SUBLANES)):
            chains[n % 2] = jnp.maximum(chains[n % 2], s_ref[r:r + SUBLANES, :])
        col_max = jnp.maximum(chains[0], chains[1])
        for shift in (4, 2, 1):
            col_max = jnp.maximum(col_max, pltpu.roll(col_max, shift, 0))
        m_old = m_ref[...]
        m_new = jnp.maximum(m_old, col_max)
        m_ref[...] = m_new
        alpha = jnp.exp2(m_old - m_new)
        packed = 2 * SUBLANES
        m_tile = jnp.concatenate([m_new, m_new], axis=0)
        for r in range(0, t, packed):
            p_ref[r:r + packed, :] = jnp.exp2(s_ref[r:r + packed, :] - m_tile).astype(BF16)
        pv = jnp.dot(vt_ref[block_of(visit)], p_ref[...], preferred_element_type=F32)
        for r in range(0, ext_rows, SUBLANES):
            acc_ref[r:r + SUBLANES, :] = alpha * acc_ref[r:r + SUBLANES, :] + pv[r:r + SUBLANES, :]

    def pipeline_step(visit, slot):
        far_logits(visit, slot)
        softmax_accumulate(visit - 1, 1 - slot)

    near_logits(0, 0)
    near_logits(1, 1)
    softmax_accumulate(0, 0)

    def pair(k, _):
        pipeline_step(2 + 2 * k, 0)

        @pl.when(3 + 2 * k <= n_blocks)
        def _():
            pipeline_step(3 + 2 * k, 1)

        return 0

    lax.fori_loop(0, n_blocks // 2, pair, 0)

    lam = lam_ref[...]
    lam_full = (jnp.exp(jnp.sum(lam[0:1] * lam[1:2], axis=-1, keepdims=True))
                - jnp.exp(jnp.sum(lam[2:3] * lam[3:4], axis=-1, keepdims=True)) + lam_init)
    inv_sum = 1.0 / acc_ref[head_w:head_w + SUBLANES, :]
    out_t = acc_ref[:head_w, :] * jnp.tile(inv_sum, (head_w // SUBLANES, 1))
    out = (out_t[:, :t] - lam_full * out_t[:, t:]).T
    o_ref[...] = (_rms(out) * nw_ref[...] * (1.0 - lam_init)).astype(o_ref.dtype)


def diff_attention(q, k, v, bias, lam, subln_w, batch, seq, lam_init):
    m, width = q.shape
    head_w = 2 * DIFF_HEAD_DIM
    n_heads = width // head_w
    t = ATTN_BLOCK
    nq = seq // t
    cols = 2 * t
    ext_rows = head_w + 2 * SUBLANES
    return pl.pallas_call(
        functools.partial(_diff_attn_kernel, lam_init=lam_init),
        grid=(batch, n_heads, nq),
        in_specs=[pl.BlockSpec((t, head_w), lambda b, h, i: (b * nq + i, h)),
                  pl.BlockSpec((seq, head_w), lambda b, h, i: (b, h)),
                  pl.BlockSpec((seq, head_w), lambda b, h, i: (b, h)),
                  pl.BlockSpec((None, 2, t, t), lambda b, h, i: (h, 0, 0, 0)),
                  _full_spec(lam.shape), _full_spec((1, head_w))],
        out_specs=pl.BlockSpec((t, head_w), lambda b, h, i: (b * nq + i, h)),
        out_shape=jax.ShapeDtypeStruct((m, width), BF16),
        scratch_shapes=[pltpu.VMEM((nq, ext_rows, t), BF16),
                        pltpu.VMEM((t, cols), F32), pltpu.VMEM((t, cols), F32),
                        pltpu.VMEM((t, cols), BF16),
                        pltpu.VMEM((SUBLANES, cols), F32),
                        pltpu.VMEM((ext_rows, cols), F32)],
        compiler_params=_params(("parallel", "parallel", "arbitrary")),
        name="diff_attention",
    )(q, k, v, bias, lam, subln_w.reshape(1, head_w))


def _retention_kernel(q_ref, k_ref, v_ref, g_ref, cos_ref, sin_ref, decay_ref, zeta_ref, xi_ref, o_ref,
                      state_ref, *, chunk_decay):
    @pl.when(pl.program_id(1) == 0)
    def _():
        state_ref[...] = jnp.zeros_like(state_ref)

    cos = cos_ref[...]
    sin = sin_ref[...]
    half = RET_QK_DIM // 2

    def rope(t):
        return t * cos + pltpu.roll(t, half, 1) * sin

    for h in range(RET_HEADS):
        qs = slice(h * RET_QK_DIM, (h + 1) * RET_QK_DIM)
        vs = slice(h * RET_V_DIM, (h + 1) * RET_V_DIM)
        qr = rope(q_ref[:, qs].astype(F32))
        kr = rope(k_ref[:, qs].astype(F32)) * (RET_QK_DIM ** -0.5)
        vh = v_ref[:, vs]
        scores = lax.dot_general(qr.astype(BF16), kr.astype(BF16), NT_DIMS,
                                 preferred_element_type=F32) * decay_ref[h]
        inner = jnp.dot(scores.astype(BF16), vh, preferred_element_type=F32)
        st = state_ref[h]
        cross = jnp.dot((qr * xi_ref[h]).astype(BF16), st.astype(BF16), preferred_element_type=F32)
        kv = jnp.dot((kr.T * zeta_ref[h]).astype(BF16), vh, preferred_element_type=F32)
        state_ref[h] = st * chunk_decay[h] + kv
        out = _rms(inner + cross) * _silu(g_ref[:, vs].astype(F32))
        o_ref[:, vs] = out.astype(o_ref.dtype)


def retention(rq, rk, rv, rg, batch, seq):
    m = rq.shape[0]
    nc = seq // CHUNK
    half = RET_QK_DIM // 2
    pos = jnp.arange(seq, dtype=F32)
    inv = ROPE_BASE ** (-jnp.arange(half, dtype=F32) / half)
    ang = pos[:, None] * inv[None]
    cos = jnp.concatenate([jnp.cos(ang), jnp.cos(ang)], axis=-1)
    sin = jnp.concatenate([-jnp.sin(ang), jnp.sin(ang)], axis=-1)
    log_gamma = jnp.log(1.0 - 2.0 ** (-5.0 - jnp.arange(RET_HEADS, dtype=F32)))
    idx = jnp.arange(CHUNK, dtype=F32)
    rel = idx[:, None] - idx[None, :]
    decay = jnp.where(rel >= 0, jnp.exp(jnp.maximum(rel, 0.0)[None] * log_gamma[:, None, None]), 0.0)
    zeta = jnp.exp((CHUNK - 1 - idx)[None] * log_gamma[:, None])[:, None, :]
    xi = jnp.broadcast_to(jnp.exp((idx + 1.0)[None] * log_gamma[:, None])[:, :, None],
                          (RET_HEADS, CHUNK, RET_QK_DIM))
    gamma = 1.0 - 2.0 ** (-5.0 - np.arange(RET_HEADS, dtype=np.float64))
    chunk_decay = tuple(float(g ** CHUNK) for g in gamma)
    row_spec = lambda width: pl.BlockSpec((CHUNK, width), lambda b, c: (b * nc + c, 0))
    return pl.pallas_call(
        functools.partial(_retention_kernel, chunk_decay=chunk_decay),
        grid=(batch, nc),
        in_specs=[row_spec(rq.shape[1]), row_spec(rk.shape[1]), row_spec(rv.shape[1]), row_spec(rg.shape[1]),
                  pl.BlockSpec((CHUNK, RET_QK_DIM), lambda b, c: (c, 0)),
                  pl.BlockSpec((CHUNK, RET_QK_DIM), lambda b, c: (c, 0)),
                  _full_spec(decay.shape), _full_spec(zeta.shape), _full_spec(xi.shape)],
        out_specs=row_spec(rv.shape[1]),
        out_shape=jax.ShapeDtypeStruct((m, rv.shape[1]), BF16),
        scratch_shapes=[pltpu.VMEM((RET_HEADS, RET_QK_DIM, RET_V_DIM), F32)],
        compiler_params=_params(("parallel", "arbitrary")),
        name="retention",
    )(rq, rk, rv, rg, cos, sin, decay, zeta, xi)


def _swa_kernel(sink_ref, q_ref, kp_ref, kc_ref, vp_ref, vc_ref, bias_ref, o_ref):
    i = pl.program_id(1)
    blk = q_ref.shape[0]
    rep = SWA_HEADS // SWA_KV_HEADS
    kb = jnp.concatenate([kp_ref[...], kc_ref[...]], axis=0)
    vb = jnp.concatenate([vp_ref[...], vc_ref[...]], axis=0)
    col = lax.broadcasted_iota(jnp.int32, (blk, 2 * blk), 1)
    no_prev = jnp.logical_and(col < blk, i == 0)
    lane = lax.broadcasted_iota(jnp.int32, (blk, LANES), 1)
    lower = lane < SWA_HEAD_DIM
    outs = []
    for p in range(SWA_HEADS // 2):
        qp = q_ref[:, p * LANES:(p + 1) * LANES].astype(F32)
        pair_out = []
        for parity in range(2):
            h = 2 * p + parity
            g = h // rep
            qh = qp if parity == g else pltpu.roll(qp, SWA_HEAD_DIM, 1)
            qh = jnp.where(lower if g == 0 else jnp.logical_not(lower), qh, 0.0).astype(BF16)
            s = lax.dot_general(qh, kb, NT_DIMS, preferred_element_type=F32) + bias_ref[h]
            s = jnp.where(no_prev, MASK_VALUE, s)
            sink = sink_ref[h]
            m = jnp.maximum(jnp.max(s, axis=-1, keepdims=True), sink)
            e = jnp.exp(s - m)
            denom = jnp.sum(e, axis=-1, keepdims=True) + jnp.exp(sink - m)
            o = jnp.dot(e.astype(BF16), vb, preferred_element_type=F32) / denom
            pair_out.append(o if parity == g else pltpu.roll(o, SWA_HEAD_DIM, 1))
        outs.append(jnp.where(lower, pair_out[0], pair_out[1]))
    o_ref[...] = jnp.concatenate(outs, axis=-1).astype(o_ref.dtype)


def sliding_window_attention(sq, sk, sv, sinks, bias, batch, seq):
    m, qw = sq.shape
    kw = sk.shape[1]
    nb = seq // SWA_BLOCK
    cur = lambda b, i: (b * nb + i, 0)
    prev = lambda b, i: (b * nb + jnp.maximum(i - 1, 0), 0)
    return pl.pallas_call(
        _swa_kernel,
        grid=(batch, nb),
        in_specs=[pl.BlockSpec(memory_space=pltpu.SMEM),
                  pl.BlockSpec((SWA_BLOCK, qw), cur),
                  pl.BlockSpec((SWA_BLOCK, kw), prev), pl.BlockSpec((SWA_BLOCK, kw), cur),
                  pl.BlockSpec((SWA_BLOCK, kw), prev), pl.BlockSpec((SWA_BLOCK, kw), cur),
                  _full_spec(bias.shape)],
        out_specs=pl.BlockSpec((SWA_BLOCK, qw), cur),
        out_shape=jax.ShapeDtypeStruct((m, qw), BF16),
        compiler_params=_params(("parallel", "parallel")),
        name="sliding_window_attention",
    )(sinks, sq, sk, sk, sv, sv, bias)


def _split_cols(w, sizes):
    offs = np.cumsum((0,) + tuple(sizes))
    return [w[:, offs[j]:offs[j + 1]] for j in range(len(sizes))]


def even_layer_mixer(x2, mod, g_pre, g_post, w_in, conv_w, conv_b, dt_bias, a_log, d_skip, ssd_norm, lam,
                     diff_norm, w_out, diff_bias, layer_idx, batch, seq):
    d = x2.shape[1]
    d_inner = d
    n_ssd_heads = d_inner // SSD_HEAD_DIM
    conv_ch = d_inner + 2 * SSD_GROUPS * SSD_STATE
    wz, wxbc, wdt, wq, wk, wv = _split_cols(w_in, (d_inner, conv_ch, n_ssd_heads, d, d, d))
    wdt = jnp.pad(wdt, ((0, 0), (0, LANES - n_ssd_heads)))
    wq = wq * (DIFF_HEAD_DIM ** -0.5 * LOG2_E)
    weights = [w.astype(BF16) for w in (wz, wxbc, wdt, wq, wk, wv)]
    z, xbc, dt_raw, q, k, v = norm_proj(x2, mod, g_pre, weights, (BF16, BF16, F32, BF16, BF16, BF16), seq)
    y_ssd = ssd_mixer(z, xbc, dt_raw, conv_w, conv_b, dt_bias, a_log, d_skip, ssd_norm, batch, seq)
    lam_init = 0.8 - 0.6 * math.exp(-0.3 * layer_idx)
    y_diff = diff_attention(q, k, v, diff_bias, lam, diff_norm, batch, seq, lam_init)
    w_out = w_out.astype(BF16)
    return out_proj(x2, mod, g_post, [y_ssd, y_diff], [w_out[:d_inner], w_out[d_inner:]], seq)


def odd_layer_mixer(x2, mod, g_pre, g_post, w_in, sinks, w_out, swa_bias, batch, seq):
    ret_qk = RET_HEADS * RET_QK_DIM
    ret_v = RET_HEADS * RET_V_DIM
    swa_q = SWA_HEADS * SWA_HEAD_DIM
    swa_kv = SWA_KV_HEADS * SWA_HEAD_DIM
    wrq, wrk, wrv, wrg, wsq, wsk, wsv = _split_cols(w_in, (ret_qk, ret_qk, ret_v, ret_v, swa_q, swa_kv, swa_kv))
    wsq = wsq * (SWA_HEAD_DIM ** -0.5)
    weights = [w.astype(BF16) for w in (wrq, wrk, wrv, wrg, wsq, wsk, wsv)]
    rq, rk, rv, rg, sq, sk, sv = norm_proj(x2, mod, g_pre, weights, (BF16,) * 7, seq)
    y_ret = retention(rq, rk, rv, rg, batch, seq)
    y_swa = sliding_window_attention(sq, sk, sv, sinks, swa_bias, batch, seq)
    w_out = w_out.astype(BF16)
    return out_proj(x2, mod, g_post, [y_ret, y_swa], [w_out[:ret_v], w_out[ret_v:]], seq)


def kernel(x, c, rel_bias, norm_gains, mod_w, mod_b, mlp_w1, mlp_w2, e_w_in, e_conv_w, e_conv_b, e_dt_bias,
           e_A_log, e_D, e_ssd_norm, e_lambda, e_diff_norm, e_w_out, o_w_in, o_sinks, o_w_out):
    batch, seq, d = x.shape
    depth = norm_gains.shape[0]
    assert seq % 512 == 0 and seq % ATTN_BLOCK == 0 and d % LANES == 0
    mods = modulation(c, mod_w.reshape(depth * 2, d, 3 * d), mod_b.reshape(depth * 2, 3 * d))
    far_bucket = REL_BUCKETS - 1
    diff_bias = bias_tiles(rel_bias, _diff_bucket_idx(ATTN_BLOCK), shift_bucket=far_bucket, scale=LOG2_E)
    diff_bias = diff_bias.reshape(rel_bias.shape[1], 2, ATTN_BLOCK, ATTN_BLOCK)
    swa_bias = bias_tiles(rel_bias, _swa_bucket_idx())

    x2 = x.reshape(batch * seq, d)
    for layer in range(depth):
        j = layer // 2
        gains = norm_gains[layer]
        if layer % 2 == 0:
            x2 = even_layer_mixer(x2, mods[2 * layer], gains[0], gains[1], e_w_in[j], e_conv_w[j], e_conv_b[j],
                                  e_dt_bias[j], e_A_log[j], e_D[j], e_ssd_norm[j], e_lambda[j], e_diff_norm[j],
                                  e_w_out[j], diff_bias, layer, batch, seq)
        else:
            x2 = odd_layer_mixer(x2, mods[2 * layer], gains[0], gains[1], o_w_in[j], o_sinks[j], o_w_out[j],
                                 swa_bias, batch, seq)
        x2 = mlp(x2, mods[2 * layer + 1], gains[2], gains[3], mlp_w1[layer].astype(BF16),
                 mlp_w2[layer].astype(BF16), seq)
    return x2.reshape(batch, seq, d)
```

```python
import functools
import math

import jax
import jax.numpy as jnp
import numpy as np
from jax import lax
from jax.experimental import pallas as pl
from jax.experimental.pallas import tpu as pltpu

EPS = 1e-6
MASK_VALUE = -1e30
LOG2_E = math.log2(math.e)
LANES = 128
SUBLANES = 8
VMEM_LIMIT = 56 * 1024 * 1024

CHUNK = 128
REL_BUCKETS = 32
REL_MAX_DIST = 128
SSD_HEAD_DIM = 64
SSD_GROUPS = 4
SSD_STATE = 128
SSD_CONV = 4
DIFF_HEAD_DIM = 64
RET_HEADS = 4
RET_QK_DIM = 128
RET_V_DIM = 256
ROPE_BASE = 10000.0
SWA_HEADS = 8
SWA_KV_HEADS = 2
SWA_HEAD_DIM = 64
SWA_BLOCK = 128
ATTN_BLOCK = 512

BF16 = jnp.bfloat16
F32 = jnp.float32
NT_DIMS = (((1,), (1,)), ((), ()))


def _params(semantics, flags=None):
    return pltpu.CompilerParams(dimension_semantics=semantics, vmem_limit_bytes=VMEM_LIMIT, flags=flags)


def _full_spec(shape):
    return pl.BlockSpec(shape, lambda *_: (0,) * len(shape))


def _silu(x):
    return x * jax.nn.sigmoid(x)


def _rms(x, eps=EPS):
    return x * lax.rsqrt(jnp.mean(x * x, axis=-1, keepdims=True) + eps)


def _mod_kernel(c_ref, w_ref, b_ref, o_ref):
    c_act = _silu(c_ref[...])
    o_ref[...] = jnp.dot(c_act, w_ref[...], preferred_element_type=F32,
                         precision=lax.Precision.HIGHEST) + b_ref[...]


def modulation(c, mod_w, mod_b):
    b, d = c.shape
    n = mod_w.shape[0]
    rows = SUBLANES
    c_pad = jnp.zeros((rows, d), F32).at[:b].set(c)
    tn = d
    out = pl.pallas_call(
        _mod_kernel,
        grid=(n, 3 * d // tn),
        in_specs=[pl.BlockSpec((rows, d), lambda s, j: (0, 0)),
                  pl.BlockSpec((None, d, tn), lambda s, j: (s, 0, j)),
                  pl.BlockSpec((None, 1, tn), lambda s, j: (s, 0, j))],
        out_specs=pl.BlockSpec((None, rows, tn), lambda s, j: (s, 0, j)),
        out_shape=jax.ShapeDtypeStruct((n, rows, 3 * d), F32),
        compiler_params=_params(("parallel", "parallel")),
        name="modulation",
    )(c_pad, mod_w, mod_b.reshape(n, 1, 3 * d))
    return out[:, :b].reshape(n, b, 3, d)


def _t5_bucket_np(dist):
    max_exact = REL_BUCKETS // 2
    logd = np.log(np.maximum(dist, 1).astype(np.float32) / np.float32(max_exact))
    large = max_exact + (logd / np.float32(math.log(REL_MAX_DIST / max_exact))
                         * np.float32(REL_BUCKETS - max_exact)).astype(np.int32)
    large = np.minimum(large, REL_BUCKETS - 1)
    return np.where(dist < max_exact, dist, large).astype(np.int32)


def _bias_kernel(rb_ref, idx_ref, o_ref, *, shift_bucket, scale):
    h = pl.program_id(0)
    idx = idx_ref[...]
    acc = jnp.full(idx.shape, MASK_VALUE, F32)
    shift = rb_ref[shift_bucket, h] if shift_bucket is not None else 0.0
    for bucket in range(REL_BUCKETS):
        acc = jnp.where(idx == bucket, (rb_ref[bucket, h] - shift) * scale, acc)
    o_ref[...] = acc


def bias_tiles(rel_bias, bucket_idx, shift_bucket=None, scale=1.0):
    n_heads = rel_bias.shape[1]
    r, c = bucket_idx.shape
    return pl.pallas_call(
        functools.partial(_bias_kernel, shift_bucket=shift_bucket, scale=scale),
        grid=(n_heads,),
        in_specs=[pl.BlockSpec(memory_space=pltpu.SMEM),
                  pl.BlockSpec((r, c), lambda h: (0, 0))],
        out_specs=pl.BlockSpec((None, r, c), lambda h: (h, 0, 0)),
        out_shape=jax.ShapeDtypeStruct((n_heads, r, c), F32),
        compiler_params=_params(("parallel",)),
        name="bias_tiles",
    )(rel_bias, jnp.asarray(bucket_idx))


def _diff_bucket_idx(t):
    q = np.arange(t)[None, :]
    k = np.arange(t)[:, None]
    diag = q - k
    idx_diag = np.where(diag >= 0, _t5_bucket_np(np.maximum(diag, 0)), -1)
    idx_prev = _t5_bucket_np(diag + t)
    assert (_t5_bucket_np(diag + 2 * t) == REL_BUCKETS - 1).all()
    return np.concatenate([idx_diag, idx_prev], axis=0).astype(np.int32)


def _swa_bucket_idx():
    q = np.arange(SWA_BLOCK)[:, None]
    k = np.arange(2 * SWA_BLOCK)[None, :] - SWA_BLOCK
    dist = q - k
    valid = (dist >= 0) & (dist < SWA_BLOCK)
    return np.where(valid, _t5_bucket_np(np.maximum(dist, 0)), -1).astype(np.int32)


def _norm_proj_kernel(x_ref, mod_ref, g_ref, *refs):
    n = len(refs) // 2
    w_refs, o_refs = refs[:n], refs[n:]
    mod = mod_ref[...]
    h = _rms(x_ref[...]) * g_ref[...] * (1.0 + mod[1:2]) + mod[0:1]
    hb = h.astype(BF16)
    for w_ref, o_ref in zip(w_refs, o_refs):
        o_ref[...] = jnp.dot(hb, w_ref[...], preferred_element_type=F32).astype(o_ref.dtype)


def norm_proj(x2, mod, gain, weights, out_dtypes, seq, tm=512):
    m, d = x2.shape
    per_batch = seq // tm
    in_specs = [pl.BlockSpec((tm, d), lambda i: (i, 0)),
                pl.BlockSpec((None, 3, d), lambda i: (i // per_batch, 0, 0)),
                _full_spec((1, d))]
    in_specs += [_full_spec(w.shape) for w in weights]
    out_specs = [pl.BlockSpec((tm, w.shape[1]), lambda i: (i, 0)) for w in weights]
    out_shape = [jax.ShapeDtypeStruct((m, w.shape[1]), dt) for w, dt in zip(weights, out_dtypes)]
    return pl.pallas_call(
        _norm_proj_kernel,
        grid=(m // tm,),
        in_specs=in_specs,
        out_specs=out_specs,
        out_shape=out_shape,
        compiler_params=_params(("parallel",)),
        name="norm_proj",
    )(x2, mod, gain.reshape(1, d), *weights)


def _out_proj_kernel(x_ref, mod_ref, g_ref, *refs):
    n = (len(refs) - 1) // 2
    y_refs, w_refs, o_ref = refs[:n], refs[n:2 * n], refs[2 * n]
    acc = None
    for y_ref, w_ref in zip(y_refs, w_refs):
        part = jnp.dot(y_ref[...], w_ref[...], preferred_element_type=F32)
        acc = part if acc is None else acc + part
    gate = mod_ref[...][2:3]
    o_ref[...] = x_ref[...] + gate * (_rms(acc) * g_ref[...])


def out_proj(x2, mod, gain, ys, weights, seq, tm=512):
    m, d = x2.shape
    per_batch = seq // tm
    in_specs = [pl.BlockSpec((tm, d), lambda i: (i, 0)),
                pl.BlockSpec((None, 3, d), lambda i: (i // per_batch, 0, 0)),
                _full_spec((1, d))]
    in_specs += [pl.BlockSpec((tm, y.shape[1]), lambda i: (i, 0)) for y in ys]
    in_specs += [_full_spec(w.shape) for w in weights]
    return pl.pallas_call(
        _out_proj_kernel,
        grid=(m // tm,),
        in_specs=in_specs,
        out_specs=pl.BlockSpec((tm, d), lambda i: (i, 0)),
        out_shape=jax.ShapeDtypeStruct((m, d), F32),
        compiler_params=_params(("parallel",)),
        name="out_proj",
    )(x2, mod, gain.reshape(1, d), *ys, *weights)


def _mlp_kernel(x_ref, mod_ref, g_pre_ref, g_post_ref, w1_ref, w2_ref, o_ref, *, ff_chunk):
    x = x_ref[...]
    mod = mod_ref[...]
    h = _rms(x) * g_pre_ref[...] * (1.0 + mod[1:2]) + mod[0:1]
    hb = h.astype(BF16)
    d_ff = w1_ref.shape[1]
    acc = None
    for c0 in range(0, d_ff, ff_chunk):
        a = jnp.dot(hb, w1_ref[:, c0:c0 + ff_chunk], preferred_element_type=F32)
        a = jnp.square(jnp.maximum(a, 0.0)).astype(BF16)
        part = jnp.dot(a, w2_ref[c0:c0 + ff_chunk, :], preferred_element_type=F32)
        acc = part if acc is None else acc + part
    o_ref[...] = x + mod[2:3] * (_rms(acc) * g_post_ref[...])


def mlp(x2, mod, g_pre, g_post, w1, w2, seq, tm=512, ff_chunk=1024):
    m, d = x2.shape
    per_batch = seq // tm
    return pl.pallas_call(
        functools.partial(_mlp_kernel, ff_chunk=ff_chunk),
        grid=(m // tm,),
        in_specs=[pl.BlockSpec((tm, d), lambda i: (i, 0)),
                  pl.BlockSpec((None, 3, d), lambda i: (i // per_batch, 0, 0)),
                  _full_spec((1, d)), _full_spec((1, d)),
                  _full_spec(w1.shape), _full_spec(w2.shape)],
        out_specs=pl.BlockSpec((tm, d), lambda i: (i, 0)),
        out_shape=jax.ShapeDtypeStruct((m, d), F32),
        compiler_params=_params(("parallel",)),
        name="mlp",
    )(x2, mod, g_pre.reshape(1, d), g_post.reshape(1, d), w1, w2)


def _ssd_kernel(z_ref, xbc_ref, dt_ref, cw_ref, cb_ref, dtb_ref, alog_ref, dskip_ref, nw_ref, expand_ref,
                o_ref, conv_buf, state_ref, *, d_inner):
    chunk = z_ref.shape[0]
    n_state = SSD_STATE
    gn = SSD_GROUPS * n_state
    pair = 2 * SSD_HEAD_DIM
    heads_per_group = d_inner // SSD_HEAD_DIM // SSD_GROUPS
    halo = SUBLANES

    @pl.when(pl.program_id(1) == 0)
    def _():
        state_ref[...] = jnp.zeros_like(state_ref)
        conv_buf[0:halo, :] = jnp.zeros((halo, conv_buf.shape[1]), F32)

    conv_buf[halo:halo + chunk, :] = xbc_ref[...].astype(F32)
    acc = cb_ref[...]
    for tap in range(SSD_CONV):
        off = halo - (SSD_CONV - 1) + tap
        acc = acc + cw_ref[tap:tap + 1, :] * conv_buf[off:off + chunk, :]
    conv_buf[0:halo, :] = conv_buf[chunk:chunk + halo, :]
    xbc = _silu(acc)
    xs = xbc[:, :d_inner]
    b_all = xbc[:, d_inner:d_inner + gn]
    c_all = xbc[:, d_inner + gn:]

    dt_in = dt_ref[...] + dtb_ref[...]
    dt = jnp.maximum(dt_in, 0.0) + jnp.log1p(jnp.exp(-jnp.abs(dt_in)))
    a = dt * (-jnp.exp(alog_ref[...]))
    row = lax.broadcasted_iota(jnp.int32, (chunk, chunk), 0)
    col = lax.broadcasted_iota(jnp.int32, (chunk, chunk), 1)
    causal = row >= col
    tril = causal.astype(F32)
    hi = lax.Precision.HIGHEST
    a_cs = jnp.dot(tril, a, preferred_element_type=F32, precision=hi)
    a_cs_t = a_cs.T
    expand = expand_ref[...]
    dt_e = jnp.dot(dt, expand, preferred_element_type=F32, precision=hi)
    acs_e = jnp.dot(a_cs, expand, preferred_element_type=F32, precision=hi)
    a_last_e = acs_e[chunk - 1:chunk, :]
    x_dt = xs * dt_e
    w_state = (x_dt * jnp.exp(a_last_e - acs_e)).astype(BF16)
    out_scale = jnp.exp(acs_e)
    chunk_decay = jnp.exp(a_last_e)
    x_dt_b = x_dt.astype(BF16)
    lane = lax.broadcasted_iota(jnp.int32, (chunk, pair), 1)
    first_head = lane < SSD_HEAD_DIM

    y_blocks = []
    for g in range(SSD_GROUPS):
        bg = b_all[:, g * n_state:(g + 1) * n_state]
        cg = c_all[:, g * n_state:(g + 1) * n_state].astype(BF16)
        cb = lax.dot_general(cg, bg.astype(BF16), NT_DIMS, preferred_element_type=F32)
        bg_t = bg.T.astype(BF16)
        for pi in range(heads_per_group // 2):
            p = g * (heads_per_group // 2) + pi
            sl = slice(p * pair, (p + 1) * pair)
            xp = x_dt_b[:, sl]
            y_diag = None
            for k in range(2):
                hh = 2 * p + k
                seg = a_cs[:, hh:hh + 1] - a_cs_t[hh:hh + 1, :]
                decay = jnp.exp(jnp.where(causal, seg, MASK_VALUE))
                m = (cb * decay).astype(BF16)
                xh = jnp.where(first_head if k == 0 else jnp.logical_not(first_head), xp, jnp.zeros_like(xp))
                part = jnp.dot(m, xh, preferred_element_type=F32)
                y_diag = part if y_diag is None else y_diag + part
            st = state_ref[p]
            y_off = jnp.dot(cg, st.astype(BF16), preferred_element_type=F32) * out_scale[:, sl]
            state_ref[p] = st * chunk_decay[:, sl] + jnp.dot(bg_t, w_state[:, sl], preferred_element_type=F32)
            y_blocks.append(y_diag + y_off)
    y = jnp.concatenate(y_blocks, axis=-1) + xs * dskip_ref[...]
    y = y * _silu(z_ref[...].astype(F32))
    o_ref[...] = (_rms(y) * nw_ref[...]).astype(o_ref.dtype)


def ssd_mixer(z, xbc, dt_raw, conv_w, conv_b, dt_bias, a_log, d_skip, norm_w, batch, seq):
    m, d_inner = z.shape
    conv_ch = xbc.shape[1]
    n_heads = d_inner // SSD_HEAD_DIM
    nc = seq // CHUNK
    pad = LANES - n_heads
    expand = np.zeros((LANES, d_inner), np.float32)
    for h in range(n_heads):
        expand[h, h * SSD_HEAD_DIM:(h + 1) * SSD_HEAD_DIM] = 1.0
    row_spec = lambda width: pl.BlockSpec((CHUNK, width), lambda b, c: (b * nc + c, 0))
    return pl.pallas_call(
        functools.partial(_ssd_kernel, d_inner=d_inner),
        grid=(batch, nc),
        in_specs=[row_spec(d_inner), row_spec(conv_ch), row_spec(LANES),
                  _full_spec((SSD_CONV, conv_ch)), _full_spec((1, conv_ch)),
                  _full_spec((1, LANES)), _full_spec((1, LANES)),
                  _full_spec((1, d_inner)), _full_spec((1, d_inner)), _full_spec((LANES, d_inner))],
        out_specs=row_spec(d_inner),
        out_shape=jax.ShapeDtypeStruct((m, d_inner), BF16),
        scratch_shapes=[pltpu.VMEM((CHUNK + SUBLANES, conv_ch), F32),
                        pltpu.VMEM((n_heads // 2, SSD_STATE, 2 * SSD_HEAD_DIM), F32)],
        compiler_params=_params(("parallel", "arbitrary")),
        name="ssd_mixer",
    )(z, xbc, dt_raw, conv_w, conv_b.reshape(1, conv_ch),
      jnp.pad(dt_bias, (0, pad)).reshape(1, LANES), jnp.pad(a_log, (0, pad)).reshape(1, LANES),
      jnp.repeat(d_skip, SSD_HEAD_DIM).reshape(1, d_inner), norm_w.reshape(1, d_inner), jnp.asarray(expand))


def _diff_attn_kernel(q_ref, k_ref, v_ref, bias_ref, lam_ref, nw_ref, o_ref,
                      vt_ref, s0_ref, s1_ref, p_ref, m_ref, acc_ref, *, lam_init):
    s_refs = (s0_ref, s1_ref)
    t = q_ref.shape[0]
    head_w = v_ref.shape[1]
    ext_rows = vt_ref.shape[1]
    i = pl.program_id(2)
    n_blocks = i + 1
    q_t = q_ref[...].astype(F32).T.astype(BF16)
    dim = lax.broadcasted_iota(jnp.int32, q_t.shape, 0)
    zero = jnp.zeros_like(q_t)
    qq_t = jnp.concatenate([jnp.where(dim < DIFF_HEAD_DIM, q_t, zero),
                            jnp.where(dim >= DIFF_HEAD_DIM, q_t, zero)], axis=1)

    @pl.when(i == 0)
    def _():
        for j in range(vt_ref.shape[0]):
            vt_ref[j, :head_w, :] = v_ref[j * t:(j + 1) * t, :].astype(F32).T.astype(BF16)
            vt_ref[j, head_w:, :] = jnp.ones((ext_rows - head_w, t), BF16)

    m_ref[...] = jnp.full(m_ref.shape, MASK_VALUE, F32)
    acc_ref[...] = jnp.zeros(acc_ref.shape, F32)

    def block_of(visit):
        return jnp.where(visit == 0, n_blocks - 1, jnp.where(visit == 1, jnp.maximum(n_blocks - 2, 0), visit - 2))

    def raw_logits(block):
        kb = k_ref[pl.ds(pl.multiple_of(block * t, t), t), :]
        return jnp.dot(kb, qq_t, preferred_element_type=F32)

    def near_logits(visit, slot):
        bias = bias_ref[visit]
        s_refs[slot][...] = raw_logits(block_of(visit)) + jnp.concatenate([bias, bias], axis=1)

    def far_logits(visit, slot):
        s_refs[slot][...] = raw_logits(jnp.clip(visit - 2, 0, jnp.maximum(n_blocks - 3, 0)))

    def softmax_accumulate(visit, slot):
        s_ref = s_refs[slot]
        chains = [s_ref[0:SUBLANES, :], s_ref[SUBLANES:2 * SUBLANES, :]]
        for n, r in enumerate(range(2 * SUBLANES, t, SUBLANES)):
            chains[n % 2] = jnp.maximum(chains[n % 2], s_ref[r:r + SUBLANES, :])
        col_max = jnp.maximum(chains[0], chains[1])
        for shift in (4, 2, 1):
            col_max = jnp.maximum(col_max, pltpu.roll(col_max, shift, 0))
        m_old = m_ref[...]
        m_new = jnp.maximum(m_old, col_max)
        m_ref[...] = m_new
        alpha = jnp.exp2(m_old - m_new)
        packed = 2 * SUBLANES
        m_tile = jnp.concatenate([m_new, m_new], axis=0)
        for r in range(0, t, packed):
            p_ref[r:r + packed, :] = jnp.exp2(s_ref[r:r + packed, :] - m_tile).astype(BF16)
        pv = jnp.dot(vt_ref[block_of(visit)], p_ref[...], preferred_element_type=F32)
        for r in range(0, ext_rows, SUBLANES):
            acc_ref[r:r + SUBLANES, :] = alpha * acc_ref[r:r + SUBLANES, :] + pv[r:r + SUBLANES, :]

    def pipeline_step(visit, slot):
        far_logits(visit, slot)
        softmax_accumulate(visit - 1, 1 - slot)

    near_logits(0, 0)
    near_logits(1, 1)
    softmax_accumulate(0, 0)

    def pair(k, _):
        pipeline_step(2 + 2 * k, 0)

        @pl.when(3 + 2 * k <= n_blocks)
        def _():
            pipeline_step(3 + 2 * k, 1)

        return 0

    lax.fori_loop(0, n_blocks // 2, pair, 0)

    lam = lam_ref[...]
    lam_full = (jnp.exp(jnp.sum(lam[0:1] * lam[1:2], axis=-1, keepdims=True))
                - jnp.exp(jnp.sum(lam[2:3] * lam[3:4], axis=-1, keepdims=True)) + lam_init)
    inv_sum = 1.0 / acc_ref[head_w:head_w + SUBLANES, :]
    out_t = acc_ref[:head_w, :] * jnp.tile(inv_sum, (head_w // SUBLANES, 1))
    out = (out_t[:, :t] - lam_full * out_t[:, t:]).T
    o_ref[...] = (_rms(out) * nw_ref[...] * (1.0 - lam_init)).astype(o_ref.dtype)


def diff_attention(q, k, v, bias, lam, subln_w, batch, seq, lam_init):
    m, width = q.shape
    head_w = 2 * DIFF_HEAD_DIM
    n_heads = width // head_w
    t = ATTN_BLOCK
    nq = seq // t
    cols = 2 * t
    ext_rows = head_w + 2 * SUBLANES
    return pl.pallas_call(
        functools.partial(_diff_attn_kernel, lam_init=lam_init),
        grid=(batch, n_heads, nq),
        in_specs=[pl.BlockSpec((t, head_w), lambda b, h, i: (b * nq + i, h)),
                  pl.BlockSpec((seq, head_w), lambda b, h, i: (b, h)),
                  pl.BlockSpec((seq, head_w), lambda b, h, i: (b, h)),
                  pl.BlockSpec((None, 2, t, t), lambda b, h, i: (h, 0, 0, 0)),
                  _full_spec(lam.shape), _full_spec((1, head_w))],
        out_specs=pl.BlockSpec((t, head_w), lambda b, h, i: (b * nq + i, h)),
        out_shape=jax.ShapeDtypeStruct((m, width), BF16),
        scratch_shapes=[pltpu.VMEM((nq, ext_rows, t), BF16),
                        pltpu.VMEM((t, cols), F32), pltpu.VMEM((t, cols), F32),
                        pltpu.VMEM((t, cols), BF16),
                        pltpu.VMEM((SUBLANES, cols), F32),
                        pltpu.VMEM((ext_rows, cols), F32)],
        compiler_params=_params(("parallel", "parallel", "arbitrary")),
        name="diff_attention",
    )(q, k, v, bias, lam, subln_w.reshape(1, head_w))


def _retention_kernel(q_ref, k_ref, v_ref, g_ref, cos_ref, sin_ref, decay_ref, zeta_ref, xi_ref, o_ref,
                      state_ref, *, chunk_decay):
    @pl.when(pl.program_id(1) == 0)
    def _():
        state_ref[...] = jnp.zeros_like(state_ref)

    cos = cos_ref[...]
    sin = sin_ref[...]
    half = RET_QK_DIM // 2

    def rope(t):
        return t * cos + pltpu.roll(t, half, 1) * sin

    for h in range(RET_HEADS):
        qs = slice(h * RET_QK_DIM, (h + 1) * RET_QK_DIM)
        vs = slice(h * RET_V_DIM, (h + 1) * RET_V_DIM)
        qr = rope(q_ref[:, qs].astype(F32))
        kr = rope(k_ref[:, qs].astype(F32)) * (RET_QK_DIM ** -0.5)
        vh = v_ref[:, vs]
        scores = lax.dot_general(qr.astype(BF16), kr.astype(BF16), NT_DIMS,
                                 preferred_element_type=F32) * decay_ref[h]
        inner = jnp.dot(scores.astype(BF16), vh, preferred_element_type=F32)
        st = state_ref[h]
        cross = jnp.dot((qr * xi_ref[h]).astype(BF16), st.astype(BF16), preferred_element_type=F32)
        kv = jnp.dot((kr.T * zeta_ref[h]).astype(BF16), vh, preferred_element_type=F32)
        state_ref[h] = st * chunk_decay[h] + kv
        out = _rms(inner + cross) * _silu(g_ref[:, vs].astype(F32))
        o_ref[:, vs] = out.astype(o_ref.dtype)


def retention(rq, rk, rv, rg, batch, seq):
    m = rq.shape[0]
    nc = seq // CHUNK
    half = RET_QK_DIM // 2
    pos = jnp.arange(seq, dtype=F32)
    inv = ROPE_BASE ** (-jnp.arange(half, dtype=F32) / half)
    ang = pos[:, None] * inv[None]
    cos = jnp.concatenate([jnp.cos(ang), jnp.cos(ang)], axis=-1)
    sin = jnp.concatenate([-jnp.sin(ang), jnp.sin(ang)], axis=-1)
    log_gamma = jnp.log(1.0 - 2.0 ** (-5.0 - jnp.arange(RET_HEADS, dtype=F32)))
    idx = jnp.arange(CHUNK, dtype=F32)
    rel = idx[:, None] - idx[None, :]
    decay = jnp.where(rel >= 0, jnp.exp(jnp.maximum(rel, 0.0)[None] * log_gamma[:, None, None]), 0.0)
    zeta = jnp.exp((CHUNK - 1 - idx)[None] * log_gamma[:, None])[:, None, :]
    xi = jnp.broadcast_to(jnp.exp((idx + 1.0)[None] * log_gamma[:, None])[:, :, None],
                          (RET_HEADS, CHUNK, RET_QK_DIM))
    gamma = 1.0 - 2.0 ** (-5.0 - np.arange(RET_HEADS, dtype=np.float64))
    chunk_decay = tuple(float(g ** CHUNK) for g in gamma)
    row_spec = lambda width: pl.BlockSpec((CHUNK, width), lambda b, c: (b * nc + c, 0))
    return pl.pallas_call(
        functools.partial(_retention_kernel, chunk_decay=chunk_decay),
        grid=(batch, nc),
        in_specs=[row_spec(rq.shape[1]), row_spec(rk.shape[1]), row_spec(rv.shape[1]), row_spec(rg.shape[1]),
                  pl.BlockSpec((CHUNK, RET_QK_DIM), lambda b, c: (c, 0)),
                  pl.BlockSpec((CHUNK, RET_QK_DIM), lambda b, c: (c, 0)),
                  _full_spec(decay.shape), _full_spec(zeta.shape), _full_spec(xi.shape)],
        out_specs=row_spec(rv.shape[1]),
        out_shape=jax.ShapeDtypeStruct((m, rv.shape[1]), BF16),
        scratch_shapes=[pltpu.VMEM((RET_HEADS, RET_QK_DIM, RET_V_DIM), F32)],
        compiler_params=_params(("parallel", "arbitrary")),
        name="retention",
    )(rq, rk, rv, rg, cos, sin, decay, zeta, xi)


def _swa_kernel(sink_ref, q_ref, kp_ref, kc_ref, vp_ref, vc_ref, bias_ref, o_ref):
    i = pl.program_id(1)
    blk = q_ref.shape[0]
    rep = SWA_HEADS // SWA_KV_HEADS
    kb = jnp.concatenate([kp_ref[...], kc_ref[...]], axis=0)
    vb = jnp.concatenate([vp_ref[...], vc_ref[...]], axis=0)
    col = lax.broadcasted_iota(jnp.int32, (blk, 2 * blk), 1)
    no_prev = jnp.logical_and(col < blk, i == 0)
    lane = lax.broadcasted_iota(jnp.int32, (blk, LANES), 1)
    lower = lane < SWA_HEAD_DIM
    outs = []
    for p in range(SWA_HEADS // 2):
        qp = q_ref[:, p * LANES:(p + 1) * LANES].astype(F32)
        pair_out = []
        for parity in range(2):
            h = 2 * p + parity
            g = h // rep
            qh = qp if parity == g else pltpu.roll(qp, SWA_HEAD_DIM, 1)
            qh = jnp.where(lower if g == 0 else jnp.logical_not(lower), qh, 0.0).astype(BF16)
            s = lax.dot_general(qh, kb, NT_DIMS, preferred_element_type=F32) + bias_ref[h]
            s = jnp.where(no_prev, MASK_VALUE, s)
            sink = sink_ref[h]
            m = jnp.maximum(jnp.max(s, axis=-1, keepdims=True), sink)
            e = jnp.exp(s - m)
            denom = jnp.sum(e, axis=-1, keepdims=True) + jnp.exp(sink - m)
            o = jnp.dot(e.astype(BF16), vb, preferred_element_type=F32) / denom
            pair_out.append(o if parity == g else pltpu.roll(o, SWA_HEAD_DIM, 1))
        outs.append(jnp.where(lower, pair_out[0], pair_out[1]))
    o_ref[...] = jnp.concatenate(outs, axis=-1).astype(o_ref.dtype)


def sliding_window_attention(sq, sk, sv, sinks, bias, batch, seq):
    m, qw = sq.shape
    kw = sk.shape[1]
    nb = seq // SWA_BLOCK
    cur = lambda b, i: (b * nb + i, 0)
    prev = lambda b, i: (b * nb + jnp.maximum(i - 1, 0), 0)
    return pl.pallas_call(
        _swa_kernel,
        grid=(batch, nb),
        in_specs=[pl.BlockSpec(memory_space=pltpu.SMEM),
                  pl.BlockSpec((SWA_BLOCK, qw), cur),
                  pl.BlockSpec((SWA_BLOCK, kw), prev), pl.BlockSpec((SWA_BLOCK, kw), cur),
                  pl.BlockSpec((SWA_BLOCK, kw), prev), pl.BlockSpec((SWA_BLOCK, kw), cur),
                  _full_spec(bias.shape)],
        out_specs=pl.BlockSpec((SWA_BLOCK, qw), cur),
        out_shape=jax.ShapeDtypeStruct((m, qw), BF16),
        compiler_params=_params(("parallel", "parallel")),
        name="sliding_window_attention",
    )(sinks, sq, sk, sk, sv, sv, bias)


def _split_cols(w, sizes):
    offs = np.cumsum((0,) + tuple(sizes))
    return [w[:, offs[j]:offs[j + 1]] for j in range(len(sizes))]


def even_layer_mixer(x2, mod, g_pre, g_post, w_in, conv_w, conv_b, dt_bias, a_log, d_skip, ssd_norm, lam,
                     diff_norm, w_out, diff_bias, layer_idx, batch, seq):
    d = x2.shape[1]
    d_inner = d
    n_ssd_heads = d_inner // SSD_HEAD_DIM
    conv_ch = d_inner + 2 * SSD_GROUPS * SSD_STATE
    wz, wxbc, wdt, wq, wk, wv = _split_cols(w_in, (d_inner, conv_ch, n_ssd_heads, d, d, d))
    wdt = jnp.pad(wdt, ((0, 0), (0, LANES - n_ssd_heads)))
    wq = wq * (DIFF_HEAD_DIM ** -0.5 * LOG2_E)
    weights = [w.astype(BF16) for w in (wz, wxbc, wdt, wq, wk, wv)]
    z, xbc, dt_raw, q, k, v = norm_proj(x2, mod, g_pre, weights, (BF16, BF16, F32, BF16, BF16, BF16), seq)
    y_ssd = ssd_mixer(z, xbc, dt_raw, conv_w, conv_b, dt_bias, a_log, d_skip, ssd_norm, batch, seq)
    lam_init = 0.8 - 0.6 * math.exp(-0.3 * layer_idx)
    y_diff = diff_attention(q, k, v, diff_bias, lam, diff_norm, batch, seq, lam_init)
    w_out = w_out.astype(BF16)
    return out_proj(x2, mod, g_post, [y_ssd, y_diff], [w_out[:d_inner], w_out[d_inner:]], seq)


def odd_layer_mixer(x2, mod, g_pre, g_post, w_in, sinks, w_out, swa_bias, batch, seq):
    ret_qk = RET_HEADS * RET_QK_DIM
    ret_v = RET_HEADS * RET_V_DIM
    swa_q = SWA_HEADS * SWA_HEAD_DIM
    swa_kv = SWA_KV_HEADS * SWA_HEAD_DIM
    wrq, wrk, wrv, wrg, wsq, wsk, wsv = _split_cols(w_in, (ret_qk, ret_qk, ret_v, ret_v, swa_q, swa_kv, swa_kv))
    wsq = wsq * (SWA_HEAD_DIM ** -0.5)
    weights = [w.astype(BF16) for w in (wrq, wrk, wrv, wrg, wsq, wsk, wsv)]
    rq, rk, rv, rg, sq, sk, sv = norm_proj(x2, mod, g_pre, weights, (BF16,) * 7, seq)
    y_ret = retention(rq, rk, rv, rg, batch, seq)
    y_swa = sliding_window_attention(sq, sk, sv, sinks, swa_bias, batch, seq)
    w_out = w_out.astype(BF16)
    return out_proj(x2, mod, g_post, [y_ret, y_swa], [w_out[:ret_v], w_out[ret_v:]], seq)


def kernel(x, c, rel_bias, norm_gains, mod_w, mod_b, mlp_w1, mlp_w2, e_w_in, e_conv_w, e_conv_b, e_dt_bias,
           e_A_log, e_D, e_ssd_norm, e_lambda, e_diff_norm, e_w_out, o_w_in, o_sinks, o_w_out):
    batch, seq, d = x.shape
    depth = norm_gains.shape[0]
    assert seq % 512 == 0 and seq % ATTN_BLOCK == 0 and d % LANES == 0
    mods = modulation(c, mod_w.reshape(depth * 2, d, 3 * d), mod_b.reshape(depth * 2, 3 * d))
    far_bucket = REL_BUCKETS - 1
    diff_bias = bias_tiles(rel_bias, _diff_bucket_idx(ATTN_BLOCK), shift_bucket=far_bucket, scale=LOG2_E)
    diff_bias = diff_bias.reshape(rel_bias.shape[1], 2, ATTN_BLOCK, ATTN_BLOCK)
    swa_bias = bias_tiles(rel_bias, _swa_bucket_idx())

    x2 = x.reshape(batch * seq, d)
    for layer in range(depth):
        j = layer // 2
        gains = norm_gains[layer]
        if layer % 2 == 0:
            x2 = even_layer_mixer(x2, mods[2 * layer], gains[0], gains[1], e_w_in[j], e_conv_w[j], e_conv_b[j],
                                  e_dt_bias[j], e_A_log[j], e_D[j], e_ssd_norm[j], e_lambda[j], e_diff_norm[j],
                                  e_w_out[j], diff_bias, layer, batch, seq)
        else:
            x2 = odd_layer_mixer(x2, mods[2 * layer], gains[0], gains[1], o_w_in[j], o_sinks[j], o_w_out[j],
                                 swa_bias, batch, seq)
        x2 = mlp(x2, mods[2 * layer + 1], gains[2], gains[3], mlp_w1[layer].astype(BF16),
                 mlp_w2[layer].astype(BF16), seq)
    return x2.reshape(batch, seq, d)
```

```python
import functools
import math

import jax
import jax.numpy as jnp
import numpy as np
from jax import lax
from jax.experimental import pallas as pl
from jax.experimental.pallas import tpu as pltpu

EPS = 1e-6
MASK_VALUE = -1e30
LOG2_E = math.log2(math.e)
LANES = 128
SUBLANES = 8
VMEM_LIMIT = 56 * 1024 * 1024

CHUNK = 128
REL_BUCKETS = 32
REL_MAX_DIST = 128
SSD_HEAD_DIM = 64
SSD_GROUPS = 4
SSD_STATE = 128
SSD_CONV = 4
DIFF_HEAD_DIM = 64
RET_HEADS = 4
RET_QK_DIM = 128
RET_V_DIM = 256
ROPE_BASE = 10000.0
SWA_HEADS = 8
SWA_KV_HEADS = 2
SWA_HEAD_DIM = 64
SWA_BLOCK = 128
ATTN_BLOCK = 512

BF16 = jnp.bfloat16
F32 = jnp.float32
NT_DIMS = (((1,), (1,)), ((), ()))


def _params(semantics, flags=None):
    return pltpu.CompilerParams(dimension_semantics=semantics, vmem_limit_bytes=VMEM_LIMIT, flags=flags)


def _full_spec(shape):
    return pl.BlockSpec(shape, lambda *_: (0,) * len(shape))


def _silu(x):
    return x * jax.nn.sigmoid(x)


def _rms(x, eps=EPS):
    return x * lax.rsqrt(jnp.mean(x * x, axis=-1, keepdims=True) + eps)


def _mod_kernel(c_ref, w_ref, b_ref, o_ref):
    c_act = _silu(c_ref[...])
    o_ref[...] = jnp.dot(c_act, w_ref[...], preferred_element_type=F32,
                         precision=lax.Precision.HIGHEST) + b_ref[...]


def modulation(c, mod_w, mod_b):
    b, d = c.shape
    n = mod_w.shape[0]
    rows = SUBLANES
    c_pad = jnp.zeros((rows, d), F32).at[:b].set(c)
    tn = d
    out = pl.pallas_call(
        _mod_kernel,
        grid=(n, 3 * d // tn),
        in_specs=[pl.BlockSpec((rows, d), lambda s, j: (0, 0)),
                  pl.BlockSpec((None, d, tn), lambda s, j: (s, 0, j)),
                  pl.BlockSpec((None, 1, tn), lambda s, j: (s, 0, j))],
        out_specs=pl.BlockSpec((None, rows, tn), lambda s, j: (s, 0, j)),
        out_shape=jax.ShapeDtypeStruct((n, rows, 3 * d), F32),
        compiler_params=_params(("parallel", "parallel")),
        name="modulation",
    )(c_pad, mod_w, mod_b.reshape(n, 1, 3 * d))
    return out[:, :b].reshape(n, b, 3, d)


def _t5_bucket_np(dist):
    max_exact = REL_BUCKETS // 2
    logd = np.log(np.maximum(dist, 1).astype(np.float32) / np.float32(max_exact))
    large = max_exact + (logd / np.float32(math.log(REL_MAX_DIST / max_exact))
                         * np.float32(REL_BUCKETS - max_exact)).astype(np.int32)
    large = np.minimum(large, REL_BUCKETS - 1)
    return np.where(dist < max_exact, dist, large).astype(np.int32)


def _bias_kernel(rb_ref, idx_ref, o_ref, *, shift_bucket, scale):
    h = pl.program_id(0)
    idx = idx_ref[...]
    acc = jnp.full(idx.shape, MASK_VALUE, F32)
    shift = rb_ref[shift_bucket, h] if shift_bucket is not None else 0.0
    for bucket in range(REL_BUCKETS):
        acc = jnp.where(idx == bucket, (rb_ref[bucket, h] - shift) * scale, acc)
    o_ref[...] = acc


def bias_tiles(rel_bias, bucket_idx, shift_bucket=None, scale=1.0):
    n_heads = rel_bias.shape[1]
    r, c = bucket_idx.shape
    return pl.pallas_call(
        functools.partial(_bias_kernel, shift_bucket=shift_bucket, scale=scale),
        grid=(n_heads,),
        in_specs=[pl.BlockSpec(memory_space=pltpu.SMEM),
                  pl.BlockSpec((r, c), lambda h: (0, 0))],
        out_specs=pl.BlockSpec((None, r, c), lambda h: (h, 0, 0)),
        out_shape=jax.ShapeDtypeStruct((n_heads, r, c), F32),
        compiler_params=_params(("parallel",)),
        name="bias_tiles",
    )(rel_bias, jnp.asarray(bucket_idx))


def _diff_bucket_idx(t):
    q = np.arange(t)[None, :]
    k = np.arange(t)[:, None]
    diag = q - k
    idx_diag = np.where(diag >= 0, _t5_bucket_np(np.maximum(diag, 0)), -1)
    idx_prev = _t5_bucket_np(diag + t)
    assert (_t5_bucket_np(diag + 2 * t) == REL_BUCKETS - 1).all()
    return np.concatenate([idx_diag, idx_prev], axis=0).astype(np.int32)


def _swa_bucket_idx():
    q = np.arange(SWA_BLOCK)[:, None]
    k = np.arange(2 * SWA_BLOCK)[None, :] - SWA_BLOCK
    dist = q - k
    valid = (dist >= 0) & (dist < SWA_BLOCK)
    return np.where(valid, _t5_bucket_np(np.maximum(dist, 0)), -1).astype(np.int32)


def _norm_proj_kernel(x_ref, mod_ref, g_ref, *refs):
    n = len(refs) // 2
    w_refs, o_refs = refs[:n], refs[n:]
    mod = mod_ref[...]
    h = _rms(x_ref[...]) * g_ref[...] * (1.0 + mod[1:2]) + mod[0:1]
    hb = h.astype(BF16)
    for w_ref, o_ref in zip(w_refs, o_refs):
        o_ref[...] = jnp.dot(hb, w_ref[...], preferred_element_type=F32).astype(o_ref.dtype)


def norm_proj(x2, mod, gain, weights, out_dtypes, seq, tm=512):
    m, d = x2.shape
    per_batch = seq // tm
    in_specs = [pl.BlockSpec((tm, d), lambda i: (i, 0)),
                pl.BlockSpec((None, 3, d), lambda i: (i // per_batch, 0, 0)),
                _full_spec((1, d))]
    in_specs += [_full_spec(w.shape) for w in weights]
    out_specs = [pl.BlockSpec((tm, w.shape[1]), lambda i: (i, 0)) for w in weights]
    out_shape = [jax.ShapeDtypeStruct((m, w.shape[1]), dt) for w, dt in zip(weights, out_dtypes)]
    return pl.pallas_call(
        _norm_proj_kernel,
        grid=(m // tm,),
        in_specs=in_specs,
        out_specs=out_specs,
        out_shape=out_shape,
        compiler_params=_params(("parallel",)),
        name="norm_proj",
    )(x2, mod, gain.reshape(1, d), *weights)


def _out_proj_kernel(x_ref, mod_ref, g_ref, *refs):
    n = (len(refs) - 1) // 2
    y_refs, w_refs, o_ref = refs[:n], refs[n:2 * n], refs[2 * n]
    acc = None
    for y_ref, w_ref in zip(y_refs, w_refs):
        part = jnp.dot(y_ref[...], w_ref[...], preferred_element_type=F32)
        acc = part if acc is None else acc + part
    gate = mod_ref[...][2:3]
    o_ref[...] = x_ref[...] + gate * (_rms(acc) * g_ref[...])


def out_proj(x2, mod, gain, ys, weights, seq, tm=512):
    m, d = x2.shape
    per_batch = seq // tm
    in_specs = [pl.BlockSpec((tm, d), lambda i: (i, 0)),
                pl.BlockSpec((None, 3, d), lambda i: (i // per_batch, 0, 0)),
                _full_spec((1, d))]
    in_specs += [pl.BlockSpec((tm, y.shape[1]), lambda i: (i, 0)) for y in ys]
    in_specs += [_full_spec(w.shape) for w in weights]
    return pl.pallas_call(
        _out_proj_kernel,
        grid=(m // tm,),
        in_specs=in_specs,
        out_specs=pl.BlockSpec((tm, d), lambda i: (i, 0)),
        out_shape=jax.ShapeDtypeStruct((m, d), F32),
        compiler_params=_params(("parallel",)),
        name="out_proj",
    )(x2, mod, gain.reshape(1, d), *ys, *weights)


def _mlp_kernel(x_ref, mod_ref, g_pre_ref, g_post_ref, w1_ref, w2_ref, o_ref, *, ff_chunk):
    x = x_ref[...]
    mod = mod_ref[...]
    h = _rms(x) * g_pre_ref[...] * (1.0 + mod[1:2]) + mod[0:1]
    hb = h.astype(BF16)
    d_ff = w1_ref.shape[1]
    acc = None
    for c0 in range(0, d_ff, ff_chunk):
        a = jnp.dot(hb, w1_ref[:, c0:c0 + ff_chunk], preferred_element_type=F32)
        a = jnp.square(jnp.maximum(a, 0.0)).astype(BF16)
        part = jnp.dot(a, w2_ref[c0:c0 + ff_chunk, :], preferred_element_type=F32)
        acc = part if acc is None else acc + part
    o_ref[...] = x + mod[2:3] * (_rms(acc) * g_post_ref[...])


def mlp(x2, mod, g_pre, g_post, w1, w2, seq, tm=512, ff_chunk=1024):
    m, d = x2.shape
    per_batch = seq // tm
    return pl.pallas_call(
        functools.partial(_mlp_kernel, ff_chunk=ff_chunk),
        grid=(m // tm,),
        in_specs=[pl.BlockSpec((tm, d), lambda i: (i, 0)),
                  pl.BlockSpec((None, 3, d), lambda i: (i // per_batch, 0, 0)),
                  _full_spec((1, d)), _full_spec((1, d)),
                  _full_spec(w1.shape), _full_spec(w2.shape)],
        out_specs=pl.BlockSpec((tm, d), lambda i: (i, 0)),
        out_shape=jax.ShapeDtypeStruct((m, d), F32),
        compiler_params=_params(("parallel",)),
        name="mlp",
    )(x2, mod, g_pre.reshape(1, d), g_post.reshape(1, d), w1, w2)


def _ssd_kernel(z_ref, xbc_ref, dt_ref, cw_ref, cb_ref, dtb_ref, alog_ref, dskip_ref, nw_ref, expand_ref,
                o_ref, conv_buf, state_ref, *, d_inner):
    chunk = z_ref.shape[0]
    n_state = SSD_STATE
    gn = SSD_GROUPS * n_state
    pair = 2 * SSD_HEAD_DIM
    heads_per_group = d_inner // SSD_HEAD_DIM // SSD_GROUPS
    halo = SUBLANES

    @pl.when(pl.program_id(1) == 0)
    def _():
        state_ref[...] = jnp.zeros_like(state_ref)
        conv_buf[0:halo, :] = jnp.zeros((halo, conv_buf.shape[1]), F32)

    conv_buf[halo:halo + chunk, :] = xbc_ref[...].astype(F32)
    acc = cb_ref[...]
    for tap in range(SSD_CONV):
        off = halo - (SSD_CONV - 1) + tap
        acc = acc + cw_ref[tap:tap + 1, :] * conv_buf[off:off + chunk, :]
    conv_buf[0:halo, :] = conv_buf[chunk:chunk + halo, :]
    xbc = _silu(acc)
    xs = xbc[:, :d_inner]
    b_all = xbc[:, d_inner:d_inner + gn]
    c_all = xbc[:, d_inner + gn:]

    dt_in = dt_ref[...] + dtb_ref[...]
    dt = jnp.maximum(dt_in, 0.0) + jnp.log1p(jnp.exp(-jnp.abs(dt_in)))
    a = dt * (-jnp.exp(alog_ref[...]))
    row = lax.broadcasted_iota(jnp.int32, (chunk, chunk), 0)
    col = lax.broadcasted_iota(jnp.int32, (chunk, chunk), 1)
    causal = row >= col
    tril = causal.astype(F32)
    hi = lax.Precision.HIGHEST
    a_cs = jnp.dot(tril, a, preferred_element_type=F32, precision=hi)
    a_cs_t = a_cs.T
    expand = expand_ref[...]
    dt_e = jnp.dot(dt, expand, preferred_element_type=F32, precision=hi)
    acs_e = jnp.dot(a_cs, expand, preferred_element_type=F32, precision=hi)
    a_last_e = acs_e[chunk - 1:chunk, :]
    x_dt = xs * dt_e
    w_state = (x_dt * jnp.exp(a_last_e - acs_e)).astype(BF16)
    out_scale = jnp.exp(acs_e)
    chunk_decay = jnp.exp(a_last_e)
    x_dt_b = x_dt.astype(BF16)
    lane = lax.broadcasted_iota(jnp.int32, (chunk, pair), 1)
    first_head = lane < SSD_HEAD_DIM

    y_blocks = []
    for g in range(SSD_GROUPS):
        bg = b_all[:, g * n_state:(g + 1) * n_state]
        cg = c_all[:, g * n_state:(g + 1) * n_state].astype(BF16)
        cb = lax.dot_general(cg, bg.astype(BF16), NT_DIMS, preferred_element_type=F32)
        bg_t = bg.T.astype(BF16)
        for pi in range(heads_per_group // 2):
            p = g * (heads_per_group // 2) + pi
            sl = slice(p * pair, (p + 1) * pair)
            xp = x_dt_b[:, sl]
            y_diag = None
            for k in range(2):
                hh = 2 * p + k
                seg = a_cs[:, hh:hh + 1] - a_cs_t[hh:hh + 1, :]
                decay = jnp.exp(jnp.where(causal, seg, MASK_VALUE))
                m = (cb * decay).astype(BF16)
                xh = jnp.where(first_head if k == 0 else jnp.logical_not(first_head), xp, jnp.zeros_like(xp))
                part = jnp.dot(m, xh, preferred_element_type=F32)
                y_diag = part if y_diag is None else y_diag + part
            st = state_ref[p]
            y_off = jnp.dot(cg, st.astype(BF16), preferred_element_type=F32) * out_scale[:, sl]
            state_ref[p] = st * chunk_decay[:, sl] + jnp.dot(bg_t, w_state[:, sl], preferred_element_type=F32)
            y_blocks.append(y_diag + y_off)
    y = jnp.concatenate(y_blocks, axis=-1) + xs * dskip_ref[...]
    y = y * _silu(z_ref[...].astype(F32))
    o_ref[...] = (_rms(y) * nw_ref[...]).astype(o_ref.dtype)


def ssd_mixer(z, xbc, dt_raw, conv_w, conv_b, dt_bias, a_log, d_skip, norm_w, batch, seq):
    m, d_inner = z.shape
    conv_ch = xbc.shape[1]
    n_heads = d_inner // SSD_HEAD_DIM
    nc = seq // CHUNK
    pad = LANES - n_heads
    expand = np.zeros((LANES, d_inner), np.float32)
    for h in range(n_heads):
        expand[h, h * SSD_HEAD_DIM:(h + 1) * SSD_HEAD_DIM] = 1.0
    row_spec = lambda width: pl.BlockSpec((CHUNK, width), lambda b, c: (b * nc + c, 0))
    return pl.pallas_call(
        functools.partial(_ssd_kernel, d_inner=d_inner),
        grid=(batch, nc),
        in_specs=[row_spec(d_inner), row_spec(conv_ch), row_spec(LANES),
                  _full_spec((SSD_CONV, conv_ch)), _full_spec((1, conv_ch)),
                  _full_spec((1, LANES)), _full_spec((1, LANES)),
                  _full_spec((1, d_inner)), _full_spec((1, d_inner)), _full_spec((LANES, d_inner))],
        out_specs=row_spec(d_inner),
        out_shape=jax.ShapeDtypeStruct((m, d_inner), BF16),
        scratch_shapes=[pltpu.VMEM((CHUNK + SUBLANES, conv_ch), F32),
                        pltpu.VMEM((n_heads // 2, SSD_STATE, 2 * SSD_HEAD_DIM), F32)],
        compiler_params=_params(("parallel", "arbitrary")),
        name="ssd_mixer",
    )(z, xbc, dt_raw, conv_w, conv_b.reshape(1, conv_ch),
      jnp.pad(dt_bias, (0, pad)).reshape(1, LANES), jnp.pad(a_log, (0, pad)).reshape(1, LANES),
      jnp.repeat(d_skip, SSD_HEAD_DIM).reshape(1, d_inner), norm_w.reshape(1, d_inner), jnp.asarray(expand))


def _diff_attn_kernel(q_ref, k_ref, v_ref, bias_ref, lam_ref, nw_ref, o_ref,
                      vt_ref, s0_ref, s1_ref, cmax0_ref, cmax1_ref, p_ref, m_ref, alpha_ref, acc_ref,
                      *, lam_init):
    s_refs, cmax_refs = (s0_ref, s1_ref), (cmax0_ref, cmax1_ref)
    t = s0_ref.shape[0]
    tq = q_ref.shape[0]
    head_w = v_ref.shape[1]
    ext_rows = vt_ref.shape[1]
    i = pl.program_id(2)
    q_t = q_ref[...].astype(F32).T.astype(BF16)
    dim = lax.broadcasted_iota(jnp.int32, q_t.shape, 0)
    zero = jnp.zeros_like(q_t)
    qq_t = jnp.concatenate([jnp.where(dim < DIFF_HEAD_DIM, q_t, zero),
                            jnp.where(dim >= DIFF_HEAD_DIM, q_t, zero)], axis=1)

    @pl.when(i == 0)
    def _():
        for j in range(vt_ref.shape[0]):
            vt_ref[j, :head_w, :] = v_ref[j * t:(j + 1) * t, :].astype(F32).T.astype(BF16)
            vt_ref[j, head_w:, :] = jnp.ones((ext_rows - head_w, t), BF16)

    m_ref[...] = jnp.full(m_ref.shape, MASK_VALUE, F32)
    acc_ref[...] = jnp.zeros(acc_ref.shape, F32)

    diag_tile, prev_tile = bias_ref[0], bias_ref[1]
    near_tiles = ((jnp.full((t, t), MASK_VALUE, F32), diag_tile),
                  (diag_tile, prev_tile),
                  (prev_tile, jnp.zeros((t, t), F32)))
    near_blocks = (2 * i + 1, 2 * i, jnp.maximum(2 * i - 1, 0))

    def raw_logits(block):
        kb = k_ref[pl.ds(pl.multiple_of(block * t, t), t), :]
        return jnp.dot(kb, qq_t, preferred_element_type=F32)

    def store_logits(s, slot):
        s_refs[slot][...] = s
        chains = [s[0:SUBLANES, :], s[SUBLANES:2 * SUBLANES, :]]
        for n, r in enumerate(range(2 * SUBLANES, t, SUBLANES)):
            chains[n % 2] = jnp.maximum(chains[n % 2], s[r:r + SUBLANES, :])
        col_max = jnp.maximum(chains[0], chains[1])
        for shift in (4, 2, 1):
            col_max = jnp.maximum(col_max, pltpu.roll(col_max, shift, 0))
        cmax_refs[slot][...] = col_max

    def near_logits(visit, slot):
        bias = jnp.concatenate(near_tiles[visit] * 2, axis=1)
        store_logits(raw_logits(near_blocks[visit]) + bias, slot)

    def far_logits(visit, slot):
        store_logits(raw_logits(visit - len(near_blocks)), slot)

    def block_of(visit):
        return near_blocks[visit] if isinstance(visit, int) else visit - len(near_blocks)

    def softmax_accumulate(visit, slot):
        s_ref = s_refs[slot]
        packed = 2 * SUBLANES
        m_old = m_ref[...]
        m_new = jnp.maximum(m_old, cmax_refs[slot][...])
        m_ref[...] = m_new
        alpha_ref[...] = jnp.exp2(m_old - m_new)
        m_tile = jnp.concatenate([m_new, m_new], axis=0)
        for r in range(0, t, packed):
            p_ref[r:r + packed, :] = jnp.exp2(s_ref[r:r + packed, :] - m_tile).astype(BF16)
        pv = jnp.dot(vt_ref[block_of(visit)], p_ref[...], preferred_element_type=F32)
        alpha = alpha_ref[...]
        for r in range(0, ext_rows, SUBLANES):
            acc_ref[r:r + SUBLANES, :] = alpha * acc_ref[r:r + SUBLANES, :] + pv[r:r + SUBLANES, :]

    def pipeline_step(visit, slot):
        far_logits(visit, slot)
        softmax_accumulate(visit - 1, 1 - slot)

    near_logits(0, 0)
    near_logits(1, 1)
    softmax_accumulate(0, 0)
    near_logits(2, 0)
    softmax_accumulate(1, 1)

    @pl.when(i > 0)
    def _():
        far_logits(3, 1)
        softmax_accumulate(2, 0)
        last = 2 * i + 1

        def pair(k, _):
            pipeline_step(4 + 2 * k, 0)

            @pl.when(5 + 2 * k <= last)
            def _():
                pipeline_step(5 + 2 * k, 1)

            return 0

        lax.fori_loop(0, i - 1, pair, 0)
        softmax_accumulate(last, 1)

    lam = lam_ref[...]
    lam_full = (jnp.exp(jnp.sum(lam[0:1] * lam[1:2], axis=-1, keepdims=True))
                - jnp.exp(jnp.sum(lam[2:3] * lam[3:4], axis=-1, keepdims=True)) + lam_init)
    inv_sum = 1.0 / acc_ref[head_w:head_w + SUBLANES, :]
    out_t = acc_ref[:head_w, :] * jnp.tile(inv_sum, (head_w // SUBLANES, 1))
    out = (out_t[:, :tq] - lam_full * out_t[:, tq:]).T
    o_ref[...] = (_rms(out) * nw_ref[...] * (1.0 - lam_init)).astype(o_ref.dtype)


def diff_attention(q, k, v, bias, lam, subln_w, batch, seq, lam_init):
    m, width = q.shape
    head_w = 2 * DIFF_HEAD_DIM
    n_heads = width // head_w
    t = ATTN_BLOCK
    tq = 2 * t
    nq = seq // tq
    cols = 2 * tq
    ext_rows = head_w + 2 * SUBLANES
    return pl.pallas_call(
        functools.partial(_diff_attn_kernel, lam_init=lam_init),
        grid=(batch, n_heads, nq),
        in_specs=[pl.BlockSpec((tq, head_w), lambda b, h, i: (b * nq + i, h)),
                  pl.BlockSpec((seq, head_w), lambda b, h, i: (b, h)),
                  pl.BlockSpec((seq, head_w), lambda b, h, i: (b, h)),
                  pl.BlockSpec((None, 2, t, t), lambda b, h, i: (h, 0, 0, 0)),
                  _full_spec(lam.shape), _full_spec((1, head_w))],
        out_specs=pl.BlockSpec((tq, head_w), lambda b, h, i: (b * nq + i, h)),
        out_shape=jax.ShapeDtypeStruct((m, width), BF16),
        scratch_shapes=[pltpu.VMEM((seq // t, ext_rows, t), BF16),
                        pltpu.VMEM((t, cols), F32), pltpu.VMEM((t, cols), F32),
                        pltpu.VMEM((SUBLANES, cols), F32), pltpu.VMEM((SUBLANES, cols), F32),
                        pltpu.VMEM((t, cols), BF16),
                        pltpu.VMEM((SUBLANES, cols), F32),
                        pltpu.VMEM((SUBLANES, cols), F32),
                        pltpu.VMEM((ext_rows, cols), F32)],
        compiler_params=_params(("parallel", "parallel", "arbitrary")),
        name="diff_attention",
    )(q, k, v, bias, lam, subln_w.reshape(1, head_w))


def _retention_kernel(q_ref, k_ref, v_ref, g_ref, cos_ref, sin_ref, decay_ref, zeta_ref, xi_ref, o_ref,
                      state_ref, *, chunk_decay):
    @pl.when(pl.program_id(1) == 0)
    def _():
        state_ref[...] = jnp.zeros_like(state_ref)

    cos = cos_ref[...]
    sin = sin_ref[...]
    half = RET_QK_DIM // 2

    def rope(t):
        return t * cos + pltpu.roll(t, half, 1) * sin

    for h in range(RET_HEADS):
        qs = slice(h * RET_QK_DIM, (h + 1) * RET_QK_DIM)
        vs = slice(h * RET_V_DIM, (h + 1) * RET_V_DIM)
        qr = rope(q_ref[:, qs].astype(F32))
        kr = rope(k_ref[:, qs].astype(F32)) * (RET_QK_DIM ** -0.5)
        vh = v_ref[:, vs]
        scores = lax.dot_general(qr.astype(BF16), kr.astype(BF16), NT_DIMS,
                                 preferred_element_type=F32) * decay_ref[h]
        inner = jnp.dot(scores.astype(BF16), vh, preferred_element_type=F32)
        st = state_ref[h]
        cross = jnp.dot((qr * xi_ref[h]).astype(BF16), st.astype(BF16), preferred_element_type=F32)
        kv = jnp.dot((kr.T * zeta_ref[h]).astype(BF16), vh, preferred_element_type=F32)
        state_ref[h] = st * chunk_decay[h] + kv
        out = _rms(inner + cross) * _silu(g_ref[:, vs].astype(F32))
        o_ref[:, vs] = out.astype(o_ref.dtype)


def retention(rq, rk, rv, rg, batch, seq):
    m = rq.shape[0]
    nc = seq // CHUNK
    half = RET_QK_DIM // 2
    pos = jnp.arange(seq, dtype=F32)
    inv = ROPE_BASE ** (-jnp.arange(half, dtype=F32) / half)
    ang = pos[:, None] * inv[None]
    cos = jnp.concatenate([jnp.cos(ang), jnp.cos(ang)], axis=-1)
    sin = jnp.concatenate([-jnp.sin(ang), jnp.sin(ang)], axis=-1)
    log_gamma = jnp.log(1.0 - 2.0 ** (-5.0 - jnp.arange(RET_HEADS, dtype=F32)))
    idx = jnp.arange(CHUNK, dtype=F32)
    rel = idx[:, None] - idx[None, :]
    decay = jnp.where(rel >= 0, jnp.exp(jnp.maximum(rel, 0.0)[None] * log_gamma[:, None, None]), 0.0)
    zeta = jnp.exp((CHUNK - 1 - idx)[None] * log_gamma[:, None])[:, None, :]
    xi = jnp.broadcast_to(jnp.exp((idx + 1.0)[None] * log_gamma[:, None])[:, :, None],
                          (RET_HEADS, CHUNK, RET_QK_DIM))
    gamma = 1.0 - 2.0 ** (-5.0 - np.arange(RET_HEADS, dtype=np.float64))
    chunk_decay = tuple(float(g ** CHUNK) for g in gamma)
    row_spec = lambda width: pl.BlockSpec((CHUNK, width), lambda b, c: (b * nc + c, 0))
    return pl.pallas_call(
        functools.partial(_retention_kernel, chunk_decay=chunk_decay),
        grid=(batch, nc),
        in_specs=[row_spec(rq.shape[1]), row_spec(rk.shape[1]), row_spec(rv.shape[1]), row_spec(rg.shape[1]),
                  pl.BlockSpec((CHUNK, RET_QK_DIM), lambda b, c: (c, 0)),
                  pl.BlockSpec((CHUNK, RET_QK_DIM), lambda b, c: (c, 0)),
                  _full_spec(decay.shape), _full_spec(zeta.shape), _full_spec(xi.shape)],
        out_specs=row_spec(rv.shape[1]),
        out_shape=jax.ShapeDtypeStruct((m, rv.shape[1]), BF16),
        scratch_shapes=[pltpu.VMEM((RET_HEADS, RET_QK_DIM, RET_V_DIM), F32)],
        compiler_params=_params(("parallel", "arbitrary")),
        name="retention",
    )(rq, rk, rv, rg, cos, sin, decay, zeta, xi)


def _swa_kernel(sink_ref, q_ref, kp_ref, kc_ref, vp_ref, vc_ref, bias_ref, o_ref):
    i = pl.program_id(1)
    blk = q_ref.shape[0]
    rep = SWA_HEADS // SWA_KV_HEADS
    kb = jnp.concatenate([kp_ref[...], kc_ref[...]], axis=0)
    vb = jnp.concatenate([vp_ref[...], vc_ref[...]], axis=0)
    col = lax.broadcasted_iota(jnp.int32, (blk, 2 * blk), 1)
    no_prev = jnp.logical_and(col < blk, i == 0)
    lane = lax.broadcasted_iota(jnp.int32, (blk, LANES), 1)
    lower = lane < SWA_HEAD_DIM
    outs = []
    for p in range(SWA_HEADS // 2):
        qp = q_ref[:, p * LANES:(p + 1) * LANES].astype(F32)
        pair_out = []
        for parity in range(2):
            h = 2 * p + parity
            g = h // rep
            qh = qp if parity == g else pltpu.roll(qp, SWA_HEAD_DIM, 1)
            qh = jnp.where(lower if g == 0 else jnp.logical_not(lower), qh, 0.0).astype(BF16)
            s = lax.dot_general(qh, kb, NT_DIMS, preferred_element_type=F32) + bias_ref[h]
            s = jnp.where(no_prev, MASK_VALUE, s)
            sink = sink_ref[h]
            m = jnp.maximum(jnp.max(s, axis=-1, keepdims=True), sink)
            e = jnp.exp(s - m)
            denom = jnp.sum(e, axis=-1, keepdims=True) + jnp.exp(sink - m)
            o = jnp.dot(e.astype(BF16), vb, preferred_element_type=F32) / denom
            pair_out.append(o if parity == g else pltpu.roll(o, SWA_HEAD_DIM, 1))
        outs.append(jnp.where(lower, pair_out[0], pair_out[1]))
    o_ref[...] = jnp.concatenate(outs, axis=-1).astype(o_ref.dtype)


def sliding_window_attention(sq, sk, sv, sinks, bias, batch, seq):
    m, qw = sq.shape
    kw = sk.shape[1]
    nb = seq // SWA_BLOCK
    cur = lambda b, i: (b * nb + i, 0)
    prev = lambda b, i: (b * nb + jnp.maximum(i - 1, 0), 0)
    return pl.pallas_call(
        _swa_kernel,
        grid=(batch, nb),
        in_specs=[pl.BlockSpec(memory_space=pltpu.SMEM),
                  pl.BlockSpec((SWA_BLOCK, qw), cur),
                  pl.BlockSpec((SWA_BLOCK, kw), prev), pl.BlockSpec((SWA_BLOCK, kw), cur),
                  pl.BlockSpec((SWA_BLOCK, kw), prev), pl.BlockSpec((SWA_BLOCK, kw), cur),
                  _full_spec(bias.shape)],
        out_specs=pl.BlockSpec((SWA_BLOCK, qw), cur),
        out_shape=jax.ShapeDtypeStruct((m, qw), BF16),
        compiler_params=_params(("parallel", "parallel")),
        name="sliding_window_attention",
    )(sinks, sq, sk, sk, sv, sv, bias)


def _split_cols(w, sizes):
    offs = np.cumsum((0,) + tuple(sizes))
    return [w[:, offs[j]:offs[j + 1]] for j in range(len(sizes))]


def even_layer_mixer(x2, mod, g_pre, g_post, w_in, conv_w, conv_b, dt_bias, a_log, d_skip, ssd_norm, lam,
                     diff_norm, w_out, diff_bias, layer_idx, batch, seq):
    d = x2.shape[1]
    d_inner = d
    n_ssd_heads = d_inner // SSD_HEAD_DIM
    conv_ch = d_inner + 2 * SSD_GROUPS * SSD_STATE
    wz, wxbc, wdt, wq, wk, wv = _split_cols(w_in, (d_inner, conv_ch, n_ssd_heads, d, d, d))
    wdt = jnp.pad(wdt, ((0, 0), (0, LANES - n_ssd_heads)))
    wq = wq * (DIFF_HEAD_DIM ** -0.5 * LOG2_E)
    weights = [w.astype(BF16) for w in (wz, wxbc, wdt, wq, wk, wv)]
    z, xbc, dt_raw, q, k, v = norm_proj(x2, mod, g_pre, weights, (BF16, BF16, F32, BF16, BF16, BF16), seq)
    y_ssd = ssd_mixer(z, xbc, dt_raw, conv_w, conv_b, dt_bias, a_log, d_skip, ssd_norm, batch, seq)
    lam_init = 0.8 - 0.6 * math.exp(-0.3 * layer_idx)
    y_diff = diff_attention(q, k, v, diff_bias, lam, diff_norm, batch, seq, lam_init)
    w_out = w_out.astype(BF16)
    return out_proj(x2, mod, g_post, [y_ssd, y_diff], [w_out[:d_inner], w_out[d_inner:]], seq)


def odd_layer_mixer(x2, mod, g_pre, g_post, w_in, sinks, w_out, swa_bias, batch, seq):
    ret_qk = RET_HEADS * RET_QK_DIM
    ret_v = RET_HEADS * RET_V_DIM
    swa_q = SWA_HEADS * SWA_HEAD_DIM
    swa_kv = SWA_KV_HEADS * SWA_HEAD_DIM
    wrq, wrk, wrv, wrg, wsq, wsk, wsv = _split_cols(w_in, (ret_qk, ret_qk, ret_v, ret_v, swa_q, swa_kv, swa_kv))
    wsq = wsq * (SWA_HEAD_DIM ** -0.5)
    weights = [w.astype(BF16) for w in (wrq, wrk, wrv, wrg, wsq, wsk, wsv)]
    rq, rk, rv, rg, sq, sk, sv = norm_proj(x2, mod, g_pre, weights, (BF16,) * 7, seq)
    y_ret = retention(rq, rk, rv, rg, batch, seq)
    y_swa = sliding_window_attention(sq, sk, sv, sinks, swa_bias, batch, seq)
    w_out = w_out.astype(BF16)
    return out_proj(x2, mod, g_post, [y_ret, y_swa], [w_out[:ret_v], w_out[ret_v:]], seq)


def kernel(x, c, rel_bias, norm_gains, mod_w, mod_b, mlp_w1, mlp_w2, e_w_in, e_conv_w, e_conv_b, e_dt_bias,
           e_A_log, e_D, e_ssd_norm, e_lambda, e_diff_norm, e_w_out, o_w_in, o_sinks, o_w_out):
    batch, seq, d = x.shape
    depth = norm_gains.shape[0]
    assert seq % (2 * ATTN_BLOCK) == 0 and d % LANES == 0
    mods = modulation(c, mod_w.reshape(depth * 2, d, 3 * d), mod_b.reshape(depth * 2, 3 * d))
    far_bucket = REL_BUCKETS - 1
    diff_bias = bias_tiles(rel_bias, _diff_bucket_idx(ATTN_BLOCK), shift_bucket=far_bucket, scale=LOG2_E)
    diff_bias = diff_bias.reshape(rel_bias.shape[1], 2, ATTN_BLOCK, ATTN_BLOCK)
    swa_bias = bias_tiles(rel_bias, _swa_bucket_idx())

    x2 = x.reshape(batch * seq, d)
    for layer in range(depth):
        j = layer // 2
        gains = norm_gains[layer]
        if layer % 2 == 0:
            x2 = even_layer_mixer(x2, mods[2 * layer], gains[0], gains[1], e_w_in[j], e_conv_w[j], e_conv_b[j],
                                  e_dt_bias[j], e_A_log[j], e_D[j], e_ssd_norm[j], e_lambda[j], e_diff_norm[j],
                                  e_w_out[j], diff_bias, layer, batch, seq)
        else:
            x2 = odd_layer_mixer(x2, mods[2 * layer], gains[0], gains[1], o_w_in[j], o_sinks[j], o_w_out[j],
                                 swa_bias, batch, seq)
        x2 = mlp(x2, mods[2 * layer + 1], gains[2], gains[3], mlp_w1[layer].astype(BF16),
                 mlp_w2[layer].astype(BF16), seq)
    return x2.reshape(batch, seq, d)
```

```python
import functools
import math

import jax
import jax.numpy as jnp
import numpy as np
from jax import lax
from jax.experimental import pallas as pl
from jax.experimental.pallas import tpu as pltpu

EPS = 1e-6
MASK_VALUE = -1e30
LOG2_E = math.log2(math.e)
LANES = 128
SUBLANES = 8
VMEM_LIMIT = 56 * 1024 * 1024

CHUNK = 128
REL_BUCKETS = 32
REL_MAX_DIST = 128
SSD_HEAD_DIM = 64
SSD_GROUPS = 4
SSD_STATE = 128
SSD_CONV = 4
DIFF_HEAD_DIM = 64
RET_HEADS = 4
RET_QK_DIM = 128
RET_V_DIM = 256
ROPE_BASE = 10000.0
SWA_HEADS = 8
SWA_KV_HEADS = 2
SWA_HEAD_DIM = 64
SWA_BLOCK = 128
ATTN_BLOCK = 512

BF16 = jnp.bfloat16
F32 = jnp.float32
NT_DIMS = (((1,), (1,)), ((), ()))


def _params(semantics, flags=None):
    return pltpu.CompilerParams(dimension_semantics=semantics, vmem_limit_bytes=VMEM_LIMIT, flags=flags)


def _full_spec(shape):
    return pl.BlockSpec(shape, lambda *_: (0,) * len(shape))


def _silu(x):
    h = 0.5 * x
    return h * jnp.tanh(h) + h


def _split3(x):
    hi = x.astype(BF16)
    r1 = x - hi.astype(F32)
    mid = r1.astype(BF16)
    lo = (r1 - mid.astype(F32)).astype(BF16)
    return hi, mid, lo


def _rms(x, eps=EPS):
    return x * lax.rsqrt(jnp.mean(x * x, axis=-1, keepdims=True) + eps)


def _mod_kernel(c_ref, w_ref, b_ref, o_ref):
    c_act = _silu(c_ref[...])
    o_ref[...] = jnp.dot(c_act, w_ref[...], preferred_element_type=F32,
                         precision=lax.Precision.HIGHEST) + b_ref[...]


def modulation(c, mod_w, mod_b):
    b, d = c.shape
    n = mod_w.shape[0]
    rows = SUBLANES
    c_pad = jnp.zeros((rows, d), F32).at[:b].set(c)
    tn = d
    out = pl.pallas_call(
        _mod_kernel,
        grid=(n, 3 * d // tn),
        in_specs=[pl.BlockSpec((rows, d), lambda s, j: (0, 0)),
                  pl.BlockSpec((None, d, tn), lambda s, j: (s, 0, j)),
                  pl.BlockSpec((None, 1, tn), lambda s, j: (s, 0, j))],
        out_specs=pl.BlockSpec((None, rows, tn), lambda s, j: (s, 0, j)),
        out_shape=jax.ShapeDtypeStruct((n, rows, 3 * d), F32),
        compiler_params=_params(("parallel", "parallel")),
        name="modulation",
    )(c_pad, mod_w, mod_b.reshape(n, 1, 3 * d))
    return out[:, :b].reshape(n, b, 3, d)


def _t5_bucket_np(dist):
    max_exact = REL_BUCKETS // 2
    logd = np.log(np.maximum(dist, 1).astype(np.float32) / np.float32(max_exact))
    large = max_exact + (logd / np.float32(math.log(REL_MAX_DIST / max_exact))
                         * np.float32(REL_BUCKETS - max_exact)).astype(np.int32)
    large = np.minimum(large, REL_BUCKETS - 1)
    return np.where(dist < max_exact, dist, large).astype(np.int32)


def _bias_kernel(rb_ref, idx_ref, o_ref, *, shift_bucket, scale, block_buckets):
    h = pl.program_id(0)
    shift = rb_ref[shift_bucket, h] if shift_bucket is not None else 0.0

    def value(bucket):
        return (rb_ref[bucket, h] - shift) * scale

    for (r0, c0), buckets in block_buckets:
        window = (slice(r0, r0 + LANES), slice(c0, c0 + LANES))
        idx = idx_ref[window]
        acc = jnp.full(idx.shape, MASK_VALUE if -1 in buckets else value(max(buckets)), F32)
        for bucket in buckets:
            if bucket >= 0 and len(buckets) > 1:
                acc = jnp.where(idx == bucket, value(bucket), acc)
        o_ref[window] = acc


def bias_tiles(rel_bias, bucket_idx, shift_bucket=None, scale=1.0):
    n_heads = rel_bias.shape[1]
    r, c = bucket_idx.shape
    block_buckets = tuple(((r0, c0), tuple(int(b) for b in np.unique(bucket_idx[r0:r0 + LANES, c0:c0 + LANES])))
                          for r0 in range(0, r, LANES) for c0 in range(0, c, LANES))
    return pl.pallas_call(
        functools.partial(_bias_kernel, shift_bucket=shift_bucket, scale=scale, block_buckets=block_buckets),
        grid=(n_heads,),
        in_specs=[pl.BlockSpec(memory_space=pltpu.SMEM),
                  pl.BlockSpec((r, c), lambda h: (0, 0))],
        out_specs=pl.BlockSpec((None, r, c), lambda h: (h, 0, 0)),
        out_shape=jax.ShapeDtypeStruct((n_heads, r, c), F32),
        compiler_params=_params(("parallel",)),
        name="bias_tiles",
    )(rel_bias, jnp.asarray(bucket_idx))


def _diff_bucket_idx(t):
    q = np.arange(t)[None, :]
    k = np.arange(t)[:, None]
    diag = q - k
    idx_diag = np.where(diag >= 0, _t5_bucket_np(np.maximum(diag, 0)), -1)
    idx_prev = _t5_bucket_np(diag + t)
    assert (_t5_bucket_np(diag + 2 * t) == REL_BUCKETS - 1).all()
    return np.concatenate([idx_diag, idx_prev], axis=0).astype(np.int32)


def _swa_bucket_idx():
    q = np.arange(SWA_BLOCK)[:, None]
    k = np.arange(2 * SWA_BLOCK)[None, :] - SWA_BLOCK
    dist = q - k
    valid = (dist >= 0) & (dist < SWA_BLOCK)
    return np.where(valid, _t5_bucket_np(np.maximum(dist, 0)), -1).astype(np.int32)


def _norm_proj_kernel(x_ref, mod_ref, g_ref, *refs):
    n = len(refs) // 2
    w_refs, o_refs = refs[:n], refs[n:]
    mod = mod_ref[...]
    h = _rms(x_ref[...]) * g_ref[...] * (1.0 + mod[1:2]) + mod[0:1]
    hb = h.astype(BF16)
    for w_ref, o_ref in zip(w_refs, o_refs):
        o_ref[...] = jnp.dot(hb, w_ref[...], preferred_element_type=F32).astype(o_ref.dtype)


def norm_proj(x2, mod, gain, weights, out_dtypes, seq, tm=512):
    m, d = x2.shape
    per_batch = seq // tm
    in_specs = [pl.BlockSpec((tm, d), lambda i: (i, 0)),
                pl.BlockSpec((None, 3, d), lambda i: (i // per_batch, 0, 0)),
                _full_spec((1, d))]
    in_specs += [_full_spec(w.shape) for w in weights]
    out_specs = [pl.BlockSpec((tm, w.shape[1]), lambda i: (i, 0)) for w in weights]
    out_shape = [jax.ShapeDtypeStruct((m, w.shape[1]), dt) for w, dt in zip(weights, out_dtypes)]
    return pl.pallas_call(
        _norm_proj_kernel,
        grid=(m // tm,),
        in_specs=in_specs,
        out_specs=out_specs,
        out_shape=out_shape,
        compiler_params=_params(("parallel",)),
        name="norm_proj",
    )(x2, mod, gain.reshape(1, d), *weights)


def _out_proj_kernel(x_ref, mod_ref, g_ref, *refs):
    n = (len(refs) - 1) // 2
    y_refs, w_refs, o_ref = refs[:n], refs[n:2 * n], refs[2 * n]
    acc = None
    for y_ref, w_ref in zip(y_refs, w_refs):
        part = jnp.dot(y_ref[...], w_ref[...], preferred_element_type=F32)
        acc = part if acc is None else acc + part
    gate = mod_ref[...][2:3]
    o_ref[...] = x_ref[...] + gate * (_rms(acc) * g_ref[...])


def out_proj(x2, mod, gain, ys, weights, seq, tm=512):
    m, d = x2.shape
    per_batch = seq // tm
    in_specs = [pl.BlockSpec((tm, d), lambda i: (i, 0)),
                pl.BlockSpec((None, 3, d), lambda i: (i // per_batch, 0, 0)),
                _full_spec((1, d))]
    in_specs += [pl.BlockSpec((tm, y.shape[1]), lambda i: (i, 0)) for y in ys]
    in_specs += [_full_spec(w.shape) for w in weights]
    return pl.pallas_call(
        _out_proj_kernel,
        grid=(m // tm,),
        in_specs=in_specs,
        out_specs=pl.BlockSpec((tm, d), lambda i: (i, 0)),
        out_shape=jax.ShapeDtypeStruct((m, d), F32),
        compiler_params=_params(("parallel",)),
        name="out_proj",
    )(x2, mod, gain.reshape(1, d), *ys, *weights)


def _mlp_kernel(x_ref, mod_ref, g_pre_ref, g_post_ref, w1_ref, w2_ref, o_ref, *, ff_chunk):
    x = x_ref[...]
    mod = mod_ref[...]
    h = _rms(x) * g_pre_ref[...] * (1.0 + mod[1:2]) + mod[0:1]
    hb = h.astype(BF16)
    d_ff = w1_ref.shape[1]
    acc = None
    for c0 in range(0, d_ff, ff_chunk):
        a = jnp.dot(hb, w1_ref[:, c0:c0 + ff_chunk], preferred_element_type=F32)
        a = jnp.square(jnp.maximum(a, 0.0)).astype(BF16)
        part = jnp.dot(a, w2_ref[c0:c0 + ff_chunk, :], preferred_element_type=F32)
        acc = part if acc is None else acc + part
    o_ref[...] = x + mod[2:3] * (_rms(acc) * g_post_ref[...])


def mlp(x2, mod, g_pre, g_post, w1, w2, seq, tm=512, ff_chunk=1024):
    m, d = x2.shape
    per_batch = seq // tm
    return pl.pallas_call(
        functools.partial(_mlp_kernel, ff_chunk=ff_chunk),
        grid=(m // tm,),
        in_specs=[pl.BlockSpec((tm, d), lambda i: (i, 0)),
                  pl.BlockSpec((None, 3, d), lambda i: (i // per_batch, 0, 0)),
                  _full_spec((1, d)), _full_spec((1, d)),
                  _full_spec(w1.shape), _full_spec(w2.shape)],
        out_specs=pl.BlockSpec((tm, d), lambda i: (i, 0)),
        out_shape=jax.ShapeDtypeStruct((m, d), F32),
        compiler_params=_params(("parallel",)),
        name="mlp",
    )(x2, mod, g_pre.reshape(1, d), g_post.reshape(1, d), w1, w2)


def _ssd_kernel(z_ref, xbc_ref, dt_ref, cw_ref, cb_ref, dtb_ref, alog_ref, dskip_ref, nw_ref, expand_ref,
                shift_ref, o_ref, conv_buf, state_ref, *, d_inner):
    chunk = z_ref.shape[0]
    n_state = SSD_STATE
    gn = SSD_GROUPS * n_state
    pair = 2 * SSD_HEAD_DIM
    heads_per_group = d_inner // SSD_HEAD_DIM // SSD_GROUPS
    halo = conv_buf.shape[0] - chunk

    @pl.when(pl.program_id(1) == 0)
    def _():
        state_ref[...] = jnp.zeros_like(state_ref)
        conv_buf[0:halo, :] = jnp.zeros((halo, conv_buf.shape[1]), BF16)

    cur = xbc_ref[...]
    conv_buf[halo:halo + chunk, :] = cur
    shifted = jnp.dot(shift_ref[...], conv_buf[...], preferred_element_type=F32)
    acc = cb_ref[...] + cw_ref[SSD_CONV - 1:SSD_CONV, :] * cur.astype(F32)
    for tap in range(SSD_CONV - 1):
        acc = acc + cw_ref[tap:tap + 1, :] * shifted[tap * chunk:(tap + 1) * chunk, :]
    conv_buf[0:halo, :] = conv_buf[chunk:chunk + halo, :]
    xbc = _silu(acc)
    xs = xbc[:, :d_inner]
    b_all = xbc[:, d_inner:d_inner + gn]
    c_all = xbc[:, d_inner + gn:]

    dt_in = dt_ref[...] + dtb_ref[...]
    dt = jnp.maximum(dt_in, 0.0) + jnp.log1p(jnp.exp(-jnp.abs(dt_in)))
    a = dt * (-jnp.exp(alog_ref[...]))
    row = lax.broadcasted_iota(jnp.int32, (chunk, chunk), 0)
    col = lax.broadcasted_iota(jnp.int32, (chunk, chunk), 1)
    causal = row >= col
    tril = jnp.where(causal, 1.0, 0.0).astype(BF16)
    parts = jnp.dot(tril, jnp.concatenate(_split3(a), axis=1), preferred_element_type=F32)
    a_cs = parts[:, :LANES] + parts[:, LANES:2 * LANES] + parts[:, 2 * LANES:]
    a_cs_t = a_cs.T
    lhs = jnp.concatenate([jnp.concatenate(_split3(dt), axis=1), jnp.concatenate(_split3(a_cs), axis=1)], axis=0)
    expanded = jnp.dot(lhs, expand_ref[...], preferred_element_type=F32)
    dt_e, acs_e = expanded[:chunk], expanded[chunk:]
    a_last_e = acs_e[chunk - 1:chunk, :]
    x_dt = xs * dt_e
    w_state = (x_dt * jnp.exp(a_last_e - acs_e)).astype(BF16)
    out_scale = jnp.exp(acs_e)
    chunk_decay = jnp.exp(a_last_e)
    x_dt_b = x_dt.astype(BF16)
    lane = lax.broadcasted_iota(jnp.int32, (chunk, pair), 1)
    first_head = lane < SSD_HEAD_DIM

    y_blocks = []
    for g in range(SSD_GROUPS):
        bg = b_all[:, g * n_state:(g + 1) * n_state]
        cg = c_all[:, g * n_state:(g + 1) * n_state].astype(BF16)
        cb = lax.dot_general(cg, bg.astype(BF16), NT_DIMS, preferred_element_type=F32)
        bg_t = bg.T.astype(BF16)
        for pi in range(heads_per_group // 2):
            p = g * (heads_per_group // 2) + pi
            sl = slice(p * pair, (p + 1) * pair)
            xp = x_dt_b[:, sl]
            y_diag = None
            for k in range(2):
                hh = 2 * p + k
                seg = a_cs[:, hh:hh + 1] - a_cs_t[hh:hh + 1, :]
                decay = jnp.exp(jnp.where(causal, seg, MASK_VALUE))
                m = (cb * decay).astype(BF16)
                xh = jnp.where(first_head if k == 0 else jnp.logical_not(first_head), xp, jnp.zeros_like(xp))
                part = jnp.dot(m, xh, preferred_element_type=F32)
                y_diag = part if y_diag is None else y_diag + part
            st = state_ref[p]
            y_off = jnp.dot(cg, st.astype(BF16), preferred_element_type=F32) * out_scale[:, sl]
            state_ref[p] = st * chunk_decay[:, sl] + jnp.dot(bg_t, w_state[:, sl], preferred_element_type=F32)
            y_blocks.append(y_diag + y_off)
    y = jnp.concatenate(y_blocks, axis=-1) + xs * dskip_ref[...]
    y = y * _silu(z_ref[...].astype(F32))
    o_ref[...] = (_rms(y) * nw_ref[...]).astype(o_ref.dtype)


def ssd_mixer(z, xbc, dt_raw, conv_w, conv_b, dt_bias, a_log, d_skip, norm_w, batch, seq):
    m, d_inner = z.shape
    conv_ch = xbc.shape[1]
    n_heads = d_inner // SSD_HEAD_DIM
    nc = seq // CHUNK
    pad = LANES - n_heads
    expand = np.zeros((LANES, d_inner), np.float32)
    for h in range(n_heads):
        expand[h, h * SSD_HEAD_DIM:(h + 1) * SSD_HEAD_DIM] = 1.0
    expand3 = jnp.asarray(np.tile(expand, (3, 1)), BF16)
    halo = 2 * SUBLANES
    shift = np.zeros(((SSD_CONV - 1) * CHUNK, halo + CHUNK), np.float32)
    for tap in range(SSD_CONV - 1):
        shift[tap * CHUNK + np.arange(CHUNK), halo - (SSD_CONV - 1) + tap + np.arange(CHUNK)] = 1.0
    shift = jnp.asarray(shift, BF16)
    row_spec = lambda width: pl.BlockSpec((CHUNK, width), lambda b, c: (b * nc + c, 0))
    return pl.pallas_call(
        functools.partial(_ssd_kernel, d_inner=d_inner),
        grid=(batch, nc),
        in_specs=[row_spec(d_inner), row_spec(conv_ch), row_spec(LANES),
                  _full_spec((SSD_CONV, conv_ch)), _full_spec((1, conv_ch)),
                  _full_spec((1, LANES)), _full_spec((1, LANES)),
                  _full_spec((1, d_inner)), _full_spec((1, d_inner)), _full_spec(expand3.shape),
                  _full_spec(shift.shape)],
        out_specs=row_spec(d_inner),
        out_shape=jax.ShapeDtypeStruct((m, d_inner), BF16),
        scratch_shapes=[pltpu.VMEM((halo + CHUNK, conv_ch), BF16),
                        pltpu.VMEM((n_heads // 2, SSD_STATE, 2 * SSD_HEAD_DIM), F32)],
        compiler_params=_params(("parallel", "arbitrary")),
        name="ssd_mixer",
    )(z, xbc, dt_raw, conv_w, conv_b.reshape(1, conv_ch),
      jnp.pad(dt_bias, (0, pad)).reshape(1, LANES), jnp.pad(a_log, (0, pad)).reshape(1, LANES),
      jnp.repeat(d_skip, SSD_HEAD_DIM).reshape(1, d_inner), norm_w.reshape(1, d_inner), expand3, shift)


def _diff_attn_kernel(q_ref, k_ref, v_ref, bias_ref, lam_ref, nw_ref, o_ref,
                      vt_ref, s0_ref, s1_ref, cmax0_ref, cmax1_ref, p_ref, m_ref, alpha_ref, acc_ref,
                      *, lam_init):
    s_refs, cmax_refs = (s0_ref, s1_ref), (cmax0_ref, cmax1_ref)
    t = s0_ref.shape[0]
    tq = q_ref.shape[0]
    head_w = v_ref.shape[1]
    ext_rows = vt_ref.shape[1]
    i = pl.program_id(2)
    q_t = q_ref[...].astype(F32).T.astype(BF16)
    dim = lax.broadcasted_iota(jnp.int32, q_t.shape, 0)
    zero = jnp.zeros_like(q_t)
    qq_t = jnp.concatenate([jnp.where(dim < DIFF_HEAD_DIM, q_t, zero),
                            jnp.where(dim >= DIFF_HEAD_DIM, q_t, zero)], axis=1)

    @pl.when(i == 0)
    def _():
        for j in range(vt_ref.shape[0]):
            vt_ref[j, :head_w, :] = v_ref[j * t:(j + 1) * t, :].astype(F32).T.astype(BF16)
            vt_ref[j, head_w:, :] = jnp.ones((ext_rows - head_w, t), BF16)

    m_ref[...] = jnp.full(m_ref.shape, MASK_VALUE, F32)
    acc_ref[...] = jnp.zeros(acc_ref.shape, F32)

    diag_tile, prev_tile = bias_ref[0], bias_ref[1]
    near_tiles = ((jnp.full((t, t), MASK_VALUE, F32), diag_tile),
                  (diag_tile, prev_tile),
                  (prev_tile, jnp.zeros((t, t), F32)))
    near_blocks = (2 * i + 1, 2 * i, jnp.maximum(2 * i - 1, 0))

    def raw_logits(block):
        kb = k_ref[pl.ds(pl.multiple_of(block * t, t), t), :]
        return jnp.dot(kb, qq_t, preferred_element_type=F32)

    def store_logits(s, slot):
        s_refs[slot][...] = s
        chains = [s[0:SUBLANES, :], s[SUBLANES:2 * SUBLANES, :]]
        for n, r in enumerate(range(2 * SUBLANES, t, SUBLANES)):
            chains[n % 2] = jnp.maximum(chains[n % 2], s[r:r + SUBLANES, :])
        col_max = jnp.maximum(chains[0], chains[1])
        for shift in (4, 2, 1):
            col_max = jnp.maximum(col_max, pltpu.roll(col_max, shift, 0))
        cmax_refs[slot][...] = col_max

    def near_logits(visit, slot):
        bias = jnp.concatenate(near_tiles[visit] * 2, axis=1)
        store_logits(raw_logits(near_blocks[visit]) + bias, slot)

    def far_logits(visit, slot):
        store_logits(raw_logits(visit - len(near_blocks)), slot)

    def block_of(visit):
        return near_blocks[visit] if isinstance(visit, int) else visit - len(near_blocks)

    def softmax_accumulate(visit, slot):
        s_ref = s_refs[slot]
        packed = 2 * SUBLANES
        m_old = m_ref[...]
        m_new = jnp.maximum(m_old, cmax_refs[slot][...])
        m_ref[...] = m_new
        alpha_ref[...] = jnp.exp2(m_old - m_new)
        m_tile = jnp.concatenate([m_new, m_new], axis=0)
        for r in range(0, t, packed):
            p_ref[r:r + packed, :] = jnp.exp2(s_ref[r:r + packed, :] - m_tile).astype(BF16)
        pv = jnp.dot(vt_ref[block_of(visit)], p_ref[...], preferred_element_type=F32)
        alpha = alpha_ref[...]
        for r in range(0, ext_rows, SUBLANES):
            acc_ref[r:r + SUBLANES, :] = alpha * acc_ref[r:r + SUBLANES, :] + pv[r:r + SUBLANES, :]

    def pipeline_step(visit, slot):
        far_logits(visit, slot)
        softmax_accumulate(visit - 1, 1 - slot)

    near_logits(0, 0)
    near_logits(1, 1)
    softmax_accumulate(0, 0)
    near_logits(2, 0)
    softmax_accumulate(1, 1)

    @pl.when(i > 0)
    def _():
        far_logits(3, 1)
        softmax_accumulate(2, 0)
        last = 2 * i + 1

        def pair(k, _):
            pipeline_step(4 + 2 * k, 0)

            @pl.when(5 + 2 * k <= last)
            def _():
                pipeline_step(5 + 2 * k, 1)

            return 0

        lax.fori_loop(0, i - 1, pair, 0)
        softmax_accumulate(last, 1)

    lam = lam_ref[...]
    lam_full = (jnp.exp(jnp.sum(lam[0:1] * lam[1:2], axis=-1, keepdims=True))
                - jnp.exp(jnp.sum(lam[2:3] * lam[3:4], axis=-1, keepdims=True)) + lam_init)
    inv_sum = 1.0 / acc_ref[head_w:head_w + SUBLANES, :]
    out_t = acc_ref[:head_w, :] * jnp.tile(inv_sum, (head_w // SUBLANES, 1))
    out = (out_t[:, :tq] - lam_full * out_t[:, tq:]).T
    o_ref[...] = (_rms(out) * nw_ref[...] * (1.0 - lam_init)).astype(o_ref.dtype)


def diff_attention(q, k, v, bias, lam, subln_w, batch, seq, lam_init):
    m, width = q.shape
    head_w = 2 * DIFF_HEAD_DIM
    n_heads = width // head_w
    t = ATTN_BLOCK
    tq = 2 * t
    nq = seq // tq
    cols = 2 * tq
    ext_rows = head_w + 2 * SUBLANES
    return pl.pallas_call(
        functools.partial(_diff_attn_kernel, lam_init=lam_init),
        grid=(batch, n_heads, nq),
        in_specs=[pl.BlockSpec((tq, head_w), lambda b, h, i: (b * nq + i, h)),
                  pl.BlockSpec((seq, head_w), lambda b, h, i: (b, h)),
                  pl.BlockSpec((seq, head_w), lambda b, h, i: (b, h)),
                  pl.BlockSpec((None, 2, t, t), lambda b, h, i: (h, 0, 0, 0)),
                  _full_spec(lam.shape), _full_spec((1, head_w))],
        out_specs=pl.BlockSpec((tq, head_w), lambda b, h, i: (b * nq + i, h)),
        out_shape=jax.ShapeDtypeStruct((m, width), BF16),
        scratch_shapes=[pltpu.VMEM((seq // t, ext_rows, t), BF16),
                        pltpu.VMEM((t, cols), F32), pltpu.VMEM((t, cols), F32),
                        pltpu.VMEM((SUBLANES, cols), F32), pltpu.VMEM((SUBLANES, cols), F32),
                        pltpu.VMEM((t, cols), BF16),
                        pltpu.VMEM((SUBLANES, cols), F32),
                        pltpu.VMEM((SUBLANES, cols), F32),
                        pltpu.VMEM((ext_rows, cols), F32)],
        compiler_params=_params(("parallel", "parallel", "arbitrary")),
        name="diff_attention",
    )(q, k, v, bias, lam, subln_w.reshape(1, head_w))


def _retention_kernel(q_ref, k_ref, v_ref, g_ref, cos_ref, sin_ref, decay_ref, zeta_ref, xi_ref, o_ref,
                      state_ref, *, chunk_decay):
    @pl.when(pl.program_id(1) == 0)
    def _():
        state_ref[...] = jnp.zeros_like(state_ref)

    half = RET_QK_DIM // 2
    for h in range(RET_HEADS):
        qs = slice(h * RET_QK_DIM, (h + 1) * RET_QK_DIM)
        vs = slice(h * RET_V_DIM, (h + 1) * RET_V_DIM)
        st = state_ref[h]
        for r0 in range(0, q_ref.shape[0], CHUNK):
            rows = slice(r0, r0 + CHUNK)
            cos, sin = cos_ref[rows, :], sin_ref[rows, :]

            def rope(t):
                return t * cos + pltpu.roll(t, half, 1) * sin

            qr = rope(q_ref[rows, qs].astype(F32))
            kr = rope(k_ref[rows, qs].astype(F32)) * (RET_QK_DIM ** -0.5)
            vh = v_ref[rows, vs]
            scores = lax.dot_general(qr.astype(BF16), kr.astype(BF16), NT_DIMS,
                                     preferred_element_type=F32) * decay_ref[h]
            inner = jnp.dot(scores.astype(BF16), vh, preferred_element_type=F32)
            cross = jnp.dot((qr * xi_ref[h]).astype(BF16), st.astype(BF16), preferred_element_type=F32)
            kv = jnp.dot((kr.T * zeta_ref[h]).astype(BF16), vh, preferred_element_type=F32)
            st = st * chunk_decay[h] + kv
            out = _rms(inner + cross) * _silu(g_ref[rows, vs].astype(F32))
            o_ref[rows, vs] = out.astype(o_ref.dtype)
        state_ref[h] = st


def retention(rq, rk, rv, rg, batch, seq, chunks_per_step=8):
    m = rq.shape[0]
    rows = chunks_per_step * CHUNK
    nc = seq // rows
    half = RET_QK_DIM // 2
    pos = jnp.arange(seq, dtype=F32)
    inv = ROPE_BASE ** (-jnp.arange(half, dtype=F32) / half)
    ang = pos[:, None] * inv[None]
    cos = jnp.concatenate([jnp.cos(ang), jnp.cos(ang)], axis=-1)
    sin = jnp.concatenate([-jnp.sin(ang), jnp.sin(ang)], axis=-1)
    log_gamma = jnp.log(1.0 - 2.0 ** (-5.0 - jnp.arange(RET_HEADS, dtype=F32)))
    idx = jnp.arange(CHUNK, dtype=F32)
    rel = idx[:, None] - idx[None, :]
    decay = jnp.where(rel >= 0, jnp.exp(jnp.maximum(rel, 0.0)[None] * log_gamma[:, None, None]), 0.0)
    zeta = jnp.exp((CHUNK - 1 - idx)[None] * log_gamma[:, None])[:, None, :]
    xi = jnp.broadcast_to(jnp.exp((idx + 1.0)[None] * log_gamma[:, None])[:, :, None],
                          (RET_HEADS, CHUNK, RET_QK_DIM))
    gamma = 1.0 - 2.0 ** (-5.0 - np.arange(RET_HEADS, dtype=np.float64))
    chunk_decay = tuple(float(g ** CHUNK) for g in gamma)
    row_spec = lambda width: pl.BlockSpec((rows, width), lambda b, c: (b * nc + c, 0))
    return pl.pallas_call(
        functools.partial(_retention_kernel, chunk_decay=chunk_decay),
        grid=(batch, nc),
        in_specs=[row_spec(rq.shape[1]), row_spec(rk.shape[1]), row_spec(rv.shape[1]), row_spec(rg.shape[1]),
                  pl.BlockSpec((rows, RET_QK_DIM), lambda b, c: (c, 0)),
                  pl.BlockSpec((rows, RET_QK_DIM), lambda b, c: (c, 0)),
                  _full_spec(decay.shape), _full_spec(zeta.shape), _full_spec(xi.shape)],
        out_specs=row_spec(rv.shape[1]),
        out_shape=jax.ShapeDtypeStruct((m, rv.shape[1]), BF16),
        scratch_shapes=[pltpu.VMEM((RET_HEADS, RET_QK_DIM, RET_V_DIM), F32)],
        compiler_params=_params(("parallel", "arbitrary")),
        name="retention",
    )(rq, rk, rv, rg, cos, sin, decay, zeta, xi)


def _swa_kernel(sink_ref, q_ref, kp_ref, kc_ref, vp_ref, vc_ref, bias_ref, o_ref):
    i = pl.program_id(1)
    blk = SWA_BLOCK
    rep = SWA_HEADS // SWA_KV_HEADS
    k_all = jnp.concatenate([kp_ref[...], kc_ref[...]], axis=0)
    v_all = jnp.concatenate([vp_ref[...], vc_ref[...]], axis=0)
    col = lax.broadcasted_iota(jnp.int32, (blk, 2 * blk), 1)
    lane = lax.broadcasted_iota(jnp.int32, (blk, LANES), 1)
    lower = lane < SWA_HEAD_DIM
    for r0 in range(0, q_ref.shape[0], blk):
        rows = slice(r0, r0 + blk)
        kb = k_all[r0:r0 + 2 * blk]
        vb = v_all[r0:r0 + 2 * blk]
        outs = []
        for p in range(SWA_HEADS // 2):
            qp = q_ref[rows, p * LANES:(p + 1) * LANES].astype(F32)
            pair_out = []
            for parity in range(2):
                h = 2 * p + parity
                g = h // rep
                qh = qp if parity == g else pltpu.roll(qp, SWA_HEAD_DIM, 1)
                qh = jnp.where(lower if g == 0 else jnp.logical_not(lower), qh, 0.0).astype(BF16)
                s = lax.dot_general(qh, kb, NT_DIMS, preferred_element_type=F32) + bias_ref[h]
                if r0 == 0:
                    s = jnp.where(jnp.logical_and(col < blk, i == 0), MASK_VALUE, s)
                sink = sink_ref[h]
                m = jnp.maximum(jnp.max(s, axis=-1, keepdims=True), sink)
                e = jnp.exp(s - m)
                denom = jnp.sum(e, axis=-1, keepdims=True) + jnp.exp(sink - m)
                o = jnp.dot(e.astype(BF16), vb, preferred_element_type=F32) / denom
                pair_out.append(o if parity == g else pltpu.roll(o, SWA_HEAD_DIM, 1))
            outs.append(jnp.where(lower, pair_out[0], pair_out[1]))
        o_ref[rows, :] = jnp.concatenate(outs, axis=-1).astype(o_ref.dtype)


def sliding_window_attention(sq, sk, sv, sinks, bias, batch, seq, blocks_per_step=2):
    m, qw = sq.shape
    kw = sk.shape[1]
    rows = blocks_per_step * SWA_BLOCK
    nb = seq // rows
    cur = lambda b, i: (b * nb + i, 0)
    prev = lambda b, i: ((b * nb + i) * blocks_per_step - jnp.minimum(i, 1), 0)
    return pl.pallas_call(
        _swa_kernel,
        grid=(batch, nb),
        in_specs=[pl.BlockSpec(memory_space=pltpu.SMEM),
                  pl.BlockSpec((rows, qw), cur),
                  pl.BlockSpec((SWA_BLOCK, kw), prev), pl.BlockSpec((rows, kw), cur),
                  pl.BlockSpec((SWA_BLOCK, kw), prev), pl.BlockSpec((rows, kw), cur),
                  _full_spec(bias.shape)],
        out_specs=pl.BlockSpec((rows, qw), cur),
        out_shape=jax.ShapeDtypeStruct((m, qw), BF16),
        compiler_params=_params(("parallel", "parallel")),
        name="sliding_window_attention",
    )(sinks, sq, sk, sk, sv, sv, bias)


def _split_cols(w, sizes):
    offs = np.cumsum((0,) + tuple(sizes))
    return [w[:, offs[j]:offs[j + 1]] for j in range(len(sizes))]


def even_layer_mixer(x2, mod, g_pre, g_post, w_in, conv_w, conv_b, dt_bias, a_log, d_skip, ssd_norm, lam,
                     diff_norm, w_out, diff_bias, layer_idx, batch, seq):
    d = x2.shape[1]
    d_inner = d
    n_ssd_heads = d_inner // SSD_HEAD_DIM
    conv_ch = d_inner + 2 * SSD_GROUPS * SSD_STATE
    wz, wxbc, wdt, wq, wk, wv = _split_cols(w_in, (d_inner, conv_ch, n_ssd_heads, d, d, d))
    wdt = jnp.pad(wdt, ((0, 0), (0, LANES - n_ssd_heads)))
    wq = wq * (DIFF_HEAD_DIM ** -0.5 * LOG2_E)
    weights = [w.astype(BF16) for w in (wz, wxbc, wdt, wq, wk, wv)]
    z, xbc, dt_raw, q, k, v = norm_proj(x2, mod, g_pre, weights, (BF16, BF16, F32, BF16, BF16, BF16), seq)
    y_ssd = ssd_mixer(z, xbc, dt_raw, conv_w, conv_b, dt_bias, a_log, d_skip, ssd_norm, batch, seq)
    lam_init = 0.8 - 0.6 * math.exp(-0.3 * layer_idx)
    y_diff = diff_attention(q, k, v, diff_bias, lam, diff_norm, batch, seq, lam_init)
    w_out = w_out.astype(BF16)
    return out_proj(x2, mod, g_post, [y_ssd, y_diff], [w_out[:d_inner], w_out[d_inner:]], seq)


def odd_layer_mixer(x2, mod, g_pre, g_post, w_in, sinks, w_out, swa_bias, batch, seq):
    ret_qk = RET_HEADS * RET_QK_DIM
    ret_v = RET_HEADS * RET_V_DIM
    swa_q = SWA_HEADS * SWA_HEAD_DIM
    swa_kv = SWA_KV_HEADS * SWA_HEAD_DIM
    wrq, wrk, wrv, wrg, wsq, wsk, wsv = _split_cols(w_in, (ret_qk, ret_qk, ret_v, ret_v, swa_q, swa_kv, swa_kv))
    wsq = wsq * (SWA_HEAD_DIM ** -0.5)
    weights = [w.astype(BF16) for w in (wrq, wrk, wrv, wrg, wsq, wsk, wsv)]
    rq, rk, rv, rg, sq, sk, sv = norm_proj(x2, mod, g_pre, weights, (BF16,) * 7, seq)
    y_ret = retention(rq, rk, rv, rg, batch, seq)
    y_swa = sliding_window_attention(sq, sk, sv, sinks, swa_bias, batch, seq)
    w_out = w_out.astype(BF16)
    return out_proj(x2, mod, g_post, [y_ret, y_swa], [w_out[:ret_v], w_out[ret_v:]], seq)


def kernel(x, c, rel_bias, norm_gains, mod_w, mod_b, mlp_w1, mlp_w2, e_w_in, e_conv_w, e_conv_b, e_dt_bias,
           e_A_log, e_D, e_ssd_norm, e_lambda, e_diff_norm, e_w_out, o_w_in, o_sinks, o_w_out):
    batch, seq, d = x.shape
    depth = norm_gains.shape[0]
    assert seq % (2 * ATTN_BLOCK) == 0 and d % LANES == 0
    mods = modulation(c, mod_w.reshape(depth * 2, d, 3 * d), mod_b.reshape(depth * 2, 3 * d))
    far_bucket = REL_BUCKETS - 1
    diff_bias = bias_tiles(rel_bias, _diff_bucket_idx(ATTN_BLOCK), shift_bucket=far_bucket, scale=LOG2_E)
    diff_bias = diff_bias.reshape(rel_bias.shape[1], 2, ATTN_BLOCK, ATTN_BLOCK)
    swa_bias = bias_tiles(rel_bias, _swa_bucket_idx())

    x2 = x.reshape(batch * seq, d)
    for layer in range(depth):
        j = layer // 2
        gains = norm_gains[layer]
        if layer % 2 == 0:
            x2 = even_layer_mixer(x2, mods[2 * layer], gains[0], gains[1], e_w_in[j], e_conv_w[j], e_conv_b[j],
                                  e_dt_bias[j], e_A_log[j], e_D[j], e_ssd_norm[j], e_lambda[j], e_diff_norm[j],
                                  e_w_out[j], diff_bias, layer, batch, seq)
        else:
            x2 = odd_layer_mixer(x2, mods[2 * layer], gains[0], gains[1], o_w_in[j], o_sinks[j], o_w_out[j],
                                 swa_bias, batch, seq)
        x2 = mlp(x2, mods[2 * layer + 1], gains[2], gains[3], mlp_w1[layer].astype(BF16),
                 mlp_w2[layer].astype(BF16), seq)
    return x2.reshape(batch, seq, d)
```

```python
import functools
import math

import jax
import jax.numpy as jnp
import numpy as np
from jax import lax
from jax.experimental import pallas as pl
from jax.experimental.pallas import tpu as pltpu

EPS = 1e-6
MASK_VALUE = -1e30
LOG2_E = math.log2(math.e)
LANES = 128
SUBLANES = 8
VMEM_LIMIT = 56 * 1024 * 1024

CHUNK = 128
REL_BUCKETS = 32
REL_MAX_DIST = 128
SSD_HEAD_DIM = 64
SSD_GROUPS = 4
SSD_STATE = 128
SSD_CONV = 4
DIFF_HEAD_DIM = 64
RET_HEADS = 4
RET_QK_DIM = 128
RET_V_DIM = 256
ROPE_BASE = 10000.0
SWA_HEADS = 8
SWA_KV_HEADS = 2
SWA_HEAD_DIM = 64
SWA_BLOCK = 128
ATTN_BLOCK = 512

BF16 = jnp.bfloat16
F32 = jnp.float32
NT_DIMS = (((1,), (1,)), ((), ()))


def _params(semantics, flags=None):
    return pltpu.CompilerParams(dimension_semantics=semantics, vmem_limit_bytes=VMEM_LIMIT, flags=flags)


def _full_spec(shape):
    return pl.BlockSpec(shape, lambda *_: (0,) * len(shape))


def _silu(x):
    h = 0.5 * x
    return h * jnp.tanh(h) + h


def _split3(x):
    hi = x.astype(BF16)
    r1 = x - hi.astype(F32)
    mid = r1.astype(BF16)
    lo = (r1 - mid.astype(F32)).astype(BF16)
    return hi, mid, lo


def _rms(x, eps=EPS):
    return x * lax.rsqrt(jnp.mean(x * x, axis=-1, keepdims=True) + eps)


def _mod_kernel(c_ref, w_ref, b_ref, o_ref):
    c_act = _silu(c_ref[...])
    o_ref[...] = jnp.dot(c_act, w_ref[...], preferred_element_type=F32,
                         precision=lax.Precision.HIGHEST) + b_ref[...]


def modulation(c, mod_w, mod_b):
    b, d = c.shape
    n = mod_w.shape[0]
    rows = SUBLANES
    c_pad = jnp.zeros((rows, d), F32).at[:b].set(c)
    tn = d
    out = pl.pallas_call(
        _mod_kernel,
        grid=(n, 3 * d // tn),
        in_specs=[pl.BlockSpec((rows, d), lambda s, j: (0, 0)),
                  pl.BlockSpec((None, d, tn), lambda s, j: (s, 0, j)),
                  pl.BlockSpec((None, 1, tn), lambda s, j: (s, 0, j))],
        out_specs=pl.BlockSpec((None, rows, tn), lambda s, j: (s, 0, j)),
        out_shape=jax.ShapeDtypeStruct((n, rows, 3 * d), F32),
        compiler_params=_params(("parallel", "parallel")),
        name="modulation",
    )(c_pad, mod_w, mod_b.reshape(n, 1, 3 * d))
    return out[:, :b].reshape(n, b, 3, d)


def _t5_bucket_np(dist):
    max_exact = REL_BUCKETS // 2
    logd = np.log(np.maximum(dist, 1).astype(np.float32) / np.float32(max_exact))
    large = max_exact + (logd / np.float32(math.log(REL_MAX_DIST / max_exact))
                         * np.float32(REL_BUCKETS - max_exact)).astype(np.int32)
    large = np.minimum(large, REL_BUCKETS - 1)
    return np.where(dist < max_exact, dist, large).astype(np.int32)


def _bias_kernel(rb_ref, idx_ref, o_ref, *, shift_bucket, scale, block_buckets):
    h = pl.program_id(0)
    shift = rb_ref[shift_bucket, h] if shift_bucket is not None else 0.0

    def value(bucket):
        return (rb_ref[bucket, h] - shift) * scale

    for (r0, c0), buckets in block_buckets:
        window = (slice(r0, r0 + LANES), slice(c0, c0 + LANES))
        idx = idx_ref[window]
        acc = jnp.full(idx.shape, MASK_VALUE if -1 in buckets else value(max(buckets)), F32)
        for bucket in buckets:
            if bucket >= 0 and len(buckets) > 1:
                acc = jnp.where(idx == bucket, value(bucket), acc)
        o_ref[window] = acc


def bias_tiles(rel_bias, bucket_idx, shift_bucket=None, scale=1.0):
    n_heads = rel_bias.shape[1]
    r, c = bucket_idx.shape
    block_buckets = tuple(((r0, c0), tuple(int(b) for b in np.unique(bucket_idx[r0:r0 + LANES, c0:c0 + LANES])))
                          for r0 in range(0, r, LANES) for c0 in range(0, c, LANES))
    return pl.pallas_call(
        functools.partial(_bias_kernel, shift_bucket=shift_bucket, scale=scale, block_buckets=block_buckets),
        grid=(n_heads,),
        in_specs=[pl.BlockSpec(memory_space=pltpu.SMEM),
                  pl.BlockSpec((r, c), lambda h: (0, 0))],
        out_specs=pl.BlockSpec((None, r, c), lambda h: (h, 0, 0)),
        out_shape=jax.ShapeDtypeStruct((n_heads, r, c), F32),
        compiler_params=_params(("parallel",)),
        name="bias_tiles",
    )(rel_bias, jnp.asarray(bucket_idx))


def _diff_bucket_idx(t):
    q = np.arange(t)[None, :]
    k = np.arange(t)[:, None]
    diag = q - k
    idx_diag = np.where(diag >= 0, _t5_bucket_np(np.maximum(diag, 0)), -1)
    idx_prev = _t5_bucket_np(diag + t)
    assert (_t5_bucket_np(diag + 2 * t) == REL_BUCKETS - 1).all()
    return np.concatenate([idx_diag, idx_prev], axis=0).astype(np.int32)


def _swa_bucket_idx():
    q = np.arange(SWA_BLOCK)[:, None]
    k = np.arange(2 * SWA_BLOCK)[None, :] - SWA_BLOCK
    dist = q - k
    valid = (dist >= 0) & (dist < SWA_BLOCK)
    return np.where(valid, _t5_bucket_np(np.maximum(dist, 0)), -1).astype(np.int32)


def _norm_proj_kernel(x_ref, mod_ref, g_ref, *refs):
    n = len(refs) // 2
    w_refs, o_refs = refs[:n], refs[n:]
    mod = mod_ref[...]
    h = _rms(x_ref[...]) * g_ref[...] * (1.0 + mod[1:2]) + mod[0:1]
    hb = h.astype(BF16)
    for w_ref, o_ref in zip(w_refs, o_refs):
        o_ref[...] = jnp.dot(hb, w_ref[...], preferred_element_type=F32).astype(o_ref.dtype)


def norm_proj(x2, mod, gain, weights, out_dtypes, seq, tm=512):
    m, d = x2.shape
    per_batch = seq // tm
    in_specs = [pl.BlockSpec((tm, d), lambda i: (i, 0)),
                pl.BlockSpec((None, 3, d), lambda i: (i // per_batch, 0, 0)),
                _full_spec((1, d))]
    in_specs += [_full_spec(w.shape) for w in weights]
    out_specs = [pl.BlockSpec((tm, w.shape[1]), lambda i: (i, 0)) for w in weights]
    out_shape = [jax.ShapeDtypeStruct((m, w.shape[1]), dt) for w, dt in zip(weights, out_dtypes)]
    return pl.pallas_call(
        _norm_proj_kernel,
        grid=(m // tm,),
        in_specs=in_specs,
        out_specs=out_specs,
        out_shape=out_shape,
        compiler_params=_params(("parallel",)),
        name="norm_proj",
    )(x2, mod, gain.reshape(1, d), *weights)


def _out_proj_kernel(x_ref, mod_ref, g_ref, *refs):
    n = (len(refs) - 1) // 2
    y_refs, w_refs, o_ref = refs[:n], refs[n:2 * n], refs[2 * n]
    acc = None
    for y_ref, w_ref in zip(y_refs, w_refs):
        part = jnp.dot(y_ref[...], w_ref[...], preferred_element_type=F32)
        acc = part if acc is None else acc + part
    gate = mod_ref[...][2:3]
    o_ref[...] = x_ref[...] + gate * (_rms(acc) * g_ref[...])


def out_proj(x2, mod, gain, ys, weights, seq, tm=512):
    m, d = x2.shape
    per_batch = seq // tm
    in_specs = [pl.BlockSpec((tm, d), lambda i: (i, 0)),
                pl.BlockSpec((None, 3, d), lambda i: (i // per_batch, 0, 0)),
                _full_spec((1, d))]
    in_specs += [pl.BlockSpec((tm, y.shape[1]), lambda i: (i, 0)) for y in ys]
    in_specs += [_full_spec(w.shape) for w in weights]
    return pl.pallas_call(
        _out_proj_kernel,
        grid=(m // tm,),
        in_specs=in_specs,
        out_specs=pl.BlockSpec((tm, d), lambda i: (i, 0)),
        out_shape=jax.ShapeDtypeStruct((m, d), F32),
        compiler_params=_params(("parallel",)),
        name="out_proj",
    )(x2, mod, gain.reshape(1, d), *ys, *weights)


def _mlp_kernel(x_ref, mod_ref, g_pre_ref, g_post_ref, w1_ref, w2_ref, o_ref, *, ff_chunk):
    x = x_ref[...]
    mod = mod_ref[...]
    h = _rms(x) * g_pre_ref[...] * (1.0 + mod[1:2]) + mod[0:1]
    hb = h.astype(BF16)
    d_ff = w1_ref.shape[1]
    acc = None
    for c0 in range(0, d_ff, ff_chunk):
        a = jnp.dot(hb, w1_ref[:, c0:c0 + ff_chunk], preferred_element_type=F32)
        a = jnp.square(jnp.maximum(a, 0.0)).astype(BF16)
        part = jnp.dot(a, w2_ref[c0:c0 + ff_chunk, :], preferred_element_type=F32)
        acc = part if acc is None else acc + part
    o_ref[...] = x + mod[2:3] * (_rms(acc) * g_post_ref[...])


def mlp(x2, mod, g_pre, g_post, w1, w2, seq, tm=512, ff_chunk=1024):
    m, d = x2.shape
    per_batch = seq // tm
    return pl.pallas_call(
        functools.partial(_mlp_kernel, ff_chunk=ff_chunk),
        grid=(m // tm,),
        in_specs=[pl.BlockSpec((tm, d), lambda i: (i, 0)),
                  pl.BlockSpec((None, 3, d), lambda i: (i // per_batch, 0, 0)),
                  _full_spec((1, d)), _full_spec((1, d)),
                  _full_spec(w1.shape), _full_spec(w2.shape)],
        out_specs=pl.BlockSpec((tm, d), lambda i: (i, 0)),
        out_shape=jax.ShapeDtypeStruct((m, d), F32),
        compiler_params=_params(("parallel",)),
        name="mlp",
    )(x2, mod, g_pre.reshape(1, d), g_post.reshape(1, d), w1, w2)


def _ssd_kernel(z_ref, xbc_ref, dt_ref, cw_ref, cb_ref, dtb_ref, alog_ref, dskip_ref, nw_ref, expand_ref,
                shift_ref, o_ref, conv_buf, state_ref, *, d_inner):
    chunk = z_ref.shape[0]
    n_state = SSD_STATE
    gn = SSD_GROUPS * n_state
    pair = 2 * SSD_HEAD_DIM
    heads_per_group = d_inner // SSD_HEAD_DIM // SSD_GROUPS
    halo = conv_buf.shape[0] - chunk

    @pl.when(pl.program_id(1) == 0)
    def _():
        state_ref[...] = jnp.zeros_like(state_ref)
        conv_buf[0:halo, :] = jnp.zeros((halo, conv_buf.shape[1]), BF16)

    cur = xbc_ref[...]
    conv_buf[halo:halo + chunk, :] = cur
    shifted = jnp.dot(shift_ref[...], conv_buf[...], preferred_element_type=F32)
    acc = cb_ref[...] + cw_ref[SSD_CONV - 1:SSD_CONV, :] * cur.astype(F32)
    for tap in range(SSD_CONV - 1):
        acc = acc + cw_ref[tap:tap + 1, :] * shifted[tap * chunk:(tap + 1) * chunk, :]
    conv_buf[0:halo, :] = conv_buf[chunk:chunk + halo, :]
    xbc = _silu(acc)
    xs = xbc[:, :d_inner]
    b_all = xbc[:, d_inner:d_inner + gn]
    c_all = xbc[:, d_inner + gn:]

    dt_in = dt_ref[...] + dtb_ref[...]
    dt = jnp.maximum(dt_in, 0.0) + jnp.log1p(jnp.exp(-jnp.abs(dt_in)))
    a = dt * (-jnp.exp(alog_ref[...]))
    row = lax.broadcasted_iota(jnp.int32, (chunk, chunk), 0)
    col = lax.broadcasted_iota(jnp.int32, (chunk, chunk), 1)
    causal = row >= col
    tril = jnp.where(causal, 1.0, 0.0).astype(BF16)
    parts = jnp.dot(tril, jnp.concatenate(_split3(a), axis=1), preferred_element_type=F32)
    a_cs = parts[:, :LANES] + parts[:, LANES:2 * LANES] + parts[:, 2 * LANES:]
    a_cs_t = a_cs.T
    lhs = jnp.concatenate([jnp.concatenate(_split3(dt), axis=1), jnp.concatenate(_split3(a_cs), axis=1)], axis=0)
    expanded = jnp.dot(lhs, expand_ref[...], preferred_element_type=F32)
    dt_e, acs_e = expanded[:chunk], expanded[chunk:]
    a_last_e = acs_e[chunk - 1:chunk, :]
    x_dt = xs * dt_e
    w_state = (x_dt * jnp.exp(a_last_e - acs_e)).astype(BF16)
    out_scale = jnp.exp(acs_e)
    chunk_decay = jnp.exp(a_last_e)
    x_dt_b = x_dt.astype(BF16)
    lane = lax.broadcasted_iota(jnp.int32, (chunk, pair), 1)
    first_head = lane < SSD_HEAD_DIM

    y_blocks = []
    for g in range(SSD_GROUPS):
        bg = b_all[:, g * n_state:(g + 1) * n_state]
        cg = c_all[:, g * n_state:(g + 1) * n_state].astype(BF16)
        cb = lax.dot_general(cg, bg.astype(BF16), NT_DIMS, preferred_element_type=F32)
        bg_t = bg.T.astype(BF16)
        for pi in range(heads_per_group // 2):
            p = g * (heads_per_group // 2) + pi
            sl = slice(p * pair, (p + 1) * pair)
            xp = x_dt_b[:, sl]
            y_diag = None
            for k in range(2):
                hh = 2 * p + k
                seg = a_cs[:, hh:hh + 1] - a_cs_t[hh:hh + 1, :]
                decay = jnp.exp(jnp.where(causal, seg, MASK_VALUE))
                m = (cb * decay).astype(BF16)
                xh = jnp.where(first_head if k == 0 else jnp.logical_not(first_head), xp, jnp.zeros_like(xp))
                part = jnp.dot(m, xh, preferred_element_type=F32)
                y_diag = part if y_diag is None else y_diag + part
            st = state_ref[p]
            y_off = jnp.dot(cg, st.astype(BF16), preferred_element_type=F32) * out_scale[:, sl]
            state_ref[p] = st * chunk_decay[:, sl] + jnp.dot(bg_t, w_state[:, sl], preferred_element_type=F32)
            y_blocks.append(y_diag + y_off)
    y = jnp.concatenate(y_blocks, axis=-1) + xs * dskip_ref[...]
    y = y * _silu(z_ref[...].astype(F32))
    o_ref[...] = (_rms(y) * nw_ref[...]).astype(o_ref.dtype)


def ssd_mixer(z, xbc, dt_raw, conv_w, conv_b, dt_bias, a_log, d_skip, norm_w, batch, seq):
    m, d_inner = z.shape
    conv_ch = xbc.shape[1]
    n_heads = d_inner // SSD_HEAD_DIM
    nc = seq // CHUNK
    pad = LANES - n_heads
    expand = np.zeros((LANES, d_inner), np.float32)
    for h in range(n_heads):
        expand[h, h * SSD_HEAD_DIM:(h + 1) * SSD_HEAD_DIM] = 1.0
    expand3 = jnp.asarray(np.tile(expand, (3, 1)), BF16)
    halo = 2 * SUBLANES
    shift = np.zeros(((SSD_CONV - 1) * CHUNK, halo + CHUNK), np.float32)
    for tap in range(SSD_CONV - 1):
        shift[tap * CHUNK + np.arange(CHUNK), halo - (SSD_CONV - 1) + tap + np.arange(CHUNK)] = 1.0
    shift = jnp.asarray(shift, BF16)
    row_spec = lambda width: pl.BlockSpec((CHUNK, width), lambda b, c: (b * nc + c, 0))
    return pl.pallas_call(
        functools.partial(_ssd_kernel, d_inner=d_inner),
        grid=(batch, nc),
        in_specs=[row_spec(d_inner), row_spec(conv_ch), row_spec(LANES),
                  _full_spec((SSD_CONV, conv_ch)), _full_spec((1, conv_ch)),
                  _full_spec((1, LANES)), _full_spec((1, LANES)),
                  _full_spec((1, d_inner)), _full_spec((1, d_inner)), _full_spec(expand3.shape),
                  _full_spec(shift.shape)],
        out_specs=row_spec(d_inner),
        out_shape=jax.ShapeDtypeStruct((m, d_inner), BF16),
        scratch_shapes=[pltpu.VMEM((halo + CHUNK, conv_ch), BF16),
                        pltpu.VMEM((n_heads // 2, SSD_STATE, 2 * SSD_HEAD_DIM), F32)],
        compiler_params=_params(("parallel", "arbitrary")),
        name="ssd_mixer",
    )(z, xbc, dt_raw, conv_w, conv_b.reshape(1, conv_ch),
      jnp.pad(dt_bias, (0, pad)).reshape(1, LANES), jnp.pad(a_log, (0, pad)).reshape(1, LANES),
      jnp.repeat(d_skip, SSD_HEAD_DIM).reshape(1, d_inner), norm_w.reshape(1, d_inner), expand3, shift)


def _diff_attn_kernel(q_ref, k_ref, v_ref, bias_ref, lam_ref, nw_ref, o_ref,
                      vt_ref, s0_ref, s1_ref, cmax0_ref, cmax1_ref, p_ref, m_ref, alpha_ref, acc_ref,
                      *, lam_init):
    s_refs, cmax_refs = (s0_ref, s1_ref), (cmax0_ref, cmax1_ref)
    t = s0_ref.shape[0]
    tq = q_ref.shape[0]
    head_w = v_ref.shape[1]
    ext_rows = vt_ref.shape[1]
    i = pl.program_id(2)
    q_t = q_ref[...].astype(F32).T.astype(BF16)
    dim = lax.broadcasted_iota(jnp.int32, q_t.shape, 0)
    zero = jnp.zeros_like(q_t)
    qq_t = jnp.concatenate([jnp.where(dim < DIFF_HEAD_DIM, q_t, zero),
                            jnp.where(dim >= DIFF_HEAD_DIM, q_t, zero)], axis=1)

    @pl.when(i == 0)
    def _():
        for j in range(vt_ref.shape[0]):
            vt_ref[j, :head_w, :] = v_ref[j * t:(j + 1) * t, :].astype(F32).T.astype(BF16)
            vt_ref[j, head_w:, :] = jnp.ones((ext_rows - head_w, t), BF16)

    m_ref[...] = jnp.full(m_ref.shape, MASK_VALUE, F32)
    acc_ref[...] = jnp.zeros(acc_ref.shape, F32)

    diag_tile, prev_tile = bias_ref[0], bias_ref[1]
    all_cols = ((0, t), (t, t), (tq, t), (tq + t, t))
    near_cols = (all_cols[1::2], all_cols, all_cols)
    near_tiles = ((diag_tile, diag_tile),
                  (diag_tile, prev_tile, diag_tile, prev_tile),
                  (prev_tile, None, prev_tile, None))
    near_blocks = (2 * i + 1, 2 * i, jnp.maximum(2 * i - 1, 0))

    def raw_logits(block, cols=all_cols):
        kb = k_ref[pl.ds(pl.multiple_of(block * t, t), t), :]
        rhs = qq_t if cols is all_cols else jnp.concatenate([qq_t[:, c0:c0 + w] for c0, w in cols], axis=1)
        return jnp.dot(kb, rhs, preferred_element_type=F32)

    def store_logits(pieces, slot, cols=all_cols):
        for piece, (c0, w) in zip(pieces, cols):
            s_refs[slot][:, c0:c0 + w] = piece
            chains = [piece[0:SUBLANES, :], piece[SUBLANES:2 * SUBLANES, :]]
            for n, r in enumerate(range(2 * SUBLANES, t, SUBLANES)):
                chains[n % 2] = jnp.maximum(chains[n % 2], piece[r:r + SUBLANES, :])
            col_max = jnp.maximum(chains[0], chains[1])
            for shift in (4, 2, 1):
                col_max = jnp.maximum(col_max, pltpu.roll(col_max, shift, 0))
            cmax_refs[slot][:, c0:c0 + w] = col_max

    def split(s, cols):
        widths = np.cumsum([0] + [w for _, w in cols])
        return [s[:, widths[n]:widths[n + 1]] for n in range(len(cols))]

    def near_logits(visit, slot):
        cols = near_cols[visit]
        pieces = split(raw_logits(near_blocks[visit], cols), cols)
        pieces = [p if tile is None else p + tile for p, tile in zip(pieces, near_tiles[visit])]
        store_logits(pieces, slot, cols)

    def far_logits(visit, slot):
        store_logits(split(raw_logits(visit - len(near_blocks)), all_cols), slot)

    def block_of(visit):
        return near_blocks[visit] if isinstance(visit, int) else visit - len(near_blocks)

    def softmax_accumulate(visit, slot, col_ranges=((0, 2 * tq),)):
        s_ref = s_refs[slot]
        vt = vt_ref[block_of(visit)]
        packed = 2 * SUBLANES
        for c0, w in col_ranges:
            c = slice(c0, c0 + w)
            m_old = m_ref[:, c]
            m_new = jnp.maximum(m_old, cmax_refs[slot][:, c])
            m_ref[:, c] = m_new
            alpha_ref[:, c] = jnp.exp2(m_old - m_new)
            m_tile = jnp.concatenate([m_new, m_new], axis=0)
            for r in range(0, t, packed):
                p_ref[r:r + packed, c] = jnp.exp2(s_ref[r:r + packed, c] - m_tile).astype(BF16)
            pv = jnp.dot(vt, p_ref[:, c], preferred_element_type=F32)
            alpha = alpha_ref[:, c]
            for r in range(0, ext_rows, SUBLANES):
                acc_ref[r:r + SUBLANES, c] = alpha * acc_ref[r:r + SUBLANES, c] + pv[r:r + SUBLANES, :]

    def pipeline_step(visit, slot):
        far_logits(visit, slot)
        softmax_accumulate(visit - 1, 1 - slot)

    near_logits(0, 0)
    near_logits(1, 1)
    softmax_accumulate(0, 0, near_cols[0])
    near_logits(2, 0)
    softmax_accumulate(1, 1)

    @pl.when(i > 0)
    def _():
        far_logits(3, 1)
        softmax_accumulate(2, 0)
        last = 2 * i + 1

        def pair(k, _):
            pipeline_step(4 + 2 * k, 0)

            @pl.when(5 + 2 * k <= last)
            def _():
                pipeline_step(5 + 2 * k, 1)

            return 0

        lax.fori_loop(0, i - 1, pair, 0)
        softmax_accumulate(last, 1)

    lam = lam_ref[...]
    lam_full = (jnp.exp(jnp.sum(lam[0:1] * lam[1:2], axis=-1, keepdims=True))
                - jnp.exp(jnp.sum(lam[2:3] * lam[3:4], axis=-1, keepdims=True)) + lam_init)
    inv_sum = 1.0 / acc_ref[head_w:head_w + SUBLANES, :]
    out_t = acc_ref[:head_w, :] * jnp.tile(inv_sum, (head_w // SUBLANES, 1))
    out = (out_t[:, :tq] - lam_full * out_t[:, tq:]).T
    o_ref[...] = (_rms(out) * nw_ref[...] * (1.0 - lam_init)).astype(o_ref.dtype)


def diff_attention(q, k, v, bias, lam, subln_w, batch, seq, lam_init):
    m, width = q.shape
    head_w = 2 * DIFF_HEAD_DIM
    n_heads = width // head_w
    t = ATTN_BLOCK
    tq = 2 * t
    nq = seq // tq
    cols = 2 * tq
    ext_rows = head_w + 2 * SUBLANES
    return pl.pallas_call(
        functools.partial(_diff_attn_kernel, lam_init=lam_init),
        grid=(batch, n_heads, nq),
        in_specs=[pl.BlockSpec((tq, head_w), lambda b, h, i: (b * nq + i, h)),
                  pl.BlockSpec((seq, head_w), lambda b, h, i: (b, h)),
                  pl.BlockSpec((seq, head_w), lambda b, h, i: (b, h)),
                  pl.BlockSpec((None, 2, t, t), lambda b, h, i: (h, 0, 0, 0)),
                  _full_spec(lam.shape), _full_spec((1, head_w))],
        out_specs=pl.BlockSpec((tq, head_w), lambda b, h, i: (b * nq + i, h)),
        out_shape=jax.ShapeDtypeStruct((m, width), BF16),
        scratch_shapes=[pltpu.VMEM((seq // t, ext_rows, t), BF16),
                        pltpu.VMEM((t, cols), F32), pltpu.VMEM((t, cols), F32),
                        pltpu.VMEM((SUBLANES, cols), F32), pltpu.VMEM((SUBLANES, cols), F32),
                        pltpu.VMEM((t, cols), BF16),
                        pltpu.VMEM((SUBLANES, cols), F32),
                        pltpu.VMEM((SUBLANES, cols), F32),
                        pltpu.VMEM((ext_rows, cols), F32)],
        compiler_params=_params(("parallel", "parallel", "arbitrary")),
        name="diff_attention",
    )(q, k, v, bias, lam, subln_w.reshape(1, head_w))


def _retention_kernel(q_ref, k_ref, v_ref, g_ref, cos_ref, sin_ref, decay_ref, zeta_ref, xi_ref, o_ref,
                      state_ref, *, chunk_decay):
    @pl.when(pl.program_id(1) == 0)
    def _():
        state_ref[...] = jnp.zeros_like(state_ref)

    half = RET_QK_DIM // 2
    for h in range(RET_HEADS):
        qs = slice(h * RET_QK_DIM, (h + 1) * RET_QK_DIM)
        vs = slice(h * RET_V_DIM, (h + 1) * RET_V_DIM)
        st = state_ref[h]
        for r0 in range(0, q_ref.shape[0], CHUNK):
            rows = slice(r0, r0 + CHUNK)
            cos, sin = cos_ref[rows, :], sin_ref[rows, :]

            def rope(t):
                return t * cos + pltpu.roll(t, half, 1) * sin

            qr = rope(q_ref[rows, qs].astype(F32))
            kr = rope(k_ref[rows, qs].astype(F32)) * (RET_QK_DIM ** -0.5)
            vh = v_ref[rows, vs]
            scores = lax.dot_general(qr.astype(BF16), kr.astype(BF16), NT_DIMS,
                                     preferred_element_type=F32) * decay_ref[h]
            inner = jnp.dot(scores.astype(BF16), vh, preferred_element_type=F32)
            cross = jnp.dot((qr * xi_ref[h]).astype(BF16), st.astype(BF16), preferred_element_type=F32)
            kv = jnp.dot((kr.T * zeta_ref[h]).astype(BF16), vh, preferred_element_type=F32)
            st = st * chunk_decay[h] + kv
            out = _rms(inner + cross) * _silu(g_ref[rows, vs].astype(F32))
            o_ref[rows, vs] = out.astype(o_ref.dtype)
        state_ref[h] = st


def retention(rq, rk, rv, rg, batch, seq, chunks_per_step=8):
    m = rq.shape[0]
    rows = chunks_per_step * CHUNK
    nc = seq // rows
    half = RET_QK_DIM // 2
    pos = jnp.arange(seq, dtype=F32)
    inv = ROPE_BASE ** (-jnp.arange(half, dtype=F32) / half)
    ang = pos[:, None] * inv[None]
    cos = jnp.concatenate([jnp.cos(ang), jnp.cos(ang)], axis=-1)
    sin = jnp.concatenate([-jnp.sin(ang), jnp.sin(ang)], axis=-1)
    log_gamma = jnp.log(1.0 - 2.0 ** (-5.0 - jnp.arange(RET_HEADS, dtype=F32)))
    idx = jnp.arange(CHUNK, dtype=F32)
    rel = idx[:, None] - idx[None, :]
    decay = jnp.where(rel >= 0, jnp.exp(jnp.maximum(rel, 0.0)[None] * log_gamma[:, None, None]), 0.0)
    zeta = jnp.exp((CHUNK - 1 - idx)[None] * log_gamma[:, None])[:, None, :]
    xi = jnp.broadcast_to(jnp.exp((idx + 1.0)[None] * log_gamma[:, None])[:, :, None],
                          (RET_HEADS, CHUNK, RET_QK_DIM))
    gamma = 1.0 - 2.0 ** (-5.0 - np.arange(RET_HEADS, dtype=np.float64))
    chunk_decay = tuple(float(g ** CHUNK) for g in gamma)
    row_spec = lambda width: pl.BlockSpec((rows, width), lambda b, c: (b * nc + c, 0))
    return pl.pallas_call(
        functools.partial(_retention_kernel, chunk_decay=chunk_decay),
        grid=(batch, nc),
        in_specs=[row_spec(rq.shape[1]), row_spec(rk.shape[1]), row_spec(rv.shape[1]), row_spec(rg.shape[1]),
                  pl.BlockSpec((rows, RET_QK_DIM), lambda b, c: (c, 0)),
                  pl.BlockSpec((rows, RET_QK_DIM), lambda b, c: (c, 0)),
                  _full_spec(decay.shape), _full_spec(zeta.shape), _full_spec(xi.shape)],
        out_specs=row_spec(rv.shape[1]),
        out_shape=jax.ShapeDtypeStruct((m, rv.shape[1]), BF16),
        scratch_shapes=[pltpu.VMEM((RET_HEADS, RET_QK_DIM, RET_V_DIM), F32)],
        compiler_params=_params(("parallel", "arbitrary")),
        name="retention",
    )(rq, rk, rv, rg, cos, sin, decay, zeta, xi)


def _swa_kernel(sink_ref, q_ref, kp_ref, kc_ref, vp_ref, vc_ref, bias_ref, o_ref):
    i = pl.program_id(1)
    blk = SWA_BLOCK
    rep = SWA_HEADS // SWA_KV_HEADS
    k_all = jnp.concatenate([kp_ref[...], kc_ref[...]], axis=0)
    v_all = jnp.concatenate([vp_ref[...], vc_ref[...]], axis=0)
    col = lax.broadcasted_iota(jnp.int32, (blk, 2 * blk), 1)
    lane = lax.broadcasted_iota(jnp.int32, (blk, LANES), 1)
    lower = lane < SWA_HEAD_DIM
    for r0 in range(0, q_ref.shape[0], blk):
        rows = slice(r0, r0 + blk)
        kb = k_all[r0:r0 + 2 * blk]
        vb = v_all[r0:r0 + 2 * blk]
        outs = []
        for p in range(SWA_HEADS // 2):
            qp = q_ref[rows, p * LANES:(p + 1) * LANES].astype(F32)
            pair_out = []
            for parity in range(2):
                h = 2 * p + parity
                g = h // rep
                qh = qp if parity == g else pltpu.roll(qp, SWA_HEAD_DIM, 1)
                qh = jnp.where(lower if g == 0 else jnp.logical_not(lower), qh, 0.0).astype(BF16)
                s = lax.dot_general(qh, kb, NT_DIMS, preferred_element_type=F32) + bias_ref[h]
                if r0 == 0:
                    s = jnp.where(jnp.logical_and(col < blk, i == 0), MASK_VALUE, s)
                sink = sink_ref[h]
                m = jnp.maximum(jnp.max(s, axis=-1, keepdims=True), sink)
                e = jnp.exp(s - m)
                denom = jnp.sum(e, axis=-1, keepdims=True) + jnp.exp(sink - m)
                o = jnp.dot(e.astype(BF16), vb, preferred_element_type=F32) / denom
                pair_out.append(o if parity == g else pltpu.roll(o, SWA_HEAD_DIM, 1))
            outs.append(jnp.where(lower, pair_out[0], pair_out[1]))
        o_ref[rows, :] = jnp.concatenate(outs, axis=-1).astype(o_ref.dtype)


def sliding_window_attention(sq, sk, sv, sinks, bias, batch, seq, blocks_per_step=2):
    m, qw = sq.shape
    kw = sk.shape[1]
    rows = blocks_per_step * SWA_BLOCK
    nb = seq // rows
    cur = lambda b, i: (b * nb + i, 0)
    prev = lambda b, i: ((b * nb + i) * blocks_per_step - jnp.minimum(i, 1), 0)
    return pl.pallas_call(
        _swa_kernel,
        grid=(batch, nb),
        in_specs=[pl.BlockSpec(memory_space=pltpu.SMEM),
                  pl.BlockSpec((rows, qw), cur),
                  pl.BlockSpec((SWA_BLOCK, kw), prev), pl.BlockSpec((rows, kw), cur),
                  pl.BlockSpec((SWA_BLOCK, kw), prev), pl.BlockSpec((rows, kw), cur),
                  _full_spec(bias.shape)],
        out_specs=pl.BlockSpec((rows, qw), cur),
        out_shape=jax.ShapeDtypeStruct((m, qw), BF16),
        compiler_params=_params(("parallel", "parallel")),
        name="sliding_window_attention",
    )(sinks, sq, sk, sk, sv, sv, bias)


def _split_cols(w, sizes):
    offs = np.cumsum((0,) + tuple(sizes))
    return [w[:, offs[j]:offs[j + 1]] for j in range(len(sizes))]


def even_layer_mixer(x2, mod, g_pre, g_post, w_in, conv_w, conv_b, dt_bias, a_log, d_skip, ssd_norm, lam,
                     diff_norm, w_out, diff_bias, layer_idx, batch, seq):
    d = x2.shape[1]
    d_inner = d
    n_ssd_heads = d_inner // SSD_HEAD_DIM
    conv_ch = d_inner + 2 * SSD_GROUPS * SSD_STATE
    wz, wxbc, wdt, wq, wk, wv = _split_cols(w_in, (d_inner, conv_ch, n_ssd_heads, d, d, d))
    wdt = jnp.pad(wdt, ((0, 0), (0, LANES - n_ssd_heads)))
    wq = wq * (DIFF_HEAD_DIM ** -0.5 * LOG2_E)
    weights = [w.astype(BF16) for w in (wz, wxbc, wdt, wq, wk, wv)]
    z, xbc, dt_raw, q, k, v = norm_proj(x2, mod, g_pre, weights, (BF16, BF16, F32, BF16, BF16, BF16), seq)
    y_ssd = ssd_mixer(z, xbc, dt_raw, conv_w, conv_b, dt_bias, a_log, d_skip, ssd_norm, batch, seq)
    lam_init = 0.8 - 0.6 * math.exp(-0.3 * layer_idx)
    y_diff = diff_attention(q, k, v, diff_bias, lam, diff_norm, batch, seq, lam_init)
    w_out = w_out.astype(BF16)
    return out_proj(x2, mod, g_post, [y_ssd, y_diff], [w_out[:d_inner], w_out[d_inner:]], seq)


def odd_layer_mixer(x2, mod, g_pre, g_post, w_in, sinks, w_out, swa_bias, batch, seq):
    ret_qk = RET_HEADS * RET_QK_DIM
    ret_v = RET_HEADS * RET_V_DIM
    swa_q = SWA_HEADS * SWA_HEAD_DIM
    swa_kv = SWA_KV_HEADS * SWA_HEAD_DIM
    wrq, wrk, wrv, wrg, wsq, wsk, wsv = _split_cols(w_in, (ret_qk, ret_qk, ret_v, ret_v, swa_q, swa_kv, swa_kv))
    wsq = wsq * (SWA_HEAD_DIM ** -0.5)
    weights = [w.astype(BF16) for w in (wrq, wrk, wrv, wrg, wsq, wsk, wsv)]
    rq, rk, rv, rg, sq, sk, sv = norm_proj(x2, mod, g_pre, weights, (BF16,) * 7, seq)
    y_ret = retention(rq, rk, rv, rg, batch, seq)
    y_swa = sliding_window_attention(sq, sk, sv, sinks, swa_bias, batch, seq)
    w_out = w_out.astype(BF16)
    return out_proj(x2, mod, g_post, [y_ret, y_swa], [w_out[:ret_v], w_out[ret_v:]], seq)


def kernel(x, c, rel_bias, norm_gains, mod_w, mod_b, mlp_w1, mlp_w2, e_w_in, e_conv_w, e_conv_b, e_dt_bias,
           e_A_log, e_D, e_ssd_norm, e_lambda, e_diff_norm, e_w_out, o_w_in, o_sinks, o_w_out):
    batch, seq, d = x.shape
    depth = norm_gains.shape[0]
    assert seq % (2 * ATTN_BLOCK) == 0 and d % LANES == 0
    mods = modulation(c, mod_w.reshape(depth * 2, d, 3 * d), mod_b.reshape(depth * 2, 3 * d))
    far_bucket = REL_BUCKETS - 1
    diff_bias = bias_tiles(rel_bias, _diff_bucket_idx(ATTN_BLOCK), shift_bucket=far_bucket, scale=LOG2_E)
    diff_bias = diff_bias.reshape(rel_bias.shape[1], 2, ATTN_BLOCK, ATTN_BLOCK)
    swa_bias = bias_tiles(rel_bias, _swa_bucket_idx())

    x2 = x.reshape(batch * seq, d)
    for layer in range(depth):
        j = layer // 2
        gains = norm_gains[layer]
        if layer % 2 == 0:
            x2 = even_layer_mixer(x2, mods[2 * layer], gains[0], gains[1], e_w_in[j], e_conv_w[j], e_conv_b[j],
                                  e_dt_bias[j], e_A_log[j], e_D[j], e_ssd_norm[j], e_lambda[j], e_diff_norm[j],
                                  e_w_out[j], diff_bias, layer, batch, seq)
        else:
            x2 = odd_layer_mixer(x2, mods[2 * layer], gains[0], gains[1], o_w_in[j], o_sinks[j], o_w_out[j],
                                 swa_bias, batch, seq)
        x2 = mlp(x2, mods[2 * layer + 1], gains[2], gains[3], mlp_w1[layer].astype(BF16),
                 mlp_w2[layer].astype(BF16), seq)
    return x2.reshape(batch, seq, d)
```

```python
import functools
import math

import jax
import jax.numpy as jnp
import numpy as np
from jax import lax
from jax.experimental import pallas as pl
from jax.experimental.pallas import tpu as pltpu

EPS = 1e-6
MASK_VALUE = -1e30
LOG2_E = math.log2(math.e)
LANES = 128
SUBLANES = 8
VMEM_LIMIT = 56 * 1024 * 1024

CHUNK = 128
REL_BUCKETS = 32
REL_MAX_DIST = 128
SSD_HEAD_DIM = 64
SSD_GROUPS = 4
SSD_STATE = 128
SSD_CONV = 4
DIFF_HEAD_DIM = 64
RET_HEADS = 4
RET_QK_DIM = 128
RET_V_DIM = 256
ROPE_BASE = 10000.0
SWA_HEADS = 8
SWA_KV_HEADS = 2
SWA_HEAD_DIM = 64
SWA_BLOCK = 128
ATTN_BLOCK = 512

BF16 = jnp.bfloat16
F32 = jnp.float32
NT_DIMS = (((1,), (1,)), ((), ()))


def _params(semantics, flags=None):
    return pltpu.CompilerParams(dimension_semantics=semantics, vmem_limit_bytes=VMEM_LIMIT, flags=flags)


def _full_spec(shape):
    return pl.BlockSpec(shape, lambda *_: (0,) * len(shape))


def _silu(x):
    h = 0.5 * x
    return h * jnp.tanh(h) + h


def _split3(x):
    hi = x.astype(BF16)
    r1 = x - hi.astype(F32)
    mid = r1.astype(BF16)
    lo = (r1 - mid.astype(F32)).astype(BF16)
    return hi, mid, lo


def _rms(x, eps=EPS):
    return x * lax.rsqrt(jnp.mean(x * x, axis=-1, keepdims=True) + eps)


def _mod_kernel(c_ref, w_ref, b_ref, o_ref):
    c_act = _silu(c_ref[...])
    o_ref[...] = jnp.dot(c_act, w_ref[...], preferred_element_type=F32,
                         precision=lax.Precision.HIGHEST) + b_ref[...]


def modulation(c, mod_w, mod_b):
    b, d = c.shape
    n = mod_b.shape[0]
    per_layer = mod_w.shape[1]
    rows = SUBLANES
    c_pad = jnp.zeros((rows, d), F32).at[:b].set(c)
    tn = d
    out = pl.pallas_call(
        _mod_kernel,
        grid=(n, 3 * d // tn),
        in_specs=[pl.BlockSpec((rows, d), lambda s, j: (0, 0)),
                  pl.BlockSpec((None, None, d, tn), lambda s, j: (s // per_layer, s % per_layer, 0, j)),
                  pl.BlockSpec((None, 1, tn), lambda s, j: (s, 0, j))],
        out_specs=pl.BlockSpec((None, rows, tn), lambda s, j: (s, 0, j)),
        out_shape=jax.ShapeDtypeStruct((n, rows, 3 * d), F32),
        compiler_params=_params(("parallel", "parallel")),
        name="modulation",
    )(c_pad, mod_w, mod_b.reshape(n, 1, 3 * d))
    return out[:, :b].reshape(n, b, 3, d)


def _t5_bucket_np(dist):
    max_exact = REL_BUCKETS // 2
    logd = np.log(np.maximum(dist, 1).astype(np.float32) / np.float32(max_exact))
    large = max_exact + (logd / np.float32(math.log(REL_MAX_DIST / max_exact))
                         * np.float32(REL_BUCKETS - max_exact)).astype(np.int32)
    large = np.minimum(large, REL_BUCKETS - 1)
    return np.where(dist < max_exact, dist, large).astype(np.int32)


def _bias_kernel(rb_ref, idx_ref, o_ref, *, shift_bucket, scale, block_buckets):
    h = pl.program_id(0)
    shift = rb_ref[shift_bucket, h] if shift_bucket is not None else 0.0

    def value(bucket):
        return (rb_ref[bucket, h] - shift) * scale

    for (r0, c0), buckets in block_buckets:
        window = (slice(r0, r0 + LANES), slice(c0, c0 + LANES))
        idx = idx_ref[window]
        acc = jnp.full(idx.shape, MASK_VALUE if -1 in buckets else value(max(buckets)), F32)
        for bucket in buckets:
            if bucket >= 0 and len(buckets) > 1:
                acc = jnp.where(idx == bucket, value(bucket), acc)
        o_ref[window] = acc


def bias_tiles(rel_bias, bucket_idx, shift_bucket=None, scale=1.0):
    n_heads = rel_bias.shape[1]
    r, c = bucket_idx.shape
    block_buckets = tuple(((r0, c0), tuple(int(b) for b in np.unique(bucket_idx[r0:r0 + LANES, c0:c0 + LANES])))
                          for r0 in range(0, r, LANES) for c0 in range(0, c, LANES))
    return pl.pallas_call(
        functools.partial(_bias_kernel, shift_bucket=shift_bucket, scale=scale, block_buckets=block_buckets),
        grid=(n_heads,),
        in_specs=[pl.BlockSpec(memory_space=pltpu.SMEM),
                  pl.BlockSpec((r, c), lambda h: (0, 0))],
        out_specs=pl.BlockSpec((None, r, c), lambda h: (h, 0, 0)),
        out_shape=jax.ShapeDtypeStruct((n_heads, r, c), F32),
        compiler_params=_params(("parallel",)),
        name="bias_tiles",
    )(rel_bias, jnp.asarray(bucket_idx))


def _diff_bucket_idx(t):
    q = np.arange(t)[None, :]
    k = np.arange(t)[:, None]
    diag = q - k
    idx_diag = np.where(diag >= 0, _t5_bucket_np(np.maximum(diag, 0)), -1)
    idx_prev = _t5_bucket_np(diag + t)
    assert (_t5_bucket_np(diag + 2 * t) == REL_BUCKETS - 1).all()
    return np.concatenate([idx_diag, idx_prev], axis=0).astype(np.int32)


def _swa_bucket_idx():
    q = np.arange(SWA_BLOCK)[:, None]
    k = np.arange(2 * SWA_BLOCK)[None, :] - SWA_BLOCK
    dist = q - k
    valid = (dist >= 0) & (dist < SWA_BLOCK)
    return np.where(valid, _t5_bucket_np(np.maximum(dist, 0)), -1).astype(np.int32)


def _norm_proj_kernel(x_ref, mod_ref, g_ref, *refs):
    n = len(refs) // 2
    w_refs, o_refs = refs[:n], refs[n:]
    mod = mod_ref[...]
    h = _rms(x_ref[...]) * g_ref[...] * (1.0 + mod[1:2]) + mod[0:1]
    hb = h.astype(BF16)
    for w_ref, o_ref in zip(w_refs, o_refs):
        o_ref[...] = jnp.dot(hb, w_ref[...], preferred_element_type=F32).astype(o_ref.dtype)


def norm_proj(x2, mod, gain, weights, out_dtypes, seq, tm=512):
    m, d = x2.shape
    per_batch = seq // tm
    in_specs = [pl.BlockSpec((tm, d), lambda i: (i, 0)),
                pl.BlockSpec((None, 3, d), lambda i: (i // per_batch, 0, 0)),
                _full_spec((1, d))]
    in_specs += [_full_spec(w.shape) for w in weights]
    out_specs = [pl.BlockSpec((tm, w.shape[1]), lambda i: (i, 0)) for w in weights]
    out_shape = [jax.ShapeDtypeStruct((m, w.shape[1]), dt) for w, dt in zip(weights, out_dtypes)]
    return pl.pallas_call(
        _norm_proj_kernel,
        grid=(m // tm,),
        in_specs=in_specs,
        out_specs=out_specs,
        out_shape=out_shape,
        compiler_params=_params(("parallel",)),
        name="norm_proj",
    )(x2, mod, gain.reshape(1, d), *weights)


def _out_proj_kernel(x_ref, mod_ref, g_ref, *refs):
    n = (len(refs) - 1) // 2
    y_refs, w_refs, o_ref = refs[:n], refs[n:2 * n], refs[2 * n]
    acc = None
    for y_ref, w_ref in zip(y_refs, w_refs):
        part = jnp.dot(y_ref[...], w_ref[...], preferred_element_type=F32)
        acc = part if acc is None else acc + part
    gate = mod_ref[...][2:3]
    o_ref[...] = x_ref[...] + gate * (_rms(acc) * g_ref[...])


def out_proj(x2, mod, gain, ys, weights, seq, tm=512):
    m, d = x2.shape
    per_batch = seq // tm
    in_specs = [pl.BlockSpec((tm, d), lambda i: (i, 0)),
                pl.BlockSpec((None, 3, d), lambda i: (i // per_batch, 0, 0)),
                _full_spec((1, d))]
    in_specs += [pl.BlockSpec((tm, y.shape[1]), lambda i: (i, 0)) for y in ys]
    in_specs += [_full_spec(w.shape) for w in weights]
    return pl.pallas_call(
        _out_proj_kernel,
        grid=(m // tm,),
        in_specs=in_specs,
        out_specs=pl.BlockSpec((tm, d), lambda i: (i, 0)),
        out_shape=jax.ShapeDtypeStruct((m, d), F32),
        compiler_params=_params(("parallel",)),
        name="out_proj",
    )(x2, mod, gain.reshape(1, d), *ys, *weights)


def _mlp_kernel(x_ref, mod_ref, g_pre_ref, g_post_ref, w1_ref, w2_ref, o_ref, *, ff_chunk):
    x = x_ref[...]
    mod = mod_ref[...]
    h = _rms(x) * g_pre_ref[...] * (1.0 + mod[1:2]) + mod[0:1]
    hb = h.astype(BF16)
    d_ff = w1_ref.shape[1]
    acc = None
    for c0 in range(0, d_ff, ff_chunk):
        a = jnp.dot(hb, w1_ref[:, c0:c0 + ff_chunk], preferred_element_type=F32)
        a = jnp.square(jnp.maximum(a, 0.0)).astype(BF16)
        part = jnp.dot(a, w2_ref[c0:c0 + ff_chunk, :], preferred_element_type=F32)
        acc = part if acc is None else acc + part
    o_ref[...] = x + mod[2:3] * (_rms(acc) * g_post_ref[...])


def mlp(x2, mod, g_pre, g_post, w1, w2, layer, seq, tm=512, ff_chunk=1024):
    m, d = x2.shape
    per_batch = seq // tm
    return pl.pallas_call(
        functools.partial(_mlp_kernel, ff_chunk=ff_chunk),
        grid=(m // tm,),
        in_specs=[pl.BlockSpec((tm, d), lambda i: (i, 0)),
                  pl.BlockSpec((None, 3, d), lambda i: (i // per_batch, 0, 0)),
                  _full_spec((1, d)), _full_spec((1, d)),
                  pl.BlockSpec((None,) + w1.shape[1:], lambda i: (layer, 0, 0)),
                  pl.BlockSpec((None,) + w2.shape[1:], lambda i: (layer, 0, 0))],
        out_specs=pl.BlockSpec((tm, d), lambda i: (i, 0)),
        out_shape=jax.ShapeDtypeStruct((m, d), F32),
        compiler_params=_params(("parallel",)),
        name="mlp",
    )(x2, mod, g_pre.reshape(1, d), g_post.reshape(1, d), w1, w2)


def _ssd_kernel(z_ref, xbc_ref, dt_ref, cw_ref, cb_ref, dtb_ref, alog_ref, dskip_ref, nw_ref, expand_ref,
                shift_ref, o_ref, conv_buf, state_ref, *, d_inner):
    chunk = z_ref.shape[0]
    n_state = SSD_STATE
    gn = SSD_GROUPS * n_state
    pair = 2 * SSD_HEAD_DIM
    heads_per_group = d_inner // SSD_HEAD_DIM // SSD_GROUPS
    halo = conv_buf.shape[0] - chunk

    @pl.when(pl.program_id(1) == 0)
    def _():
        state_ref[...] = jnp.zeros_like(state_ref)
        conv_buf[0:halo, :] = jnp.zeros((halo, conv_buf.shape[1]), BF16)

    cur = xbc_ref[...]
    conv_buf[halo:halo + chunk, :] = cur
    shifted = jnp.dot(shift_ref[...], conv_buf[...], preferred_element_type=F32)
    acc = cb_ref[...] + cw_ref[SSD_CONV - 1:SSD_CONV, :] * cur.astype(F32)
    for tap in range(SSD_CONV - 1):
        acc = acc + cw_ref[tap:tap + 1, :] * shifted[tap * chunk:(tap + 1) * chunk, :]
    conv_buf[0:halo, :] = conv_buf[chunk:chunk + halo, :]
    xbc = _silu(acc)
    xs = xbc[:, :d_inner]
    b_all = xbc[:, d_inner:d_inner + gn]
    c_all = xbc[:, d_inner + gn:]

    dt_in = dt_ref[...] + dtb_ref[...]
    dt = jnp.maximum(dt_in, 0.0) + jnp.log1p(jnp.exp(-jnp.abs(dt_in)))
    a = dt * (-jnp.exp(alog_ref[...]))
    row = lax.broadcasted_iota(jnp.int32, (chunk, chunk), 0)
    col = lax.broadcasted_iota(jnp.int32, (chunk, chunk), 1)
    causal = row >= col
    tril = jnp.where(causal, 1.0, 0.0).astype(BF16)
    parts = jnp.dot(tril, jnp.concatenate(_split3(a), axis=1), preferred_element_type=F32)
    a_cs = parts[:, :LANES] + parts[:, LANES:2 * LANES] + parts[:, 2 * LANES:]
    a_cs_t = a_cs.T
    lhs = jnp.concatenate([jnp.concatenate(_split3(dt), axis=1), jnp.concatenate(_split3(a_cs), axis=1)], axis=0)
    expanded = jnp.dot(lhs, expand_ref[...], preferred_element_type=F32)
    dt_e, acs_e = expanded[:chunk], expanded[chunk:]
    a_last_e = acs_e[chunk - 1:chunk, :]
    x_dt = xs * dt_e
    w_state = (x_dt * jnp.exp(a_last_e - acs_e)).astype(BF16)
    out_scale = jnp.exp(acs_e)
    chunk_decay = jnp.exp(a_last_e)
    x_dt_b = x_dt.astype(BF16)
    lane = lax.broadcasted_iota(jnp.int32, (chunk, pair), 1)
    first_head = lane < SSD_HEAD_DIM

    y_blocks = []
    for g in range(SSD_GROUPS):
        bg = b_all[:, g * n_state:(g + 1) * n_state]
        cg = c_all[:, g * n_state:(g + 1) * n_state].astype(BF16)
        cb = lax.dot_general(cg, bg.astype(BF16), NT_DIMS, preferred_element_type=F32)
        bg_t = bg.T.astype(BF16)
        for pi in range(heads_per_group // 2):
            p = g * (heads_per_group // 2) + pi
            sl = slice(p * pair, (p + 1) * pair)
            xp = x_dt_b[:, sl]
            y_diag = None
            for k in range(2):
                hh = 2 * p + k
                seg = a_cs[:, hh:hh + 1] - a_cs_t[hh:hh + 1, :]
                decay = jnp.exp(jnp.where(causal, seg, MASK_VALUE))
                m = (cb * decay).astype(BF16)
                xh = jnp.where(first_head if k == 0 else jnp.logical_not(first_head), xp, jnp.zeros_like(xp))
                part = jnp.dot(m, xh, preferred_element_type=F32)
                y_diag = part if y_diag is None else y_diag + part
            st = state_ref[p]
            y_off = jnp.dot(cg, st.astype(BF16), preferred_element_type=F32) * out_scale[:, sl]
            state_ref[p] = st * chunk_decay[:, sl] + jnp.dot(bg_t, w_state[:, sl], preferred_element_type=F32)
            y_blocks.append(y_diag + y_off)
    y = jnp.concatenate(y_blocks, axis=-1) + xs * dskip_ref[...]
    y = y * _silu(z_ref[...].astype(F32))
    o_ref[...] = (_rms(y) * nw_ref[...]).astype(o_ref.dtype)


def ssd_mixer(z, xbc, dt_raw, conv_w, conv_b, dt_bias, a_log, d_skip, norm_w, batch, seq):
    m, d_inner = z.shape
    conv_ch = xbc.shape[1]
    n_heads = d_inner // SSD_HEAD_DIM
    nc = seq // CHUNK
    pad = LANES - n_heads
    expand = np.zeros((LANES, d_inner), np.float32)
    for h in range(n_heads):
        expand[h, h * SSD_HEAD_DIM:(h + 1) * SSD_HEAD_DIM] = 1.0
    expand3 = jnp.asarray(np.tile(expand, (3, 1)), BF16)
    halo = 2 * SUBLANES
    shift = np.zeros(((SSD_CONV - 1) * CHUNK, halo + CHUNK), np.float32)
    for tap in range(SSD_CONV - 1):
        shift[tap * CHUNK + np.arange(CHUNK), halo - (SSD_CONV - 1) + tap + np.arange(CHUNK)] = 1.0
    shift = jnp.asarray(shift, BF16)
    row_spec = lambda width: pl.BlockSpec((CHUNK, width), lambda b, c: (b * nc + c, 0))
    return pl.pallas_call(
        functools.partial(_ssd_kernel, d_inner=d_inner),
        grid=(batch, nc),
        in_specs=[row_spec(d_inner), row_spec(conv_ch), row_spec(LANES),
                  _full_spec((SSD_CONV, conv_ch)), _full_spec((1, conv_ch)),
                  _full_spec((1, LANES)), _full_spec((1, LANES)),
                  _full_spec((1, d_inner)), _full_spec((1, d_inner)), _full_spec(expand3.shape),
                  _full_spec(shift.shape)],
        out_specs=row_spec(d_inner),
        out_shape=jax.ShapeDtypeStruct((m, d_inner), BF16),
        scratch_shapes=[pltpu.VMEM((halo + CHUNK, conv_ch), BF16),
                        pltpu.VMEM((n_heads // 2, SSD_STATE, 2 * SSD_HEAD_DIM), F32)],
        compiler_params=_params(("parallel", "arbitrary")),
        name="ssd_mixer",
    )(z, xbc, dt_raw, conv_w, conv_b.reshape(1, conv_ch),
      jnp.pad(dt_bias, (0, pad)).reshape(1, LANES), jnp.pad(a_log, (0, pad)).reshape(1, LANES),
      jnp.repeat(d_skip, SSD_HEAD_DIM).reshape(1, d_inner), norm_w.reshape(1, d_inner), expand3, shift)


def _diff_attn_kernel(q_ref, k_ref, v_ref, bias_ref, lam_ref, nw_ref, o_ref,
                      vt_ref, s0_ref, s1_ref, cmax0_ref, cmax1_ref, p_ref, m_ref, alpha_ref, acc_ref,
                      *, lam_init):
    s_refs, cmax_refs = (s0_ref, s1_ref), (cmax0_ref, cmax1_ref)
    t = s0_ref.shape[0]
    tq = q_ref.shape[0]
    head_w = v_ref.shape[1]
    ext_rows = vt_ref.shape[1]
    i = pl.program_id(2)
    q_t = q_ref[...].astype(F32).T.astype(BF16)
    dim = lax.broadcasted_iota(jnp.int32, q_t.shape, 0)
    zero = jnp.zeros_like(q_t)
    qq_t = jnp.concatenate([jnp.where(dim < DIFF_HEAD_DIM, q_t, zero),
                            jnp.where(dim >= DIFF_HEAD_DIM, q_t, zero)], axis=1)

    @pl.when(i == 0)
    def _():
        for j in range(vt_ref.shape[0]):
            vt_ref[j, :head_w, :] = v_ref[j * t:(j + 1) * t, :].astype(F32).T.astype(BF16)
            vt_ref[j, head_w:, :] = jnp.ones((ext_rows - head_w, t), BF16)

    m_ref[...] = jnp.full(m_ref.shape, MASK_VALUE, F32)
    acc_ref[...] = jnp.zeros(acc_ref.shape, F32)

    diag_tile, prev_tile = bias_ref[0], bias_ref[1]
    all_cols = ((0, t), (t, t), (tq, t), (tq + t, t))
    near_cols = (all_cols[1::2], all_cols, all_cols)
    near_tiles = ((diag_tile, diag_tile),
                  (diag_tile, prev_tile, diag_tile, prev_tile),
                  (prev_tile, None, prev_tile, None))
    near_blocks = (2 * i + 1, 2 * i, jnp.maximum(2 * i - 1, 0))

    def raw_logits(block, cols=all_cols):
        kb = k_ref[pl.ds(pl.multiple_of(block * t, t), t), :]
        rhs = qq_t if cols is all_cols else jnp.concatenate([qq_t[:, c0:c0 + w] for c0, w in cols], axis=1)
        return jnp.dot(kb, rhs, preferred_element_type=F32)

    def store_logits(pieces, slot, cols=all_cols):
        for piece, (c0, w) in zip(pieces, cols):
            s_refs[slot][:, c0:c0 + w] = piece
            chains = [piece[0:SUBLANES, :], piece[SUBLANES:2 * SUBLANES, :]]
            for n, r in enumerate(range(2 * SUBLANES, t, SUBLANES)):
                chains[n % 2] = jnp.maximum(chains[n % 2], piece[r:r + SUBLANES, :])
            col_max = jnp.maximum(chains[0], chains[1])
            for shift in (4, 2, 1):
                col_max = jnp.maximum(col_max, pltpu.roll(col_max, shift, 0))
            cmax_refs[slot][:, c0:c0 + w] = col_max

    def split(s, cols):
        widths = np.cumsum([0] + [w for _, w in cols])
        return [s[:, widths[n]:widths[n + 1]] for n in range(len(cols))]

    def near_logits(visit, slot):
        cols = near_cols[visit]
        pieces = split(raw_logits(near_blocks[visit], cols), cols)
        pieces = [p if tile is None else p + tile for p, tile in zip(pieces, near_tiles[visit])]
        store_logits(pieces, slot, cols)

    def far_logits(visit, slot):
        store_logits(split(raw_logits(visit - len(near_blocks)), all_cols), slot)

    def block_of(visit):
        return near_blocks[visit] if isinstance(visit, int) else visit - len(near_blocks)

    def softmax_accumulate(visit, slot, col_ranges=((0, 2 * tq),)):
        s_ref = s_refs[slot]
        vt = vt_ref[block_of(visit)]
        packed = 2 * SUBLANES
        for c0, w in col_ranges:
            c = slice(c0, c0 + w)
            m_old = m_ref[:, c]
            m_new = jnp.maximum(m_old, cmax_refs[slot][:, c])
            m_ref[:, c] = m_new
            alpha_ref[:, c] = jnp.exp2(m_old - m_new)
            m_tile = jnp.concatenate([m_new, m_new], axis=0)
            for r in range(0, t, packed):
                p_ref[r:r + packed, c] = jnp.exp2(s_ref[r:r + packed, c] - m_tile).astype(BF16)
            pv = jnp.dot(vt, p_ref[:, c], preferred_element_type=F32)
            alpha = alpha_ref[:, c]
            for r in range(0, ext_rows, SUBLANES):
                acc_ref[r:r + SUBLANES, c] = alpha * acc_ref[r:r + SUBLANES, c] + pv[r:r + SUBLANES, :]

    def pipeline_step(visit, slot):
        far_logits(visit, slot)
        softmax_accumulate(visit - 1, 1 - slot)

    near_logits(0, 0)
    near_logits(1, 1)
    softmax_accumulate(0, 0, near_cols[0])
    near_logits(2, 0)
    softmax_accumulate(1, 1)

    @pl.when(i > 0)
    def _():
        far_logits(3, 1)
        softmax_accumulate(2, 0)
        last = 2 * i + 1

        def pair(k, _):
            pipeline_step(4 + 2 * k, 0)

            @pl.when(5 + 2 * k <= last)
            def _():
                pipeline_step(5 + 2 * k, 1)

            return 0

        lax.fori_loop(0, i - 1, pair, 0)
        softmax_accumulate(last, 1)

    lam = lam_ref[...]
    lam_full = (jnp.exp(jnp.sum(lam[0:1] * lam[1:2], axis=-1, keepdims=True))
                - jnp.exp(jnp.sum(lam[2:3] * lam[3:4], axis=-1, keepdims=True)) + lam_init)
    inv_sum = 1.0 / acc_ref[head_w:head_w + SUBLANES, :]
    out_t = acc_ref[:head_w, :] * jnp.tile(inv_sum, (head_w // SUBLANES, 1))
    out = (out_t[:, :tq] - lam_full * out_t[:, tq:]).T
    o_ref[...] = (_rms(out) * nw_ref[...] * (1.0 - lam_init)).astype(o_ref.dtype)


def diff_attention(q, k, v, bias, lam, subln_w, batch, seq, lam_init):
    m, width = q.shape
    head_w = 2 * DIFF_HEAD_DIM
    n_heads = width // head_w
    t = ATTN_BLOCK
    tq = 2 * t
    nq = seq // tq
    cols = 2 * tq
    ext_rows = head_w + 2 * SUBLANES
    return pl.pallas_call(
        functools.partial(_diff_attn_kernel, lam_init=lam_init),
        grid=(batch, n_heads, nq),
        in_specs=[pl.BlockSpec((tq, head_w), lambda b, h, i: (b * nq + i, h)),
                  pl.BlockSpec((seq, head_w), lambda b, h, i: (b, h)),
                  pl.BlockSpec((seq, head_w), lambda b, h, i: (b, h)),
                  pl.BlockSpec((None, 2, t, t), lambda b, h, i: (h, 0, 0, 0)),
                  _full_spec(lam.shape), _full_spec((1, head_w))],
        out_specs=pl.BlockSpec((tq, head_w), lambda b, h, i: (b * nq + i, h)),
        out_shape=jax.ShapeDtypeStruct((m, width), BF16),
        scratch_shapes=[pltpu.VMEM((seq // t, ext_rows, t), BF16),
                        pltpu.VMEM((t, cols), F32), pltpu.VMEM((t, cols), F32),
                        pltpu.VMEM((SUBLANES, cols), F32), pltpu.VMEM((SUBLANES, cols), F32),
                        pltpu.VMEM((t, cols), BF16),
                        pltpu.VMEM((SUBLANES, cols), F32),
                        pltpu.VMEM((SUBLANES, cols), F32),
                        pltpu.VMEM((ext_rows, cols), F32)],
        compiler_params=_params(("parallel", "parallel", "arbitrary")),
        name="diff_attention",
    )(q, k, v, bias, lam, subln_w.reshape(1, head_w))


def _retention_kernel(q_ref, k_ref, v_ref, g_ref, cos_hi_ref, sin_hi_ref, cos_lo_ref, sin_lo_ref, sign_ref,
                      decay_ref, zeta_ref, xi_ref, o_ref, state_ref, *, chunk_decay):
    @pl.when(pl.program_id(1) == 0)
    def _():
        state_ref[...] = jnp.zeros_like(state_ref)

    c_hi, s_hi = cos_hi_ref[...], sin_hi_ref[...]
    c_lo, s_lo = cos_lo_ref[...], sin_lo_ref[...]
    cos_all = c_hi * c_lo - s_hi * s_lo
    sin_all = (s_hi * c_lo + c_hi * s_lo) * sign_ref[...]

    half = RET_QK_DIM // 2
    for h in range(RET_HEADS):
        qs = slice(h * RET_QK_DIM, (h + 1) * RET_QK_DIM)
        vs = slice(h * RET_V_DIM, (h + 1) * RET_V_DIM)
        st = state_ref[h]
        for r0 in range(0, q_ref.shape[0], CHUNK):
            rows = slice(r0, r0 + CHUNK)
            cos, sin = cos_all[rows, :], sin_all[rows, :]

            def rope(t):
                return t * cos + pltpu.roll(t, half, 1) * sin

            qr = rope(q_ref[rows, qs].astype(F32))
            kr = rope(k_ref[rows, qs].astype(F32)) * (RET_QK_DIM ** -0.5)
            vh = v_ref[rows, vs]
            scores = lax.dot_general(qr.astype(BF16), kr.astype(BF16), NT_DIMS,
                                     preferred_element_type=F32) * decay_ref[h]
            inner = jnp.dot(scores.astype(BF16), vh, preferred_element_type=F32)
            cross = jnp.dot((qr * xi_ref[h]).astype(BF16), st.astype(BF16), preferred_element_type=F32)
            kv = jnp.dot((kr.T * zeta_ref[h]).astype(BF16), vh, preferred_element_type=F32)
            st = st * chunk_decay[h] + kv
            out = _rms(inner + cross) * _silu(g_ref[rows, vs].astype(F32))
            o_ref[rows, vs] = out.astype(o_ref.dtype)
        state_ref[h] = st


def retention(rq, rk, rv, rg, batch, seq, chunks_per_step=8):
    m = rq.shape[0]
    rows = chunks_per_step * CHUNK
    nc = seq // rows
    half = RET_QK_DIM // 2
    inv = ROPE_BASE ** (-jnp.arange(half, dtype=F32) / half)
    both_halves = lambda a: jnp.concatenate([a, a], axis=-1)
    ang_hi = (jnp.arange(nc, dtype=F32) * rows)[:, None] * inv[None]
    ang_lo = jnp.arange(rows, dtype=F32)[:, None] * inv[None]
    cos_hi = both_halves(jnp.cos(ang_hi)).reshape(nc, 1, RET_QK_DIM)
    sin_hi = both_halves(jnp.sin(ang_hi)).reshape(nc, 1, RET_QK_DIM)
    cos_lo, sin_lo = both_halves(jnp.cos(ang_lo)), both_halves(jnp.sin(ang_lo))
    sign = jnp.concatenate([-jnp.ones((1, half), F32), jnp.ones((1, half), F32)], axis=-1)
    log_gamma = jnp.log(1.0 - 2.0 ** (-5.0 - jnp.arange(RET_HEADS, dtype=F32)))
    idx = jnp.arange(CHUNK, dtype=F32)
    rel = idx[:, None] - idx[None, :]
    decay = jnp.where(rel >= 0, jnp.exp(jnp.maximum(rel, 0.0)[None] * log_gamma[:, None, None]), 0.0)
    zeta = jnp.exp((CHUNK - 1 - idx)[None] * log_gamma[:, None])[:, None, :]
    xi = jnp.broadcast_to(jnp.exp((idx + 1.0)[None] * log_gamma[:, None])[:, :, None],
                          (RET_HEADS, CHUNK, RET_QK_DIM))
    gamma = 1.0 - 2.0 ** (-5.0 - np.arange(RET_HEADS, dtype=np.float64))
    chunk_decay = tuple(float(g ** CHUNK) for g in gamma)
    row_spec = lambda width: pl.BlockSpec((rows, width), lambda b, c: (b * nc + c, 0))
    return pl.pallas_call(
        functools.partial(_retention_kernel, chunk_decay=chunk_decay),
        grid=(batch, nc),
        in_specs=[row_spec(rq.shape[1]), row_spec(rk.shape[1]), row_spec(rv.shape[1]), row_spec(rg.shape[1]),
                  pl.BlockSpec((None, 1, RET_QK_DIM), lambda b, c: (c, 0, 0)),
                  pl.BlockSpec((None, 1, RET_QK_DIM), lambda b, c: (c, 0, 0)),
                  _full_spec(cos_lo.shape), _full_spec(sin_lo.shape), _full_spec(sign.shape),
                  _full_spec(decay.shape), _full_spec(zeta.shape), _full_spec(xi.shape)],
        out_specs=row_spec(rv.shape[1]),
        out_shape=jax.ShapeDtypeStruct((m, rv.shape[1]), BF16),
        scratch_shapes=[pltpu.VMEM((RET_HEADS, RET_QK_DIM, RET_V_DIM), F32)],
        compiler_params=_params(("parallel", "arbitrary")),
        name="retention",
    )(rq, rk, rv, rg, cos_hi, sin_hi, cos_lo, sin_lo, sign, decay, zeta, xi)


def _swa_kernel(sink_ref, q_ref, kp_ref, kc_ref, vp_ref, vc_ref, bias_ref, o_ref):
    i = pl.program_id(1)
    blk = SWA_BLOCK
    rep = SWA_HEADS // SWA_KV_HEADS
    k_all = jnp.concatenate([kp_ref[...], kc_ref[...]], axis=0)
    v_all = jnp.concatenate([vp_ref[...], vc_ref[...]], axis=0)
    col = lax.broadcasted_iota(jnp.int32, (blk, 2 * blk), 1)
    lane = lax.broadcasted_iota(jnp.int32, (blk, LANES), 1)
    lower = lane < SWA_HEAD_DIM
    for r0 in range(0, q_ref.shape[0], blk):
        rows = slice(r0, r0 + blk)
        kb = k_all[r0:r0 + 2 * blk]
        vb = v_all[r0:r0 + 2 * blk]
        outs = []
        for p in range(SWA_HEADS // 2):
            qp = q_ref[rows, p * LANES:(p + 1) * LANES].astype(F32)
            pair_out = []
            for parity in range(2):
                h = 2 * p + parity
                g = h // rep
                qh = qp if parity == g else pltpu.roll(qp, SWA_HEAD_DIM, 1)
                qh = jnp.where(lower if g == 0 else jnp.logical_not(lower), qh, 0.0).astype(BF16)
                s = lax.dot_general(qh, kb, NT_DIMS, preferred_element_type=F32) + bias_ref[h]
                if r0 == 0:
                    s = jnp.where(jnp.logical_and(col < blk, i == 0), MASK_VALUE, s)
                sink = sink_ref[h]
                m = jnp.maximum(jnp.max(s, axis=-1, keepdims=True), sink)
                e = jnp.exp(s - m)
                denom = jnp.sum(e, axis=-1, keepdims=True) + jnp.exp(sink - m)
                o = jnp.dot(e.astype(BF16), vb, preferred_element_type=F32) / denom
                pair_out.append(o if parity == g else pltpu.roll(o, SWA_HEAD_DIM, 1))
            outs.append(jnp.where(lower, pair_out[0], pair_out[1]))
        o_ref[rows, :] = jnp.concatenate(outs, axis=-1).astype(o_ref.dtype)


def sliding_window_attention(sq, sk, sv, sinks, bias, batch, seq, blocks_per_step=2):
    m, qw = sq.shape
    kw = sk.shape[1]
    rows = blocks_per_step * SWA_BLOCK
    nb = seq // rows
    cur = lambda b, i: (b * nb + i, 0)
    prev = lambda b, i: ((b * nb + i) * blocks_per_step - jnp.minimum(i, 1), 0)
    return pl.pallas_call(
        _swa_kernel,
        grid=(batch, nb),
        in_specs=[pl.BlockSpec(memory_space=pltpu.SMEM),
                  pl.BlockSpec((rows, qw), cur),
                  pl.BlockSpec((SWA_BLOCK, kw), prev), pl.BlockSpec((rows, kw), cur),
                  pl.BlockSpec((SWA_BLOCK, kw), prev), pl.BlockSpec((rows, kw), cur),
                  _full_spec(bias.shape)],
        out_specs=pl.BlockSpec((rows, qw), cur),
        out_shape=jax.ShapeDtypeStruct((m, qw), BF16),
        compiler_params=_params(("parallel", "parallel")),
        name="sliding_window_attention",
    )(sinks, sq, sk, sk, sv, sv, bias)


def _split_cols(w, sizes):
    offs = np.cumsum((0,) + tuple(sizes))
    return [w[:, offs[j]:offs[j + 1]] for j in range(len(sizes))]


def even_layer_mixer(x2, mod, g_pre, g_post, w_in, conv_w, conv_b, dt_bias, a_log, d_skip, ssd_norm, lam,
                     diff_norm, w_out, diff_bias, layer_idx, batch, seq):
    d = x2.shape[1]
    d_inner = d
    n_ssd_heads = d_inner // SSD_HEAD_DIM
    conv_ch = d_inner + 2 * SSD_GROUPS * SSD_STATE
    wz, wxbc, wdt, wq, wk, wv = _split_cols(w_in, (d_inner, conv_ch, n_ssd_heads, d, d, d))
    wdt = jnp.pad(wdt, ((0, 0), (0, LANES - n_ssd_heads)))
    wq = wq * (DIFF_HEAD_DIM ** -0.5 * LOG2_E)
    weights = [w.astype(BF16) for w in (wz, wxbc, wdt, wq, wk, wv)]
    z, xbc, dt_raw, q, k, v = norm_proj(x2, mod, g_pre, weights, (BF16, BF16, F32, BF16, BF16, BF16), seq)
    y_ssd = ssd_mixer(z, xbc, dt_raw, conv_w, conv_b, dt_bias, a_log, d_skip, ssd_norm, batch, seq)
    lam_init = 0.8 - 0.6 * math.exp(-0.3 * layer_idx)
    y_diff = diff_attention(q, k, v, diff_bias, lam, diff_norm, batch, seq, lam_init)
    w_out = w_out.astype(BF16)
    return out_proj(x2, mod, g_post, [y_ssd, y_diff], [w_out[:d_inner], w_out[d_inner:]], seq)


def odd_layer_mixer(x2, mod, g_pre, g_post, w_in, sinks, w_out, swa_bias, batch, seq):
    ret_qk = RET_HEADS * RET_QK_DIM
    ret_v = RET_HEADS * RET_V_DIM
    swa_q = SWA_HEADS * SWA_HEAD_DIM
    swa_kv = SWA_KV_HEADS * SWA_HEAD_DIM
    wrq, wrk, wrv, wrg, wsq, wsk, wsv = _split_cols(w_in, (ret_qk, ret_qk, ret_v, ret_v, swa_q, swa_kv, swa_kv))
    wsq = wsq * (SWA_HEAD_DIM ** -0.5)
    weights = [w.astype(BF16) for w in (wrq, wrk, wrv, wrg, wsq, wsk, wsv)]
    rq, rk, rv, rg, sq, sk, sv = norm_proj(x2, mod, g_pre, weights, (BF16,) * 7, seq)
    y_ret = retention(rq, rk, rv, rg, batch, seq)
    y_swa = sliding_window_attention(sq, sk, sv, sinks, swa_bias, batch, seq)
    w_out = w_out.astype(BF16)
    return out_proj(x2, mod, g_post, [y_ret, y_swa], [w_out[:ret_v], w_out[ret_v:]], seq)


def kernel(x, c, rel_bias, norm_gains, mod_w, mod_b, mlp_w1, mlp_w2, e_w_in, e_conv_w, e_conv_b, e_dt_bias,
           e_A_log, e_D, e_ssd_norm, e_lambda, e_diff_norm, e_w_out, o_w_in, o_sinks, o_w_out):
    batch, seq, d = x.shape
    depth = norm_gains.shape[0]
    assert seq % (2 * ATTN_BLOCK) == 0 and d % LANES == 0
    mods = modulation(c, mod_w, mod_b.reshape(depth * 2, 3 * d))
    mlp_w1_b, mlp_w2_b = mlp_w1.astype(BF16), mlp_w2.astype(BF16)
    far_bucket = REL_BUCKETS - 1
    diff_bias = bias_tiles(rel_bias, _diff_bucket_idx(ATTN_BLOCK), shift_bucket=far_bucket, scale=LOG2_E)
    diff_bias = diff_bias.reshape(rel_bias.shape[1], 2, ATTN_BLOCK, ATTN_BLOCK)
    swa_bias = bias_tiles(rel_bias, _swa_bucket_idx())

    x2 = x.reshape(batch * seq, d)
    for layer in range(depth):
        j = layer // 2
        gains = norm_gains[layer]
        if layer % 2 == 0:
            x2 = even_layer_mixer(x2, mods[2 * layer], gains[0], gains[1], e_w_in[j], e_conv_w[j], e_conv_b[j],
                                  e_dt_bias[j], e_A_log[j], e_D[j], e_ssd_norm[j], e_lambda[j], e_diff_norm[j],
                                  e_w_out[j], diff_bias, layer, batch, seq)
        else:
            x2 = odd_layer_mixer(x2, mods[2 * layer], gains[0], gains[1], o_w_in[j], o_sinks[j], o_w_out[j],
                                 swa_bias, batch, seq)
        x2 = mlp(x2, mods[2 * layer + 1], gains[2], gains[3], mlp_w1_b, mlp_w2_b, layer, seq)
    return x2.reshape(batch, seq, d)
```

```python
import functools
import math

import jax
import jax.numpy as jnp
import numpy as np
from jax import lax
from jax.experimental import pallas as pl
from jax.experimental.pallas import tpu as pltpu

EPS = 1e-6
MASK_VALUE = -1e30
LOG2_E = math.log2(math.e)
LANES = 128
SUBLANES = 8
VMEM_LIMIT = 56 * 1024 * 1024

CHUNK = 128
REL_BUCKETS = 32
REL_MAX_DIST = 128
SSD_HEAD_DIM = 64
SSD_GROUPS = 4
SSD_STATE = 128
SSD_CONV = 4
DIFF_HEAD_DIM = 64
RET_HEADS = 4
RET_QK_DIM = 128
RET_V_DIM = 256
ROPE_BASE = 10000.0
SWA_HEADS = 8
SWA_KV_HEADS = 2
SWA_HEAD_DIM = 64
SWA_BLOCK = 128
ATTN_BLOCK = 512

BF16 = jnp.bfloat16
F32 = jnp.float32
NT_DIMS = (((1,), (1,)), ((), ()))


def _params(semantics, flags=None):
    return pltpu.CompilerParams(dimension_semantics=semantics, vmem_limit_bytes=VMEM_LIMIT, flags=flags)


def _full_spec(shape):
    return pl.BlockSpec(shape, lambda *_: (0,) * len(shape))


def _silu(x):
    h = 0.5 * x
    return h * jnp.tanh(h) + h


def _split3(x):
    hi = x.astype(BF16)
    r1 = x - hi.astype(F32)
    mid = r1.astype(BF16)
    lo = (r1 - mid.astype(F32)).astype(BF16)
    return hi, mid, lo


def _rms(x, eps=EPS):
    return x * lax.rsqrt(jnp.mean(x * x, axis=-1, keepdims=True) + eps)


def _mod_kernel(c_ref, w_ref, b_ref, o_ref):
    c_act = _silu(c_ref[...])
    o_ref[...] = jnp.dot(c_act, w_ref[...], preferred_element_type=F32,
                         precision=lax.Precision.HIGHEST) + b_ref[...]


def modulation(c, mod_w, mod_b):
    b, d = c.shape
    n = mod_b.shape[0]
    per_layer = mod_w.shape[1]
    rows = SUBLANES
    c_pad = jnp.zeros((rows, d), F32).at[:b].set(c)
    tn = d
    out = pl.pallas_call(
        _mod_kernel,
        grid=(n, 3 * d // tn),
        in_specs=[pl.BlockSpec((rows, d), lambda s, j: (0, 0)),
                  pl.BlockSpec((None, None, d, tn), lambda s, j: (s // per_layer, s % per_layer, 0, j)),
                  pl.BlockSpec((None, 1, tn), lambda s, j: (s, 0, j))],
        out_specs=pl.BlockSpec((None, rows, tn), lambda s, j: (s, 0, j)),
        out_shape=jax.ShapeDtypeStruct((n, rows, 3 * d), F32),
        compiler_params=_params(("parallel", "parallel")),
        name="modulation",
    )(c_pad, mod_w, mod_b.reshape(n, 1, 3 * d))
    return out[:, :b].reshape(n, b, 3, d)


def _t5_bucket_np(dist):
    max_exact = REL_BUCKETS // 2
    logd = np.log(np.maximum(dist, 1).astype(np.float32) / np.float32(max_exact))
    large = max_exact + (logd / np.float32(math.log(REL_MAX_DIST / max_exact))
                         * np.float32(REL_BUCKETS - max_exact)).astype(np.int32)
    large = np.minimum(large, REL_BUCKETS - 1)
    return np.where(dist < max_exact, dist, large).astype(np.int32)


def _bias_kernel(rb_ref, idx_ref, o_ref, *, shift_bucket, scale, block_buckets):
    h = pl.program_id(0)
    shift = rb_ref[shift_bucket, h] if shift_bucket is not None else 0.0

    def value(bucket):
        return (rb_ref[bucket, h] - shift) * scale

    for (r0, c0), buckets in block_buckets:
        window = (slice(r0, r0 + LANES), slice(c0, c0 + LANES))
        idx = idx_ref[window]
        acc = jnp.full(idx.shape, MASK_VALUE if -1 in buckets else value(max(buckets)), F32)
        for bucket in buckets:
            if bucket >= 0 and len(buckets) > 1:
                acc = jnp.where(idx == bucket, value(bucket), acc)
        o_ref[window] = acc


def bias_tiles(rel_bias, bucket_idx, shift_bucket=None, scale=1.0):
    n_heads = rel_bias.shape[1]
    r, c = bucket_idx.shape
    block_buckets = tuple(((r0, c0), tuple(int(b) for b in np.unique(bucket_idx[r0:r0 + LANES, c0:c0 + LANES])))
                          for r0 in range(0, r, LANES) for c0 in range(0, c, LANES))
    return pl.pallas_call(
        functools.partial(_bias_kernel, shift_bucket=shift_bucket, scale=scale, block_buckets=block_buckets),
        grid=(n_heads,),
        in_specs=[pl.BlockSpec(memory_space=pltpu.SMEM),
                  pl.BlockSpec((r, c), lambda h: (0, 0))],
        out_specs=pl.BlockSpec((None, r, c), lambda h: (h, 0, 0)),
        out_shape=jax.ShapeDtypeStruct((n_heads, r, c), F32),
        compiler_params=_params(("parallel",)),
        name="bias_tiles",
    )(rel_bias, jnp.asarray(bucket_idx))


def _diff_bucket_idx(t):
    q = np.arange(t)[None, :]
    k = np.arange(t)[:, None]
    diag = q - k
    idx_diag = np.where(diag >= 0, _t5_bucket_np(np.maximum(diag, 0)), -1)
    idx_prev = _t5_bucket_np(diag + t)
    assert (_t5_bucket_np(diag + 2 * t) == REL_BUCKETS - 1).all()
    return np.concatenate([idx_diag, idx_prev], axis=0).astype(np.int32)


def _swa_bucket_idx():
    q = np.arange(SWA_BLOCK)[None, :]
    k = np.arange(2 * SWA_BLOCK)[:, None] - SWA_BLOCK
    dist = q - k
    valid = (dist >= 0) & (dist < SWA_BLOCK)
    return np.where(valid, _t5_bucket_np(np.maximum(dist, 0)), -1).astype(np.int32)


def _norm_proj_kernel(x_ref, mod_ref, g_ref, *refs):
    n = len(refs) // 2
    w_refs, o_refs = refs[:n], refs[n:]
    mod = mod_ref[...]
    h = _rms(x_ref[...]) * g_ref[...] * (1.0 + mod[1:2]) + mod[0:1]
    hb = h.astype(BF16)
    for w_ref, o_ref in zip(w_refs, o_refs):
        o_ref[...] = jnp.dot(hb, w_ref[...], preferred_element_type=F32).astype(o_ref.dtype)


def norm_proj(x2, mod, gain, weights, out_dtypes, seq, tm=512):
    m, d = x2.shape
    per_batch = seq // tm
    in_specs = [pl.BlockSpec((tm, d), lambda i: (i, 0)),
                pl.BlockSpec((None, 3, d), lambda i: (i // per_batch, 0, 0)),
                _full_spec((1, d))]
    in_specs += [_full_spec(w.shape) for w in weights]
    out_specs = [pl.BlockSpec((tm, w.shape[1]), lambda i: (i, 0)) for w in weights]
    out_shape = [jax.ShapeDtypeStruct((m, w.shape[1]), dt) for w, dt in zip(weights, out_dtypes)]
    return pl.pallas_call(
        _norm_proj_kernel,
        grid=(m // tm,),
        in_specs=in_specs,
        out_specs=out_specs,
        out_shape=out_shape,
        compiler_params=_params(("parallel",)),
        name="norm_proj",
    )(x2, mod, gain.reshape(1, d), *weights)


def _out_proj_kernel(x_ref, mod_ref, g_ref, *refs):
    n = (len(refs) - 1) // 2
    y_refs, w_refs, o_ref = refs[:n], refs[n:2 * n], refs[2 * n]
    acc = None
    for y_ref, w_ref in zip(y_refs, w_refs):
        part = jnp.dot(y_ref[...], w_ref[...], preferred_element_type=F32)
        acc = part if acc is None else acc + part
    gate = mod_ref[...][2:3]
    o_ref[...] = x_ref[...] + gate * (_rms(acc) * g_ref[...])


def out_proj(x2, mod, gain, ys, weights, seq, tm=512):
    m, d = x2.shape
    per_batch = seq // tm
    in_specs = [pl.BlockSpec((tm, d), lambda i: (i, 0)),
                pl.BlockSpec((None, 3, d), lambda i: (i // per_batch, 0, 0)),
                _full_spec((1, d))]
    in_specs += [pl.BlockSpec((tm, y.shape[1]), lambda i: (i, 0)) for y in ys]
    in_specs += [_full_spec(w.shape) for w in weights]
    return pl.pallas_call(
        _out_proj_kernel,
        grid=(m // tm,),
        in_specs=in_specs,
        out_specs=pl.BlockSpec((tm, d), lambda i: (i, 0)),
        out_shape=jax.ShapeDtypeStruct((m, d), F32),
        compiler_params=_params(("parallel",)),
        name="out_proj",
    )(x2, mod, gain.reshape(1, d), *ys, *weights)


def _mlp_kernel(x_ref, mod_ref, g_pre_ref, g_post_ref, w1_ref, w2_ref, o_ref, *, ff_chunk):
    x = x_ref[...]
    mod = mod_ref[...]
    h = _rms(x) * g_pre_ref[...] * (1.0 + mod[1:2]) + mod[0:1]
    hb = h.astype(BF16)
    d_ff = w1_ref.shape[1]
    acc = None
    for c0 in range(0, d_ff, ff_chunk):
        a = jnp.dot(hb, w1_ref[:, c0:c0 + ff_chunk], preferred_element_type=F32)
        a = jnp.square(jnp.maximum(a, 0.0)).astype(BF16)
        part = jnp.dot(a, w2_ref[c0:c0 + ff_chunk, :], preferred_element_type=F32)
        acc = part if acc is None else acc + part
    o_ref[...] = x + mod[2:3] * (_rms(acc) * g_post_ref[...])


def mlp(x2, mod, g_pre, g_post, w1, w2, layer, seq, tm=512, ff_chunk=1024):
    m, d = x2.shape
    per_batch = seq // tm
    return pl.pallas_call(
        functools.partial(_mlp_kernel, ff_chunk=ff_chunk),
        grid=(m // tm,),
        in_specs=[pl.BlockSpec((tm, d), lambda i: (i, 0)),
                  pl.BlockSpec((None, 3, d), lambda i: (i // per_batch, 0, 0)),
                  _full_spec((1, d)), _full_spec((1, d)),
                  pl.BlockSpec((None,) + w1.shape[1:], lambda i: (layer, 0, 0)),
                  pl.BlockSpec((None,) + w2.shape[1:], lambda i: (layer, 0, 0))],
        out_specs=pl.BlockSpec((tm, d), lambda i: (i, 0)),
        out_shape=jax.ShapeDtypeStruct((m, d), F32),
        compiler_params=_params(("parallel",)),
        name="mlp",
    )(x2, mod, g_pre.reshape(1, d), g_post.reshape(1, d), w1, w2)


def _ssd_kernel(z_ref, xbc_ref, dt_ref, cw_ref, cb_ref, dtb_ref, alog_ref, dskip_ref, nw_ref, expand_ref,
                shift_ref, o_ref, conv_buf, state_ref, *, d_inner):
    chunk = z_ref.shape[0]
    n_state = SSD_STATE
    gn = SSD_GROUPS * n_state
    pair = 2 * SSD_HEAD_DIM
    heads_per_group = d_inner // SSD_HEAD_DIM // SSD_GROUPS
    halo = conv_buf.shape[0] - chunk

    @pl.when(pl.program_id(1) == 0)
    def _():
        state_ref[...] = jnp.zeros_like(state_ref)
        conv_buf[0:halo, :] = jnp.zeros((halo, conv_buf.shape[1]), BF16)

    cur = xbc_ref[...]
    conv_buf[halo:halo + chunk, :] = cur
    shifted = jnp.dot(shift_ref[...], conv_buf[...], preferred_element_type=F32)
    acc = cb_ref[...] + cw_ref[SSD_CONV - 1:SSD_CONV, :] * cur.astype(F32)
    for tap in range(SSD_CONV - 1):
        acc = acc + cw_ref[tap:tap + 1, :] * shifted[tap * chunk:(tap + 1) * chunk, :]
    conv_buf[0:halo, :] = conv_buf[chunk:chunk + halo, :]
    xbc = _silu(acc)
    xs = xbc[:, :d_inner]
    b_all = xbc[:, d_inner:d_inner + gn]
    c_all = xbc[:, d_inner + gn:]

    dt_in = dt_ref[...] + dtb_ref[...]
    dt = jnp.maximum(dt_in, 0.0) + jnp.log1p(jnp.exp(-jnp.abs(dt_in)))
    a = dt * (-jnp.exp(alog_ref[...]))
    row = lax.broadcasted_iota(jnp.int32, (chunk, chunk), 0)
    col = lax.broadcasted_iota(jnp.int32, (chunk, chunk), 1)
    causal = row >= col
    tril = jnp.where(causal, 1.0, 0.0).astype(BF16)
    parts = jnp.dot(tril, jnp.concatenate(_split3(a), axis=1), preferred_element_type=F32)
    a_cs = parts[:, :LANES] + parts[:, LANES:2 * LANES] + parts[:, 2 * LANES:]
    a_cs_t = a_cs.T
    lhs = jnp.concatenate([jnp.concatenate(_split3(dt), axis=1), jnp.concatenate(_split3(a_cs), axis=1)], axis=0)
    expanded = jnp.dot(lhs, expand_ref[...], preferred_element_type=F32)
    dt_e, acs_e = expanded[:chunk], expanded[chunk:]
    a_last_e = acs_e[chunk - 1:chunk, :]
    x_dt = xs * dt_e
    w_state = (x_dt * jnp.exp(a_last_e - acs_e)).astype(BF16)
    out_scale = jnp.exp(acs_e)
    chunk_decay = jnp.exp(a_last_e)
    x_dt_b = x_dt.astype(BF16)
    lane = lax.broadcasted_iota(jnp.int32, (chunk, pair), 1)
    first_head = lane < SSD_HEAD_DIM

    y_blocks = []
    for g in range(SSD_GROUPS):
        bg = b_all[:, g * n_state:(g + 1) * n_state]
        cg = c_all[:, g * n_state:(g + 1) * n_state].astype(BF16)
        cb = lax.dot_general(cg, bg.astype(BF16), NT_DIMS, preferred_element_type=F32)
        bg_t = bg.T.astype(BF16)
        for pi in range(heads_per_group // 2):
            p = g * (heads_per_group // 2) + pi
            sl = slice(p * pair, (p + 1) * pair)
            xp = x_dt_b[:, sl]
            y_diag = None
            for k in range(2):
                hh = 2 * p + k
                seg = a_cs[:, hh:hh + 1] - a_cs_t[hh:hh + 1, :]
                decay = jnp.exp(jnp.where(causal, seg, MASK_VALUE))
                m = (cb * decay).astype(BF16)
                xh = jnp.where(first_head if k == 0 else jnp.logical_not(first_head), xp, jnp.zeros_like(xp))
                part = jnp.dot(m, xh, preferred_element_type=F32)
                y_diag = part if y_diag is None else y_diag + part
            st = state_ref[p]
            y_off = jnp.dot(cg, st.astype(BF16), preferred_element_type=F32) * out_scale[:, sl]
            state_ref[p] = st * chunk_decay[:, sl] + jnp.dot(bg_t, w_state[:, sl], preferred_element_type=F32)
            y_blocks.append(y_diag + y_off)
    y = jnp.concatenate(y_blocks, axis=-1) + xs * dskip_ref[...]
    y = y * _silu(z_ref[...].astype(F32))
    o_ref[...] = (_rms(y) * nw_ref[...]).astype(o_ref.dtype)


def ssd_mixer(z, xbc, dt_raw, conv_w, conv_b, dt_bias, a_log, d_skip, norm_w, batch, seq):
    m, d_inner = z.shape
    conv_ch = xbc.shape[1]
    n_heads = d_inner // SSD_HEAD_DIM
    nc = seq // CHUNK
    pad = LANES - n_heads
    expand = np.zeros((LANES, d_inner), np.float32)
    for h in range(n_heads):
        expand[h, h * SSD_HEAD_DIM:(h + 1) * SSD_HEAD_DIM] = 1.0
    expand3 = jnp.asarray(np.tile(expand, (3, 1)), BF16)
    halo = 2 * SUBLANES
    shift = np.zeros(((SSD_CONV - 1) * CHUNK, halo + CHUNK), np.float32)
    for tap in range(SSD_CONV - 1):
        shift[tap * CHUNK + np.arange(CHUNK), halo - (SSD_CONV - 1) + tap + np.arange(CHUNK)] = 1.0
    shift = jnp.asarray(shift, BF16)
    row_spec = lambda width: pl.BlockSpec((CHUNK, width), lambda b, c: (b * nc + c, 0))
    return pl.pallas_call(
        functools.partial(_ssd_kernel, d_inner=d_inner),
        grid=(batch, nc),
        in_specs=[row_spec(d_inner), row_spec(conv_ch), row_spec(LANES),
                  _full_spec((SSD_CONV, conv_ch)), _full_spec((1, conv_ch)),
                  _full_spec((1, LANES)), _full_spec((1, LANES)),
                  _full_spec((1, d_inner)), _full_spec((1, d_inner)), _full_spec(expand3.shape),
                  _full_spec(shift.shape)],
        out_specs=row_spec(d_inner),
        out_shape=jax.ShapeDtypeStruct((m, d_inner), BF16),
        scratch_shapes=[pltpu.VMEM((halo + CHUNK, conv_ch), BF16),
                        pltpu.VMEM((n_heads // 2, SSD_STATE, 2 * SSD_HEAD_DIM), F32)],
        compiler_params=_params(("parallel", "arbitrary")),
        name="ssd_mixer",
    )(z, xbc, dt_raw, conv_w, conv_b.reshape(1, conv_ch),
      jnp.pad(dt_bias, (0, pad)).reshape(1, LANES), jnp.pad(a_log, (0, pad)).reshape(1, LANES),
      jnp.repeat(d_skip, SSD_HEAD_DIM).reshape(1, d_inner), norm_w.reshape(1, d_inner), expand3, shift)


def _diff_attn_kernel(q_ref, k_ref, v_ref, bias_ref, lam_ref, nw_ref, o_ref,
                      vt_ref, s0_ref, s1_ref, cmax0_ref, cmax1_ref, p_ref, m_ref, alpha_ref, acc_ref,
                      *, lam_init):
    s_refs, cmax_refs = (s0_ref, s1_ref), (cmax0_ref, cmax1_ref)
    t = s0_ref.shape[0]
    tq = q_ref.shape[0]
    head_w = v_ref.shape[1]
    ext_rows = vt_ref.shape[1]
    i = pl.program_id(2)
    q_t = q_ref[...].astype(F32).T.astype(BF16)
    dim = lax.broadcasted_iota(jnp.int32, q_t.shape, 0)
    zero = jnp.zeros_like(q_t)
    qq_t = jnp.concatenate([jnp.where(dim < DIFF_HEAD_DIM, q_t, zero),
                            jnp.where(dim >= DIFF_HEAD_DIM, q_t, zero)], axis=1)

    @pl.when(i == 0)
    def _():
        for j in range(vt_ref.shape[0]):
            vt_ref[j, :head_w, :] = v_ref[j * t:(j + 1) * t, :].astype(F32).T.astype(BF16)
            vt_ref[j, head_w:, :] = jnp.ones((ext_rows - head_w, t), BF16)

    m_ref[...] = jnp.full(m_ref.shape, MASK_VALUE, F32)
    acc_ref[...] = jnp.zeros(acc_ref.shape, F32)

    diag_tile, prev_tile = bias_ref[0], bias_ref[1]
    all_cols = ((0, t), (t, t), (tq, t), (tq + t, t))
    near_cols = (all_cols[1::2], all_cols, all_cols)
    near_tiles = ((diag_tile, diag_tile),
                  (diag_tile, prev_tile, diag_tile, prev_tile),
                  (prev_tile, None, prev_tile, None))
    near_blocks = (2 * i + 1, 2 * i, jnp.maximum(2 * i - 1, 0))

    def raw_logits(block, cols=all_cols):
        kb = k_ref[pl.ds(pl.multiple_of(block * t, t), t), :]
        rhs = qq_t if cols is all_cols else jnp.concatenate([qq_t[:, c0:c0 + w] for c0, w in cols], axis=1)
        return jnp.dot(kb, rhs, preferred_element_type=F32)

    def store_logits(pieces, slot, cols=all_cols):
        for piece, (c0, w) in zip(pieces, cols):
            s_refs[slot][:, c0:c0 + w] = piece
            chains = [piece[0:SUBLANES, :], piece[SUBLANES:2 * SUBLANES, :]]
            for n, r in enumerate(range(2 * SUBLANES, t, SUBLANES)):
                chains[n % 2] = jnp.maximum(chains[n % 2], piece[r:r + SUBLANES, :])
            col_max = jnp.maximum(chains[0], chains[1])
            for shift in (4, 2, 1):
                col_max = jnp.maximum(col_max, pltpu.roll(col_max, shift, 0))
            cmax_refs[slot][:, c0:c0 + w] = col_max

    def split(s, cols):
        widths = np.cumsum([0] + [w for _, w in cols])
        return [s[:, widths[n]:widths[n + 1]] for n in range(len(cols))]

    def near_logits(visit, slot):
        cols = near_cols[visit]
        pieces = split(raw_logits(near_blocks[visit], cols), cols)
        pieces = [p if tile is None else p + tile for p, tile in zip(pieces, near_tiles[visit])]
        store_logits(pieces, slot, cols)

    def far_logits(visit, slot):
        store_logits(split(raw_logits(visit - len(near_blocks)), all_cols), slot)

    def block_of(visit):
        return near_blocks[visit] if isinstance(visit, int) else visit - len(near_blocks)

    def softmax_accumulate(visit, slot, col_ranges=((0, 2 * tq),)):
        s_ref = s_refs[slot]
        vt = vt_ref[block_of(visit)]
        packed = 2 * SUBLANES
        for c0, w in col_ranges:
            c = slice(c0, c0 + w)
            m_old = m_ref[:, c]
            m_new = jnp.maximum(m_old, cmax_refs[slot][:, c])
            m_ref[:, c] = m_new
            alpha_ref[:, c] = jnp.exp2(m_old - m_new)
            m_tile = jnp.concatenate([m_new, m_new], axis=0)
            for r in range(0, t, packed):
                p_ref[r:r + packed, c] = jnp.exp2(s_ref[r:r + packed, c] - m_tile).astype(BF16)
            pv = jnp.dot(vt, p_ref[:, c], preferred_element_type=F32)
            alpha = alpha_ref[:, c]
            for r in range(0, ext_rows, SUBLANES):
                acc_ref[r:r + SUBLANES, c] = alpha * acc_ref[r:r + SUBLANES, c] + pv[r:r + SUBLANES, :]

    def pipeline_step(visit, slot):
        far_logits(visit, slot)
        softmax_accumulate(visit - 1, 1 - slot)

    near_logits(0, 0)
    near_logits(1, 1)
    softmax_accumulate(0, 0, near_cols[0])
    near_logits(2, 0)
    softmax_accumulate(1, 1)

    @pl.when(i > 0)
    def _():
        far_logits(3, 1)
        softmax_accumulate(2, 0)
        last = 2 * i + 1

        def pair(k, _):
            pipeline_step(4 + 2 * k, 0)

            @pl.when(5 + 2 * k <= last)
            def _():
                pipeline_step(5 + 2 * k, 1)

            return 0

        lax.fori_loop(0, i - 1, pair, 0)
        softmax_accumulate(last, 1)

    lam = lam_ref[...]
    lam_full = (jnp.exp(jnp.sum(lam[0:1] * lam[1:2], axis=-1, keepdims=True))
                - jnp.exp(jnp.sum(lam[2:3] * lam[3:4], axis=-1, keepdims=True)) + lam_init)
    inv_sum = 1.0 / acc_ref[head_w:head_w + SUBLANES, :]
    out_t = acc_ref[:head_w, :] * jnp.tile(inv_sum, (head_w // SUBLANES, 1))
    out = (out_t[:, :tq] - lam_full * out_t[:, tq:]).T
    o_ref[...] = (_rms(out) * nw_ref[...] * (1.0 - lam_init)).astype(o_ref.dtype)


def diff_attention(q, k, v, bias, lam, subln_w, batch, seq, lam_init):
    m, width = q.shape
    head_w = 2 * DIFF_HEAD_DIM
    n_heads = width // head_w
    t = ATTN_BLOCK
    tq = 2 * t
    nq = seq // tq
    cols = 2 * tq
    ext_rows = head_w + 2 * SUBLANES
    return pl.pallas_call(
        functools.partial(_diff_attn_kernel, lam_init=lam_init),
        grid=(batch, n_heads, nq),
        in_specs=[pl.BlockSpec((tq, head_w), lambda b, h, i: (b * nq + i, h)),
                  pl.BlockSpec((seq, head_w), lambda b, h, i: (b, h)),
                  pl.BlockSpec((seq, head_w), lambda b, h, i: (b, h)),
                  pl.BlockSpec((None, 2, t, t), lambda b, h, i: (h, 0, 0, 0)),
                  _full_spec(lam.shape), _full_spec((1, head_w))],
        out_specs=pl.BlockSpec((tq, head_w), lambda b, h, i: (b * nq + i, h)),
        out_shape=jax.ShapeDtypeStruct((m, width), BF16),
        scratch_shapes=[pltpu.VMEM((seq // t, ext_rows, t), BF16),
                        pltpu.VMEM((t, cols), F32), pltpu.VMEM((t, cols), F32),
                        pltpu.VMEM((SUBLANES, cols), F32), pltpu.VMEM((SUBLANES, cols), F32),
                        pltpu.VMEM((t, cols), BF16),
                        pltpu.VMEM((SUBLANES, cols), F32),
                        pltpu.VMEM((SUBLANES, cols), F32),
                        pltpu.VMEM((ext_rows, cols), F32)],
        compiler_params=_params(("parallel", "parallel", "arbitrary")),
        name="diff_attention",
    )(q, k, v, bias, lam, subln_w.reshape(1, head_w))


def _retention_kernel(q_ref, k_ref, v_ref, g_ref, cos_hi_ref, sin_hi_ref, cos_lo_ref, sin_lo_ref, sign_ref,
                      decay_ref, zeta_ref, xi_ref, o_ref, state_ref, *, chunk_decay):
    @pl.when(pl.program_id(1) == 0)
    def _():
        state_ref[...] = jnp.zeros_like(state_ref)

    c_hi, s_hi = cos_hi_ref[...], sin_hi_ref[...]
    c_lo, s_lo = cos_lo_ref[...], sin_lo_ref[...]
    cos_all = c_hi * c_lo - s_hi * s_lo
    sin_all = (s_hi * c_lo + c_hi * s_lo) * sign_ref[...]

    half = RET_QK_DIM // 2
    for h in range(RET_HEADS):
        qs = slice(h * RET_QK_DIM, (h + 1) * RET_QK_DIM)
        vs = slice(h * RET_V_DIM, (h + 1) * RET_V_DIM)
        st = state_ref[h]
        for r0 in range(0, q_ref.shape[0], CHUNK):
            rows = slice(r0, r0 + CHUNK)
            cos, sin = cos_all[rows, :], sin_all[rows, :]

            def rope(t):
                return t * cos + pltpu.roll(t, half, 1) * sin

            qr = rope(q_ref[rows, qs].astype(F32))
            kr = rope(k_ref[rows, qs].astype(F32)) * (RET_QK_DIM ** -0.5)
            vh = v_ref[rows, vs]
            scores = lax.dot_general(qr.astype(BF16), kr.astype(BF16), NT_DIMS,
                                     preferred_element_type=F32) * decay_ref[h]
            inner = jnp.dot(scores.astype(BF16), vh, preferred_element_type=F32)
            cross = jnp.dot((qr * xi_ref[h]).astype(BF16), st.astype(BF16), preferred_element_type=F32)
            kv = jnp.dot((kr.T * zeta_ref[h]).astype(BF16), vh, preferred_element_type=F32)
            st = st * chunk_decay[h] + kv
            out = _rms(inner + cross) * _silu(g_ref[rows, vs].astype(F32))
            o_ref[rows, vs] = out.astype(o_ref.dtype)
        state_ref[h] = st


def retention(rq, rk, rv, rg, batch, seq, chunks_per_step=8):
    m = rq.shape[0]
    rows = chunks_per_step * CHUNK
    nc = seq // rows
    half = RET_QK_DIM // 2
    inv = ROPE_BASE ** (-jnp.arange(half, dtype=F32) / half)
    both_halves = lambda a: jnp.concatenate([a, a], axis=-1)
    ang_hi = (jnp.arange(nc, dtype=F32) * rows)[:, None] * inv[None]
    ang_lo = jnp.arange(rows, dtype=F32)[:, None] * inv[None]
    cos_hi = both_halves(jnp.cos(ang_hi)).reshape(nc, 1, RET_QK_DIM)
    sin_hi = both_halves(jnp.sin(ang_hi)).reshape(nc, 1, RET_QK_DIM)
    cos_lo, sin_lo = both_halves(jnp.cos(ang_lo)), both_halves(jnp.sin(ang_lo))
    sign = jnp.concatenate([-jnp.ones((1, half), F32), jnp.ones((1, half), F32)], axis=-1)
    log_gamma = jnp.log(1.0 - 2.0 ** (-5.0 - jnp.arange(RET_HEADS, dtype=F32)))
    idx = jnp.arange(CHUNK, dtype=F32)
    rel = idx[:, None] - idx[None, :]
    decay = jnp.where(rel >= 0, jnp.exp(jnp.maximum(rel, 0.0)[None] * log_gamma[:, None, None]), 0.0)
    zeta = jnp.exp((CHUNK - 1 - idx)[None] * log_gamma[:, None])[:, None, :]
    xi = jnp.broadcast_to(jnp.exp((idx + 1.0)[None] * log_gamma[:, None])[:, :, None],
                          (RET_HEADS, CHUNK, RET_QK_DIM))
    gamma = 1.0 - 2.0 ** (-5.0 - np.arange(RET_HEADS, dtype=np.float64))
    chunk_decay = tuple(float(g ** CHUNK) for g in gamma)
    row_spec = lambda width: pl.BlockSpec((rows, width), lambda b, c: (b * nc + c, 0))
    return pl.pallas_call(
        functools.partial(_retention_kernel, chunk_decay=chunk_decay),
        grid=(batch, nc),
        in_specs=[row_spec(rq.shape[1]), row_spec(rk.shape[1]), row_spec(rv.shape[1]), row_spec(rg.shape[1]),
                  pl.BlockSpec((None, 1, RET_QK_DIM), lambda b, c: (c, 0, 0)),
                  pl.BlockSpec((None, 1, RET_QK_DIM), lambda b, c: (c, 0, 0)),
                  _full_spec(cos_lo.shape), _full_spec(sin_lo.shape), _full_spec(sign.shape),
                  _full_spec(decay.shape), _full_spec(zeta.shape), _full_spec(xi.shape)],
        out_specs=row_spec(rv.shape[1]),
        out_shape=jax.ShapeDtypeStruct((m, rv.shape[1]), BF16),
        scratch_shapes=[pltpu.VMEM((RET_HEADS, RET_QK_DIM, RET_V_DIM), F32)],
        compiler_params=_params(("parallel", "arbitrary")),
        name="retention",
    )(rq, rk, rv, rg, cos_hi, sin_hi, cos_lo, sin_lo, sign, decay, zeta, xi)


def _swa_kernel(sink_ref, q_ref, kp_ref, kc_ref, vp_ref, vc_ref, bias_ref, o_ref):
    i = pl.program_id(1)
    blk = SWA_BLOCK
    rep = SWA_HEADS // SWA_KV_HEADS
    d = SWA_HEAD_DIM
    cols = rep * blk
    k_all = jnp.concatenate([kp_ref[...], kc_ref[...]], axis=0)
    v_all = jnp.concatenate([vp_ref[...], vc_ref[...]], axis=0)
    key_row = lax.broadcasted_iota(jnp.int32, (2 * blk, cols), 0)
    head_of_lane = lax.broadcasted_iota(jnp.int32, (SUBLANES, cols), 1) // blk
    zero = jnp.zeros((d, blk), F32)
    ones = jnp.ones((2 * SUBLANES, 2 * blk), F32)
    for r0 in range(0, q_ref.shape[0], blk):
        kb = k_all[r0:r0 + 2 * blk]
        v_t = v_all[r0:r0 + 2 * blk].astype(F32).T
        outs = []
        for g in range(SWA_KV_HEADS):
            q_t = q_ref[r0:r0 + blk, g * rep * d:(g + 1) * rep * d].astype(F32).T
            rhs = jnp.concatenate(
                [jnp.concatenate([q_t[r * d:(r + 1) * d], zero] if g == 0 else [zero, q_t[r * d:(r + 1) * d]],
                                 axis=0) for r in range(rep)], axis=1).astype(BF16)
            s = jnp.dot(kb, rhs, preferred_element_type=F32)
            s = s + jnp.concatenate([bias_ref[g * rep + r] for r in range(rep)], axis=1)
            if r0 == 0:
                s = jnp.where(jnp.logical_and(key_row < blk, i == 0), MASK_VALUE, s)
            sink = jnp.zeros((SUBLANES, cols), F32)
            for r in range(rep):
                sink = jnp.where(head_of_lane == r, sink_ref[g * rep + r] * LOG2_E, sink)
            chains = [s[0:SUBLANES], s[SUBLANES:2 * SUBLANES]]
            for n, r in enumerate(range(2 * SUBLANES, 2 * blk, SUBLANES)):
                chains[n % 2] = jnp.maximum(chains[n % 2], s[r:r + SUBLANES])
            col_max = jnp.maximum(chains[0], chains[1])
            for shift in (4, 2, 1):
                col_max = jnp.maximum(col_max, pltpu.roll(col_max, shift, 0))
            m = jnp.maximum(col_max, sink)
            p = jnp.exp2(s - jnp.tile(m, (2 * blk // SUBLANES, 1))).astype(BF16)
            v_ext = jnp.concatenate([v_t[g * d:(g + 1) * d], ones], axis=0).astype(BF16)
            pv = jnp.dot(v_ext, p, preferred_element_type=F32)
            inv = 1.0 / (pv[d:d + SUBLANES] + jnp.exp2(sink - m))
            o_t = pv[:d] * jnp.tile(inv, (d // SUBLANES, 1))
            for j in range(rep // 2):
                pair_t = jnp.concatenate([o_t[:, (2 * j) * blk:(2 * j + 1) * blk],
                                          o_t[:, (2 * j + 1) * blk:(2 * j + 2) * blk]], axis=0)
                outs.append(pair_t.T)
        o_ref[r0:r0 + blk, :] = jnp.concatenate(outs, axis=-1).astype(o_ref.dtype)


def sliding_window_attention(sq, sk, sv, sinks, bias, batch, seq, blocks_per_step=8):
    m, qw = sq.shape
    kw = sk.shape[1]
    rows = blocks_per_step * SWA_BLOCK
    nb = seq // rows
    cur = lambda b, i: (b * nb + i, 0)
    prev = lambda b, i: ((b * nb + i) * blocks_per_step - jnp.minimum(i, 1), 0)
    return pl.pallas_call(
        _swa_kernel,
        grid=(batch, nb),
        in_specs=[pl.BlockSpec(memory_space=pltpu.SMEM),
                  pl.BlockSpec((rows, qw), cur),
                  pl.BlockSpec((SWA_BLOCK, kw), prev), pl.BlockSpec((rows, kw), cur),
                  pl.BlockSpec((SWA_BLOCK, kw), prev), pl.BlockSpec((rows, kw), cur),
                  _full_spec(bias.shape)],
        out_specs=pl.BlockSpec((rows, qw), cur),
        out_shape=jax.ShapeDtypeStruct((m, qw), BF16),
        compiler_params=_params(("parallel", "parallel")),
        name="sliding_window_attention",
    )(sinks, sq, sk, sk, sv, sv, bias)


def _split_cols(w, sizes):
    offs = np.cumsum((0,) + tuple(sizes))
    return [w[:, offs[j]:offs[j + 1]] for j in range(len(sizes))]


def even_layer_mixer(x2, mod, g_pre, g_post, w_in, conv_w, conv_b, dt_bias, a_log, d_skip, ssd_norm, lam,
                     diff_norm, w_out, diff_bias, layer_idx, batch, seq):
    d = x2.shape[1]
    d_inner = d
    n_ssd_heads = d_inner // SSD_HEAD_DIM
    conv_ch = d_inner + 2 * SSD_GROUPS * SSD_STATE
    wz, wxbc, wdt, wq, wk, wv = _split_cols(w_in, (d_inner, conv_ch, n_ssd_heads, d, d, d))
    wdt = jnp.pad(wdt, ((0, 0), (0, LANES - n_ssd_heads)))
    wq = wq * (DIFF_HEAD_DIM ** -0.5 * LOG2_E)
    weights = [w.astype(BF16) for w in (wz, wxbc, wdt, wq, wk, wv)]
    z, xbc, dt_raw, q, k, v = norm_proj(x2, mod, g_pre, weights, (BF16, BF16, F32, BF16, BF16, BF16), seq)
    y_ssd = ssd_mixer(z, xbc, dt_raw, conv_w, conv_b, dt_bias, a_log, d_skip, ssd_norm, batch, seq)
    lam_init = 0.8 - 0.6 * math.exp(-0.3 * layer_idx)
    y_diff = diff_attention(q, k, v, diff_bias, lam, diff_norm, batch, seq, lam_init)
    w_out = w_out.astype(BF16)
    return out_proj(x2, mod, g_post, [y_ssd, y_diff], [w_out[:d_inner], w_out[d_inner:]], seq)


def odd_layer_mixer(x2, mod, g_pre, g_post, w_in, sinks, w_out, swa_bias, batch, seq):
    ret_qk = RET_HEADS * RET_QK_DIM
    ret_v = RET_HEADS * RET_V_DIM
    swa_q = SWA_HEADS * SWA_HEAD_DIM
    swa_kv = SWA_KV_HEADS * SWA_HEAD_DIM
    wrq, wrk, wrv, wrg, wsq, wsk, wsv = _split_cols(w_in, (ret_qk, ret_qk, ret_v, ret_v, swa_q, swa_kv, swa_kv))
    wsq = wsq * (SWA_HEAD_DIM ** -0.5 * LOG2_E)
    weights = [w.astype(BF16) for w in (wrq, wrk, wrv, wrg, wsq, wsk, wsv)]
    rq, rk, rv, rg, sq, sk, sv = norm_proj(x2, mod, g_pre, weights, (BF16,) * 7, seq)
    y_ret = retention(rq, rk, rv, rg, batch, seq)
    y_swa = sliding_window_attention(sq, sk, sv, sinks, swa_bias, batch, seq)
    w_out = w_out.astype(BF16)
    return out_proj(x2, mod, g_post, [y_ret, y_swa], [w_out[:ret_v], w_out[ret_v:]], seq)


def kernel(x, c, rel_bias, norm_gains, mod_w, mod_b, mlp_w1, mlp_w2, e_w_in, e_conv_w, e_conv_b, e_dt_bias,
           e_A_log, e_D, e_ssd_norm, e_lambda, e_diff_norm, e_w_out, o_w_in, o_sinks, o_w_out):
    batch, seq, d = x.shape
    depth = norm_gains.shape[0]
    assert seq % (2 * ATTN_BLOCK) == 0 and d % LANES == 0
    mods = modulation(c, mod_w, mod_b.reshape(depth * 2, 3 * d))
    mlp_w1_b, mlp_w2_b = mlp_w1.astype(BF16), mlp_w2.astype(BF16)
    far_bucket = REL_BUCKETS - 1
    diff_bias = bias_tiles(rel_bias, _diff_bucket_idx(ATTN_BLOCK), shift_bucket=far_bucket, scale=LOG2_E)
    diff_bias = diff_bias.reshape(rel_bias.shape[1], 2, ATTN_BLOCK, ATTN_BLOCK)
    swa_bias = bias_tiles(rel_bias, _swa_bucket_idx(), scale=LOG2_E)

    x2 = x.reshape(batch * seq, d)
    for layer in range(depth):
        j = layer // 2
        gains = norm_gains[layer]
        if layer % 2 == 0:
            x2 = even_layer_mixer(x2, mods[2 * layer], gains[0], gains[1], e_w_in[j], e_conv_w[j], e_conv_b[j],
                                  e_dt_bias[j], e_A_log[j], e_D[j], e_ssd_norm[j], e_lambda[j], e_diff_norm[j],
                                  e_w_out[j], diff_bias, layer, batch, seq)
        else:
            x2 = odd_layer_mixer(x2, mods[2 * layer], gains[0], gains[1], o_w_in[j], o_sinks[j], o_w_out[j],
                                 swa_bias, batch, seq)
        x2 = mlp(x2, mods[2 * layer + 1], gains[2], gains[3], mlp_w1_b, mlp_w2_b, layer, seq)
    return x2.reshape(batch, seq, d)
```

```python
import functools
import math

import jax
import jax.numpy as jnp
import numpy as np
from jax import lax
from jax.experimental import pallas as pl
from jax.experimental.pallas import tpu as pltpu

EPS = 1e-6
MASK_VALUE = -1e30
LOG2_E = math.log2(math.e)
LANES = 128
SUBLANES = 8
VMEM_LIMIT = 56 * 1024 * 1024

CHUNK = 128
REL_BUCKETS = 32
REL_MAX_DIST = 128
SSD_HEAD_DIM = 64
SSD_GROUPS = 4
SSD_STATE = 128
SSD_CONV = 4
DIFF_HEAD_DIM = 64
RET_HEADS = 4
RET_QK_DIM = 128
RET_V_DIM = 256
ROPE_BASE = 10000.0
SWA_HEADS = 8
SWA_KV_HEADS = 2
SWA_HEAD_DIM = 64
SWA_BLOCK = 128
ATTN_BLOCK = 512

BF16 = jnp.bfloat16
F32 = jnp.float32
NT_DIMS = (((1,), (1,)), ((), ()))


def _params(semantics, flags=None):
    return pltpu.CompilerParams(dimension_semantics=semantics, vmem_limit_bytes=VMEM_LIMIT, flags=flags)


def _full_spec(shape):
    return pl.BlockSpec(shape, lambda *_: (0,) * len(shape))


def _silu(x):
    h = 0.5 * x
    return h * jnp.tanh(h) + h


def _split3(x):
    hi = x.astype(BF16)
    r1 = x - hi.astype(F32)
    mid = r1.astype(BF16)
    lo = (r1 - mid.astype(F32)).astype(BF16)
    return hi, mid, lo


def _rms(x, eps=EPS):
    return x * lax.rsqrt(jnp.mean(x * x, axis=-1, keepdims=True) + eps)


def _mod_kernel(c_ref, w_ref, b_ref, o_ref):
    c_act = _silu(c_ref[...])
    o_ref[...] = jnp.dot(c_act, w_ref[...], preferred_element_type=F32,
                         precision=lax.Precision.HIGHEST) + b_ref[...]


def modulation(c, mod_w, mod_b):
    b, d = c.shape
    n = mod_b.shape[0]
    per_layer = mod_w.shape[1]
    rows = SUBLANES
    c_pad = jnp.zeros((rows, d), F32).at[:b].set(c)
    tn = d
    out = pl.pallas_call(
        _mod_kernel,
        grid=(n, 3 * d // tn),
        in_specs=[pl.BlockSpec((rows, d), lambda s, j: (0, 0)),
                  pl.BlockSpec((None, None, d, tn), lambda s, j: (s // per_layer, s % per_layer, 0, j)),
                  pl.BlockSpec((None, 1, tn), lambda s, j: (s, 0, j))],
        out_specs=pl.BlockSpec((None, rows, tn), lambda s, j: (s, 0, j)),
        out_shape=jax.ShapeDtypeStruct((n, rows, 3 * d), F32),
        compiler_params=_params(("parallel", "parallel")),
        name="modulation",
    )(c_pad, mod_w, mod_b.reshape(n, 1, 3 * d))
    return out[:, :b].reshape(n, b, 3, d)


def _t5_bucket_np(dist):
    max_exact = REL_BUCKETS // 2
    logd = np.log(np.maximum(dist, 1).astype(np.float32) / np.float32(max_exact))
    large = max_exact + (logd / np.float32(math.log(REL_MAX_DIST / max_exact))
                         * np.float32(REL_BUCKETS - max_exact)).astype(np.int32)
    large = np.minimum(large, REL_BUCKETS - 1)
    return np.where(dist < max_exact, dist, large).astype(np.int32)


def _bias_kernel(rb_ref, idx_ref, o_ref, *, shift_bucket, scale, block_buckets):
    h = pl.program_id(0)
    shift = rb_ref[shift_bucket, h] if shift_bucket is not None else 0.0

    def value(bucket):
        return (rb_ref[bucket, h] - shift) * scale

    for (r0, c0), buckets in block_buckets:
        window = (slice(r0, r0 + LANES), slice(c0, c0 + LANES))
        idx = idx_ref[window]
        acc = jnp.full(idx.shape, MASK_VALUE if -1 in buckets else value(max(buckets)), F32)
        for bucket in buckets:
            if bucket >= 0 and len(buckets) > 1:
                acc = jnp.where(idx == bucket, value(bucket), acc)
        o_ref[window] = acc


def bias_tiles(rel_bias, bucket_idx, shift_bucket=None, scale=1.0):
    n_heads = rel_bias.shape[1]
    r, c = bucket_idx.shape
    block_buckets = tuple(((r0, c0), tuple(int(b) for b in np.unique(bucket_idx[r0:r0 + LANES, c0:c0 + LANES])))
                          for r0 in range(0, r, LANES) for c0 in range(0, c, LANES))
    return pl.pallas_call(
        functools.partial(_bias_kernel, shift_bucket=shift_bucket, scale=scale, block_buckets=block_buckets),
        grid=(n_heads,),
        in_specs=[pl.BlockSpec(memory_space=pltpu.SMEM),
                  pl.BlockSpec((r, c), lambda h: (0, 0))],
        out_specs=pl.BlockSpec((None, r, c), lambda h: (h, 0, 0)),
        out_shape=jax.ShapeDtypeStruct((n_heads, r, c), F32),
        compiler_params=_params(("parallel",)),
        name="bias_tiles",
    )(rel_bias, jnp.asarray(bucket_idx))


def _diff_bucket_idx(t):
    q = np.arange(t)[None, :]
    k = np.arange(t)[:, None]
    diag = q - k
    idx_diag = np.where(diag >= 0, _t5_bucket_np(np.maximum(diag, 0)), -1)
    idx_prev = _t5_bucket_np(diag + t)
    assert (_t5_bucket_np(diag + 2 * t) == REL_BUCKETS - 1).all()
    return np.concatenate([idx_diag, idx_prev], axis=0).astype(np.int32)


def _swa_bucket_idx():
    q = np.arange(SWA_BLOCK)[None, :]
    k = np.arange(2 * SWA_BLOCK)[:, None] - SWA_BLOCK
    dist = q - k
    valid = (dist >= 0) & (dist < SWA_BLOCK)
    return np.where(valid, _t5_bucket_np(np.maximum(dist, 0)), -1).astype(np.int32)


def _norm_proj_kernel(x_ref, mod_ref, g_ref, *refs):
    n = len(refs) // 2
    w_refs, o_refs = refs[:n], refs[n:]
    mod = mod_ref[...]
    h = _rms(x_ref[...]) * g_ref[...] * (1.0 + mod[1:2]) + mod[0:1]
    hb = h.astype(BF16)
    for w_ref, o_ref in zip(w_refs, o_refs):
        o_ref[...] = jnp.dot(hb, w_ref[...], preferred_element_type=F32).astype(o_ref.dtype)


def norm_proj(x2, mod, gain, weights, out_dtypes, seq, tm=512):
    m, d = x2.shape
    per_batch = seq // tm
    in_specs = [pl.BlockSpec((tm, d), lambda i: (i, 0)),
                pl.BlockSpec((None, 3, d), lambda i: (i // per_batch, 0, 0)),
                _full_spec((1, d))]
    in_specs += [_full_spec(w.shape) for w in weights]
    out_specs = [pl.BlockSpec((tm, w.shape[1]), lambda i: (i, 0)) for w in weights]
    out_shape = [jax.ShapeDtypeStruct((m, w.shape[1]), dt) for w, dt in zip(weights, out_dtypes)]
    return pl.pallas_call(
        _norm_proj_kernel,
        grid=(m // tm,),
        in_specs=in_specs,
        out_specs=out_specs,
        out_shape=out_shape,
        compiler_params=_params(("parallel",)),
        name="norm_proj",
    )(x2, mod, gain.reshape(1, d), *weights)


def _out_proj_kernel(x_ref, mod_ref, g_ref, *refs):
    n = (len(refs) - 1) // 2
    y_refs, w_refs, o_ref = refs[:n], refs[n:2 * n], refs[2 * n]
    acc = None
    for y_ref, w_ref in zip(y_refs, w_refs):
        part = jnp.dot(y_ref[...], w_ref[...], preferred_element_type=F32)
        acc = part if acc is None else acc + part
    gate = mod_ref[...][2:3]
    o_ref[...] = x_ref[...] + gate * (_rms(acc) * g_ref[...])


def out_proj(x2, mod, gain, ys, weights, seq, tm=512):
    m, d = x2.shape
    per_batch = seq // tm
    in_specs = [pl.BlockSpec((tm, d), lambda i: (i, 0)),
                pl.BlockSpec((None, 3, d), lambda i: (i // per_batch, 0, 0)),
                _full_spec((1, d))]
    in_specs += [pl.BlockSpec((tm, y.shape[1]), lambda i: (i, 0)) for y in ys]
    in_specs += [_full_spec(w.shape) for w in weights]
    return pl.pallas_call(
        _out_proj_kernel,
        grid=(m // tm,),
        in_specs=in_specs,
        out_specs=pl.BlockSpec((tm, d), lambda i: (i, 0)),
        out_shape=jax.ShapeDtypeStruct((m, d), F32),
        compiler_params=_params(("parallel",)),
        name="out_proj",
    )(x2, mod, gain.reshape(1, d), *ys, *weights)


def _mlp_kernel(x_ref, mod_ref, g_pre_ref, g_post_ref, w1_ref, w2_ref, o_ref, *, ff_chunk):
    x = x_ref[...]
    mod = mod_ref[...]
    h = _rms(x) * g_pre_ref[...] * (1.0 + mod[1:2]) + mod[0:1]
    hb = h.astype(BF16)
    d_ff = w1_ref.shape[1]
    acc = None
    for c0 in range(0, d_ff, ff_chunk):
        a = jnp.dot(hb, w1_ref[:, c0:c0 + ff_chunk], preferred_element_type=F32)
        a = jnp.square(jnp.maximum(a, 0.0)).astype(BF16)
        part = jnp.dot(a, w2_ref[c0:c0 + ff_chunk, :], preferred_element_type=F32)
        acc = part if acc is None else acc + part
    o_ref[...] = x + mod[2:3] * (_rms(acc) * g_post_ref[...])


def mlp(x2, mod, g_pre, g_post, w1, w2, layer, seq, tm=512, ff_chunk=1024):
    m, d = x2.shape
    per_batch = seq // tm
    return pl.pallas_call(
        functools.partial(_mlp_kernel, ff_chunk=ff_chunk),
        grid=(m // tm,),
        in_specs=[pl.BlockSpec((tm, d), lambda i: (i, 0)),
                  pl.BlockSpec((None, 3, d), lambda i: (i // per_batch, 0, 0)),
                  _full_spec((1, d)), _full_spec((1, d)),
                  pl.BlockSpec((None,) + w1.shape[1:], lambda i: (layer, 0, 0)),
                  pl.BlockSpec((None,) + w2.shape[1:], lambda i: (layer, 0, 0))],
        out_specs=pl.BlockSpec((tm, d), lambda i: (i, 0)),
        out_shape=jax.ShapeDtypeStruct((m, d), F32),
        compiler_params=_params(("parallel",)),
        name="mlp",
    )(x2, mod, g_pre.reshape(1, d), g_post.reshape(1, d), w1, w2)


def _ssd_kernel(z_ref, xbc_ref, dt_ref, cw_ref, cb_ref, dtb_ref, alog_ref, dskip_ref, nw_ref, expand_ref,
                shift_ref, o_ref, conv_buf, state_ref, *, d_inner):
    chunk = CHUNK
    rows_per_step = z_ref.shape[0]
    n_state = SSD_STATE
    gn = SSD_GROUPS * n_state
    pair = 2 * SSD_HEAD_DIM
    heads_per_group = d_inner // SSD_HEAD_DIM // SSD_GROUPS
    halo = conv_buf.shape[0] - rows_per_step

    @pl.when(pl.program_id(1) == 0)
    def _():
        state_ref[...] = jnp.zeros_like(state_ref)
        conv_buf[0:halo, :] = jnp.zeros((halo, conv_buf.shape[1]), BF16)

    conv_buf[halo:, :] = xbc_ref[...]

    def chunk_body(r0):
        rows = slice(r0, r0 + chunk)
        cur = xbc_ref[rows, :]
        shifted = jnp.dot(shift_ref[...], conv_buf[r0:r0 + halo + chunk, :], preferred_element_type=F32)
        acc = cb_ref[...] + cw_ref[SSD_CONV - 1:SSD_CONV, :] * cur.astype(F32)
        for tap in range(SSD_CONV - 1):
            acc = acc + cw_ref[tap:tap + 1, :] * shifted[tap * chunk:(tap + 1) * chunk, :]
        xbc = _silu(acc)
        xs = xbc[:, :d_inner]
        b_all = xbc[:, d_inner:d_inner + gn]
        c_all = xbc[:, d_inner + gn:]

        dt_in = dt_ref[rows, :] + dtb_ref[...]
        dt = jnp.maximum(dt_in, 0.0) + jnp.log1p(jnp.exp(-jnp.abs(dt_in)))
        a = dt * (-jnp.exp(alog_ref[...]))
        row = lax.broadcasted_iota(jnp.int32, (chunk, chunk), 0)
        col = lax.broadcasted_iota(jnp.int32, (chunk, chunk), 1)
        causal = row >= col
        tril = jnp.where(causal, 1.0, 0.0).astype(BF16)
        parts = jnp.dot(tril, jnp.concatenate(_split3(a), axis=1), preferred_element_type=F32)
        a_cs = parts[:, :LANES] + parts[:, LANES:2 * LANES] + parts[:, 2 * LANES:]
        a_cs_t = a_cs.T
        lhs = jnp.concatenate([jnp.concatenate(_split3(dt), axis=1), jnp.concatenate(_split3(a_cs), axis=1)], axis=0)
        expanded = jnp.dot(lhs, expand_ref[...], preferred_element_type=F32)
        dt_e, acs_e = expanded[:chunk], expanded[chunk:]
        a_last_e = acs_e[chunk - 1:chunk, :]
        x_dt = xs * dt_e
        w_state = (x_dt * jnp.exp(a_last_e - acs_e)).astype(BF16)
        out_scale = jnp.exp(acs_e)
        chunk_decay = jnp.exp(a_last_e)
        x_dt_b = x_dt.astype(BF16)
        lane = lax.broadcasted_iota(jnp.int32, (chunk, pair), 1)
        first_head = lane < SSD_HEAD_DIM

        y_blocks = []
        for g in range(SSD_GROUPS):
            bg = b_all[:, g * n_state:(g + 1) * n_state]
            cg = c_all[:, g * n_state:(g + 1) * n_state].astype(BF16)
            cb = lax.dot_general(cg, bg.astype(BF16), NT_DIMS, preferred_element_type=F32)
            bg_t = bg.T.astype(BF16)
            for pi in range(heads_per_group // 2):
                p = g * (heads_per_group // 2) + pi
                sl = slice(p * pair, (p + 1) * pair)
                xp = x_dt_b[:, sl]
                y_diag = None
                for k in range(2):
                    hh = 2 * p + k
                    seg = a_cs[:, hh:hh + 1] - a_cs_t[hh:hh + 1, :]
                    decay = jnp.exp(jnp.where(causal, seg, MASK_VALUE))
                    m = (cb * decay).astype(BF16)
                    xh = jnp.where(first_head if k == 0 else jnp.logical_not(first_head), xp, jnp.zeros_like(xp))
                    part = jnp.dot(m, xh, preferred_element_type=F32)
                    y_diag = part if y_diag is None else y_diag + part
                st = state_ref[p]
                y_off = jnp.dot(cg, st.astype(BF16), preferred_element_type=F32) * out_scale[:, sl]
                state_ref[p] = st * chunk_decay[:, sl] + jnp.dot(bg_t, w_state[:, sl], preferred_element_type=F32)
                y_blocks.append(y_diag + y_off)
        y = jnp.concatenate(y_blocks, axis=-1) + xs * dskip_ref[...]
        y = y * _silu(z_ref[rows, :].astype(F32))
        o_ref[rows, :] = (_rms(y) * nw_ref[...]).astype(o_ref.dtype)

    for r0 in range(0, rows_per_step, chunk):
        chunk_body(r0)
    conv_buf[0:halo, :] = conv_buf[rows_per_step:rows_per_step + halo, :]


def ssd_mixer(z, xbc, dt_raw, conv_w, conv_b, dt_bias, a_log, d_skip, norm_w, batch, seq, chunks_per_step=4):
    m, d_inner = z.shape
    conv_ch = xbc.shape[1]
    n_heads = d_inner // SSD_HEAD_DIM
    rows = chunks_per_step * CHUNK
    nc = seq // rows
    pad = LANES - n_heads
    expand = np.zeros((LANES, d_inner), np.float32)
    for h in range(n_heads):
        expand[h, h * SSD_HEAD_DIM:(h + 1) * SSD_HEAD_DIM] = 1.0
    expand3 = jnp.asarray(np.tile(expand, (3, 1)), BF16)
    halo = 2 * SUBLANES
    shift = np.zeros(((SSD_CONV - 1) * CHUNK, halo + CHUNK), np.float32)
    for tap in range(SSD_CONV - 1):
        shift[tap * CHUNK + np.arange(CHUNK), halo - (SSD_CONV - 1) + tap + np.arange(CHUNK)] = 1.0
    shift = jnp.asarray(shift, BF16)
    row_spec = lambda width: pl.BlockSpec((rows, width), lambda b, c: (b * nc + c, 0))
    return pl.pallas_call(
        functools.partial(_ssd_kernel, d_inner=d_inner),
        grid=(batch, nc),
        in_specs=[row_spec(d_inner), row_spec(conv_ch), row_spec(LANES),
                  _full_spec((SSD_CONV, conv_ch)), _full_spec((1, conv_ch)),
                  _full_spec((1, LANES)), _full_spec((1, LANES)),
                  _full_spec((1, d_inner)), _full_spec((1, d_inner)), _full_spec(expand3.shape),
                  _full_spec(shift.shape)],
        out_specs=row_spec(d_inner),
        out_shape=jax.ShapeDtypeStruct((m, d_inner), BF16),
        scratch_shapes=[pltpu.VMEM((halo + rows, conv_ch), BF16),
                        pltpu.VMEM((n_heads // 2, SSD_STATE, 2 * SSD_HEAD_DIM), F32)],
        compiler_params=_params(("parallel", "arbitrary")),
        name="ssd_mixer",
    )(z, xbc, dt_raw, conv_w, conv_b.reshape(1, conv_ch),
      jnp.pad(dt_bias, (0, pad)).reshape(1, LANES), jnp.pad(a_log, (0, pad)).reshape(1, LANES),
      jnp.repeat(d_skip, SSD_HEAD_DIM).reshape(1, d_inner), norm_w.reshape(1, d_inner), expand3, shift)


def _diff_attn_kernel(q_ref, k_ref, v_ref, bias_ref, lam_ref, nw_ref, o_ref,
                      vt_ref, s0_ref, s1_ref, cmax0_ref, cmax1_ref, p_ref, m_ref, alpha_ref, acc_ref,
                      *, lam_init):
    s_refs, cmax_refs = (s0_ref, s1_ref), (cmax0_ref, cmax1_ref)
    t = s0_ref.shape[0]
    tq = q_ref.shape[0]
    head_w = v_ref.shape[1]
    ext_rows = vt_ref.shape[1]
    i = pl.program_id(2)
    q_t = q_ref[...].astype(F32).T.astype(BF16)
    dim = lax.broadcasted_iota(jnp.int32, q_t.shape, 0)
    zero = jnp.zeros_like(q_t)
    qq_t = jnp.concatenate([jnp.where(dim < DIFF_HEAD_DIM, q_t, zero),
                            jnp.where(dim >= DIFF_HEAD_DIM, q_t, zero)], axis=1)

    @pl.when(i == 0)
    def _():
        for j in range(vt_ref.shape[0]):
            vt_ref[j, :head_w, :] = v_ref[j * t:(j + 1) * t, :].astype(F32).T.astype(BF16)
            vt_ref[j, head_w:, :] = jnp.ones((ext_rows - head_w, t), BF16)

    m_ref[...] = jnp.full(m_ref.shape, MASK_VALUE, F32)
    acc_ref[...] = jnp.zeros(acc_ref.shape, F32)

    diag_tile, prev_tile = bias_ref[0], bias_ref[1]
    all_cols = ((0, t), (t, t), (tq, t), (tq + t, t))
    near_cols = (all_cols[1::2], all_cols, all_cols)
    near_tiles = ((diag_tile, diag_tile),
                  (diag_tile, prev_tile, diag_tile, prev_tile),
                  (prev_tile, None, prev_tile, None))
    near_blocks = (2 * i + 1, 2 * i, jnp.maximum(2 * i - 1, 0))

    def raw_logits(block, cols=all_cols):
        kb = k_ref[pl.ds(pl.multiple_of(block * t, t), t), :]
        rhs = qq_t if cols is all_cols else jnp.concatenate([qq_t[:, c0:c0 + w] for c0, w in cols], axis=1)
        return jnp.dot(kb, rhs, preferred_element_type=F32)

    def store_logits(pieces, slot, cols=all_cols):
        for piece, (c0, w) in zip(pieces, cols):
            s_refs[slot][:, c0:c0 + w] = piece
            chains = [piece[0:SUBLANES, :], piece[SUBLANES:2 * SUBLANES, :]]
            for n, r in enumerate(range(2 * SUBLANES, t, SUBLANES)):
                chains[n % 2] = jnp.maximum(chains[n % 2], piece[r:r + SUBLANES, :])
            col_max = jnp.maximum(chains[0], chains[1])
            for shift in (4, 2, 1):
                col_max = jnp.maximum(col_max, pltpu.roll(col_max, shift, 0))
            cmax_refs[slot][:, c0:c0 + w] = col_max

    def split(s, cols):
        widths = np.cumsum([0] + [w for _, w in cols])
        return [s[:, widths[n]:widths[n + 1]] for n in range(len(cols))]

    def near_logits(visit, slot):
        cols = near_cols[visit]
        pieces = split(raw_logits(near_blocks[visit], cols), cols)
        pieces = [p if tile is None else p + tile for p, tile in zip(pieces, near_tiles[visit])]
        store_logits(pieces, slot, cols)

    def far_logits(visit, slot):
        store_logits(split(raw_logits(visit - len(near_blocks)), all_cols), slot)

    def block_of(visit):
        return near_blocks[visit] if isinstance(visit, int) else visit - len(near_blocks)

    def softmax_accumulate(visit, slot, col_ranges=((0, 2 * tq),)):
        s_ref = s_refs[slot]
        vt = vt_ref[block_of(visit)]
        packed = 2 * SUBLANES
        for c0, w in col_ranges:
            c = slice(c0, c0 + w)
            m_old = m_ref[:, c]
            m_new = jnp.maximum(m_old, cmax_refs[slot][:, c])
            m_ref[:, c] = m_new
            alpha_ref[:, c] = jnp.exp2(m_old - m_new)
            m_tile = jnp.concatenate([m_new, m_new], axis=0)
            for r in range(0, t, packed):
                p_ref[r:r + packed, c] = jnp.exp2(s_ref[r:r + packed, c] - m_tile).astype(BF16)
            pv = jnp.dot(vt, p_ref[:, c], preferred_element_type=F32)
            alpha = alpha_ref[:, c]
            for r in range(0, ext_rows, SUBLANES):
                acc_ref[r:r + SUBLANES, c] = alpha * acc_ref[r:r + SUBLANES, c] + pv[r:r + SUBLANES, :]

    def pipeline_step(visit, slot):
        far_logits(visit, slot)
        softmax_accumulate(visit - 1, 1 - slot)

    near_logits(0, 0)
    near_logits(1, 1)
    softmax_accumulate(0, 0, near_cols[0])
    near_logits(2, 0)
    softmax_accumulate(1, 1)

    @pl.when(i > 0)
    def _():
        far_logits(3, 1)
        softmax_accumulate(2, 0)
        last = 2 * i + 1

        def pair(k, _):
            pipeline_step(4 + 2 * k, 0)

            @pl.when(5 + 2 * k <= last)
            def _():
                pipeline_step(5 + 2 * k, 1)

            return 0

        lax.fori_loop(0, i - 1, pair, 0)
        softmax_accumulate(last, 1)

    lam = lam_ref[...]
    lam_full = (jnp.exp(jnp.sum(lam[0:1] * lam[1:2], axis=-1, keepdims=True))
                - jnp.exp(jnp.sum(lam[2:3] * lam[3:4], axis=-1, keepdims=True)) + lam_init)
    inv_sum = 1.0 / acc_ref[head_w:head_w + SUBLANES, :]
    out_t = acc_ref[:head_w, :] * jnp.tile(inv_sum, (head_w // SUBLANES, 1))
    out = (out_t[:, :tq] - lam_full * out_t[:, tq:]).T
    o_ref[...] = (_rms(out) * nw_ref[...] * (1.0 - lam_init)).astype(o_ref.dtype)


def diff_attention(q, k, v, bias, lam, subln_w, batch, seq, lam_init):
    m, width = q.shape
    head_w = 2 * DIFF_HEAD_DIM
    n_heads = width // head_w
    t = ATTN_BLOCK
    tq = 2 * t
    nq = seq // tq
    cols = 2 * tq
    ext_rows = head_w + 2 * SUBLANES
    return pl.pallas_call(
        functools.partial(_diff_attn_kernel, lam_init=lam_init),
        grid=(batch, n_heads, nq),
        in_specs=[pl.BlockSpec((tq, head_w), lambda b, h, i: (b * nq + i, h)),
                  pl.BlockSpec((seq, head_w), lambda b, h, i: (b, h)),
                  pl.BlockSpec((seq, head_w), lambda b, h, i: (b, h)),
                  pl.BlockSpec((None, 2, t, t), lambda b, h, i: (h, 0, 0, 0)),
                  _full_spec(lam.shape), _full_spec((1, head_w))],
        out_specs=pl.BlockSpec((tq, head_w), lambda b, h, i: (b * nq + i, h)),
        out_shape=jax.ShapeDtypeStruct((m, width), BF16),
        scratch_shapes=[pltpu.VMEM((seq // t, ext_rows, t), BF16),
                        pltpu.VMEM((t, cols), F32), pltpu.VMEM((t, cols), F32),
                        pltpu.VMEM((SUBLANES, cols), F32), pltpu.VMEM((SUBLANES, cols), F32),
                        pltpu.VMEM((t, cols), BF16),
                        pltpu.VMEM((SUBLANES, cols), F32),
                        pltpu.VMEM((SUBLANES, cols), F32),
                        pltpu.VMEM((ext_rows, cols), F32)],
        compiler_params=_params(("parallel", "parallel", "arbitrary")),
        name="diff_attention",
    )(q, k, v, bias, lam, subln_w.reshape(1, head_w))


def _retention_kernel(q_ref, k_ref, v_ref, g_ref, cos_hi_ref, sin_hi_ref, cos_lo_ref, sin_lo_ref, sign_ref,
                      decay_ref, zeta_ref, xi_ref, o_ref, state_ref, *, chunk_decay):
    @pl.when(pl.program_id(1) == 0)
    def _():
        state_ref[...] = jnp.zeros_like(state_ref)

    c_hi, s_hi = cos_hi_ref[...], sin_hi_ref[...]
    c_lo, s_lo = cos_lo_ref[...], sin_lo_ref[...]
    cos_all = c_hi * c_lo - s_hi * s_lo
    sin_all = (s_hi * c_lo + c_hi * s_lo) * sign_ref[...]

    half = RET_QK_DIM // 2
    for h in range(RET_HEADS):
        qs = slice(h * RET_QK_DIM, (h + 1) * RET_QK_DIM)
        vs = slice(h * RET_V_DIM, (h + 1) * RET_V_DIM)
        st = state_ref[h]
        for r0 in range(0, q_ref.shape[0], CHUNK):
            rows = slice(r0, r0 + CHUNK)
            cos, sin = cos_all[rows, :], sin_all[rows, :]

            def rope(t):
                return t * cos + pltpu.roll(t, half, 1) * sin

            qr = rope(q_ref[rows, qs].astype(F32))
            kr = rope(k_ref[rows, qs].astype(F32)) * (RET_QK_DIM ** -0.5)
            vh = v_ref[rows, vs]
            scores = lax.dot_general(qr.astype(BF16), kr.astype(BF16), NT_DIMS,
                                     preferred_element_type=F32) * decay_ref[h]
            inner = jnp.dot(scores.astype(BF16), vh, preferred_element_type=F32)
            cross = jnp.dot((qr * xi_ref[h]).astype(BF16), st.astype(BF16), preferred_element_type=F32)
            kv = jnp.dot((kr.T * zeta_ref[h]).astype(BF16), vh, preferred_element_type=F32)
            st = st * chunk_decay[h] + kv
            out = _rms(inner + cross) * _silu(g_ref[rows, vs].astype(F32))
            o_ref[rows, vs] = out.astype(o_ref.dtype)
        state_ref[h] = st


def retention(rq, rk, rv, rg, batch, seq, chunks_per_step=8):
    m = rq.shape[0]
    rows = chunks_per_step * CHUNK
    nc = seq // rows
    half = RET_QK_DIM // 2
    inv = ROPE_BASE ** (-jnp.arange(half, dtype=F32) / half)
    both_halves = lambda a: jnp.concatenate([a, a], axis=-1)
    ang_hi = (jnp.arange(nc, dtype=F32) * rows)[:, None] * inv[None]
    ang_lo = jnp.arange(rows, dtype=F32)[:, None] * inv[None]
    cos_hi = both_halves(jnp.cos(ang_hi)).reshape(nc, 1, RET_QK_DIM)
    sin_hi = both_halves(jnp.sin(ang_hi)).reshape(nc, 1, RET_QK_DIM)
    cos_lo, sin_lo = both_halves(jnp.cos(ang_lo)), both_halves(jnp.sin(ang_lo))
    sign = jnp.concatenate([-jnp.ones((1, half), F32), jnp.ones((1, half), F32)], axis=-1)
    log_gamma = jnp.log(1.0 - 2.0 ** (-5.0 - jnp.arange(RET_HEADS, dtype=F32)))
    idx = jnp.arange(CHUNK, dtype=F32)
    rel = idx[:, None] - idx[None, :]
    decay = jnp.where(rel >= 0, jnp.exp(jnp.maximum(rel, 0.0)[None] * log_gamma[:, None, None]), 0.0)
    zeta = jnp.exp((CHUNK - 1 - idx)[None] * log_gamma[:, None])[:, None, :]
    xi = jnp.broadcast_to(jnp.exp((idx + 1.0)[None] * log_gamma[:, None])[:, :, None],
                          (RET_HEADS, CHUNK, RET_QK_DIM))
    gamma = 1.0 - 2.0 ** (-5.0 - np.arange(RET_HEADS, dtype=np.float64))
    chunk_decay = tuple(float(g ** CHUNK) for g in gamma)
    row_spec = lambda width: pl.BlockSpec((rows, width), lambda b, c: (b * nc + c, 0))
    return pl.pallas_call(
        functools.partial(_retention_kernel, chunk_decay=chunk_decay),
        grid=(batch, nc),
        in_specs=[row_spec(rq.shape[1]), row_spec(rk.shape[1]), row_spec(rv.shape[1]), row_spec(rg.shape[1]),
                  pl.BlockSpec((None, 1, RET_QK_DIM), lambda b, c: (c, 0, 0)),
                  pl.BlockSpec((None, 1, RET_QK_DIM), lambda b, c: (c, 0, 0)),
                  _full_spec(cos_lo.shape), _full_spec(sin_lo.shape), _full_spec(sign.shape),
                  _full_spec(decay.shape), _full_spec(zeta.shape), _full_spec(xi.shape)],
        out_specs=row_spec(rv.shape[1]),
        out_shape=jax.ShapeDtypeStruct((m, rv.shape[1]), BF16),
        scratch_shapes=[pltpu.VMEM((RET_HEADS, RET_QK_DIM, RET_V_DIM), F32)],
        compiler_params=_params(("parallel", "arbitrary")),
        name="retention",
    )(rq, rk, rv, rg, cos_hi, sin_hi, cos_lo, sin_lo, sign, decay, zeta, xi)


def _swa_kernel(sink_ref, q_ref, kp_ref, kc_ref, vp_ref, vc_ref, bias_ref, o_ref):
    i = pl.program_id(1)
    blk = SWA_BLOCK
    rep = SWA_HEADS // SWA_KV_HEADS
    d = SWA_HEAD_DIM
    cols = rep * blk
    k_all = jnp.concatenate([kp_ref[...], kc_ref[...]], axis=0)
    v_all = jnp.concatenate([vp_ref[...], vc_ref[...]], axis=0)
    key_row = lax.broadcasted_iota(jnp.int32, (2 * blk, cols), 0)
    head_of_lane = lax.broadcasted_iota(jnp.int32, (SUBLANES, cols), 1) // blk
    zero = jnp.zeros((d, blk), F32)
    ones = jnp.ones((2 * SUBLANES, 2 * blk), F32)
    for r0 in range(0, q_ref.shape[0], blk):
        kb = k_all[r0:r0 + 2 * blk]
        v_t = v_all[r0:r0 + 2 * blk].astype(F32).T
        outs = []
        for g in range(SWA_KV_HEADS):
            q_t = q_ref[r0:r0 + blk, g * rep * d:(g + 1) * rep * d].astype(F32).T
            rhs = jnp.concatenate(
                [jnp.concatenate([q_t[r * d:(r + 1) * d], zero] if g == 0 else [zero, q_t[r * d:(r + 1) * d]],
                                 axis=0) for r in range(rep)], axis=1).astype(BF16)
            s = jnp.dot(kb, rhs, preferred_element_type=F32)
            s = s + jnp.concatenate([bias_ref[g * rep + r] for r in range(rep)], axis=1)
            if r0 == 0:
                s = jnp.where(jnp.logical_and(key_row < blk, i == 0), MASK_VALUE, s)
            sink = jnp.zeros((SUBLANES, cols), F32)
            for r in range(rep):
                sink = jnp.where(head_of_lane == r, sink_ref[g * rep + r] * LOG2_E, sink)
            chains = [s[0:SUBLANES], s[SUBLANES:2 * SUBLANES]]
            for n, r in enumerate(range(2 * SUBLANES, 2 * blk, SUBLANES)):
                chains[n % 2] = jnp.maximum(chains[n % 2], s[r:r + SUBLANES])
            col_max = jnp.maximum(chains[0], chains[1])
            for shift in (4, 2, 1):
                col_max = jnp.maximum(col_max, pltpu.roll(col_max, shift, 0))
            m = jnp.maximum(col_max, sink)
            p = jnp.exp2(s - jnp.tile(m, (2 * blk // SUBLANES, 1))).astype(BF16)
            v_ext = jnp.concatenate([v_t[g * d:(g + 1) * d], ones], axis=0).astype(BF16)
            pv = jnp.dot(v_ext, p, preferred_element_type=F32)
            inv = 1.0 / (pv[d:d + SUBLANES] + jnp.exp2(sink - m))
            o_t = pv[:d] * jnp.tile(inv, (d // SUBLANES, 1))
            for j in range(rep // 2):
                pair_t = jnp.concatenate([o_t[:, (2 * j) * blk:(2 * j + 1) * blk],
                                          o_t[:, (2 * j + 1) * blk:(2 * j + 2) * blk]], axis=0)
                outs.append(pair_t.T)
        o_ref[r0:r0 + blk, :] = jnp.concatenate(outs, axis=-1).astype(o_ref.dtype)


def sliding_window_attention(sq, sk, sv, sinks, bias, batch, seq, blocks_per_step=8):
    m, qw = sq.shape
    kw = sk.shape[1]
    rows = blocks_per_step * SWA_BLOCK
    nb = seq // rows
    cur = lambda b, i: (b * nb + i, 0)
    prev = lambda b, i: ((b * nb + i) * blocks_per_step - jnp.minimum(i, 1), 0)
    return pl.pallas_call(
        _swa_kernel,
        grid=(batch, nb),
        in_specs=[pl.BlockSpec(memory_space=pltpu.SMEM),
                  pl.BlockSpec((rows, qw), cur),
                  pl.BlockSpec((SWA_BLOCK, kw), prev), pl.BlockSpec((rows, kw), cur),
                  pl.BlockSpec((SWA_BLOCK, kw), prev), pl.BlockSpec((rows, kw), cur),
                  _full_spec(bias.shape)],
        out_specs=pl.BlockSpec((rows, qw), cur),
        out_shape=jax.ShapeDtypeStruct((m, qw), BF16),
        compiler_params=_params(("parallel", "parallel")),
        name="sliding_window_attention",
    )(sinks, sq, sk, sk, sv, sv, bias)


def _split_cols(w, sizes):
    offs = np.cumsum((0,) + tuple(sizes))
    return [w[:, offs[j]:offs[j + 1]] for j in range(len(sizes))]


def even_layer_mixer(x2, mod, g_pre, g_post, w_in, conv_w, conv_b, dt_bias, a_log, d_skip, ssd_norm, lam,
                     diff_norm, w_out, diff_bias, layer_idx, batch, seq):
    d = x2.shape[1]
    d_inner = d
    n_ssd_heads = d_inner // SSD_HEAD_DIM
    conv_ch = d_inner + 2 * SSD_GROUPS * SSD_STATE
    wz, wxbc, wdt, wq, wk, wv = _split_cols(w_in, (d_inner, conv_ch, n_ssd_heads, d, d, d))
    wdt = jnp.pad(wdt, ((0, 0), (0, LANES - n_ssd_heads)))
    wq = wq * (DIFF_HEAD_DIM ** -0.5 * LOG2_E)
    weights = [w.astype(BF16) for w in (wz, wxbc, wdt, wq, wk, wv)]
    z, xbc, dt_raw, q, k, v = norm_proj(x2, mod, g_pre, weights, (BF16, BF16, F32, BF16, BF16, BF16), seq)
    y_ssd = ssd_mixer(z, xbc, dt_raw, conv_w, conv_b, dt_bias, a_log, d_skip, ssd_norm, batch, seq)
    lam_init = 0.8 - 0.6 * math.exp(-0.3 * layer_idx)
    y_diff = diff_attention(q, k, v, diff_bias, lam, diff_norm, batch, seq, lam_init)
    w_out = w_out.astype(BF16)
    return out_proj(x2, mod, g_post, [y_ssd, y_diff], [w_out[:d_inner], w_out[d_inner:]], seq)


def odd_layer_mixer(x2, mod, g_pre, g_post, w_in, sinks, w_out, swa_bias, batch, seq):
    ret_qk = RET_HEADS * RET_QK_DIM
    ret_v = RET_HEADS * RET_V_DIM
    swa_q = SWA_HEADS * SWA_HEAD_DIM
    swa_kv = SWA_KV_HEADS * SWA_HEAD_DIM
    wrq, wrk, wrv, wrg, wsq, wsk, wsv = _split_cols(w_in, (ret_qk, ret_qk, ret_v, ret_v, swa_q, swa_kv, swa_kv))
    wsq = wsq * (SWA_HEAD_DIM ** -0.5 * LOG2_E)
    weights = [w.astype(BF16) for w in (wrq, wrk, wrv, wrg, wsq, wsk, wsv)]
    rq, rk, rv, rg, sq, sk, sv = norm_proj(x2, mod, g_pre, weights, (BF16,) * 7, seq)
    y_ret = retention(rq, rk, rv, rg, batch, seq)
    y_swa = sliding_window_attention(sq, sk, sv, sinks, swa_bias, batch, seq)
    w_out = w_out.astype(BF16)
    return out_proj(x2, mod, g_post, [y_ret, y_swa], [w_out[:ret_v], w_out[ret_v:]], seq)


def kernel(x, c, rel_bias, norm_gains, mod_w, mod_b, mlp_w1, mlp_w2, e_w_in, e_conv_w, e_conv_b, e_dt_bias,
           e_A_log, e_D, e_ssd_norm, e_lambda, e_diff_norm, e_w_out, o_w_in, o_sinks, o_w_out):
    batch, seq, d = x.shape
    depth = norm_gains.shape[0]
    assert seq % (2 * ATTN_BLOCK) == 0 and d % LANES == 0
    mods = modulation(c, mod_w, mod_b.reshape(depth * 2, 3 * d))
    mlp_w1_b, mlp_w2_b = mlp_w1.astype(BF16), mlp_w2.astype(BF16)
    far_bucket = REL_BUCKETS - 1
    diff_bias = bias_tiles(rel_bias, _diff_bucket_idx(ATTN_BLOCK), shift_bucket=far_bucket, scale=LOG2_E)
    diff_bias = diff_bias.reshape(rel_bias.shape[1], 2, ATTN_BLOCK, ATTN_BLOCK)
    swa_bias = bias_tiles(rel_bias, _swa_bucket_idx(), scale=LOG2_E)

    x2 = x.reshape(batch * seq, d)
    for layer in range(depth):
        j = layer // 2
        gains = norm_gains[layer]
        if layer % 2 == 0:
            x2 = even_layer_mixer(x2, mods[2 * layer], gains[0], gains[1], e_w_in[j], e_conv_w[j], e_conv_b[j],
                                  e_dt_bias[j], e_A_log[j], e_D[j], e_ssd_norm[j], e_lambda[j], e_diff_norm[j],
                                  e_w_out[j], diff_bias, layer, batch, seq)
        else:
            x2 = odd_layer_mixer(x2, mods[2 * layer], gains[0], gains[1], o_w_in[j], o_sinks[j], o_w_out[j],
                                 swa_bias, batch, seq)
        x2 = mlp(x2, mods[2 * layer + 1], gains[2], gains[3], mlp_w1_b, mlp_w2_b, layer, seq)
    return x2.reshape(batch, seq, d)
```

```python
import functools
import math

import jax
import jax.numpy as jnp
import numpy as np
from jax import lax
from jax.experimental import pallas as pl
from jax.experimental.pallas import tpu as pltpu

EPS = 1e-6
MASK_VALUE = -1e30
LOG2_E = math.log2(math.e)
LANES = 128
SUBLANES = 8
VMEM_LIMIT = 56 * 1024 * 1024

CHUNK = 128
REL_BUCKETS = 32
REL_MAX_DIST = 128
SSD_HEAD_DIM = 64
SSD_GROUPS = 4
SSD_STATE = 128
SSD_CONV = 4
DIFF_HEAD_DIM = 64
RET_HEADS = 4
RET_QK_DIM = 128
RET_V_DIM = 256
ROPE_BASE = 10000.0
SWA_HEADS = 8
SWA_KV_HEADS = 2
SWA_HEAD_DIM = 64
SWA_BLOCK = 128
ATTN_BLOCK = 512

BF16 = jnp.bfloat16
F32 = jnp.float32
NT_DIMS = (((1,), (1,)), ((), ()))


def _params(semantics, flags=None):
    return pltpu.CompilerParams(dimension_semantics=semantics, vmem_limit_bytes=VMEM_LIMIT, flags=flags)


def _full_spec(shape):
    return pl.BlockSpec(shape, lambda *_: (0,) * len(shape))


def _silu(x):
    h = 0.5 * x
    return h * jnp.tanh(h) + h


def _split3(x):
    hi = x.astype(BF16)
    r1 = x - hi.astype(F32)
    mid = r1.astype(BF16)
    lo = (r1 - mid.astype(F32)).astype(BF16)
    return hi, mid, lo


def _rms(x, eps=EPS):
    return x * lax.rsqrt(jnp.mean(x * x, axis=-1, keepdims=True) + eps)


def _mod_kernel(c_ref, w_ref, b_ref, o_ref):
    c_act = _silu(c_ref[...])
    o_ref[...] = jnp.dot(c_act, w_ref[...], preferred_element_type=F32,
                         precision=lax.Precision.HIGHEST) + b_ref[...]


def modulation(c, mod_w, mod_b):
    b, d = c.shape
    n = mod_b.shape[0]
    per_layer = mod_w.shape[1]
    rows = SUBLANES
    c_pad = jnp.zeros((rows, d), F32).at[:b].set(c)
    tn = d
    out = pl.pallas_call(
        _mod_kernel,
        grid=(n, 3 * d // tn),
        in_specs=[pl.BlockSpec((rows, d), lambda s, j: (0, 0)),
                  pl.BlockSpec((None, None, d, tn), lambda s, j: (s // per_layer, s % per_layer, 0, j)),
                  pl.BlockSpec((None, 1, tn), lambda s, j: (s, 0, j))],
        out_specs=pl.BlockSpec((None, rows, tn), lambda s, j: (s, 0, j)),
        out_shape=jax.ShapeDtypeStruct((n, rows, 3 * d), F32),
        compiler_params=_params(("parallel", "parallel")),
        name="modulation",
    )(c_pad, mod_w, mod_b.reshape(n, 1, 3 * d))
    return out[:, :b].reshape(n, b, 3, d)


def _t5_bucket_np(dist):
    max_exact = REL_BUCKETS // 2
    logd = np.log(np.maximum(dist, 1).astype(np.float32) / np.float32(max_exact))
    large = max_exact + (logd / np.float32(math.log(REL_MAX_DIST / max_exact))
                         * np.float32(REL_BUCKETS - max_exact)).astype(np.int32)
    large = np.minimum(large, REL_BUCKETS - 1)
    return np.where(dist < max_exact, dist, large).astype(np.int32)


def _bias_kernel(rb_ref, idx_ref, o_ref, *, shift_bucket, scale, block_buckets):
    h = pl.program_id(0)
    shift = rb_ref[shift_bucket, h] if shift_bucket is not None else 0.0

    def value(bucket):
        return (rb_ref[bucket, h] - shift) * scale

    for (r0, c0), buckets in block_buckets:
        window = (slice(r0, r0 + LANES), slice(c0, c0 + LANES))
        idx = idx_ref[window]
        acc = jnp.full(idx.shape, MASK_VALUE if -1 in buckets else value(max(buckets)), F32)
        for bucket in buckets:
            if bucket >= 0 and len(buckets) > 1:
                acc = jnp.where(idx == bucket, value(bucket), acc)
        o_ref[window] = acc


def bias_tiles(rel_bias, bucket_idx, shift_bucket=None, scale=1.0):
    n_heads = rel_bias.shape[1]
    r, c = bucket_idx.shape
    block_buckets = tuple(((r0, c0), tuple(int(b) for b in np.unique(bucket_idx[r0:r0 + LANES, c0:c0 + LANES])))
                          for r0 in range(0, r, LANES) for c0 in range(0, c, LANES))
    return pl.pallas_call(
        functools.partial(_bias_kernel, shift_bucket=shift_bucket, scale=scale, block_buckets=block_buckets),
        grid=(n_heads,),
        in_specs=[pl.BlockSpec(memory_space=pltpu.SMEM),
                  pl.BlockSpec((r, c), lambda h: (0, 0))],
        out_specs=pl.BlockSpec((None, r, c), lambda h: (h, 0, 0)),
        out_shape=jax.ShapeDtypeStruct((n_heads, r, c), F32),
        compiler_params=_params(("parallel",)),
        name="bias_tiles",
    )(rel_bias, jnp.asarray(bucket_idx))


def _diff_bucket_idx(t):
    q = np.arange(t)[None, :]
    k = np.arange(t)[:, None]
    diag = q - k
    idx_diag = np.where(diag >= 0, _t5_bucket_np(np.maximum(diag, 0)), -1)
    idx_prev = _t5_bucket_np(diag + t)
    assert (_t5_bucket_np(diag + 2 * t) == REL_BUCKETS - 1).all()
    return np.concatenate([idx_diag, idx_prev], axis=0).astype(np.int32)


def _swa_bucket_idx():
    q = np.arange(SWA_BLOCK)[None, :]
    k = np.arange(2 * SWA_BLOCK)[:, None] - SWA_BLOCK
    dist = q - k
    valid = (dist >= 0) & (dist < SWA_BLOCK)
    return np.where(valid, _t5_bucket_np(np.maximum(dist, 0)), -1).astype(np.int32)


def _norm_proj_kernel(x_ref, mod_ref, g_ref, *refs):
    n = len(refs) // 2
    w_refs, o_refs = refs[:n], refs[n:]
    mod = mod_ref[...]
    h = _rms(x_ref[...]) * g_ref[...] * (1.0 + mod[1:2]) + mod[0:1]
    hb = h.astype(BF16)
    for w_ref, o_ref in zip(w_refs, o_refs):
        o_ref[...] = jnp.dot(hb, w_ref[...], preferred_element_type=F32).astype(o_ref.dtype)


def norm_proj(x2, mod, gain, weights, out_dtypes, seq, tm=512):
    m, d = x2.shape
    per_batch = seq // tm
    in_specs = [pl.BlockSpec((tm, d), lambda i: (i, 0)),
                pl.BlockSpec((None, 3, d), lambda i: (i // per_batch, 0, 0)),
                _full_spec((1, d))]
    in_specs += [_full_spec(w.shape) for w in weights]
    out_specs = [pl.BlockSpec((tm, w.shape[1]), lambda i: (i, 0)) for w in weights]
    out_shape = [jax.ShapeDtypeStruct((m, w.shape[1]), dt) for w, dt in zip(weights, out_dtypes)]
    return pl.pallas_call(
        _norm_proj_kernel,
        grid=(m // tm,),
        in_specs=in_specs,
        out_specs=out_specs,
        out_shape=out_shape,
        compiler_params=_params(("parallel",)),
        name="norm_proj",
    )(x2, mod, gain.reshape(1, d), *weights)


def _out_proj_kernel(x_ref, mod_ref, g_ref, *refs):
    n = (len(refs) - 1) // 2
    y_refs, w_refs, o_ref = refs[:n], refs[n:2 * n], refs[2 * n]
    acc = None
    for y_ref, w_ref in zip(y_refs, w_refs):
        part = jnp.dot(y_ref[...], w_ref[...], preferred_element_type=F32)
        acc = part if acc is None else acc + part
    gate = mod_ref[...][2:3]
    o_ref[...] = x_ref[...] + gate * (_rms(acc) * g_ref[...])


def out_proj(x2, mod, gain, ys, weights, seq, tm=512):
    m, d = x2.shape
    per_batch = seq // tm
    in_specs = [pl.BlockSpec((tm, d), lambda i: (i, 0)),
                pl.BlockSpec((None, 3, d), lambda i: (i // per_batch, 0, 0)),
                _full_spec((1, d))]
    in_specs += [pl.BlockSpec((tm, y.shape[1]), lambda i: (i, 0)) for y in ys]
    in_specs += [_full_spec(w.shape) for w in weights]
    return pl.pallas_call(
        _out_proj_kernel,
        grid=(m // tm,),
        in_specs=in_specs,
        out_specs=pl.BlockSpec((tm, d), lambda i: (i, 0)),
        out_shape=jax.ShapeDtypeStruct((m, d), F32),
        compiler_params=_params(("parallel",)),
        name="out_proj",
    )(x2, mod, gain.reshape(1, d), *ys, *weights)


def _mlp_kernel(x_ref, mod_ref, g_pre_ref, g_post_ref, w1_ref, w2_ref, o_ref, *, ff_chunk):
    x = x_ref[...]
    mod = mod_ref[...]
    h = _rms(x) * g_pre_ref[...] * (1.0 + mod[1:2]) + mod[0:1]
    hb = h.astype(BF16)
    d_ff = w1_ref.shape[1]
    acc = None
    for c0 in range(0, d_ff, ff_chunk):
        a = jnp.dot(hb, w1_ref[:, c0:c0 + ff_chunk], preferred_element_type=F32)
        a = jnp.square(jnp.maximum(a, 0.0)).astype(BF16)
        part = jnp.dot(a, w2_ref[c0:c0 + ff_chunk, :], preferred_element_type=F32)
        acc = part if acc is None else acc + part
    o_ref[...] = x + mod[2:3] * (_rms(acc) * g_post_ref[...])


def mlp(x2, mod, g_pre, g_post, w1, w2, layer, seq, tm=512, ff_chunk=1024):
    m, d = x2.shape
    per_batch = seq // tm
    return pl.pallas_call(
        functools.partial(_mlp_kernel, ff_chunk=ff_chunk),
        grid=(m // tm,),
        in_specs=[pl.BlockSpec((tm, d), lambda i: (i, 0)),
                  pl.BlockSpec((None, 3, d), lambda i: (i // per_batch, 0, 0)),
                  _full_spec((1, d)), _full_spec((1, d)),
                  pl.BlockSpec((None,) + w1.shape[1:], lambda i: (layer, 0, 0)),
                  pl.BlockSpec((None,) + w2.shape[1:], lambda i: (layer, 0, 0))],
        out_specs=pl.BlockSpec((tm, d), lambda i: (i, 0)),
        out_shape=jax.ShapeDtypeStruct((m, d), F32),
        compiler_params=_params(("parallel",)),
        name="mlp",
    )(x2, mod, g_pre.reshape(1, d), g_post.reshape(1, d), w1, w2)


def _ssd_kernel(z_ref, xbc_ref, dt_ref, cw_ref, cb_ref, dtb_ref, alog_ref, dskip_ref, nw_ref, expand_ref,
                shift_ref, o_ref, conv_buf, state_ref, *, d_inner):
    chunk = CHUNK
    rows_per_step = z_ref.shape[0]
    n_state = SSD_STATE
    gn = SSD_GROUPS * n_state
    pair = 2 * SSD_HEAD_DIM
    heads_per_group = d_inner // SSD_HEAD_DIM // SSD_GROUPS
    halo = conv_buf.shape[0] - rows_per_step

    @pl.when(pl.program_id(1) == 0)
    def _():
        state_ref[...] = jnp.zeros_like(state_ref)
        conv_buf[0:halo, :] = jnp.zeros((halo, conv_buf.shape[1]), BF16)

    conv_buf[halo:, :] = xbc_ref[...]

    def chunk_body(r0):
        rows = slice(r0, r0 + chunk)
        cur = xbc_ref[rows, :]
        shifted = jnp.dot(shift_ref[...], conv_buf[r0:r0 + halo + chunk, :], preferred_element_type=F32)
        acc = cb_ref[...] + cw_ref[SSD_CONV - 1:SSD_CONV, :] * cur.astype(F32)
        for tap in range(SSD_CONV - 1):
            acc = acc + cw_ref[tap:tap + 1, :] * shifted[tap * chunk:(tap + 1) * chunk, :]
        xbc = _silu(acc)
        xs = xbc[:, :d_inner]
        b_all = xbc[:, d_inner:d_inner + gn]
        c_all = xbc[:, d_inner + gn:]

        dt_in = dt_ref[rows, :] + dtb_ref[...]
        dt = jnp.maximum(dt_in, 0.0) + jnp.log1p(jnp.exp(-jnp.abs(dt_in)))
        a = dt * (-jnp.exp(alog_ref[...]))
        row = lax.broadcasted_iota(jnp.int32, (chunk, chunk), 0)
        col = lax.broadcasted_iota(jnp.int32, (chunk, chunk), 1)
        causal = row >= col
        tril = jnp.where(causal, 1.0, 0.0).astype(BF16)
        parts = jnp.dot(tril, jnp.concatenate(_split3(a), axis=1), preferred_element_type=F32)
        a_cs = parts[:, :LANES] + parts[:, LANES:2 * LANES] + parts[:, 2 * LANES:]
        a_cs_t = a_cs.T
        lhs = jnp.concatenate([jnp.concatenate(_split3(dt), axis=1), jnp.concatenate(_split3(a_cs), axis=1)], axis=0)
        expanded = jnp.dot(lhs, expand_ref[...], preferred_element_type=F32)
        dt_e, acs_e = expanded[:chunk], expanded[chunk:]
        a_last_e = acs_e[chunk - 1:chunk, :]
        x_dt = xs * dt_e
        w_state = (x_dt * jnp.exp(a_last_e - acs_e)).astype(BF16)
        out_scale = jnp.exp(acs_e)
        chunk_decay = jnp.exp(a_last_e)
        x_dt_b = x_dt.astype(BF16)
        lane = lax.broadcasted_iota(jnp.int32, (chunk, pair), 1)
        first_head = lane < SSD_HEAD_DIM

        y_blocks = []
        for g in range(SSD_GROUPS):
            bg = b_all[:, g * n_state:(g + 1) * n_state]
            cg = c_all[:, g * n_state:(g + 1) * n_state].astype(BF16)
            cb = lax.dot_general(cg, bg.astype(BF16), NT_DIMS, preferred_element_type=F32)
            bg_t = bg.T.astype(BF16)
            for pi in range(heads_per_group // 2):
                p = g * (heads_per_group // 2) + pi
                sl = slice(p * pair, (p + 1) * pair)
                xp = x_dt_b[:, sl]
                y_diag = None
                for k in range(2):
                    hh = 2 * p + k
                    seg = a_cs[:, hh:hh + 1] - a_cs_t[hh:hh + 1, :]
                    decay = jnp.exp(jnp.where(causal, seg, MASK_VALUE))
                    m = (cb * decay).astype(BF16)
                    xh = jnp.where(first_head if k == 0 else jnp.logical_not(first_head), xp, jnp.zeros_like(xp))
                    part = jnp.dot(m, xh, preferred_element_type=F32)
                    y_diag = part if y_diag is None else y_diag + part
                st = state_ref[p]
                y_off = jnp.dot(cg, st.astype(BF16), preferred_element_type=F32) * out_scale[:, sl]
                state_ref[p] = st * chunk_decay[:, sl] + jnp.dot(bg_t, w_state[:, sl], preferred_element_type=F32)
                y_blocks.append(y_diag + y_off)
        y = jnp.concatenate(y_blocks, axis=-1) + xs * dskip_ref[...]
        y = y * _silu(z_ref[rows, :].astype(F32))
        o_ref[rows, :] = (_rms(y) * nw_ref[...]).astype(o_ref.dtype)

    for r0 in range(0, rows_per_step, chunk):
        chunk_body(r0)
    conv_buf[0:halo, :] = conv_buf[rows_per_step:rows_per_step + halo, :]


def ssd_mixer(z, xbc, dt_raw, conv_w, conv_b, dt_bias, a_log, d_skip, norm_w, batch, seq, chunks_per_step=4):
    m, d_inner = z.shape
    conv_ch = xbc.shape[1]
    n_heads = d_inner // SSD_HEAD_DIM
    rows = chunks_per_step * CHUNK
    nc = seq // rows
    pad = LANES - n_heads
    expand = np.zeros((LANES, d_inner), np.float32)
    for h in range(n_heads):
        expand[h, h * SSD_HEAD_DIM:(h + 1) * SSD_HEAD_DIM] = 1.0
    expand3 = jnp.asarray(np.tile(expand, (3, 1)), BF16)
    halo = 2 * SUBLANES
    shift = np.zeros(((SSD_CONV - 1) * CHUNK, halo + CHUNK), np.float32)
    for tap in range(SSD_CONV - 1):
        shift[tap * CHUNK + np.arange(CHUNK), halo - (SSD_CONV - 1) + tap + np.arange(CHUNK)] = 1.0
    shift = jnp.asarray(shift, BF16)
    row_spec = lambda width: pl.BlockSpec((rows, width), lambda b, c: (b * nc + c, 0))
    return pl.pallas_call(
        functools.partial(_ssd_kernel, d_inner=d_inner),
        grid=(batch, nc),
        in_specs=[row_spec(d_inner), row_spec(conv_ch), row_spec(LANES),
                  _full_spec((SSD_CONV, conv_ch)), _full_spec((1, conv_ch)),
                  _full_spec((1, LANES)), _full_spec((1, LANES)),
                  _full_spec((1, d_inner)), _full_spec((1, d_inner)), _full_spec(expand3.shape),
                  _full_spec(shift.shape)],
        out_specs=row_spec(d_inner),
        out_shape=jax.ShapeDtypeStruct((m, d_inner), BF16),
        scratch_shapes=[pltpu.VMEM((halo + rows, conv_ch), BF16),
                        pltpu.VMEM((n_heads // 2, SSD_STATE, 2 * SSD_HEAD_DIM), F32)],
        compiler_params=_params(("parallel", "arbitrary")),
        name="ssd_mixer",
    )(z, xbc, dt_raw, conv_w, conv_b.reshape(1, conv_ch),
      jnp.pad(dt_bias, (0, pad)).reshape(1, LANES), jnp.pad(a_log, (0, pad)).reshape(1, LANES),
      jnp.repeat(d_skip, SSD_HEAD_DIM).reshape(1, d_inner), norm_w.reshape(1, d_inner), expand3, shift)


def _diff_attn_kernel(q_ref, k_ref, v_ref, bias_ref, lam_ref, nw_ref, o_ref,
                      vt_ref, s0_ref, s1_ref, cmax0_ref, cmax1_ref, p_ref, m_ref, alpha_ref, acc_ref,
                      *, lam_init):
    s_refs, cmax_refs = (s0_ref, s1_ref), (cmax0_ref, cmax1_ref)
    t = s0_ref.shape[0]
    tq = q_ref.shape[0]
    head_w = v_ref.shape[1]
    ext_rows = vt_ref.shape[1]
    i = pl.program_id(2)
    q_t = q_ref[...].astype(F32).T.astype(BF16)
    dim = lax.broadcasted_iota(jnp.int32, q_t.shape, 0)
    zero = jnp.zeros_like(q_t)
    qq_t = jnp.concatenate([jnp.where(dim < DIFF_HEAD_DIM, q_t, zero),
                            jnp.where(dim >= DIFF_HEAD_DIM, q_t, zero)], axis=1)

    @pl.when(i == 0)
    def _():
        for j in range(vt_ref.shape[0]):
            vt_ref[j, :head_w, :] = v_ref[j * t:(j + 1) * t, :].astype(F32).T.astype(BF16)
            vt_ref[j, head_w:, :] = jnp.ones((ext_rows - head_w, t), BF16)

    m_ref[...] = jnp.full(m_ref.shape, MASK_VALUE, F32)
    acc_ref[...] = jnp.zeros(acc_ref.shape, F32)

    diag_tile, prev_tile = bias_ref[0], bias_ref[1]
    all_cols = ((0, t), (t, t), (tq, t), (tq + t, t))
    near_cols = (all_cols[1::2], all_cols, all_cols)
    near_tiles = ((diag_tile, diag_tile),
                  (diag_tile, prev_tile, diag_tile, prev_tile),
                  (prev_tile, None, prev_tile, None))
    near_blocks = (2 * i + 1, 2 * i, jnp.maximum(2 * i - 1, 0))

    def raw_logits(block, cols=all_cols):
        kb = k_ref[pl.ds(pl.multiple_of(block * t, t), t), :]
        rhs = qq_t if cols is all_cols else jnp.concatenate([qq_t[:, c0:c0 + w] for c0, w in cols], axis=1)
        return jnp.dot(kb, rhs, preferred_element_type=F32)

    def store_logits(pieces, slot, cols=all_cols):
        for piece, (c0, w) in zip(pieces, cols):
            s_refs[slot][:, c0:c0 + w] = piece
            chains = [piece[0:SUBLANES, :], piece[SUBLANES:2 * SUBLANES, :]]
            for n, r in enumerate(range(2 * SUBLANES, t, SUBLANES)):
                chains[n % 2] = jnp.maximum(chains[n % 2], piece[r:r + SUBLANES, :])
            col_max = jnp.maximum(chains[0], chains[1])
            for shift in (4, 2, 1):
                col_max = jnp.maximum(col_max, pltpu.roll(col_max, shift, 0))
            cmax_refs[slot][:, c0:c0 + w] = col_max

    def split(s, cols):
        widths = np.cumsum([0] + [w for _, w in cols])
        return [s[:, widths[n]:widths[n + 1]] for n in range(len(cols))]

    def near_logits(visit, slot):
        cols = near_cols[visit]
        pieces = split(raw_logits(near_blocks[visit], cols), cols)
        pieces = [p if tile is None else p + tile for p, tile in zip(pieces, near_tiles[visit])]
        store_logits(pieces, slot, cols)

    def far_logits(visit, slot):
        store_logits(split(raw_logits(visit - len(near_blocks)), all_cols), slot)

    def block_of(visit):
        return near_blocks[visit] if isinstance(visit, int) else visit - len(near_blocks)

    def softmax_accumulate(visit, slot, col_ranges=((0, 2 * tq),)):
        s_ref = s_refs[slot]
        vt = vt_ref[block_of(visit)]
        packed = 2 * SUBLANES
        for c0, w in col_ranges:
            c = slice(c0, c0 + w)
            m_old = m_ref[:, c]
            m_new = jnp.maximum(m_old, cmax_refs[slot][:, c])
            m_ref[:, c] = m_new
            alpha_ref[:, c] = jnp.exp2(m_old - m_new)
            m_tile = jnp.concatenate([m_new, m_new], axis=0)
            for r in range(0, t, packed):
                p_ref[r:r + packed, c] = jnp.exp2((s_ref[r:r + packed, c] - m_tile).astype(BF16))
            pv = jnp.dot(vt, p_ref[:, c], preferred_element_type=F32)
            alpha = alpha_ref[:, c]
            for r in range(0, ext_rows, SUBLANES):
                acc_ref[r:r + SUBLANES, c] = alpha * acc_ref[r:r + SUBLANES, c] + pv[r:r + SUBLANES, :]

    def pipeline_step(visit, slot):
        far_logits(visit, slot)
        softmax_accumulate(visit - 1, 1 - slot)

    near_logits(0, 0)
    near_logits(1, 1)
    softmax_accumulate(0, 0, near_cols[0])
    near_logits(2, 0)
    softmax_accumulate(1, 1)

    @pl.when(i > 0)
    def _():
        far_logits(3, 1)
        softmax_accumulate(2, 0)
        last = 2 * i + 1

        def pair(k, _):
            pipeline_step(4 + 2 * k, 0)

            @pl.when(5 + 2 * k <= last)
            def _():
                pipeline_step(5 + 2 * k, 1)

            return 0

        lax.fori_loop(0, i - 1, pair, 0)
        softmax_accumulate(last, 1)

    lam = lam_ref[...]
    lam_full = (jnp.exp(jnp.sum(lam[0:1] * lam[1:2], axis=-1, keepdims=True))
                - jnp.exp(jnp.sum(lam[2:3] * lam[3:4], axis=-1, keepdims=True)) + lam_init)
    inv_sum = 1.0 / acc_ref[head_w:head_w + SUBLANES, :]
    out_t = acc_ref[:head_w, :] * jnp.tile(inv_sum, (head_w // SUBLANES, 1))
    out = (out_t[:, :tq] - lam_full * out_t[:, tq:]).T
    o_ref[...] = (_rms(out) * nw_ref[...] * (1.0 - lam_init)).astype(o_ref.dtype)


def diff_attention(q, k, v, bias, lam, subln_w, batch, seq, lam_init):
    m, width = q.shape
    head_w = 2 * DIFF_HEAD_DIM
    n_heads = width // head_w
    t = ATTN_BLOCK
    tq = 2 * t
    nq = seq // tq
    cols = 2 * tq
    ext_rows = head_w + 2 * SUBLANES
    return pl.pallas_call(
        functools.partial(_diff_attn_kernel, lam_init=lam_init),
        grid=(batch, n_heads, nq),
        in_specs=[pl.BlockSpec((tq, head_w), lambda b, h, i: (b * nq + i, h)),
                  pl.BlockSpec((seq, head_w), lambda b, h, i: (b, h)),
                  pl.BlockSpec((seq, head_w), lambda b, h, i: (b, h)),
                  pl.BlockSpec((None, 2, t, t), lambda b, h, i: (h, 0, 0, 0)),
                  _full_spec(lam.shape), _full_spec((1, head_w))],
        out_specs=pl.BlockSpec((tq, head_w), lambda b, h, i: (b * nq + i, h)),
        out_shape=jax.ShapeDtypeStruct((m, width), BF16),
        scratch_shapes=[pltpu.VMEM((seq // t, ext_rows, t), BF16),
                        pltpu.VMEM((t, cols), F32), pltpu.VMEM((t, cols), F32),
                        pltpu.VMEM((SUBLANES, cols), F32), pltpu.VMEM((SUBLANES, cols), F32),
                        pltpu.VMEM((t, cols), BF16),
                        pltpu.VMEM((SUBLANES, cols), F32),
                        pltpu.VMEM((SUBLANES, cols), F32),
                        pltpu.VMEM((ext_rows, cols), F32)],
        compiler_params=_params(("parallel", "parallel", "arbitrary")),
        name="diff_attention",
    )(q, k, v, bias, lam, subln_w.reshape(1, head_w))


def _retention_kernel(q_ref, k_ref, v_ref, g_ref, cos_hi_ref, sin_hi_ref, cos_lo_ref, sin_lo_ref, sign_ref,
                      decay_ref, zeta_ref, xi_ref, o_ref, state_ref, *, chunk_decay):
    @pl.when(pl.program_id(1) == 0)
    def _():
        state_ref[...] = jnp.zeros_like(state_ref)

    c_hi, s_hi = cos_hi_ref[...], sin_hi_ref[...]
    c_lo, s_lo = cos_lo_ref[...], sin_lo_ref[...]
    cos_all = c_hi * c_lo - s_hi * s_lo
    sin_all = (s_hi * c_lo + c_hi * s_lo) * sign_ref[...]

    half = RET_QK_DIM // 2
    for h in range(RET_HEADS):
        qs = slice(h * RET_QK_DIM, (h + 1) * RET_QK_DIM)
        vs = slice(h * RET_V_DIM, (h + 1) * RET_V_DIM)
        st = state_ref[h]
        for r0 in range(0, q_ref.shape[0], CHUNK):
            rows = slice(r0, r0 + CHUNK)
            cos, sin = cos_all[rows, :], sin_all[rows, :]

            def rope(t):
                return t * cos + pltpu.roll(t, half, 1) * sin

            qr = rope(q_ref[rows, qs].astype(F32))
            kr = rope(k_ref[rows, qs].astype(F32)) * (RET_QK_DIM ** -0.5)
            vh = v_ref[rows, vs]
            scores = lax.dot_general(qr.astype(BF16), kr.astype(BF16), NT_DIMS,
                                     preferred_element_type=F32) * decay_ref[h]
            inner = jnp.dot(scores.astype(BF16), vh, preferred_element_type=F32)
            cross = jnp.dot((qr * xi_ref[h]).astype(BF16), st.astype(BF16), preferred_element_type=F32)
            kv = jnp.dot((kr.T * zeta_ref[h]).astype(BF16), vh, preferred_element_type=F32)
            st = st * chunk_decay[h] + kv
            out = _rms(inner + cross) * _silu(g_ref[rows, vs].astype(F32))
            o_ref[rows, vs] = out.astype(o_ref.dtype)
        state_ref[h] = st


def retention(rq, rk, rv, rg, batch, seq, chunks_per_step=8):
    m = rq.shape[0]
    rows = chunks_per_step * CHUNK
    nc = seq // rows
    half = RET_QK_DIM // 2
    inv = ROPE_BASE ** (-jnp.arange(half, dtype=F32) / half)
    both_halves = lambda a: jnp.concatenate([a, a], axis=-1)
    ang_hi = (jnp.arange(nc, dtype=F32) * rows)[:, None] * inv[None]
    ang_lo = jnp.arange(rows, dtype=F32)[:, None] * inv[None]
    cos_hi = both_halves(jnp.cos(ang_hi)).reshape(nc, 1, RET_QK_DIM)
    sin_hi = both_halves(jnp.sin(ang_hi)).reshape(nc, 1, RET_QK_DIM)
    cos_lo, sin_lo = both_halves(jnp.cos(ang_lo)), both_halves(jnp.sin(ang_lo))
    sign = jnp.concatenate([-jnp.ones((1, half), F32), jnp.ones((1, half), F32)], axis=-1)
    log_gamma = jnp.log(1.0 - 2.0 ** (-5.0 - jnp.arange(RET_HEADS, dtype=F32)))
    idx = jnp.arange(CHUNK, dtype=F32)
    rel = idx[:, None] - idx[None, :]
    decay = jnp.where(rel >= 0, jnp.exp(jnp.maximum(rel, 0.0)[None] * log_gamma[:, None, None]), 0.0)
    zeta = jnp.exp((CHUNK - 1 - idx)[None] * log_gamma[:, None])[:, None, :]
    xi = jnp.broadcast_to(jnp.exp((idx + 1.0)[None] * log_gamma[:, None])[:, :, None],
                          (RET_HEADS, CHUNK, RET_QK_DIM))
    gamma = 1.0 - 2.0 ** (-5.0 - np.arange(RET_HEADS, dtype=np.float64))
    chunk_decay = tuple(float(g ** CHUNK) for g in gamma)
    row_spec = lambda width: pl.BlockSpec((rows, width), lambda b, c: (b * nc + c, 0))
    return pl.pallas_call(
        functools.partial(_retention_kernel, chunk_decay=chunk_decay),
        grid=(batch, nc),
        in_specs=[row_spec(rq.shape[1]), row_spec(rk.shape[1]), row_spec(rv.shape[1]), row_spec(rg.shape[1]),
                  pl.BlockSpec((None, 1, RET_QK_DIM), lambda b, c: (c, 0, 0)),
                  pl.BlockSpec((None, 1, RET_QK_DIM), lambda b, c: (c, 0, 0)),
                  _full_spec(cos_lo.shape), _full_spec(sin_lo.shape), _full_spec(sign.shape),
                  _full_spec(decay.shape), _full_spec(zeta.shape), _full_spec(xi.shape)],
        out_specs=row_spec(rv.shape[1]),
        out_shape=jax.ShapeDtypeStruct((m, rv.shape[1]), BF16),
        scratch_shapes=[pltpu.VMEM((RET_HEADS, RET_QK_DIM, RET_V_DIM), F32)],
        compiler_params=_params(("parallel", "arbitrary")),
        name="retention",
    )(rq, rk, rv, rg, cos_hi, sin_hi, cos_lo, sin_lo, sign, decay, zeta, xi)


def _swa_kernel(sink_ref, q_ref, kp_ref, kc_ref, vp_ref, vc_ref, bias_ref, o_ref):
    i = pl.program_id(1)
    blk = SWA_BLOCK
    rep = SWA_HEADS // SWA_KV_HEADS
    d = SWA_HEAD_DIM
    cols = rep * blk
    k_all = jnp.concatenate([kp_ref[...], kc_ref[...]], axis=0)
    v_all = jnp.concatenate([vp_ref[...], vc_ref[...]], axis=0)
    key_row = lax.broadcasted_iota(jnp.int32, (2 * blk, cols), 0)
    head_of_lane = lax.broadcasted_iota(jnp.int32, (SUBLANES, cols), 1) // blk
    zero = jnp.zeros((d, blk), F32)
    ones = jnp.ones((2 * SUBLANES, 2 * blk), F32)
    for r0 in range(0, q_ref.shape[0], blk):
        kb = k_all[r0:r0 + 2 * blk]
        v_t = v_all[r0:r0 + 2 * blk].astype(F32).T
        outs = []
        for g in range(SWA_KV_HEADS):
            q_t = q_ref[r0:r0 + blk, g * rep * d:(g + 1) * rep * d].astype(F32).T
            rhs = jnp.concatenate(
                [jnp.concatenate([q_t[r * d:(r + 1) * d], zero] if g == 0 else [zero, q_t[r * d:(r + 1) * d]],
                                 axis=0) for r in range(rep)], axis=1).astype(BF16)
            s = jnp.dot(kb, rhs, preferred_element_type=F32)
            s = s + jnp.concatenate([bias_ref[g * rep + r] for r in range(rep)], axis=1)
            if r0 == 0:
                s = jnp.where(jnp.logical_and(key_row < blk, i == 0), MASK_VALUE, s)
            sink = jnp.zeros((SUBLANES, cols), F32)
            for r in range(rep):
                sink = jnp.where(head_of_lane == r, sink_ref[g * rep + r] * LOG2_E, sink)
            chains = [s[0:SUBLANES], s[SUBLANES:2 * SUBLANES]]
            for n, r in enumerate(range(2 * SUBLANES, 2 * blk, SUBLANES)):
                chains[n % 2] = jnp.maximum(chains[n % 2], s[r:r + SUBLANES])
            col_max = jnp.maximum(chains[0], chains[1])
            for shift in (4, 2, 1):
                col_max = jnp.maximum(col_max, pltpu.roll(col_max, shift, 0))
            m = jnp.maximum(col_max, sink)
            p = jnp.exp2(s - jnp.tile(m, (2 * blk // SUBLANES, 1))).astype(BF16)
            v_ext = jnp.concatenate([v_t[g * d:(g + 1) * d], ones], axis=0).astype(BF16)
            pv = jnp.dot(v_ext, p, preferred_element_type=F32)
            inv = 1.0 / (pv[d:d + SUBLANES] + jnp.exp2(sink - m))
            o_t = pv[:d] * jnp.tile(inv, (d // SUBLANES, 1))
            for j in range(rep // 2):
                pair_t = jnp.concatenate([o_t[:, (2 * j) * blk:(2 * j + 1) * blk],
                                          o_t[:, (2 * j + 1) * blk:(2 * j + 2) * blk]], axis=0)
                outs.append(pair_t.T)
        o_ref[r0:r0 + blk, :] = jnp.concatenate(outs, axis=-1).astype(o_ref.dtype)


def sliding_window_attention(sq, sk, sv, sinks, bias, batch, seq, blocks_per_step=8):
    m, qw = sq.shape
    kw = sk.shape[1]
    rows = blocks_per_step * SWA_BLOCK
    nb = seq // rows
    cur = lambda b, i: (b * nb + i, 0)
    prev = lambda b, i: ((b * nb + i) * blocks_per_step - jnp.minimum(i, 1), 0)
    return pl.pallas_call(
        _swa_kernel,
        grid=(batch, nb),
        in_specs=[pl.BlockSpec(memory_space=pltpu.SMEM),
                  pl.BlockSpec((rows, qw), cur),
                  pl.BlockSpec((SWA_BLOCK, kw), prev), pl.BlockSpec((rows, kw), cur),
                  pl.BlockSpec((SWA_BLOCK, kw), prev), pl.BlockSpec((rows, kw), cur),
                  _full_spec(bias.shape)],
        out_specs=pl.BlockSpec((rows, qw), cur),
        out_shape=jax.ShapeDtypeStruct((m, qw), BF16),
        compiler_params=_params(("parallel", "parallel")),
        name="sliding_window_attention",
    )(sinks, sq, sk, sk, sv, sv, bias)


def _split_cols(w, sizes):
    offs = np.cumsum((0,) + tuple(sizes))
    return [w[:, offs[j]:offs[j + 1]] for j in range(len(sizes))]


def even_layer_mixer(x2, mod, g_pre, g_post, w_in, conv_w, conv_b, dt_bias, a_log, d_skip, ssd_norm, lam,
                     diff_norm, w_out, diff_bias, layer_idx, batch, seq):
    d = x2.shape[1]
    d_inner = d
    n_ssd_heads = d_inner // SSD_HEAD_DIM
    conv_ch = d_inner + 2 * SSD_GROUPS * SSD_STATE
    wz, wxbc, wdt, wq, wk, wv = _split_cols(w_in, (d_inner, conv_ch, n_ssd_heads, d, d, d))
    wdt = jnp.pad(wdt, ((0, 0), (0, LANES - n_ssd_heads)))
    wq = wq * (DIFF_HEAD_DIM ** -0.5 * LOG2_E)
    weights = [w.astype(BF16) for w in (wz, wxbc, wdt, wq, wk, wv)]
    z, xbc, dt_raw, q, k, v = norm_proj(x2, mod, g_pre, weights, (BF16, BF16, F32, BF16, BF16, BF16), seq)
    y_ssd = ssd_mixer(z, xbc, dt_raw, conv_w, conv_b, dt_bias, a_log, d_skip, ssd_norm, batch, seq)
    lam_init = 0.8 - 0.6 * math.exp(-0.3 * layer_idx)
    y_diff = diff_attention(q, k, v, diff_bias, lam, diff_norm, batch, seq, lam_init)
    w_out = w_out.astype(BF16)
    return out_proj(x2, mod, g_post, [y_ssd, y_diff], [w_out[:d_inner], w_out[d_inner:]], seq)


def odd_layer_mixer(x2, mod, g_pre, g_post, w_in, sinks, w_out, swa_bias, batch, seq):
    ret_qk = RET_HEADS * RET_QK_DIM
    ret_v = RET_HEADS * RET_V_DIM
    swa_q = SWA_HEADS * SWA_HEAD_DIM
    swa_kv = SWA_KV_HEADS * SWA_HEAD_DIM
    wrq, wrk, wrv, wrg, wsq, wsk, wsv = _split_cols(w_in, (ret_qk, ret_qk, ret_v, ret_v, swa_q, swa_kv, swa_kv))
    wsq = wsq * (SWA_HEAD_DIM ** -0.5 * LOG2_E)
    weights = [w.astype(BF16) for w in (wrq, wrk, wrv, wrg, wsq, wsk, wsv)]
    rq, rk, rv, rg, sq, sk, sv = norm_proj(x2, mod, g_pre, weights, (BF16,) * 7, seq)
    y_ret = retention(rq, rk, rv, rg, batch, seq)
    y_swa = sliding_window_attention(sq, sk, sv, sinks, swa_bias, batch, seq)
    w_out = w_out.astype(BF16)
    return out_proj(x2, mod, g_post, [y_ret, y_swa], [w_out[:ret_v], w_out[ret_v:]], seq)


def kernel(x, c, rel_bias, norm_gains, mod_w, mod_b, mlp_w1, mlp_w2, e_w_in, e_conv_w, e_conv_b, e_dt_bias,
           e_A_log, e_D, e_ssd_norm, e_lambda, e_diff_norm, e_w_out, o_w_in, o_sinks, o_w_out):
    batch, seq, d = x.shape
    depth = norm_gains.shape[0]
    assert seq % (2 * ATTN_BLOCK) == 0 and d % LANES == 0
    mods = modulation(c, mod_w, mod_b.reshape(depth * 2, 3 * d))
    mlp_w1_b, mlp_w2_b = mlp_w1.astype(BF16), mlp_w2.astype(BF16)
    far_bucket = REL_BUCKETS - 1
    diff_bias = bias_tiles(rel_bias, _diff_bucket_idx(ATTN_BLOCK), shift_bucket=far_bucket, scale=LOG2_E)
    diff_bias = diff_bias.reshape(rel_bias.shape[1], 2, ATTN_BLOCK, ATTN_BLOCK)
    swa_bias = bias_tiles(rel_bias, _swa_bucket_idx(), scale=LOG2_E)

    x2 = x.reshape(batch * seq, d)
    for layer in range(depth):
        j = layer // 2
        gains = norm_gains[layer]
        if layer % 2 == 0:
            x2 = even_layer_mixer(x2, mods[2 * layer], gains[0], gains[1], e_w_in[j], e_conv_w[j], e_conv_b[j],
                                  e_dt_bias[j], e_A_log[j], e_D[j], e_ssd_norm[j], e_lambda[j], e_diff_norm[j],
                                  e_w_out[j], diff_bias, layer, batch, seq)
        else:
            x2 = odd_layer_mixer(x2, mods[2 * layer], gains[0], gains[1], o_w_in[j], o_sinks[j], o_w_out[j],
                                 swa_bias, batch, seq)
        x2 = mlp(x2, mods[2 * layer + 1], gains[2], gains[3], mlp_w1_b, mlp_w2_b, layer, seq)
    return x2.reshape(batch, seq, d)
```

```python
import functools
import math

import jax
import jax.numpy as jnp
import numpy as np
from jax import lax
from jax.experimental import pallas as pl
from jax.experimental.pallas import tpu as pltpu

EPS = 1e-6
MASK_VALUE = -1e30
LOG2_E = math.log2(math.e)
LANES = 128
SUBLANES = 8
VMEM_LIMIT = 56 * 1024 * 1024

CHUNK = 128
REL_BUCKETS = 32
REL_MAX_DIST = 128
SSD_HEAD_DIM = 64
SSD_GROUPS = 4
SSD_STATE = 128
SSD_CONV = 4
DIFF_HEAD_DIM = 64
RET_HEADS = 4
RET_QK_DIM = 128
RET_V_DIM = 256
ROPE_BASE = 10000.0
SWA_HEADS = 8
SWA_KV_HEADS = 2
SWA_HEAD_DIM = 64
SWA_BLOCK = 128
ATTN_BLOCK = 512

BF16 = jnp.bfloat16
F32 = jnp.float32
NT_DIMS = (((1,), (1,)), ((), ()))


def _params(semantics, flags=None):
    return pltpu.CompilerParams(dimension_semantics=semantics, vmem_limit_bytes=VMEM_LIMIT, flags=flags)


def _full_spec(shape):
    return pl.BlockSpec(shape, lambda *_: (0,) * len(shape))


def _silu(x):
    h = 0.5 * x
    return h * jnp.tanh(h) + h


def _split3(x):
    hi = x.astype(BF16)
    r1 = x - hi.astype(F32)
    mid = r1.astype(BF16)
    lo = (r1 - mid.astype(F32)).astype(BF16)
    return hi, mid, lo


def _rms(x, eps=EPS):
    return x * lax.rsqrt(jnp.mean(x * x, axis=-1, keepdims=True) + eps)


def _mod_kernel(c_ref, w_ref, b_ref, o_ref):
    c_act = _silu(c_ref[...])
    o_ref[...] = jnp.dot(c_act, w_ref[...], preferred_element_type=F32,
                         precision=lax.Precision.HIGHEST) + b_ref[...]


def modulation(c, mod_w, mod_b):
    b, d = c.shape
    n = mod_b.shape[0]
    per_layer = mod_w.shape[1]
    rows = SUBLANES
    c_pad = jnp.zeros((rows, d), F32).at[:b].set(c)
    tn = d
    out = pl.pallas_call(
        _mod_kernel,
        grid=(n, 3 * d // tn),
        in_specs=[pl.BlockSpec((rows, d), lambda s, j: (0, 0)),
                  pl.BlockSpec((None, None, d, tn), lambda s, j: (s // per_layer, s % per_layer, 0, j)),
                  pl.BlockSpec((None, 1, tn), lambda s, j: (s, 0, j))],
        out_specs=pl.BlockSpec((None, rows, tn), lambda s, j: (s, 0, j)),
        out_shape=jax.ShapeDtypeStruct((n, rows, 3 * d), F32),
        compiler_params=_params(("parallel", "parallel")),
        name="modulation",
    )(c_pad, mod_w, mod_b.reshape(n, 1, 3 * d))
    return out[:, :b].reshape(n, b, 3, d)


def _t5_bucket_np(dist):
    max_exact = REL_BUCKETS // 2
    logd = np.log(np.maximum(dist, 1).astype(np.float32) / np.float32(max_exact))
    large = max_exact + (logd / np.float32(math.log(REL_MAX_DIST / max_exact))
                         * np.float32(REL_BUCKETS - max_exact)).astype(np.int32)
    large = np.minimum(large, REL_BUCKETS - 1)
    return np.where(dist < max_exact, dist, large).astype(np.int32)


def _bias_kernel(rb_ref, idx_ref, o_ref, *, shift_bucket, scale, block_buckets):
    h = pl.program_id(0)
    shift = rb_ref[shift_bucket, h] if shift_bucket is not None else 0.0

    def value(bucket):
        return (rb_ref[bucket, h] - shift) * scale

    for (r0, c0), buckets in block_buckets:
        window = (slice(r0, r0 + LANES), slice(c0, c0 + LANES))
        idx = idx_ref[window]
        acc = jnp.full(idx.shape, MASK_VALUE if -1 in buckets else value(max(buckets)), F32)
        for bucket in buckets:
            if bucket >= 0 and len(buckets) > 1:
                acc = jnp.where(idx == bucket, value(bucket), acc)
        o_ref[window] = acc


def bias_tiles(rel_bias, bucket_idx, shift_bucket=None, scale=1.0):
    n_heads = rel_bias.shape[1]
    r, c = bucket_idx.shape
    block_buckets = tuple(((r0, c0), tuple(int(b) for b in np.unique(bucket_idx[r0:r0 + LANES, c0:c0 + LANES])))
                          for r0 in range(0, r, LANES) for c0 in range(0, c, LANES))
    return pl.pallas_call(
        functools.partial(_bias_kernel, shift_bucket=shift_bucket, scale=scale, block_buckets=block_buckets),
        grid=(n_heads,),
        in_specs=[pl.BlockSpec(memory_space=pltpu.SMEM),
                  pl.BlockSpec((r, c), lambda h: (0, 0))],
        out_specs=pl.BlockSpec((None, r, c), lambda h: (h, 0, 0)),
        out_shape=jax.ShapeDtypeStruct((n_heads, r, c), F32),
        compiler_params=_params(("parallel",)),
        name="bias_tiles",
    )(rel_bias, jnp.asarray(bucket_idx))


def _diff_bucket_idx(t):
    q = np.arange(t)[None, :]
    k = np.arange(t)[:, None]
    diag = q - k
    idx_diag = np.where(diag >= 0, _t5_bucket_np(np.maximum(diag, 0)), -1)
    idx_prev = _t5_bucket_np(diag + t)
    assert (_t5_bucket_np(diag + 2 * t) == REL_BUCKETS - 1).all()
    return np.concatenate([idx_diag, idx_prev], axis=0).astype(np.int32)


def _swa_bucket_idx():
    q = np.arange(SWA_BLOCK)[None, :]
    k = np.arange(2 * SWA_BLOCK)[:, None] - SWA_BLOCK
    dist = q - k
    valid = (dist >= 0) & (dist < SWA_BLOCK)
    return np.where(valid, _t5_bucket_np(np.maximum(dist, 0)), -1).astype(np.int32)


def _norm_proj_kernel(x_ref, mod_ref, g_ref, *refs):
    n = len(refs) // 2
    w_refs, o_refs = refs[:n], refs[n:]
    mod = mod_ref[...]
    h = _rms(x_ref[...]) * g_ref[...] * (1.0 + mod[1:2]) + mod[0:1]
    hb = h.astype(BF16)
    for w_ref, o_ref in zip(w_refs, o_refs):
        o_ref[...] = jnp.dot(hb, w_ref[...], preferred_element_type=F32).astype(o_ref.dtype)


def norm_proj(x2, mod, gain, weights, out_dtypes, seq, tm=512):
    m, d = x2.shape
    per_batch = seq // tm
    in_specs = [pl.BlockSpec((tm, d), lambda i: (i, 0)),
                pl.BlockSpec((None, 3, d), lambda i: (i // per_batch, 0, 0)),
                _full_spec((1, d))]
    in_specs += [_full_spec(w.shape) for w in weights]
    out_specs = [pl.BlockSpec((tm, w.shape[1]), lambda i: (i, 0)) for w in weights]
    out_shape = [jax.ShapeDtypeStruct((m, w.shape[1]), dt) for w, dt in zip(weights, out_dtypes)]
    return pl.pallas_call(
        _norm_proj_kernel,
        grid=(m // tm,),
        in_specs=in_specs,
        out_specs=out_specs,
        out_shape=out_shape,
        compiler_params=_params(("parallel",)),
        name="norm_proj",
    )(x2, mod, gain.reshape(1, d), *weights)


def _out_proj_kernel(x_ref, mod_ref, g_ref, *refs):
    n = (len(refs) - 1) // 2
    y_refs, w_refs, o_ref = refs[:n], refs[n:2 * n], refs[2 * n]
    acc = None
    for y_ref, w_ref in zip(y_refs, w_refs):
        part = jnp.dot(y_ref[...], w_ref[...], preferred_element_type=F32)
        acc = part if acc is None else acc + part
    gate = mod_ref[...][2:3]
    o_ref[...] = x_ref[...] + gate * (_rms(acc) * g_ref[...])


def out_proj(x2, mod, gain, ys, weights, seq, tm=512):
    m, d = x2.shape
    per_batch = seq // tm
    in_specs = [pl.BlockSpec((tm, d), lambda i: (i, 0)),
                pl.BlockSpec((None, 3, d), lambda i: (i // per_batch, 0, 0)),
                _full_spec((1, d))]
    in_specs += [pl.BlockSpec((tm, y.shape[1]), lambda i: (i, 0)) for y in ys]
    in_specs += [_full_spec(w.shape) for w in weights]
    return pl.pallas_call(
        _out_proj_kernel,
        grid=(m // tm,),
        in_specs=in_specs,
        out_specs=pl.BlockSpec((tm, d), lambda i: (i, 0)),
        out_shape=jax.ShapeDtypeStruct((m, d), F32),
        compiler_params=_params(("parallel",)),
        name="out_proj",
    )(x2, mod, gain.reshape(1, d), *ys, *weights)


def _mlp_kernel(x_ref, mod_ref, g_pre_ref, g_post_ref, w1_ref, w2_ref, o_ref, *, ff_chunk):
    x = x_ref[...]
    mod = mod_ref[...]
    h = _rms(x) * g_pre_ref[...] * (1.0 + mod[1:2]) + mod[0:1]
    hb = h.astype(BF16)
    d_ff = w1_ref.shape[1]
    acc = None
    for c0 in range(0, d_ff, ff_chunk):
        a = jnp.dot(hb, w1_ref[:, c0:c0 + ff_chunk], preferred_element_type=F32)
        a = jnp.square(jnp.maximum(a, 0.0)).astype(BF16)
        part = jnp.dot(a, w2_ref[c0:c0 + ff_chunk, :], preferred_element_type=F32)
        acc = part if acc is None else acc + part
    o_ref[...] = x + mod[2:3] * (_rms(acc) * g_post_ref[...])


def mlp(x2, mod, g_pre, g_post, w1, w2, layer, seq, tm=512, ff_chunk=1024):
    m, d = x2.shape
    per_batch = seq // tm
    return pl.pallas_call(
        functools.partial(_mlp_kernel, ff_chunk=ff_chunk),
        grid=(m // tm,),
        in_specs=[pl.BlockSpec((tm, d), lambda i: (i, 0)),
                  pl.BlockSpec((None, 3, d), lambda i: (i // per_batch, 0, 0)),
                  _full_spec((1, d)), _full_spec((1, d)),
                  pl.BlockSpec((None,) + w1.shape[1:], lambda i: (layer, 0, 0)),
                  pl.BlockSpec((None,) + w2.shape[1:], lambda i: (layer, 0, 0))],
        out_specs=pl.BlockSpec((tm, d), lambda i: (i, 0)),
        out_shape=jax.ShapeDtypeStruct((m, d), F32),
        compiler_params=_params(("parallel",)),
        name="mlp",
    )(x2, mod, g_pre.reshape(1, d), g_post.reshape(1, d), w1, w2)


def _ssd_kernel(z_ref, xbc_ref, dt_ref, cw_ref, cb_ref, dtb_ref, alog_ref, dskip_ref, nw_ref, expand_ref,
                shift_ref, o_ref, conv_buf, state_ref, *, d_inner):
    chunk = CHUNK
    rows_per_step = z_ref.shape[0]
    n_state = SSD_STATE
    gn = SSD_GROUPS * n_state
    pair = 2 * SSD_HEAD_DIM
    heads_per_group = d_inner // SSD_HEAD_DIM // SSD_GROUPS
    halo = conv_buf.shape[0] - rows_per_step

    @pl.when(pl.program_id(1) == 0)
    def _():
        state_ref[...] = jnp.zeros_like(state_ref)
        conv_buf[0:halo, :] = jnp.zeros((halo, conv_buf.shape[1]), BF16)

    conv_buf[halo:, :] = xbc_ref[...]

    def chunk_body(r0):
        rows = slice(r0, r0 + chunk)
        cur = xbc_ref[rows, :]
        shifted = jnp.dot(shift_ref[...], conv_buf[r0:r0 + halo + chunk, :], preferred_element_type=F32)
        acc = cb_ref[...] + cw_ref[SSD_CONV - 1:SSD_CONV, :] * cur.astype(F32)
        for tap in range(SSD_CONV - 1):
            acc = acc + cw_ref[tap:tap + 1, :] * shifted[tap * chunk:(tap + 1) * chunk, :]
        xbc = _silu(acc)
        xs = xbc[:, :d_inner]
        b_all = xbc[:, d_inner:d_inner + gn]
        c_all = xbc[:, d_inner + gn:]

        dt_in = dt_ref[rows, :] + dtb_ref[...]
        dt = jnp.maximum(dt_in, 0.0) + jnp.log1p(jnp.exp(-jnp.abs(dt_in)))
        a = dt * (-jnp.exp(alog_ref[...]))
        row = lax.broadcasted_iota(jnp.int32, (chunk, chunk), 0)
        col = lax.broadcasted_iota(jnp.int32, (chunk, chunk), 1)
        causal = row >= col
        tril = jnp.where(causal, 1.0, 0.0).astype(BF16)
        parts = jnp.dot(tril, jnp.concatenate(_split3(a), axis=1), preferred_element_type=F32)
        a_cs = parts[:, :LANES] + parts[:, LANES:2 * LANES] + parts[:, 2 * LANES:]
        a_cs_t = a_cs.T
        lhs = jnp.concatenate([jnp.concatenate(_split3(dt), axis=1), jnp.concatenate(_split3(a_cs), axis=1)], axis=0)
        expanded = jnp.dot(lhs, expand_ref[...], preferred_element_type=F32)
        dt_e, acs_e = expanded[:chunk], expanded[chunk:]
        a_last_e = acs_e[chunk - 1:chunk, :]
        x_dt = xs * dt_e
        w_state = (x_dt * jnp.exp(a_last_e - acs_e)).astype(BF16)
        out_scale = jnp.exp(acs_e)
        chunk_decay = jnp.exp(a_last_e)
        x_dt_b = x_dt.astype(BF16)
        lane = lax.broadcasted_iota(jnp.int32, (chunk, pair), 1)
        first_head = lane < SSD_HEAD_DIM

        y_blocks = []
        for g in range(SSD_GROUPS):
            bg = b_all[:, g * n_state:(g + 1) * n_state]
            cg = c_all[:, g * n_state:(g + 1) * n_state].astype(BF16)
            cb = lax.dot_general(cg, bg.astype(BF16), NT_DIMS, preferred_element_type=F32)
            bg_t = bg.T.astype(BF16)
            for pi in range(heads_per_group // 2):
                p = g * (heads_per_group // 2) + pi
                sl = slice(p * pair, (p + 1) * pair)
                xp = x_dt_b[:, sl]
                y_diag = None
                for k in range(2):
                    hh = 2 * p + k
                    seg = a_cs[:, hh:hh + 1] - a_cs_t[hh:hh + 1, :]
                    decay = jnp.exp(jnp.where(causal, seg, MASK_VALUE))
                    m = (cb * decay).astype(BF16)
                    xh = jnp.where(first_head if k == 0 else jnp.logical_not(first_head), xp, jnp.zeros_like(xp))
                    part = jnp.dot(m, xh, preferred_element_type=F32)
                    y_diag = part if y_diag is None else y_diag + part
                st = state_ref[p]
                y_off = jnp.dot(cg, st.astype(BF16), preferred_element_type=F32) * out_scale[:, sl]
                state_ref[p] = st * chunk_decay[:, sl] + jnp.dot(bg_t, w_state[:, sl], preferred_element_type=F32)
                y_blocks.append(y_diag + y_off)
        y = jnp.concatenate(y_blocks, axis=-1) + xs * dskip_ref[...]
        y = y * _silu(z_ref[rows, :].astype(F32))
        o_ref[rows, :] = (_rms(y) * nw_ref[...]).astype(o_ref.dtype)

    for r0 in range(0, rows_per_step, chunk):
        chunk_body(r0)
    conv_buf[0:halo, :] = conv_buf[rows_per_step:rows_per_step + halo, :]


def ssd_mixer(z, xbc, dt_raw, conv_w, conv_b, dt_bias, a_log, d_skip, norm_w, batch, seq, chunks_per_step=4):
    m, d_inner = z.shape
    conv_ch = xbc.shape[1]
    n_heads = d_inner // SSD_HEAD_DIM
    rows = chunks_per_step * CHUNK
    nc = seq // rows
    pad = LANES - n_heads
    expand = np.zeros((LANES, d_inner), np.float32)
    for h in range(n_heads):
        expand[h, h * SSD_HEAD_DIM:(h + 1) * SSD_HEAD_DIM] = 1.0
    expand3 = jnp.asarray(np.tile(expand, (3, 1)), BF16)
    halo = 2 * SUBLANES
    shift = np.zeros(((SSD_CONV - 1) * CHUNK, halo + CHUNK), np.float32)
    for tap in range(SSD_CONV - 1):
        shift[tap * CHUNK + np.arange(CHUNK), halo - (SSD_CONV - 1) + tap + np.arange(CHUNK)] = 1.0
    shift = jnp.asarray(shift, BF16)
    row_spec = lambda width: pl.BlockSpec((rows, width), lambda b, c: (b * nc + c, 0))
    return pl.pallas_call(
        functools.partial(_ssd_kernel, d_inner=d_inner),
        grid=(batch, nc),
        in_specs=[row_spec(d_inner), row_spec(conv_ch), row_spec(LANES),
                  _full_spec((SSD_CONV, conv_ch)), _full_spec((1, conv_ch)),
                  _full_spec((1, LANES)), _full_spec((1, LANES)),
                  _full_spec((1, d_inner)), _full_spec((1, d_inner)), _full_spec(expand3.shape),
                  _full_spec(shift.shape)],
        out_specs=row_spec(d_inner),
        out_shape=jax.ShapeDtypeStruct((m, d_inner), BF16),
        scratch_shapes=[pltpu.VMEM((halo + rows, conv_ch), BF16),
                        pltpu.VMEM((n_heads // 2, SSD_STATE, 2 * SSD_HEAD_DIM), F32)],
        compiler_params=_params(("parallel", "arbitrary")),
        name="ssd_mixer",
    )(z, xbc, dt_raw, conv_w, conv_b.reshape(1, conv_ch),
      jnp.pad(dt_bias, (0, pad)).reshape(1, LANES), jnp.pad(a_log, (0, pad)).reshape(1, LANES),
      jnp.repeat(d_skip, SSD_HEAD_DIM).reshape(1, d_inner), norm_w.reshape(1, d_inner), expand3, shift)


def _diff_attn_kernel(q_ref, k_ref, v_ref, bias_ref, lam_ref, nw_ref, o_ref,
                      vt_ref, s0_ref, s1_ref, cmax0_ref, cmax1_ref, p_ref, m_ref, alpha_ref, acc_ref,
                      *, lam_init):
    s_refs, cmax_refs = (s0_ref, s1_ref), (cmax0_ref, cmax1_ref)
    t = s0_ref.shape[0]
    tq = q_ref.shape[0]
    head_w = v_ref.shape[1]
    ext_rows = vt_ref.shape[1]
    i = pl.program_id(2)
    q_t = q_ref[...].astype(F32).T.astype(BF16)
    dim = lax.broadcasted_iota(jnp.int32, q_t.shape, 0)
    zero = jnp.zeros_like(q_t)
    qq_t = jnp.concatenate([jnp.where(dim < DIFF_HEAD_DIM, q_t, zero),
                            jnp.where(dim >= DIFF_HEAD_DIM, q_t, zero)], axis=1)

    @pl.when(i == 0)
    def _():
        for j in range(vt_ref.shape[0]):
            vt_ref[j, :head_w, :] = v_ref[j * t:(j + 1) * t, :].astype(F32).T.astype(BF16)
            vt_ref[j, head_w:, :] = jnp.ones((ext_rows - head_w, t), BF16)

    m_ref[...] = jnp.full(m_ref.shape, MASK_VALUE, F32)
    acc_ref[...] = jnp.zeros(acc_ref.shape, F32)

    diag_tile, prev_tile = bias_ref[0], bias_ref[1]
    all_cols = ((0, t), (t, t), (tq, t), (tq + t, t))
    near_cols = (all_cols[1::2], all_cols, all_cols)
    near_tiles = ((diag_tile, diag_tile),
                  (diag_tile, prev_tile, diag_tile, prev_tile),
                  (prev_tile, None, prev_tile, None))
    near_blocks = (2 * i + 1, 2 * i, jnp.maximum(2 * i - 1, 0))

    def raw_logits(block, cols=all_cols):
        kb = k_ref[pl.ds(pl.multiple_of(block * t, t), t), :]
        rhs = qq_t if cols is all_cols else jnp.concatenate([qq_t[:, c0:c0 + w] for c0, w in cols], axis=1)
        return jnp.dot(kb, rhs, preferred_element_type=F32)

    def store_logits(pieces, slot, cols=all_cols):
        for piece, (c0, w) in zip(pieces, cols):
            s_refs[slot][:, c0:c0 + w] = piece
            chains = [piece[0:SUBLANES, :], piece[SUBLANES:2 * SUBLANES, :]]
            for n, r in enumerate(range(2 * SUBLANES, t, SUBLANES)):
                chains[n % 2] = jnp.maximum(chains[n % 2], piece[r:r + SUBLANES, :])
            col_max = jnp.maximum(chains[0], chains[1])
            for shift in (4, 2, 1):
                col_max = jnp.maximum(col_max, pltpu.roll(col_max, shift, 0))
            cmax_refs[slot][:, c0:c0 + w] = col_max

    def is_near(visit):
        return isinstance(visit, int) and visit < len(near_blocks)

    def block_of(visit):
        return near_blocks[visit] if is_near(visit) else visit - len(near_blocks)

    def cols_of(visit):
        return near_cols[visit] if is_near(visit) else all_cols

    def logits_piece(visit, slot, n):
        col = cols_of(visit)[n]
        piece = raw_logits(block_of(visit), (col,))
        tile = near_tiles[visit][n] if is_near(visit) else None
        store_logits([piece if tile is None else piece + tile], slot, (col,))

    def softmax_accumulate(visit, slot, col_ranges=((0, 2 * tq),)):
        s_ref = s_refs[slot]
        vt = vt_ref[block_of(visit)]
        packed = 2 * SUBLANES
        for c0, w in col_ranges:
            c = slice(c0, c0 + w)
            m_old = m_ref[:, c]
            m_new = jnp.maximum(m_old, cmax_refs[slot][:, c])
            m_ref[:, c] = m_new
            alpha_ref[:, c] = jnp.exp2(m_old - m_new)
            m_tile = jnp.concatenate([m_new, m_new], axis=0)
            for r in range(0, t, packed):
                p_ref[r:r + packed, c] = jnp.exp2((s_ref[r:r + packed, c] - m_tile).astype(BF16))
            pv = jnp.dot(vt, p_ref[:, c], preferred_element_type=F32)
            alpha = alpha_ref[:, c]
            for r in range(0, ext_rows, SUBLANES):
                acc_ref[r:r + SUBLANES, c] = alpha * acc_ref[r:r + SUBLANES, c] + pv[r:r + SUBLANES, :]

    def pipeline_step(visit, slot):
        stage1, stage2 = cols_of(visit), cols_of(visit - 1)
        for n in range(max(len(stage1), len(stage2))):
            if n < len(stage1):
                logits_piece(visit, slot, n)
            if n < len(stage2):
                softmax_accumulate(visit - 1, 1 - slot, (stage2[n],))

    for n in range(len(near_cols[0])):
        logits_piece(0, 0, n)
    pipeline_step(1, 1)

    @pl.when(i < pl.num_programs(2))
    def _():
        pipeline_step(2, 0)

    @pl.when(i > 0)
    def _():
        pipeline_step(3, 1)
        last = 2 * i + 1

        def pair(k, _):
            pipeline_step(4 + 2 * k, 0)

            @pl.when(5 + 2 * k <= last)
            def _():
                pipeline_step(5 + 2 * k, 1)

            return 0

        lax.fori_loop(0, i - 1, pair, 0)
        softmax_accumulate(last, 1, all_cols)

    lam = lam_ref[...]
    lam_full = (jnp.exp(jnp.sum(lam[0:1] * lam[1:2], axis=-1, keepdims=True))
                - jnp.exp(jnp.sum(lam[2:3] * lam[3:4], axis=-1, keepdims=True)) + lam_init)
    inv_sum = 1.0 / acc_ref[head_w:head_w + SUBLANES, :]
    out_t = acc_ref[:head_w, :] * jnp.tile(inv_sum, (head_w // SUBLANES, 1))
    out = (out_t[:, :tq] - lam_full * out_t[:, tq:]).T
    o_ref[...] = (_rms(out) * nw_ref[...] * (1.0 - lam_init)).astype(o_ref.dtype)


def diff_attention(q, k, v, bias, lam, subln_w, batch, seq, lam_init):
    m, width = q.shape
    head_w = 2 * DIFF_HEAD_DIM
    n_heads = width // head_w
    t = ATTN_BLOCK
    tq = 2 * t
    nq = seq // tq
    cols = 2 * tq
    ext_rows = head_w + 2 * SUBLANES
    return pl.pallas_call(
        functools.partial(_diff_attn_kernel, lam_init=lam_init),
        grid=(batch, n_heads, nq),
        in_specs=[pl.BlockSpec((tq, head_w), lambda b, h, i: (b * nq + i, h)),
                  pl.BlockSpec((seq, head_w), lambda b, h, i: (b, h)),
                  pl.BlockSpec((seq, head_w), lambda b, h, i: (b, h)),
                  pl.BlockSpec((None, 2, t, t), lambda b, h, i: (h, 0, 0, 0)),
                  _full_spec(lam.shape), _full_spec((1, head_w))],
        out_specs=pl.BlockSpec((tq, head_w), lambda b, h, i: (b * nq + i, h)),
        out_shape=jax.ShapeDtypeStruct((m, width), BF16),
        scratch_shapes=[pltpu.VMEM((seq // t, ext_rows, t), BF16),
                        pltpu.VMEM((t, cols), F32), pltpu.VMEM((t, cols), F32),
                        pltpu.VMEM((SUBLANES, cols), F32), pltpu.VMEM((SUBLANES, cols), F32),
                        pltpu.VMEM((t, cols), BF16),
                        pltpu.VMEM((SUBLANES, cols), F32),
                        pltpu.VMEM((SUBLANES, cols), F32),
                        pltpu.VMEM((ext_rows, cols), F32)],
        compiler_params=_params(("parallel", "parallel", "arbitrary")),
        name="diff_attention",
    )(q, k, v, bias, lam, subln_w.reshape(1, head_w))


def _retention_kernel(q_ref, k_ref, v_ref, g_ref, cos_hi_ref, sin_hi_ref, cos_lo_ref, sin_lo_ref, sign_ref,
                      decay_ref, zeta_ref, xi_ref, o_ref, state_ref, *, chunk_decay):
    @pl.when(pl.program_id(1) == 0)
    def _():
        state_ref[...] = jnp.zeros_like(state_ref)

    c_hi, s_hi = cos_hi_ref[...], sin_hi_ref[...]
    c_lo, s_lo = cos_lo_ref[...], sin_lo_ref[...]
    cos_all = c_hi * c_lo - s_hi * s_lo
    sin_all = (s_hi * c_lo + c_hi * s_lo) * sign_ref[...]

    half = RET_QK_DIM // 2
    for h in range(RET_HEADS):
        qs = slice(h * RET_QK_DIM, (h + 1) * RET_QK_DIM)
        vs = slice(h * RET_V_DIM, (h + 1) * RET_V_DIM)
        st = state_ref[h]
        for r0 in range(0, q_ref.shape[0], CHUNK):
            rows = slice(r0, r0 + CHUNK)
            cos, sin = cos_all[rows, :], sin_all[rows, :]

            def rope(t):
                return t * cos + pltpu.roll(t, half, 1) * sin

            qr = rope(q_ref[rows, qs].astype(F32))
            kr = rope(k_ref[rows, qs].astype(F32)) * (RET_QK_DIM ** -0.5)
            vh = v_ref[rows, vs]
            scores = lax.dot_general(qr.astype(BF16), kr.astype(BF16), NT_DIMS,
                                     preferred_element_type=F32) * decay_ref[h]
            inner = jnp.dot(scores.astype(BF16), vh, preferred_element_type=F32)
            cross = jnp.dot((qr * xi_ref[h]).astype(BF16), st.astype(BF16), preferred_element_type=F32)
            kv = jnp.dot((kr.T * zeta_ref[h]).astype(BF16), vh, preferred_element_type=F32)
            st = st * chunk_decay[h] + kv
            out = _rms(inner + cross) * _silu(g_ref[rows, vs].astype(F32))
            o_ref[rows, vs] = out.astype(o_ref.dtype)
        state_ref[h] = st


def retention(rq, rk, rv, rg, batch, seq, chunks_per_step=8):
    m = rq.shape[0]
    rows = chunks_per_step * CHUNK
    nc = seq // rows
    half = RET_QK_DIM // 2
    inv = ROPE_BASE ** (-jnp.arange(half, dtype=F32) / half)
    both_halves = lambda a: jnp.concatenate([a, a], axis=-1)
    ang_hi = (jnp.arange(nc, dtype=F32) * rows)[:, None] * inv[None]
    ang_lo = jnp.arange(rows, dtype=F32)[:, None] * inv[None]
    cos_hi = both_halves(jnp.cos(ang_hi)).reshape(nc, 1, RET_QK_DIM)
    sin_hi = both_halves(jnp.sin(ang_hi)).reshape(nc, 1, RET_QK_DIM)
    cos_lo, sin_lo = both_halves(jnp.cos(ang_lo)), both_halves(jnp.sin(ang_lo))
    sign = jnp.concatenate([-jnp.ones((1, half), F32), jnp.ones((1, half), F32)], axis=-1)
    log_gamma = jnp.log(1.0 - 2.0 ** (-5.0 - jnp.arange(RET_HEADS, dtype=F32)))
    idx = jnp.arange(CHUNK, dtype=F32)
    rel = idx[:, None] - idx[None, :]
    decay = jnp.where(rel >= 0, jnp.exp(jnp.maximum(rel, 0.0)[None] * log_gamma[:, None, None]), 0.0)
    zeta = jnp.exp((CHUNK - 1 - idx)[None] * log_gamma[:, None])[:, None, :]
    xi = jnp.broadcast_to(jnp.exp((idx + 1.0)[None] * log_gamma[:, None])[:, :, None],
                          (RET_HEADS, CHUNK, RET_QK_DIM))
    gamma = 1.0 - 2.0 ** (-5.0 - np.arange(RET_HEADS, dtype=np.float64))
    chunk_decay = tuple(float(g ** CHUNK) for g in gamma)
    row_spec = lambda width: pl.BlockSpec((rows, width), lambda b, c: (b * nc + c, 0))
    return pl.pallas_call(
        functools.partial(_retention_kernel, chunk_decay=chunk_decay),
        grid=(batch, nc),
        in_specs=[row_spec(rq.shape[1]), row_spec(rk.shape[1]), row_spec(rv.shape[1]), row_spec(rg.shape[1]),
                  pl.BlockSpec((None, 1, RET_QK_DIM), lambda b, c: (c, 0, 0)),
                  pl.BlockSpec((None, 1, RET_QK_DIM), lambda b, c: (c, 0, 0)),
                  _full_spec(cos_lo.shape), _full_spec(sin_lo.shape), _full_spec(sign.shape),
                  _full_spec(decay.shape), _full_spec(zeta.shape), _full_spec(xi.shape)],
        out_specs=row_spec(rv.shape[1]),
        out_shape=jax.ShapeDtypeStruct((m, rv.shape[1]), BF16),
        scratch_shapes=[pltpu.VMEM((RET_HEADS, RET_QK_DIM, RET_V_DIM), F32)],
        compiler_params=_params(("parallel", "arbitrary")),
        name="retention",
    )(rq, rk, rv, rg, cos_hi, sin_hi, cos_lo, sin_lo, sign, decay, zeta, xi)


def _swa_kernel(sink_ref, q_ref, kp_ref, kc_ref, vp_ref, vc_ref, bias_ref, o_ref):
    i = pl.program_id(1)
    blk = SWA_BLOCK
    rep = SWA_HEADS // SWA_KV_HEADS
    d = SWA_HEAD_DIM
    cols = rep * blk
    k_all = jnp.concatenate([kp_ref[...], kc_ref[...]], axis=0)
    v_all = jnp.concatenate([vp_ref[...], vc_ref[...]], axis=0)
    key_row = lax.broadcasted_iota(jnp.int32, (2 * blk, cols), 0)
    head_of_lane = lax.broadcasted_iota(jnp.int32, (SUBLANES, cols), 1) // blk
    zero = jnp.zeros((d, blk), F32)
    ones = jnp.ones((2 * SUBLANES, 2 * blk), F32)
    for r0 in range(0, q_ref.shape[0], blk):
        kb = k_all[r0:r0 + 2 * blk]
        v_t = v_all[r0:r0 + 2 * blk].astype(F32).T
        outs = []
        for g in range(SWA_KV_HEADS):
            q_t = q_ref[r0:r0 + blk, g * rep * d:(g + 1) * rep * d].astype(F32).T
            rhs = jnp.concatenate(
                [jnp.concatenate([q_t[r * d:(r + 1) * d], zero] if g == 0 else [zero, q_t[r * d:(r + 1) * d]],
                                 axis=0) for r in range(rep)], axis=1).astype(BF16)
            s = jnp.dot(kb, rhs, preferred_element_type=F32)
            s = s + jnp.concatenate([bias_ref[g * rep + r] for r in range(rep)], axis=1)
            if r0 == 0:
                s = jnp.where(jnp.logical_and(key_row < blk, i == 0), MASK_VALUE, s)
            sink = jnp.zeros((SUBLANES, cols), F32)
            for r in range(rep):
                sink = jnp.where(head_of_lane == r, sink_ref[g * rep + r] * LOG2_E, sink)
            chains = [s[0:SUBLANES], s[SUBLANES:2 * SUBLANES]]
            for n, r in enumerate(range(2 * SUBLANES, 2 * blk, SUBLANES)):
                chains[n % 2] = jnp.maximum(chains[n % 2], s[r:r + SUBLANES])
            col_max = jnp.maximum(chains[0], chains[1])
            for shift in (4, 2, 1):
                col_max = jnp.maximum(col_max, pltpu.roll(col_max, shift, 0))
            m = jnp.maximum(col_max, sink)
            p = jnp.exp2(s - jnp.tile(m, (2 * blk // SUBLANES, 1))).astype(BF16)
            v_ext = jnp.concatenate([v_t[g * d:(g + 1) * d], ones], axis=0).astype(BF16)
            pv = jnp.dot(v_ext, p, preferred_element_type=F32)
            inv = 1.0 / (pv[d:d + SUBLANES] + jnp.exp2(sink - m))
            o_t = pv[:d] * jnp.tile(inv, (d // SUBLANES, 1))
            for j in range(rep // 2):
                pair_t = jnp.concatenate([o_t[:, (2 * j) * blk:(2 * j + 1) * blk],
                                          o_t[:, (2 * j + 1) * blk:(2 * j + 2) * blk]], axis=0)
                outs.append(pair_t.T)
        o_ref[r0:r0 + blk, :] = jnp.concatenate(outs, axis=-1).astype(o_ref.dtype)


def sliding_window_attention(sq, sk, sv, sinks, bias, batch, seq, blocks_per_step=8):
    m, qw = sq.shape
    kw = sk.shape[1]
    rows = blocks_per_step * SWA_BLOCK
    nb = seq // rows
    cur = lambda b, i: (b * nb + i, 0)
    prev = lambda b, i: ((b * nb + i) * blocks_per_step - jnp.minimum(i, 1), 0)
    return pl.pallas_call(
        _swa_kernel,
        grid=(batch, nb),
        in_specs=[pl.BlockSpec(memory_space=pltpu.SMEM),
                  pl.BlockSpec((rows, qw), cur),
                  pl.BlockSpec((SWA_BLOCK, kw), prev), pl.BlockSpec((rows, kw), cur),
                  pl.BlockSpec((SWA_BLOCK, kw), prev), pl.BlockSpec((rows, kw), cur),
                  _full_spec(bias.shape)],
        out_specs=pl.BlockSpec((rows, qw), cur),
        out_shape=jax.ShapeDtypeStruct((m, qw), BF16),
        compiler_params=_params(("parallel", "parallel")),
        name="sliding_window_attention",
    )(sinks, sq, sk, sk, sv, sv, bias)


def _split_cols(w, sizes):
    offs = np.cumsum((0,) + tuple(sizes))
    return [w[:, offs[j]:offs[j + 1]] for j in range(len(sizes))]


def even_layer_mixer(x2, mod, g_pre, g_post, w_in, conv_w, conv_b, dt_bias, a_log, d_skip, ssd_norm, lam,
                     diff_norm, w_out, diff_bias, layer_idx, batch, seq):
    d = x2.shape[1]
    d_inner = d
    n_ssd_heads = d_inner // SSD_HEAD_DIM
    conv_ch = d_inner + 2 * SSD_GROUPS * SSD_STATE
    wz, wxbc, wdt, wq, wk, wv = _split_cols(w_in, (d_inner, conv_ch, n_ssd_heads, d, d, d))
    wdt = jnp.pad(wdt, ((0, 0), (0, LANES - n_ssd_heads)))
    wq = wq * (DIFF_HEAD_DIM ** -0.5 * LOG2_E)
    weights = [w.astype(BF16) for w in (wz, wxbc, wdt, wq, wk, wv)]
    z, xbc, dt_raw, q, k, v = norm_proj(x2, mod, g_pre, weights, (BF16, BF16, F32, BF16, BF16, BF16), seq)
    y_ssd = ssd_mixer(z, xbc, dt_raw, conv_w, conv_b, dt_bias, a_log, d_skip, ssd_norm, batch, seq)
    lam_init = 0.8 - 0.6 * math.exp(-0.3 * layer_idx)
    y_diff = diff_attention(q, k, v, diff_bias, lam, diff_norm, batch, seq, lam_init)
    w_out = w_out.astype(BF16)
    return out_proj(x2, mod, g_post, [y_ssd, y_diff], [w_out[:d_inner], w_out[d_inner:]], seq)


def odd_layer_mixer(x2, mod, g_pre, g_post, w_in, sinks, w_out, swa_bias, batch, seq):
    ret_qk = RET_HEADS * RET_QK_DIM
    ret_v = RET_HEADS * RET_V_DIM
    swa_q = SWA_HEADS * SWA_HEAD_DIM
    swa_kv = SWA_KV_HEADS * SWA_HEAD_DIM
    wrq, wrk, wrv, wrg, wsq, wsk, wsv = _split_cols(w_in, (ret_qk, ret_qk, ret_v, ret_v, swa_q, swa_kv, swa_kv))
    wsq = wsq * (SWA_HEAD_DIM ** -0.5 * LOG2_E)
    weights = [w.astype(BF16) for w in (wrq, wrk, wrv, wrg, wsq, wsk, wsv)]
    rq, rk, rv, rg, sq, sk, sv = norm_proj(x2, mod, g_pre, weights, (BF16,) * 7, seq)
    y_ret = retention(rq, rk, rv, rg, batch, seq)
    y_swa = sliding_window_attention(sq, sk, sv, sinks, swa_bias, batch, seq)
    w_out = w_out.astype(BF16)
    return out_proj(x2, mod, g_post, [y_ret, y_swa], [w_out[:ret_v], w_out[ret_v:]], seq)


def kernel(x, c, rel_bias, norm_gains, mod_w, mod_b, mlp_w1, mlp_w2, e_w_in, e_conv_w, e_conv_b, e_dt_bias,
           e_A_log, e_D, e_ssd_norm, e_lambda, e_diff_norm, e_w_out, o_w_in, o_sinks, o_w_out):
    batch, seq, d = x.shape
    depth = norm_gains.shape[0]
    assert seq % (2 * ATTN_BLOCK) == 0 and d % LANES == 0
    mods = modulation(c, mod_w, mod_b.reshape(depth * 2, 3 * d))
    mlp_w1_b, mlp_w2_b = mlp_w1.astype(BF16), mlp_w2.astype(BF16)
    far_bucket = REL_BUCKETS - 1
    diff_bias = bias_tiles(rel_bias, _diff_bucket_idx(ATTN_BLOCK), shift_bucket=far_bucket, scale=LOG2_E)
    diff_bias = diff_bias.reshape(rel_bias.shape[1], 2, ATTN_BLOCK, ATTN_BLOCK)
    swa_bias = bias_tiles(rel_bias, _swa_bucket_idx(), scale=LOG2_E)

    x2 = x.reshape(batch * seq, d)
    for layer in range(depth):
        j = layer // 2
        gains = norm_gains[layer]
        if layer % 2 == 0:
            x2 = even_layer_mixer(x2, mods[2 * layer], gains[0], gains[1], e_w_in[j], e_conv_w[j], e_conv_b[j],
                                  e_dt_bias[j], e_A_log[j], e_D[j], e_ssd_norm[j], e_lambda[j], e_diff_norm[j],
                                  e_w_out[j], diff_bias, layer, batch, seq)
        else:
            x2 = odd_layer_mixer(x2, mods[2 * layer], gains[0], gains[1], o_w_in[j], o_sinks[j], o_w_out[j],
                                 swa_bias, batch, seq)
        x2 = mlp(x2, mods[2 * layer + 1], gains[2], gains[3], mlp_w1_b, mlp_w2_b, layer, seq)
    return x2.reshape(batch, seq, d)
```

```python
import functools
import math

import jax
import jax.numpy as jnp
import numpy as np
from jax import lax
from jax.experimental import pallas as pl
from jax.experimental.pallas import tpu as pltpu

EPS = 1e-6
MASK_VALUE = -1e30
LOG2_E = math.log2(math.e)
LANES = 128
SUBLANES = 8
VMEM_LIMIT = 56 * 1024 * 1024

CHUNK = 128
REL_BUCKETS = 32
REL_MAX_DIST = 128
SSD_HEAD_DIM = 64
SSD_GROUPS = 4
SSD_STATE = 128
SSD_CONV = 4
DIFF_HEAD_DIM = 64
RET_HEADS = 4
RET_QK_DIM = 128
RET_V_DIM = 256
ROPE_BASE = 10000.0
SWA_HEADS = 8
SWA_KV_HEADS = 2
SWA_HEAD_DIM = 64
SWA_BLOCK = 128
ATTN_BLOCK = 512

BF16 = jnp.bfloat16
F32 = jnp.float32
NT_DIMS = (((1,), (1,)), ((), ()))


def _params(semantics, flags=None):
    return pltpu.CompilerParams(dimension_semantics=semantics, vmem_limit_bytes=VMEM_LIMIT, flags=flags)


def _full_spec(shape):
    return pl.BlockSpec(shape, lambda *_: (0,) * len(shape))


def _silu(x):
    h = 0.5 * x
    return h * jnp.tanh(h) + h


def _split3(x):
    hi = x.astype(BF16)
    r1 = x - hi.astype(F32)
    mid = r1.astype(BF16)
    lo = (r1 - mid.astype(F32)).astype(BF16)
    return hi, mid, lo


def _rms(x, eps=EPS):
    return x * lax.rsqrt(jnp.mean(x * x, axis=-1, keepdims=True) + eps)


def _mod_kernel(c_ref, w_ref, b_ref, o_ref):
    c_act = _silu(c_ref[...])
    o_ref[...] = jnp.dot(c_act, w_ref[...], preferred_element_type=F32,
                         precision=lax.Precision.HIGHEST) + b_ref[...]


def modulation(c, mod_w, mod_b):
    b, d = c.shape
    n = mod_b.shape[0]
    per_layer = mod_w.shape[1]
    rows = SUBLANES
    c_pad = jnp.zeros((rows, d), F32).at[:b].set(c)
    tn = d
    out = pl.pallas_call(
        _mod_kernel,
        grid=(n, 3 * d // tn),
        in_specs=[pl.BlockSpec((rows, d), lambda s, j: (0, 0)),
                  pl.BlockSpec((None, None, d, tn), lambda s, j: (s // per_layer, s % per_layer, 0, j)),
                  pl.BlockSpec((None, 1, tn), lambda s, j: (s, 0, j))],
        out_specs=pl.BlockSpec((None, rows, tn), lambda s, j: (s, 0, j)),
        out_shape=jax.ShapeDtypeStruct((n, rows, 3 * d), F32),
        compiler_params=_params(("parallel", "parallel")),
        name="modulation",
    )(c_pad, mod_w, mod_b.reshape(n, 1, 3 * d))
    return out[:, :b].reshape(n, b, 3, d)


def _t5_bucket_np(dist):
    max_exact = REL_BUCKETS // 2
    logd = np.log(np.maximum(dist, 1).astype(np.float32) / np.float32(max_exact))
    large = max_exact + (logd / np.float32(math.log(REL_MAX_DIST / max_exact))
                         * np.float32(REL_BUCKETS - max_exact)).astype(np.int32)
    large = np.minimum(large, REL_BUCKETS - 1)
    return np.where(dist < max_exact, dist, large).astype(np.int32)


def _bias_kernel(rb_ref, idx_ref, o_ref, *, shift_bucket, scale, block_buckets):
    h = pl.program_id(0)
    shift = rb_ref[shift_bucket, h] if shift_bucket is not None else 0.0

    def value(bucket):
        return (rb_ref[bucket, h] - shift) * scale

    for (r0, c0), buckets in block_buckets:
        window = (slice(r0, r0 + LANES), slice(c0, c0 + LANES))
        idx = idx_ref[window]
        acc = jnp.full(idx.shape, MASK_VALUE if -1 in buckets else value(max(buckets)), F32)
        for bucket in buckets:
            if bucket >= 0 and len(buckets) > 1:
                acc = jnp.where(idx == bucket, value(bucket), acc)
        o_ref[window] = acc


def bias_tiles(rel_bias, bucket_idx, shift_bucket=None, scale=1.0):
    n_heads = rel_bias.shape[1]
    r, c = bucket_idx.shape
    block_buckets = tuple(((r0, c0), tuple(int(b) for b in np.unique(bucket_idx[r0:r0 + LANES, c0:c0 + LANES])))
                          for r0 in range(0, r, LANES) for c0 in range(0, c, LANES))
    return pl.pallas_call(
        functools.partial(_bias_kernel, shift_bucket=shift_bucket, scale=scale, block_buckets=block_buckets),
        grid=(n_heads,),
        in_specs=[pl.BlockSpec(memory_space=pltpu.SMEM),
                  pl.BlockSpec((r, c), lambda h: (0, 0))],
        out_specs=pl.BlockSpec((None, r, c), lambda h: (h, 0, 0)),
        out_shape=jax.ShapeDtypeStruct((n_heads, r, c), F32),
        compiler_params=_params(("parallel",)),
        name="bias_tiles",
    )(rel_bias, jnp.asarray(bucket_idx))


def _diff_bucket_idx(t):
    q = np.arange(t)[None, :]
    k = np.arange(t)[:, None]
    diag = q - k
    idx_diag = np.where(diag >= 0, _t5_bucket_np(np.maximum(diag, 0)), -1)
    idx_prev = _t5_bucket_np(diag + t)
    assert (_t5_bucket_np(diag + 2 * t) == REL_BUCKETS - 1).all()
    return np.concatenate([idx_diag, idx_prev], axis=0).astype(np.int32)


def _swa_bucket_idx():
    q = np.arange(SWA_BLOCK)[None, :]
    k = np.arange(2 * SWA_BLOCK)[:, None] - SWA_BLOCK
    dist = q - k
    valid = (dist >= 0) & (dist < SWA_BLOCK)
    return np.where(valid, _t5_bucket_np(np.maximum(dist, 0)), -1).astype(np.int32)


def _norm_proj_kernel(x_ref, mod_ref, g_ref, *refs):
    n = len(refs) // 2
    w_refs, o_refs = refs[:n], refs[n:]
    mod = mod_ref[...]
    h = _rms(x_ref[...]) * g_ref[...] * (1.0 + mod[1:2]) + mod[0:1]
    hb = h.astype(BF16)
    for w_ref, o_ref in zip(w_refs, o_refs):
        o_ref[...] = jnp.dot(hb, w_ref[...], preferred_element_type=F32).astype(o_ref.dtype)


def norm_proj(x2, mod, gain, weights, out_dtypes, seq, tm=512):
    m, d = x2.shape
    per_batch = seq // tm
    in_specs = [pl.BlockSpec((tm, d), lambda i: (i, 0)),
                pl.BlockSpec((None, 3, d), lambda i: (i // per_batch, 0, 0)),
                _full_spec((1, d))]
    in_specs += [_full_spec(w.shape) for w in weights]
    out_specs = [pl.BlockSpec((tm, w.shape[1]), lambda i: (i, 0)) for w in weights]
    out_shape = [jax.ShapeDtypeStruct((m, w.shape[1]), dt) for w, dt in zip(weights, out_dtypes)]
    return pl.pallas_call(
        _norm_proj_kernel,
        grid=(m // tm,),
        in_specs=in_specs,
        out_specs=out_specs,
        out_shape=out_shape,
        compiler_params=_params(("parallel",)),
        name="norm_proj",
    )(x2, mod, gain.reshape(1, d), *weights)


def _out_proj_kernel(x_ref, mod_ref, g_ref, *refs):
    n = (len(refs) - 1) // 2
    y_refs, w_refs, o_ref = refs[:n], refs[n:2 * n], refs[2 * n]
    acc = None
    for y_ref, w_ref in zip(y_refs, w_refs):
        part = jnp.dot(y_ref[...], w_ref[...], preferred_element_type=F32)
        acc = part if acc is None else acc + part
    gate = mod_ref[...][2:3]
    o_ref[...] = x_ref[...] + gate * (_rms(acc) * g_ref[...])


def out_proj(x2, mod, gain, ys, weights, seq, tm=512):
    m, d = x2.shape
    per_batch = seq // tm
    in_specs = [pl.BlockSpec((tm, d), lambda i: (i, 0)),
                pl.BlockSpec((None, 3, d), lambda i: (i // per_batch, 0, 0)),
                _full_spec((1, d))]
    in_specs += [pl.BlockSpec((tm, y.shape[1]), lambda i: (i, 0)) for y in ys]
    in_specs += [_full_spec(w.shape) for w in weights]
    return pl.pallas_call(
        _out_proj_kernel,
        grid=(m // tm,),
        in_specs=in_specs,
        out_specs=pl.BlockSpec((tm, d), lambda i: (i, 0)),
        out_shape=jax.ShapeDtypeStruct((m, d), F32),
        compiler_params=_params(("parallel",)),
        name="out_proj",
    )(x2, mod, gain.reshape(1, d), *ys, *weights)


def _mlp_kernel(x_ref, mod_ref, g_pre_ref, g_post_ref, w1_ref, w2_ref, o_ref, *, ff_chunk):
    x = x_ref[...]
    mod = mod_ref[...]
    h = _rms(x) * g_pre_ref[...] * (1.0 + mod[1:2]) + mod[0:1]
    hb = h.astype(BF16)
    d_ff = w1_ref.shape[1]
    acc = None
    for c0 in range(0, d_ff, ff_chunk):
        a = jnp.dot(hb, w1_ref[:, c0:c0 + ff_chunk], preferred_element_type=F32)
        a = jnp.square(jnp.maximum(a, 0.0)).astype(BF16)
        part = jnp.dot(a, w2_ref[c0:c0 + ff_chunk, :], preferred_element_type=F32)
        acc = part if acc is None else acc + part
    o_ref[...] = x + mod[2:3] * (_rms(acc) * g_post_ref[...])


def mlp(x2, mod, g_pre, g_post, w1, w2, layer, seq, tm=512, ff_chunk=1024):
    m, d = x2.shape
    per_batch = seq // tm
    return pl.pallas_call(
        functools.partial(_mlp_kernel, ff_chunk=ff_chunk),
        grid=(m // tm,),
        in_specs=[pl.BlockSpec((tm, d), lambda i: (i, 0)),
                  pl.BlockSpec((None, 3, d), lambda i: (i // per_batch, 0, 0)),
                  _full_spec((1, d)), _full_spec((1, d)),
                  pl.BlockSpec((None,) + w1.shape[1:], lambda i: (layer, 0, 0)),
                  pl.BlockSpec((None,) + w2.shape[1:], lambda i: (layer, 0, 0))],
        out_specs=pl.BlockSpec((tm, d), lambda i: (i, 0)),
        out_shape=jax.ShapeDtypeStruct((m, d), F32),
        compiler_params=_params(("parallel",)),
        name="mlp",
    )(x2, mod, g_pre.reshape(1, d), g_post.reshape(1, d), w1, w2)


def _ssd_kernel(z_ref, xbc_ref, dt_ref, cw_ref, cb_ref, dtb_ref, alog_ref, dskip_ref, nw_ref, expand_ref,
                shift_ref, o_ref, conv_buf, state_ref, *, d_inner):
    chunk = CHUNK
    rows_per_step = z_ref.shape[0]
    n_state = SSD_STATE
    gn = SSD_GROUPS * n_state
    pair = 2 * SSD_HEAD_DIM
    heads_per_group = d_inner // SSD_HEAD_DIM // SSD_GROUPS
    halo = conv_buf.shape[0] - rows_per_step

    @pl.when(pl.program_id(1) == 0)
    def _():
        state_ref[...] = jnp.zeros_like(state_ref)
        conv_buf[0:halo, :] = jnp.zeros((halo, conv_buf.shape[1]), BF16)

    conv_buf[halo:, :] = xbc_ref[...]

    def chunk_body(r0):
        rows = slice(r0, r0 + chunk)
        cur = xbc_ref[rows, :]
        shifted = jnp.dot(shift_ref[...], conv_buf[r0:r0 + halo + chunk, :], preferred_element_type=F32)
        acc = cb_ref[...] + cw_ref[SSD_CONV - 1:SSD_CONV, :] * cur.astype(F32)
        for tap in range(SSD_CONV - 1):
            acc = acc + cw_ref[tap:tap + 1, :] * shifted[tap * chunk:(tap + 1) * chunk, :]
        xbc = _silu(acc)
        xs = xbc[:, :d_inner]
        b_all = xbc[:, d_inner:d_inner + gn]
        c_all = xbc[:, d_inner + gn:]

        dt_in = dt_ref[rows, :] + dtb_ref[...]
        dt = jnp.maximum(dt_in, 0.0) + jnp.log1p(jnp.exp(-jnp.abs(dt_in)))
        a = dt * (-jnp.exp(alog_ref[...]))
        row = lax.broadcasted_iota(jnp.int32, (chunk, chunk), 0)
        col = lax.broadcasted_iota(jnp.int32, (chunk, chunk), 1)
        causal = row >= col
        tril = jnp.where(causal, 1.0, 0.0).astype(BF16)
        parts = jnp.dot(tril, jnp.concatenate(_split3(a), axis=1), preferred_element_type=F32)
        a_cs = parts[:, :LANES] + parts[:, LANES:2 * LANES] + parts[:, 2 * LANES:]
        a_cs_t = a_cs.T
        lhs = jnp.concatenate([jnp.concatenate(_split3(dt), axis=1), jnp.concatenate(_split3(a_cs), axis=1)], axis=0)
        expanded = jnp.dot(lhs, expand_ref[...], preferred_element_type=F32)
        dt_e, acs_e = expanded[:chunk], expanded[chunk:]
        a_last_e = acs_e[chunk - 1:chunk, :]
        x_dt = xs * dt_e
        w_state = (x_dt * jnp.exp(a_last_e - acs_e)).astype(BF16)
        out_scale = jnp.exp(acs_e)
        chunk_decay = jnp.exp(a_last_e)
        x_dt_b = x_dt.astype(BF16)
        lane = lax.broadcasted_iota(jnp.int32, (chunk, pair), 1)
        first_head = lane < SSD_HEAD_DIM

        n_heads = d_inner // SSD_HEAD_DIM
        cgs = [c_all[:, g * n_state:(g + 1) * n_state].astype(BF16) for g in range(SSD_GROUPS)]
        bgs = [b_all[:, g * n_state:(g + 1) * n_state] for g in range(SSD_GROUPS)]
        cbs = [lax.dot_general(cg, bg.astype(BF16), NT_DIMS, preferred_element_type=F32)
               for cg, bg in zip(cgs, bgs)]
        bg_ts = [bg.T.astype(BF16) for bg in bgs]
        y_blocks = []
        for p in range(n_heads // 2):
            g = p // (heads_per_group // 2)
            sl = slice(p * pair, (p + 1) * pair)
            xp = x_dt_b[:, sl]
            y_diag = None
            for k in range(2):
                hh = 2 * p + k
                decay = jnp.exp(jnp.where(causal, a_cs[:, hh:hh + 1] - a_cs_t[hh:hh + 1, :], MASK_VALUE))
                m = (cbs[g] * decay).astype(BF16)
                xh = jnp.where(first_head if k == 0 else jnp.logical_not(first_head), xp, jnp.zeros_like(xp))
                part = jnp.dot(m, xh, preferred_element_type=F32)
                y_diag = part if y_diag is None else y_diag + part
            st = state_ref[p]
            y_off = jnp.dot(cgs[g], st.astype(BF16), preferred_element_type=F32) * out_scale[:, sl]
            state_ref[p] = st * chunk_decay[:, sl] + jnp.dot(bg_ts[g], w_state[:, sl], preferred_element_type=F32)
            y_blocks.append(y_diag + y_off)
        y = jnp.concatenate(y_blocks, axis=-1) + xs * dskip_ref[...]
        y = y * _silu(z_ref[rows, :].astype(F32))
        o_ref[rows, :] = (_rms(y) * nw_ref[...]).astype(o_ref.dtype)

    for r0 in range(0, rows_per_step, chunk):
        chunk_body(r0)
    conv_buf[0:halo, :] = conv_buf[rows_per_step:rows_per_step + halo, :]


def ssd_mixer(z, xbc, dt_raw, conv_w, conv_b, dt_bias, a_log, d_skip, norm_w, batch, seq, chunks_per_step=4):
    m, d_inner = z.shape
    conv_ch = xbc.shape[1]
    n_heads = d_inner // SSD_HEAD_DIM
    rows = chunks_per_step * CHUNK
    nc = seq // rows
    pad = LANES - n_heads
    expand = np.zeros((LANES, d_inner), np.float32)
    for h in range(n_heads):
        expand[h, h * SSD_HEAD_DIM:(h + 1) * SSD_HEAD_DIM] = 1.0
    expand3 = jnp.asarray(np.tile(expand, (3, 1)), BF16)
    halo = 2 * SUBLANES
    shift = np.zeros(((SSD_CONV - 1) * CHUNK, halo + CHUNK), np.float32)
    for tap in range(SSD_CONV - 1):
        shift[tap * CHUNK + np.arange(CHUNK), halo - (SSD_CONV - 1) + tap + np.arange(CHUNK)] = 1.0
    shift = jnp.asarray(shift, BF16)
    row_spec = lambda width: pl.BlockSpec((rows, width), lambda b, c: (b * nc + c, 0))
    return pl.pallas_call(
        functools.partial(_ssd_kernel, d_inner=d_inner),
        grid=(batch, nc),
        in_specs=[row_spec(d_inner), row_spec(conv_ch), row_spec(LANES),
                  _full_spec((SSD_CONV, conv_ch)), _full_spec((1, conv_ch)),
                  _full_spec((1, LANES)), _full_spec((1, LANES)),
                  _full_spec((1, d_inner)), _full_spec((1, d_inner)), _full_spec(expand3.shape),
                  _full_spec(shift.shape)],
        out_specs=row_spec(d_inner),
        out_shape=jax.ShapeDtypeStruct((m, d_inner), BF16),
        scratch_shapes=[pltpu.VMEM((halo + rows, conv_ch), BF16),
                        pltpu.VMEM((n_heads // 2, SSD_STATE, 2 * SSD_HEAD_DIM), F32)],
        compiler_params=_params(("parallel", "arbitrary")),
        name="ssd_mixer",
    )(z, xbc, dt_raw, conv_w, conv_b.reshape(1, conv_ch),
      jnp.pad(dt_bias, (0, pad)).reshape(1, LANES), jnp.pad(a_log, (0, pad)).reshape(1, LANES),
      jnp.repeat(d_skip, SSD_HEAD_DIM).reshape(1, d_inner), norm_w.reshape(1, d_inner), expand3, shift)


def _diff_attn_kernel(q_ref, k_ref, v_ref, bias_ref, lam_ref, nw_ref, o_ref,
                      vt_ref, s0_ref, s1_ref, cmax0_ref, cmax1_ref, p_ref, m_ref, alpha_ref, acc_ref,
                      *, lam_init):
    s_refs, cmax_refs = (s0_ref, s1_ref), (cmax0_ref, cmax1_ref)
    t = s0_ref.shape[0]
    tq = q_ref.shape[0]
    head_w = v_ref.shape[1]
    ext_rows = vt_ref.shape[1]
    i = pl.program_id(2)
    q_t = q_ref[...].astype(F32).T.astype(BF16)
    dim = lax.broadcasted_iota(jnp.int32, q_t.shape, 0)
    zero = jnp.zeros_like(q_t)
    qq_t = jnp.concatenate([jnp.where(dim < DIFF_HEAD_DIM, q_t, zero),
                            jnp.where(dim >= DIFF_HEAD_DIM, q_t, zero)], axis=1)

    @pl.when(i == 0)
    def _():
        for j in range(vt_ref.shape[0]):
            vt_ref[j, :head_w, :] = v_ref[j * t:(j + 1) * t, :].astype(F32).T.astype(BF16)
            vt_ref[j, head_w:, :] = jnp.ones((ext_rows - head_w, t), BF16)

    m_ref[...] = jnp.full(m_ref.shape, MASK_VALUE, F32)
    acc_ref[...] = jnp.zeros(acc_ref.shape, F32)

    diag_tile, prev_tile = bias_ref[0], bias_ref[1]
    all_cols = ((0, t), (t, t), (tq, t), (tq + t, t))
    near_cols = (all_cols[1::2], all_cols, all_cols)
    near_tiles = ((diag_tile, diag_tile),
                  (diag_tile, prev_tile, diag_tile, prev_tile),
                  (prev_tile, None, prev_tile, None))
    near_blocks = (2 * i + 1, 2 * i, jnp.maximum(2 * i - 1, 0))

    def raw_logits(block, cols=all_cols):
        kb = k_ref[pl.ds(pl.multiple_of(block * t, t), t), :]
        rhs = qq_t if cols is all_cols else jnp.concatenate([qq_t[:, c0:c0 + w] for c0, w in cols], axis=1)
        return jnp.dot(kb, rhs, preferred_element_type=F32)

    def store_logits(pieces, slot, cols=all_cols):
        for piece, (c0, w) in zip(pieces, cols):
            s_refs[slot][:, c0:c0 + w] = piece
            chains = [piece[0:SUBLANES, :], piece[SUBLANES:2 * SUBLANES, :]]
            for n, r in enumerate(range(2 * SUBLANES, t, SUBLANES)):
                chains[n % 2] = jnp.maximum(chains[n % 2], piece[r:r + SUBLANES, :])
            col_max = jnp.maximum(chains[0], chains[1])
            for shift in (4, 2, 1):
                col_max = jnp.maximum(col_max, pltpu.roll(col_max, shift, 0))
            cmax_refs[slot][:, c0:c0 + w] = col_max

    def is_near(visit):
        return isinstance(visit, int) and visit < len(near_blocks)

    def block_of(visit):
        return near_blocks[visit] if is_near(visit) else visit - len(near_blocks)

    def cols_of(visit):
        return near_cols[visit] if is_near(visit) else all_cols

    def logits_piece(visit, slot, n):
        col = cols_of(visit)[n]
        piece = raw_logits(block_of(visit), (col,))
        tile = near_tiles[visit][n] if is_near(visit) else None
        store_logits([piece if tile is None else piece + tile], slot, (col,))

    def softmax_accumulate(visit, slot, col_ranges=((0, 2 * tq),)):
        s_ref = s_refs[slot]
        vt = vt_ref[block_of(visit)]
        packed = 2 * SUBLANES
        for c0, w in col_ranges:
            c = slice(c0, c0 + w)
            m_old = m_ref[:, c]
            m_new = jnp.maximum(m_old, cmax_refs[slot][:, c])
            m_ref[:, c] = m_new
            alpha_ref[:, c] = jnp.exp2(m_old - m_new)
            m_tile = jnp.concatenate([m_new, m_new], axis=0)
            for r in range(0, t, packed):
                p_ref[r:r + packed, c] = jnp.exp2((s_ref[r:r + packed, c] - m_tile).astype(BF16))
            pv = jnp.dot(vt, p_ref[:, c], preferred_element_type=F32)
            alpha = alpha_ref[:, c]
            for r in range(0, ext_rows, SUBLANES):
                acc_ref[r:r + SUBLANES, c] = alpha * acc_ref[r:r + SUBLANES, c] + pv[r:r + SUBLANES, :]

    def pipeline_step(visit, slot):
        stage1, stage2 = cols_of(visit), cols_of(visit - 1)
        for n in range(max(len(stage1), len(stage2))):
            if n < len(stage1):
                logits_piece(visit, slot, n)
            if n < len(stage2):
                softmax_accumulate(visit - 1, 1 - slot, (stage2[n],))

    for n in range(len(near_cols[0])):
        logits_piece(0, 0, n)
    pipeline_step(1, 1)

    @pl.when(i < pl.num_programs(2))
    def _():
        pipeline_step(2, 0)

    @pl.when(i > 0)
    def _():
        pipeline_step(3, 1)
        last = 2 * i + 1

        def pair(k, _):
            pipeline_step(4 + 2 * k, 0)

            @pl.when(5 + 2 * k <= last)
            def _():
                pipeline_step(5 + 2 * k, 1)

            return 0

        lax.fori_loop(0, i - 1, pair, 0)
        softmax_accumulate(last, 1, all_cols)

    lam = lam_ref[...]
    lam_full = (jnp.exp(jnp.sum(lam[0:1] * lam[1:2], axis=-1, keepdims=True))
                - jnp.exp(jnp.sum(lam[2:3] * lam[3:4], axis=-1, keepdims=True)) + lam_init)
    inv_sum = 1.0 / acc_ref[head_w:head_w + SUBLANES, :]
    out_t = acc_ref[:head_w, :] * jnp.tile(inv_sum, (head_w // SUBLANES, 1))
    out = (out_t[:, :tq] - lam_full * out_t[:, tq:]).T
    o_ref[...] = (_rms(out) * nw_ref[...] * (1.0 - lam_init)).astype(o_ref.dtype)


def diff_attention(q, k, v, bias, lam, subln_w, batch, seq, lam_init):
    m, width = q.shape
    head_w = 2 * DIFF_HEAD_DIM
    n_heads = width // head_w
    t = ATTN_BLOCK
    tq = 2 * t
    nq = seq // tq
    cols = 2 * tq
    ext_rows = head_w + 2 * SUBLANES
    return pl.pallas_call(
        functools.partial(_diff_attn_kernel, lam_init=lam_init),
        grid=(batch, n_heads, nq),
        in_specs=[pl.BlockSpec((tq, head_w), lambda b, h, i: (b * nq + i, h)),
                  pl.BlockSpec((seq, head_w), lambda b, h, i: (b, h)),
                  pl.BlockSpec((seq, head_w), lambda b, h, i: (b, h)),
                  pl.BlockSpec((None, 2, t, t), lambda b, h, i: (h, 0, 0, 0)),
                  _full_spec(lam.shape), _full_spec((1, head_w))],
        out_specs=pl.BlockSpec((tq, head_w), lambda b, h, i: (b * nq + i, h)),
        out_shape=jax.ShapeDtypeStruct((m, width), BF16),
        scratch_shapes=[pltpu.VMEM((seq // t, ext_rows, t), BF16),
                        pltpu.VMEM((t, cols), F32), pltpu.VMEM((t, cols), F32),
                        pltpu.VMEM((SUBLANES, cols), F32), pltpu.VMEM((SUBLANES, cols), F32),
                        pltpu.VMEM((t, cols), BF16),
                        pltpu.VMEM((SUBLANES, cols), F32),
                        pltpu.VMEM((SUBLANES, cols), F32),
                        pltpu.VMEM((ext_rows, cols), F32)],
        compiler_params=_params(("parallel", "parallel", "arbitrary")),
        name="diff_attention",
    )(q, k, v, bias, lam, subln_w.reshape(1, head_w))


def _retention_kernel(q_ref, k_ref, v_ref, g_ref, cos_hi_ref, sin_hi_ref, cos_lo_ref, sin_lo_ref, sign_ref,
                      decay_ref, zeta_ref, xi_ref, o_ref, state_ref, *, chunk_decay):
    @pl.when(pl.program_id(1) == 0)
    def _():
        state_ref[...] = jnp.zeros_like(state_ref)

    c_hi, s_hi = cos_hi_ref[...], sin_hi_ref[...]
    c_lo, s_lo = cos_lo_ref[...], sin_lo_ref[...]
    cos_all = c_hi * c_lo - s_hi * s_lo
    sin_all = (s_hi * c_lo + c_hi * s_lo) * sign_ref[...]

    half = RET_QK_DIM // 2
    states = [state_ref[h] for h in range(RET_HEADS)]
    for r0 in range(0, q_ref.shape[0], CHUNK):
        rows = slice(r0, r0 + CHUNK)
        cos, sin = cos_all[rows, :], sin_all[rows, :]

        def rope(t):
            return t * cos + pltpu.roll(t, half, 1) * sin

        for h in range(RET_HEADS):
            qs = slice(h * RET_QK_DIM, (h + 1) * RET_QK_DIM)
            vs = slice(h * RET_V_DIM, (h + 1) * RET_V_DIM)
            qr = rope(q_ref[rows, qs].astype(F32))
            kr = rope(k_ref[rows, qs].astype(F32)) * (RET_QK_DIM ** -0.5)
            vh = v_ref[rows, vs]
            scores = lax.dot_general(qr.astype(BF16), kr.astype(BF16), NT_DIMS,
                                     preferred_element_type=F32) * decay_ref[h]
            inner = jnp.dot(scores.astype(BF16), vh, preferred_element_type=F32)
            cross = jnp.dot((qr * xi_ref[h]).astype(BF16), states[h].astype(BF16), preferred_element_type=F32)
            kv = jnp.dot((kr.T * zeta_ref[h]).astype(BF16), vh, preferred_element_type=F32)
            states[h] = states[h] * chunk_decay[h] + kv
            out = _rms(inner + cross) * _silu(g_ref[rows, vs].astype(F32))
            o_ref[rows, vs] = out.astype(o_ref.dtype)
    for h in range(RET_HEADS):
        state_ref[h] = states[h]


def retention(rq, rk, rv, rg, batch, seq, chunks_per_step=8):
    m = rq.shape[0]
    rows = chunks_per_step * CHUNK
    nc = seq // rows
    half = RET_QK_DIM // 2
    inv = ROPE_BASE ** (-jnp.arange(half, dtype=F32) / half)
    both_halves = lambda a: jnp.concatenate([a, a], axis=-1)
    ang_hi = (jnp.arange(nc, dtype=F32) * rows)[:, None] * inv[None]
    ang_lo = jnp.arange(rows, dtype=F32)[:, None] * inv[None]
    cos_hi = both_halves(jnp.cos(ang_hi)).reshape(nc, 1, RET_QK_DIM)
    sin_hi = both_halves(jnp.sin(ang_hi)).reshape(nc, 1, RET_QK_DIM)
    cos_lo, sin_lo = both_halves(jnp.cos(ang_lo)), both_halves(jnp.sin(ang_lo))
    sign = jnp.concatenate([-jnp.ones((1, half), F32), jnp.ones((1, half), F32)], axis=-1)
    log_gamma = jnp.log(1.0 - 2.0 ** (-5.0 - jnp.arange(RET_HEADS, dtype=F32)))
    idx = jnp.arange(CHUNK, dtype=F32)
    rel = idx[:, None] - idx[None, :]
    decay = jnp.where(rel >= 0, jnp.exp(jnp.maximum(rel, 0.0)[None] * log_gamma[:, None, None]), 0.0)
    zeta = jnp.exp((CHUNK - 1 - idx)[None] * log_gamma[:, None])[:, None, :]
    xi = jnp.broadcast_to(jnp.exp((idx + 1.0)[None] * log_gamma[:, None])[:, :, None],
                          (RET_HEADS, CHUNK, RET_QK_DIM))
    gamma = 1.0 - 2.0 ** (-5.0 - np.arange(RET_HEADS, dtype=np.float64))
    chunk_decay = tuple(float(g ** CHUNK) for g in gamma)
    row_spec = lambda width: pl.BlockSpec((rows, width), lambda b, c: (b * nc + c, 0))
    return pl.pallas_call(
        functools.partial(_retention_kernel, chunk_decay=chunk_decay),
        grid=(batch, nc),
        in_specs=[row_spec(rq.shape[1]), row_spec(rk.shape[1]), row_spec(rv.shape[1]), row_spec(rg.shape[1]),
                  pl.BlockSpec((None, 1, RET_QK_DIM), lambda b, c: (c, 0, 0)),
                  pl.BlockSpec((None, 1, RET_QK_DIM), lambda b, c: (c, 0, 0)),
                  _full_spec(cos_lo.shape), _full_spec(sin_lo.shape), _full_spec(sign.shape),
                  _full_spec(decay.shape), _full_spec(zeta.shape), _full_spec(xi.shape)],
        out_specs=row_spec(rv.shape[1]),
        out_shape=jax.ShapeDtypeStruct((m, rv.shape[1]), BF16),
        scratch_shapes=[pltpu.VMEM((RET_HEADS, RET_QK_DIM, RET_V_DIM), F32)],
        compiler_params=_params(("parallel", "arbitrary")),
        name="retention",
    )(rq, rk, rv, rg, cos_hi, sin_hi, cos_lo, sin_lo, sign, decay, zeta, xi)


def _swa_kernel(sink_ref, q_ref, kp_ref, kc_ref, vp_ref, vc_ref, bias_ref, o_ref):
    i = pl.program_id(1)
    blk = SWA_BLOCK
    rep = SWA_HEADS // SWA_KV_HEADS
    d = SWA_HEAD_DIM
    cols = rep * blk
    k_all = jnp.concatenate([kp_ref[...], kc_ref[...]], axis=0)
    v_all = jnp.concatenate([vp_ref[...], vc_ref[...]], axis=0)
    key_row = lax.broadcasted_iota(jnp.int32, (2 * blk, cols), 0)
    head_of_lane = lax.broadcasted_iota(jnp.int32, (SUBLANES, cols), 1) // blk
    zero = jnp.zeros((d, blk), F32)
    ones = jnp.ones((2 * SUBLANES, 2 * blk), F32)
    for r0 in range(0, q_ref.shape[0], blk):
        kb = k_all[r0:r0 + 2 * blk]
        v_t = v_all[r0:r0 + 2 * blk].astype(F32).T
        outs = []
        for g in range(SWA_KV_HEADS):
            q_t = q_ref[r0:r0 + blk, g * rep * d:(g + 1) * rep * d].astype(F32).T
            rhs = jnp.concatenate(
                [jnp.concatenate([q_t[r * d:(r + 1) * d], zero] if g == 0 else [zero, q_t[r * d:(r + 1) * d]],
                                 axis=0) for r in range(rep)], axis=1).astype(BF16)
            s = jnp.dot(kb, rhs, preferred_element_type=F32)
            s = s + jnp.concatenate([bias_ref[g * rep + r] for r in range(rep)], axis=1)
            if r0 == 0:
                s = jnp.where(jnp.logical_and(key_row < blk, i == 0), MASK_VALUE, s)
            sink = jnp.zeros((SUBLANES, cols), F32)
            for r in range(rep):
                sink = jnp.where(head_of_lane == r, sink_ref[g * rep + r] * LOG2_E, sink)
            chains = [s[0:SUBLANES], s[SUBLANES:2 * SUBLANES]]
            for n, r in enumerate(range(2 * SUBLANES, 2 * blk, SUBLANES)):
                chains[n % 2] = jnp.maximum(chains[n % 2], s[r:r + SUBLANES])
            col_max = jnp.maximum(chains[0], chains[1])
            for shift in (4, 2, 1):
                col_max = jnp.maximum(col_max, pltpu.roll(col_max, shift, 0))
            m = jnp.maximum(col_max, sink)
            p = jnp.exp2(s - jnp.tile(m, (2 * blk // SUBLANES, 1))).astype(BF16)
            v_ext = jnp.concatenate([v_t[g * d:(g + 1) * d], ones], axis=0).astype(BF16)
            pv = jnp.dot(v_ext, p, preferred_element_type=F32)
            inv = 1.0 / (pv[d:d + SUBLANES] + jnp.exp2(sink - m))
            o_t = pv[:d] * jnp.tile(inv, (d // SUBLANES, 1))
            for j in range(rep // 2):
                pair_t = jnp.concatenate([o_t[:, (2 * j) * blk:(2 * j + 1) * blk],
                                          o_t[:, (2 * j + 1) * blk:(2 * j + 2) * blk]], axis=0)
                outs.append(pair_t.T)
        o_ref[r0:r0 + blk, :] = jnp.concatenate(outs, axis=-1).astype(o_ref.dtype)


def sliding_window_attention(sq, sk, sv, sinks, bias, batch, seq, blocks_per_step=8):
    m, qw = sq.shape
    kw = sk.shape[1]
    rows = blocks_per_step * SWA_BLOCK
    nb = seq // rows
    cur = lambda b, i: (b * nb + i, 0)
    prev = lambda b, i: ((b * nb + i) * blocks_per_step - jnp.minimum(i, 1), 0)
    return pl.pallas_call(
        _swa_kernel,
        grid=(batch, nb),
        in_specs=[pl.BlockSpec(memory_space=pltpu.SMEM),
                  pl.BlockSpec((rows, qw), cur),
                  pl.BlockSpec((SWA_BLOCK, kw), prev), pl.BlockSpec((rows, kw), cur),
                  pl.BlockSpec((SWA_BLOCK, kw), prev), pl.BlockSpec((rows, kw), cur),
                  _full_spec(bias.shape)],
        out_specs=pl.BlockSpec((rows, qw), cur),
        out_shape=jax.ShapeDtypeStruct((m, qw), BF16),
        compiler_params=_params(("parallel", "parallel")),
        name="sliding_window_attention",
    )(sinks, sq, sk, sk, sv, sv, bias)


def _split_cols(w, sizes):
    offs = np.cumsum((0,) + tuple(sizes))
    return [w[:, offs[j]:offs[j + 1]] for j in range(len(sizes))]


def even_layer_mixer(x2, mod, g_pre, g_post, w_in, conv_w, conv_b, dt_bias, a_log, d_skip, ssd_norm, lam,
                     diff_norm, w_out, diff_bias, layer_idx, batch, seq):
    d = x2.shape[1]
    d_inner = d
    n_ssd_heads = d_inner // SSD_HEAD_DIM
    conv_ch = d_inner + 2 * SSD_GROUPS * SSD_STATE
    wz, wxbc, wdt, wq, wk, wv = _split_cols(w_in, (d_inner, conv_ch, n_ssd_heads, d, d, d))
    wdt = jnp.pad(wdt, ((0, 0), (0, LANES - n_ssd_heads)))
    wq = wq * (DIFF_HEAD_DIM ** -0.5 * LOG2_E)
    weights = [w.astype(BF16) for w in (wz, wxbc, wdt, wq, wk, wv)]
    z, xbc, dt_raw, q, k, v = norm_proj(x2, mod, g_pre, weights, (BF16, BF16, F32, BF16, BF16, BF16), seq)
    y_ssd = ssd_mixer(z, xbc, dt_raw, conv_w, conv_b, dt_bias, a_log, d_skip, ssd_norm, batch, seq)
    lam_init = 0.8 - 0.6 * math.exp(-0.3 * layer_idx)
    y_diff = diff_attention(q, k, v, diff_bias, lam, diff_norm, batch, seq, lam_init)
    w_out = w_out.astype(BF16)
    return out_proj(x2, mod, g_post, [y_ssd, y_diff], [w_out[:d_inner], w_out[d_inner:]], seq)


def odd_layer_mixer(x2, mod, g_pre, g_post, w_in, sinks, w_out, swa_bias, batch, seq):
    ret_qk = RET_HEADS * RET_QK_DIM
    ret_v = RET_HEADS * RET_V_DIM
    swa_q = SWA_HEADS * SWA_HEAD_DIM
    swa_kv = SWA_KV_HEADS * SWA_HEAD_DIM
    wrq, wrk, wrv, wrg, wsq, wsk, wsv = _split_cols(w_in, (ret_qk, ret_qk, ret_v, ret_v, swa_q, swa_kv, swa_kv))
    wsq = wsq * (SWA_HEAD_DIM ** -0.5 * LOG2_E)
    weights = [w.astype(BF16) for w in (wrq, wrk, wrv, wrg, wsq, wsk, wsv)]
    rq, rk, rv, rg, sq, sk, sv = norm_proj(x2, mod, g_pre, weights, (BF16,) * 7, seq)
    y_ret = retention(rq, rk, rv, rg, batch, seq)
    y_swa = sliding_window_attention(sq, sk, sv, sinks, swa_bias, batch, seq)
    w_out = w_out.astype(BF16)
    return out_proj(x2, mod, g_post, [y_ret, y_swa], [w_out[:ret_v], w_out[ret_v:]], seq)


def kernel(x, c, rel_bias, norm_gains, mod_w, mod_b, mlp_w1, mlp_w2, e_w_in, e_conv_w, e_conv_b, e_dt_bias,
           e_A_log, e_D, e_ssd_norm, e_lambda, e_diff_norm, e_w_out, o_w_in, o_sinks, o_w_out):
    batch, seq, d = x.shape
    depth = norm_gains.shape[0]
    assert seq % (2 * ATTN_BLOCK) == 0 and d % LANES == 0
    mods = modulation(c, mod_w, mod_b.reshape(depth * 2, 3 * d))
    mlp_w1_b, mlp_w2_b = mlp_w1.astype(BF16), mlp_w2.astype(BF16)
    far_bucket = REL_BUCKETS - 1
    diff_bias = bias_tiles(rel_bias, _diff_bucket_idx(ATTN_BLOCK), shift_bucket=far_bucket, scale=LOG2_E)
    diff_bias = diff_bias.reshape(rel_bias.shape[1], 2, ATTN_BLOCK, ATTN_BLOCK)
    swa_bias = bias_tiles(rel_bias, _swa_bucket_idx(), scale=LOG2_E)

    x2 = x.reshape(batch * seq, d)
    for layer in range(depth):
        j = layer // 2
        gains = norm_gains[layer]
        if layer % 2 == 0:
            x2 = even_layer_mixer(x2, mods[2 * layer], gains[0], gains[1], e_w_in[j], e_conv_w[j], e_conv_b[j],
                                  e_dt_bias[j], e_A_log[j], e_D[j], e_ssd_norm[j], e_lambda[j], e_diff_norm[j],
                                  e_w_out[j], diff_bias, layer, batch, seq)
        else:
            x2 = odd_layer_mixer(x2, mods[2 * layer], gains[0], gains[1], o_w_in[j], o_sinks[j], o_w_out[j],
                                 swa_bias, batch, seq)
        x2 = mlp(x2, mods[2 * layer + 1], gains[2], gains[3], mlp_w1_b, mlp_w2_b, layer, seq)
    return x2.reshape(batch, seq, d)
```

```python
import functools
import math

import jax
import jax.numpy as jnp
import numpy as np
from jax import lax
from jax.experimental import pallas as pl
from jax.experimental.pallas import tpu as pltpu

EPS = 1e-6
MASK_VALUE = -1e30
LOG2_E = math.log2(math.e)
LANES = 128
SUBLANES = 8
VMEM_LIMIT = 56 * 1024 * 1024

CHUNK = 128
REL_BUCKETS = 32
REL_MAX_DIST = 128
SSD_HEAD_DIM = 64
SSD_GROUPS = 4
SSD_STATE = 128
SSD_CONV = 4
DIFF_HEAD_DIM = 64
RET_HEADS = 4
RET_QK_DIM = 128
RET_V_DIM = 256
ROPE_BASE = 10000.0
SWA_HEADS = 8
SWA_KV_HEADS = 2
SWA_HEAD_DIM = 64
SWA_BLOCK = 128
ATTN_BLOCK = 512

BF16 = jnp.bfloat16
F32 = jnp.float32
NT_DIMS = (((1,), (1,)), ((), ()))


def _params(semantics):
    return pltpu.CompilerParams(dimension_semantics=semantics, vmem_limit_bytes=VMEM_LIMIT)


def _full_spec(shape):
    return pl.BlockSpec(shape, lambda *_: (0,) * len(shape))


def _silu(x):
    h = 0.5 * x
    return h * jnp.tanh(h) + h


def _split3(x):
    hi = x.astype(BF16)
    r1 = x - hi.astype(F32)
    mid = r1.astype(BF16)
    lo = (r1 - mid.astype(F32)).astype(BF16)
    return hi, mid, lo


def _rms(x, eps=EPS):
    return x * lax.rsqrt(jnp.mean(x * x, axis=-1, keepdims=True) + eps)


def _mod_kernel(c_ref, w_ref, b_ref, o_ref):
    c_act = _silu(c_ref[...])
    o_ref[...] = jnp.dot(c_act, w_ref[...], preferred_element_type=F32,
                         precision=lax.Precision.HIGHEST) + b_ref[...]


def modulation(c, mod_w, mod_b):
    b, d = c.shape
    n = mod_b.shape[0]
    per_layer = mod_w.shape[1]
    rows = SUBLANES
    c_pad = jnp.zeros((rows, d), F32).at[:b].set(c)
    tn = d
    out = pl.pallas_call(
        _mod_kernel,
        grid=(n, 3 * d // tn),
        in_specs=[pl.BlockSpec((rows, d), lambda s, j: (0, 0)),
                  pl.BlockSpec((None, None, d, tn), lambda s, j: (s // per_layer, s % per_layer, 0, j)),
                  pl.BlockSpec((None, 1, tn), lambda s, j: (s, 0, j))],
        out_specs=pl.BlockSpec((None, rows, tn), lambda s, j: (s, 0, j)),
        out_shape=jax.ShapeDtypeStruct((n, rows, 3 * d), F32),
        compiler_params=_params(("parallel", "parallel")),
        name="modulation",
    )(c_pad, mod_w, mod_b.reshape(n, 1, 3 * d))
    return out[:, :b].reshape(n, b, 3, d)


def _t5_bucket_np(dist):
    max_exact = REL_BUCKETS // 2
    logd = np.log(np.maximum(dist, 1).astype(np.float32) / np.float32(max_exact))
    large = max_exact + (logd / np.float32(math.log(REL_MAX_DIST / max_exact))
                         * np.float32(REL_BUCKETS - max_exact)).astype(np.int32)
    large = np.minimum(large, REL_BUCKETS - 1)
    return np.where(dist < max_exact, dist, large).astype(np.int32)


def _bias_kernel(rb_ref, idx_ref, o_ref, *, shift_bucket, scale, block_buckets):
    h = pl.program_id(0)
    shift = rb_ref[shift_bucket, h] if shift_bucket is not None else 0.0

    def value(bucket):
        return (rb_ref[bucket, h] - shift) * scale

    for (r0, c0), buckets in block_buckets:
        window = (slice(r0, r0 + LANES), slice(c0, c0 + LANES))
        idx = idx_ref[window]
        acc = jnp.full(idx.shape, MASK_VALUE if -1 in buckets else value(max(buckets)), F32)
        for bucket in buckets:
            if bucket >= 0 and len(buckets) > 1:
                acc = jnp.where(idx == bucket, value(bucket), acc)
        o_ref[window] = acc


def bias_tiles(rel_bias, bucket_idx, shift_bucket=None, scale=1.0):
    n_heads = rel_bias.shape[1]
    r, c = bucket_idx.shape
    block_buckets = tuple(((r0, c0), tuple(int(b) for b in np.unique(bucket_idx[r0:r0 + LANES, c0:c0 + LANES])))
                          for r0 in range(0, r, LANES) for c0 in range(0, c, LANES))
    return pl.pallas_call(
        functools.partial(_bias_kernel, shift_bucket=shift_bucket, scale=scale, block_buckets=block_buckets),
        grid=(n_heads,),
        in_specs=[pl.BlockSpec(memory_space=pltpu.SMEM),
                  pl.BlockSpec((r, c), lambda h: (0, 0))],
        out_specs=pl.BlockSpec((None, r, c), lambda h: (h, 0, 0)),
        out_shape=jax.ShapeDtypeStruct((n_heads, r, c), F32),
        compiler_params=_params(("parallel",)),
        name="bias_tiles",
    )(rel_bias, jnp.asarray(bucket_idx))


def _diff_bucket_idx(t):
    q = np.arange(t)[None, :]
    k = np.arange(t)[:, None]
    diag = q - k
    idx_diag = np.where(diag >= 0, _t5_bucket_np(np.maximum(diag, 0)), -1)
    idx_prev = _t5_bucket_np(diag + t)
    assert (_t5_bucket_np(diag + 2 * t) == REL_BUCKETS - 1).all()
    return np.concatenate([idx_diag, idx_prev], axis=0).astype(np.int32)


def _swa_bucket_idx():
    q = np.arange(SWA_BLOCK)[None, :]
    k = np.arange(2 * SWA_BLOCK)[:, None] - SWA_BLOCK
    dist = q - k
    valid = (dist >= 0) & (dist < SWA_BLOCK)
    return np.where(valid, _t5_bucket_np(np.maximum(dist, 0)), -1).astype(np.int32)


def _norm_proj_kernel(x_ref, mod_ref, g_ref, *refs):
    n = len(refs) // 2
    w_refs, o_refs = refs[:n], refs[n:]
    mod = mod_ref[...]
    h = _rms(x_ref[...]) * g_ref[...] * (1.0 + mod[1:2]) + mod[0:1]
    hb = h.astype(BF16)
    for w_ref, o_ref in zip(w_refs, o_refs):
        o_ref[...] = jnp.dot(hb, w_ref[...], preferred_element_type=F32).astype(o_ref.dtype)


def norm_proj(x2, mod, gain, weights, out_dtypes, seq, tm=512):
    m, d = x2.shape
    per_batch = seq // tm
    in_specs = [pl.BlockSpec((tm, d), lambda i: (i, 0)),
                pl.BlockSpec((None, 3, d), lambda i: (i // per_batch, 0, 0)),
                _full_spec((1, d))]
    in_specs += [_full_spec(w.shape) for w in weights]
    out_specs = [pl.BlockSpec((tm, w.shape[1]), lambda i: (i, 0)) for w in weights]
    out_shape = [jax.ShapeDtypeStruct((m, w.shape[1]), dt) for w, dt in zip(weights, out_dtypes)]
    return pl.pallas_call(
        _norm_proj_kernel,
        grid=(m // tm,),
        in_specs=in_specs,
        out_specs=out_specs,
        out_shape=out_shape,
        compiler_params=_params(("parallel",)),
        name="norm_proj",
    )(x2, mod, gain.reshape(1, d), *weights)


def _out_proj_kernel(x_ref, mod_ref, g_ref, *refs):
    n = (len(refs) - 1) // 2
    y_refs, w_refs, o_ref = refs[:n], refs[n:2 * n], refs[2 * n]
    acc = None
    for y_ref, w_ref in zip(y_refs, w_refs):
        part = jnp.dot(y_ref[...], w_ref[...], preferred_element_type=F32)
        acc = part if acc is None else acc + part
    gate = mod_ref[...][2:3]
    o_ref[...] = x_ref[...] + gate * (_rms(acc) * g_ref[...])


def out_proj(x2, mod, gain, ys, weights, seq, tm=512):
    m, d = x2.shape
    per_batch = seq // tm
    in_specs = [pl.BlockSpec((tm, d), lambda i: (i, 0)),
                pl.BlockSpec((None, 3, d), lambda i: (i // per_batch, 0, 0)),
                _full_spec((1, d))]
    in_specs += [pl.BlockSpec((tm, y.shape[1]), lambda i: (i, 0)) for y in ys]
    in_specs += [_full_spec(w.shape) for w in weights]
    return pl.pallas_call(
        _out_proj_kernel,
        grid=(m // tm,),
        in_specs=in_specs,
        out_specs=pl.BlockSpec((tm, d), lambda i: (i, 0)),
        out_shape=jax.ShapeDtypeStruct((m, d), F32),
        compiler_params=_params(("parallel",)),
        name="out_proj",
    )(x2, mod, gain.reshape(1, d), *ys, *weights)


def _mlp_kernel(x_ref, mod_ref, g_pre_ref, g_post_ref, w1_ref, w2_ref, o_ref, *, ff_chunk):
    x = x_ref[...]
    mod = mod_ref[...]
    h = _rms(x) * g_pre_ref[...] * (1.0 + mod[1:2]) + mod[0:1]
    hb = h.astype(BF16)
    d_ff = w1_ref.shape[1]
    acc = None
    for c0 in range(0, d_ff, ff_chunk):
        a = jnp.dot(hb, w1_ref[:, c0:c0 + ff_chunk], preferred_element_type=F32)
        a = jnp.square(jnp.maximum(a, 0.0)).astype(BF16)
        part = jnp.dot(a, w2_ref[c0:c0 + ff_chunk, :], preferred_element_type=F32)
        acc = part if acc is None else acc + part
    o_ref[...] = x + mod[2:3] * (_rms(acc) * g_post_ref[...])


def mlp(x2, mod, g_pre, g_post, w1, w2, layer, seq, tm=512, ff_chunk=1024):
    m, d = x2.shape
    per_batch = seq // tm
    return pl.pallas_call(
        functools.partial(_mlp_kernel, ff_chunk=ff_chunk),
        grid=(m // tm,),
        in_specs=[pl.BlockSpec((tm, d), lambda i: (i, 0)),
                  pl.BlockSpec((None, 3, d), lambda i: (i // per_batch, 0, 0)),
                  _full_spec((1, d)), _full_spec((1, d)),
                  pl.BlockSpec((None,) + w1.shape[1:], lambda i: (layer, 0, 0)),
                  pl.BlockSpec((None,) + w2.shape[1:], lambda i: (layer, 0, 0))],
        out_specs=pl.BlockSpec((tm, d), lambda i: (i, 0)),
        out_shape=jax.ShapeDtypeStruct((m, d), F32),
        compiler_params=_params(("parallel",)),
        name="mlp",
    )(x2, mod, g_pre.reshape(1, d), g_post.reshape(1, d), w1, w2)


def _ssd_kernel(z_ref, xbc_ref, dt_ref, cw_ref, cb_ref, dtb_ref, alog_ref, dskip_ref, nw_ref, expand_ref,
                shift_ref, o_ref, conv_buf, state_ref, *, d_inner):
    chunk = CHUNK
    rows_per_step = z_ref.shape[0]
    n_state = SSD_STATE
    gn = SSD_GROUPS * n_state
    pair = 2 * SSD_HEAD_DIM
    heads_per_group = d_inner // SSD_HEAD_DIM // SSD_GROUPS
    halo = conv_buf.shape[0] - rows_per_step

    @pl.when(pl.program_id(1) == 0)
    def _():
        state_ref[...] = jnp.zeros_like(state_ref)
        conv_buf[0:halo, :] = jnp.zeros((halo, conv_buf.shape[1]), BF16)

    conv_buf[halo:, :] = xbc_ref[...]

    def chunk_body(r0):
        rows = slice(r0, r0 + chunk)
        cur = xbc_ref[rows, :]
        shifted = jnp.dot(shift_ref[...], conv_buf[r0:r0 + halo + chunk, :], preferred_element_type=F32)
        acc = cb_ref[...] + cw_ref[SSD_CONV - 1:SSD_CONV, :] * cur.astype(F32)
        for tap in range(SSD_CONV - 1):
            acc = acc + cw_ref[tap:tap + 1, :] * shifted[tap * chunk:(tap + 1) * chunk, :]
        xbc = _silu(acc)
        xs = xbc[:, :d_inner]
        b_all = xbc[:, d_inner:d_inner + gn]
        c_all = xbc[:, d_inner + gn:]

        dt_in = dt_ref[rows, :] + dtb_ref[...]
        dt = jnp.maximum(dt_in, 0.0) + jnp.log1p(jnp.exp(-jnp.abs(dt_in)))
        a = dt * (-jnp.exp(alog_ref[...]))
        row = lax.broadcasted_iota(jnp.int32, (chunk, chunk), 0)
        col = lax.broadcasted_iota(jnp.int32, (chunk, chunk), 1)
        causal = row >= col
        tril = jnp.where(causal, 1.0, 0.0).astype(BF16)
        parts = jnp.dot(tril, jnp.concatenate(_split3(a), axis=1), preferred_element_type=F32)
        a_cs = parts[:, :LANES] + parts[:, LANES:2 * LANES] + parts[:, 2 * LANES:]
        a_cs_t = a_cs.T
        lhs = jnp.concatenate([jnp.concatenate(_split3(dt), axis=1), jnp.concatenate(_split3(a_cs), axis=1)], axis=0)
        expanded = jnp.dot(lhs, expand_ref[...], preferred_element_type=F32)
        dt_e, acs_e = expanded[:chunk], expanded[chunk:]
        a_last_e = acs_e[chunk - 1:chunk, :]
        x_dt = xs * dt_e
        w_state = (x_dt * jnp.exp(a_last_e - acs_e)).astype(BF16)
        out_scale = jnp.exp(acs_e)
        chunk_decay = jnp.exp(a_last_e)
        x_dt_b = x_dt.astype(BF16)
        lane = lax.broadcasted_iota(jnp.int32, (chunk, pair), 1)
        first_head = lane < SSD_HEAD_DIM

        n_heads = d_inner // SSD_HEAD_DIM
        cgs = [c_all[:, g * n_state:(g + 1) * n_state].astype(BF16) for g in range(SSD_GROUPS)]
        bgs = [b_all[:, g * n_state:(g + 1) * n_state] for g in range(SSD_GROUPS)]
        cbs = [lax.dot_general(cg, bg.astype(BF16), NT_DIMS, preferred_element_type=F32)
               for cg, bg in zip(cgs, bgs)]
        bg_ts = [bg.T.astype(BF16) for bg in bgs]
        y_blocks = []
        for p in range(n_heads // 2):
            g = p // (heads_per_group // 2)
            sl = slice(p * pair, (p + 1) * pair)
            xp = x_dt_b[:, sl]
            y_diag = None
            for k in range(2):
                hh = 2 * p + k
                decay = jnp.exp(jnp.where(causal, a_cs[:, hh:hh + 1] - a_cs_t[hh:hh + 1, :], MASK_VALUE))
                m = (cbs[g] * decay).astype(BF16)
                xh = jnp.where(first_head if k == 0 else jnp.logical_not(first_head), xp, jnp.zeros_like(xp))
                part = jnp.dot(m, xh, preferred_element_type=F32)
                y_diag = part if y_diag is None else y_diag + part
            st = state_ref[p]
            y_off = jnp.dot(cgs[g], st.astype(BF16), preferred_element_type=F32) * out_scale[:, sl]
            state_ref[p] = st * chunk_decay[:, sl] + jnp.dot(bg_ts[g], w_state[:, sl], preferred_element_type=F32)
            y_blocks.append(y_diag + y_off)
        y = jnp.concatenate(y_blocks, axis=-1) + xs * dskip_ref[...]
        y = y * _silu(z_ref[rows, :].astype(F32))
        o_ref[rows, :] = (_rms(y) * nw_ref[...]).astype(o_ref.dtype)

    for r0 in range(0, rows_per_step, chunk):
        chunk_body(r0)
    conv_buf[0:halo, :] = conv_buf[rows_per_step:rows_per_step + halo, :]


def ssd_mixer(z, xbc, dt_raw, conv_w, conv_b, dt_bias, a_log, d_skip, norm_w, batch, seq, chunks_per_step=4):
    m, d_inner = z.shape
    conv_ch = xbc.shape[1]
    n_heads = d_inner // SSD_HEAD_DIM
    rows = chunks_per_step * CHUNK
    nc = seq // rows
    pad = LANES - n_heads
    expand = np.zeros((LANES, d_inner), np.float32)
    for h in range(n_heads):
        expand[h, h * SSD_HEAD_DIM:(h + 1) * SSD_HEAD_DIM] = 1.0
    expand3 = jnp.asarray(np.tile(expand, (3, 1)), BF16)
    halo = 2 * SUBLANES
    shift = np.zeros(((SSD_CONV - 1) * CHUNK, halo + CHUNK), np.float32)
    for tap in range(SSD_CONV - 1):
        shift[tap * CHUNK + np.arange(CHUNK), halo - (SSD_CONV - 1) + tap + np.arange(CHUNK)] = 1.0
    shift = jnp.asarray(shift, BF16)
    row_spec = lambda width: pl.BlockSpec((rows, width), lambda b, c: (b * nc + c, 0))
    return pl.pallas_call(
        functools.partial(_ssd_kernel, d_inner=d_inner),
        grid=(batch, nc),
        in_specs=[row_spec(d_inner), row_spec(conv_ch), row_spec(LANES),
                  _full_spec((SSD_CONV, conv_ch)), _full_spec((1, conv_ch)),
                  _full_spec((1, LANES)), _full_spec((1, LANES)),
                  _full_spec((1, d_inner)), _full_spec((1, d_inner)), _full_spec(expand3.shape),
                  _full_spec(shift.shape)],
        out_specs=row_spec(d_inner),
        out_shape=jax.ShapeDtypeStruct((m, d_inner), BF16),
        scratch_shapes=[pltpu.VMEM((halo + rows, conv_ch), BF16),
                        pltpu.VMEM((n_heads // 2, SSD_STATE, 2 * SSD_HEAD_DIM), F32)],
        compiler_params=_params(("parallel", "arbitrary")),
        name="ssd_mixer",
    )(z, xbc, dt_raw, conv_w, conv_b.reshape(1, conv_ch),
      jnp.pad(dt_bias, (0, pad)).reshape(1, LANES), jnp.pad(a_log, (0, pad)).reshape(1, LANES),
      jnp.repeat(d_skip, SSD_HEAD_DIM).reshape(1, d_inner), norm_w.reshape(1, d_inner), expand3, shift)


def _diff_attn_kernel(q_ref, k_ref, v_ref, bias_ref, lam_ref, nw_ref, o_ref,
                      vt_ref, s0_ref, s1_ref, cmax0_ref, cmax1_ref, p_ref, m_ref, alpha_ref, acc_ref,
                      *, lam_init):
    s_refs, cmax_refs = (s0_ref, s1_ref), (cmax0_ref, cmax1_ref)
    t = s0_ref.shape[0]
    tq = q_ref.shape[0]
    head_w = v_ref.shape[1]
    ext_rows = vt_ref.shape[1]
    i = pl.program_id(2)
    q_t = q_ref[...].astype(F32).T.astype(BF16)
    dim = lax.broadcasted_iota(jnp.int32, q_t.shape, 0)
    zero = jnp.zeros_like(q_t)
    qq_t = jnp.concatenate([jnp.where(dim < DIFF_HEAD_DIM, q_t, zero),
                            jnp.where(dim >= DIFF_HEAD_DIM, q_t, zero)], axis=1)

    @pl.when(i == 0)
    def _():
        for j in range(vt_ref.shape[0]):
            vt_ref[j, :head_w, :] = v_ref[j * t:(j + 1) * t, :].astype(F32).T.astype(BF16)
            vt_ref[j, head_w:, :] = jnp.ones((ext_rows - head_w, t), BF16)

    m_ref[...] = jnp.full(m_ref.shape, MASK_VALUE, F32)
    acc_ref[...] = jnp.zeros(acc_ref.shape, F32)

    diag_tile, prev_tile = bias_ref[0], bias_ref[1]
    all_cols = ((0, t), (t, t), (tq, t), (tq + t, t))
    near_cols = (all_cols[1::2], all_cols, all_cols)
    near_tiles = ((diag_tile, diag_tile),
                  (diag_tile, prev_tile, diag_tile, prev_tile),
                  (prev_tile, None, prev_tile, None))
    near_blocks = (2 * i + 1, 2 * i, jnp.maximum(2 * i - 1, 0))

    def raw_logits(block, cols=all_cols):
        kb = k_ref[pl.ds(pl.multiple_of(block * t, t), t), :]
        rhs = qq_t if cols is all_cols else jnp.concatenate([qq_t[:, c0:c0 + w] for c0, w in cols], axis=1)
        return jnp.dot(kb, rhs, preferred_element_type=F32)

    def store_logits(pieces, slot, cols=all_cols):
        for piece, (c0, w) in zip(pieces, cols):
            s_refs[slot][:, c0:c0 + w] = piece
            chains = [piece[0:SUBLANES, :], piece[SUBLANES:2 * SUBLANES, :]]
            for n, r in enumerate(range(2 * SUBLANES, t, SUBLANES)):
                chains[n % 2] = jnp.maximum(chains[n % 2], piece[r:r + SUBLANES, :])
            col_max = jnp.maximum(chains[0], chains[1])
            for shift in (4, 2, 1):
                col_max = jnp.maximum(col_max, pltpu.roll(col_max, shift, 0))
            cmax_refs[slot][:, c0:c0 + w] = col_max

    def is_near(visit):
        return isinstance(visit, int) and visit < len(near_blocks)

    def block_of(visit):
        return near_blocks[visit] if is_near(visit) else visit - len(near_blocks)

    def cols_of(visit):
        return near_cols[visit] if is_near(visit) else all_cols

    def logits_piece(visit, slot, n):
        col = cols_of(visit)[n]
        piece = raw_logits(block_of(visit), (col,))
        tile = near_tiles[visit][n] if is_near(visit) else None
        store_logits([piece if tile is None else piece + tile], slot, (col,))

    def softmax_accumulate(visit, slot, col_ranges=((0, 2 * tq),)):
        s_ref = s_refs[slot]
        vt = vt_ref[block_of(visit)]
        packed = 2 * SUBLANES
        for c0, w in col_ranges:
            c = slice(c0, c0 + w)
            m_old = m_ref[:, c]
            m_new = jnp.maximum(m_old, cmax_refs[slot][:, c])
            m_ref[:, c] = m_new
            alpha_ref[:, c] = jnp.exp2(m_old - m_new)
            m_tile = jnp.concatenate([m_new, m_new], axis=0)
            for r in range(0, t, packed):
                p_ref[r:r + packed, c] = jnp.exp2((s_ref[r:r + packed, c] - m_tile).astype(BF16))
            pv = jnp.dot(vt, p_ref[:, c], preferred_element_type=F32)
            alpha = alpha_ref[:, c]
            for r in range(0, ext_rows, SUBLANES):
                acc_ref[r:r + SUBLANES, c] = alpha * acc_ref[r:r + SUBLANES, c] + pv[r:r + SUBLANES, :]

    def pipeline_step(visit, slot):
        stage1, stage2 = cols_of(visit), cols_of(visit - 1)
        for n in range(max(len(stage1), len(stage2))):
            if n < len(stage1):
                logits_piece(visit, slot, n)
            if n < len(stage2):
                softmax_accumulate(visit - 1, 1 - slot, (stage2[n],))

    for n in range(len(near_cols[0])):
        logits_piece(0, 0, n)
    pipeline_step(1, 1)

    @pl.when(i < pl.num_programs(2))
    def _():
        pipeline_step(2, 0)

    @pl.when(i > 0)
    def _():
        pipeline_step(3, 1)
        last = 2 * i + 1

        def pair(k, _):
            pipeline_step(4 + 2 * k, 0)

            @pl.when(5 + 2 * k <= last)
            def _():
                pipeline_step(5 + 2 * k, 1)

            return 0

        lax.fori_loop(0, i - 1, pair, 0)
        softmax_accumulate(last, 1, all_cols)

    lam = lam_ref[...]
    lam_full = (jnp.exp(jnp.sum(lam[0:1] * lam[1:2], axis=-1, keepdims=True))
                - jnp.exp(jnp.sum(lam[2:3] * lam[3:4], axis=-1, keepdims=True)) + lam_init)
    inv_sum = 1.0 / acc_ref[head_w:head_w + SUBLANES, :]
    out_t = acc_ref[:head_w, :] * jnp.tile(inv_sum, (head_w // SUBLANES, 1))
    out = (out_t[:, :tq] - lam_full * out_t[:, tq:]).T
    o_ref[...] = (_rms(out) * nw_ref[...] * (1.0 - lam_init)).astype(o_ref.dtype)


def diff_attention(q, k, v, bias, lam, subln_w, batch, seq, lam_init):
    m, width = q.shape
    head_w = 2 * DIFF_HEAD_DIM
    n_heads = width // head_w
    t = ATTN_BLOCK
    tq = 2 * t
    nq = seq // tq
    cols = 2 * tq
    ext_rows = head_w + 2 * SUBLANES
    return pl.pallas_call(
        functools.partial(_diff_attn_kernel, lam_init=lam_init),
        grid=(batch, n_heads, nq),
        in_specs=[pl.BlockSpec((tq, head_w), lambda b, h, i: (b * nq + i, h)),
                  pl.BlockSpec((seq, head_w), lambda b, h, i: (b, h)),
                  pl.BlockSpec((seq, head_w), lambda b, h, i: (b, h)),
                  pl.BlockSpec((None, 2, t, t), lambda b, h, i: (h, 0, 0, 0)),
                  _full_spec(lam.shape), _full_spec((1, head_w))],
        out_specs=pl.BlockSpec((tq, head_w), lambda b, h, i: (b * nq + i, h)),
        out_shape=jax.ShapeDtypeStruct((m, width), BF16),
        scratch_shapes=[pltpu.VMEM((seq // t, ext_rows, t), BF16),
                        pltpu.VMEM((t, cols), F32), pltpu.VMEM((t, cols), F32),
                        pltpu.VMEM((SUBLANES, cols), F32), pltpu.VMEM((SUBLANES, cols), F32),
                        pltpu.VMEM((t, cols), BF16),
                        pltpu.VMEM((SUBLANES, cols), F32),
                        pltpu.VMEM((SUBLANES, cols), F32),
                        pltpu.VMEM((ext_rows, cols), F32)],
        compiler_params=_params(("parallel", "parallel", "arbitrary")),
        name="diff_attention",
    )(q, k, v, bias, lam, subln_w.reshape(1, head_w))


def _retention_kernel(q_ref, k_ref, v_ref, g_ref, cos_hi_ref, sin_hi_ref, cos_lo_ref, sin_lo_ref, sign_ref,
                      decay_ref, zeta_ref, xi_ref, o_ref, state_ref, *, chunk_decay):
    @pl.when(pl.program_id(1) == 0)
    def _():
        state_ref[...] = jnp.zeros_like(state_ref)

    c_hi, s_hi = cos_hi_ref[...], sin_hi_ref[...]
    c_lo, s_lo = cos_lo_ref[...], sin_lo_ref[...]
    cos_all = c_hi * c_lo - s_hi * s_lo
    sin_all = (s_hi * c_lo + c_hi * s_lo) * sign_ref[...]

    half = RET_QK_DIM // 2
    states = [state_ref[h] for h in range(RET_HEADS)]
    heads = range(RET_HEADS)
    qk_of = lambda h: slice(h * RET_QK_DIM, (h + 1) * RET_QK_DIM)
    v_of = lambda h: slice(h * RET_V_DIM, (h + 1) * RET_V_DIM)
    for r0 in range(0, q_ref.shape[0], CHUNK):
        rows = slice(r0, r0 + CHUNK)
        cos, sin = cos_all[rows, :], sin_all[rows, :]

        def rope(t):
            return t * cos + pltpu.roll(t, half, 1) * sin

        qr = [rope(q_ref[rows, qk_of(h)].astype(F32)) for h in heads]
        kr = [rope(k_ref[rows, qk_of(h)].astype(F32)) * (RET_QK_DIM ** -0.5) for h in heads]
        scores = [lax.dot_general(qr[h].astype(BF16), kr[h].astype(BF16), NT_DIMS,
                                  preferred_element_type=F32) * decay_ref[h] for h in heads]
        cross = [jnp.dot((qr[h] * xi_ref[h]).astype(BF16), states[h].astype(BF16), preferred_element_type=F32)
                 for h in heads]
        inner = [jnp.dot(scores[h].astype(BF16), v_ref[rows, v_of(h)], preferred_element_type=F32) for h in heads]
        kv = [jnp.dot((kr[h].T * zeta_ref[h]).astype(BF16), v_ref[rows, v_of(h)], preferred_element_type=F32)
              for h in heads]
        for h in heads:
            states[h] = states[h] * chunk_decay[h] + kv[h]
            out = _rms(inner[h] + cross[h]) * _silu(g_ref[rows, v_of(h)].astype(F32))
            o_ref[rows, v_of(h)] = out.astype(o_ref.dtype)
    for h in range(RET_HEADS):
        state_ref[h] = states[h]


def retention(rq, rk, rv, rg, batch, seq, chunks_per_step=8):
    m = rq.shape[0]
    rows = chunks_per_step * CHUNK
    nc = seq // rows
    half = RET_QK_DIM // 2
    inv = ROPE_BASE ** (-jnp.arange(half, dtype=F32) / half)
    both_halves = lambda a: jnp.concatenate([a, a], axis=-1)
    ang_hi = (jnp.arange(nc, dtype=F32) * rows)[:, None] * inv[None]
    ang_lo = jnp.arange(rows, dtype=F32)[:, None] * inv[None]
    cos_hi = both_halves(jnp.cos(ang_hi)).reshape(nc, 1, RET_QK_DIM)
    sin_hi = both_halves(jnp.sin(ang_hi)).reshape(nc, 1, RET_QK_DIM)
    cos_lo, sin_lo = both_halves(jnp.cos(ang_lo)), both_halves(jnp.sin(ang_lo))
    sign = jnp.concatenate([-jnp.ones((1, half), F32), jnp.ones((1, half), F32)], axis=-1)
    log_gamma = jnp.log(1.0 - 2.0 ** (-5.0 - jnp.arange(RET_HEADS, dtype=F32)))
    idx = jnp.arange(CHUNK, dtype=F32)
    rel = idx[:, None] - idx[None, :]
    decay = jnp.where(rel >= 0, jnp.exp(jnp.maximum(rel, 0.0)[None] * log_gamma[:, None, None]), 0.0)
    zeta = jnp.exp((CHUNK - 1 - idx)[None] * log_gamma[:, None])[:, None, :]
    xi = jnp.broadcast_to(jnp.exp((idx + 1.0)[None] * log_gamma[:, None])[:, :, None],
                          (RET_HEADS, CHUNK, RET_QK_DIM))
    gamma = 1.0 - 2.0 ** (-5.0 - np.arange(RET_HEADS, dtype=np.float64))
    chunk_decay = tuple(float(g ** CHUNK) for g in gamma)
    row_spec = lambda width: pl.BlockSpec((rows, width), lambda b, c: (b * nc + c, 0))
    return pl.pallas_call(
        functools.partial(_retention_kernel, chunk_decay=chunk_decay),
        grid=(batch, nc),
        in_specs=[row_spec(rq.shape[1]), row_spec(rk.shape[1]), row_spec(rv.shape[1]), row_spec(rg.shape[1]),
                  pl.BlockSpec((None, 1, RET_QK_DIM), lambda b, c: (c, 0, 0)),
                  pl.BlockSpec((None, 1, RET_QK_DIM), lambda b, c: (c, 0, 0)),
                  _full_spec(cos_lo.shape), _full_spec(sin_lo.shape), _full_spec(sign.shape),
                  _full_spec(decay.shape), _full_spec(zeta.shape), _full_spec(xi.shape)],
        out_specs=row_spec(rv.shape[1]),
        out_shape=jax.ShapeDtypeStruct((m, rv.shape[1]), BF16),
        scratch_shapes=[pltpu.VMEM((RET_HEADS, RET_QK_DIM, RET_V_DIM), F32)],
        compiler_params=_params(("parallel", "arbitrary")),
        name="retention",
    )(rq, rk, rv, rg, cos_hi, sin_hi, cos_lo, sin_lo, sign, decay, zeta, xi)


def _swa_kernel(sink_ref, q_ref, kp_ref, kc_ref, vp_ref, vc_ref, bias_ref, o_ref):
    i = pl.program_id(1)
    blk = SWA_BLOCK
    rep = SWA_HEADS // SWA_KV_HEADS
    d = SWA_HEAD_DIM
    cols = rep * blk
    k_all = jnp.concatenate([kp_ref[...], kc_ref[...]], axis=0)
    v_all = jnp.concatenate([vp_ref[...], vc_ref[...]], axis=0)
    key_row = lax.broadcasted_iota(jnp.int32, (2 * blk, cols), 0)
    head_of_lane = lax.broadcasted_iota(jnp.int32, (SUBLANES, cols), 1) // blk
    zero = jnp.zeros((d, blk), F32)
    ones = jnp.ones((2 * SUBLANES, 2 * blk), F32)
    for r0 in range(0, q_ref.shape[0], blk):
        kb = k_all[r0:r0 + 2 * blk]
        v_t = v_all[r0:r0 + 2 * blk].astype(F32).T
        logits = []
        for g in range(SWA_KV_HEADS):
            q_t = q_ref[r0:r0 + blk, g * rep * d:(g + 1) * rep * d].astype(F32).T
            rhs = jnp.concatenate(
                [jnp.concatenate([q_t[r * d:(r + 1) * d], zero] if g == 0 else [zero, q_t[r * d:(r + 1) * d]],
                                 axis=0) for r in range(rep)], axis=1).astype(BF16)
            s = jnp.dot(kb, rhs, preferred_element_type=F32)
            s = s + jnp.concatenate([bias_ref[g * rep + r] for r in range(rep)], axis=1)
            if r0 == 0:
                s = jnp.where(jnp.logical_and(key_row < blk, i == 0), MASK_VALUE, s)
            sink = jnp.zeros((SUBLANES, cols), F32)
            for r in range(rep):
                sink = jnp.where(head_of_lane == r, sink_ref[g * rep + r] * LOG2_E, sink)
            chains = [s[0:SUBLANES], s[SUBLANES:2 * SUBLANES]]
            for n, r in enumerate(range(2 * SUBLANES, 2 * blk, SUBLANES)):
                chains[n % 2] = jnp.maximum(chains[n % 2], s[r:r + SUBLANES])
            col_max = jnp.maximum(chains[0], chains[1])
            for shift in (4, 2, 1):
                col_max = jnp.maximum(col_max, pltpu.roll(col_max, shift, 0))
            logits.append((s, sink, jnp.maximum(col_max, sink)))
        probs = [jnp.exp2((s - jnp.tile(m, (2 * blk // SUBLANES, 1))).astype(BF16)) for s, _, m in logits]
        outs = []
        for g in range(SWA_KV_HEADS):
            (_, sink, m), p = logits[g], probs[g]
            v_ext = jnp.concatenate([v_t[g * d:(g + 1) * d], ones], axis=0).astype(BF16)
            pv = jnp.dot(v_ext, p, preferred_element_type=F32)
            inv = 1.0 / (pv[d:d + SUBLANES] + jnp.exp2(sink - m))
            o_t = pv[:d] * jnp.tile(inv, (d // SUBLANES, 1))
            for j in range(rep // 2):
                pair_t = jnp.concatenate([o_t[:, (2 * j) * blk:(2 * j + 1) * blk],
                                          o_t[:, (2 * j + 1) * blk:(2 * j + 2) * blk]], axis=0)
                outs.append(pair_t.T)
        o_ref[r0:r0 + blk, :] = jnp.concatenate(outs, axis=-1).astype(o_ref.dtype)


def sliding_window_attention(sq, sk, sv, sinks, bias, batch, seq, blocks_per_step=8):
    m, qw = sq.shape
    kw = sk.shape[1]
    rows = blocks_per_step * SWA_BLOCK
    nb = seq // rows
    cur = lambda b, i: (b * nb + i, 0)
    prev = lambda b, i: ((b * nb + i) * blocks_per_step - jnp.minimum(i, 1), 0)
    return pl.pallas_call(
        _swa_kernel,
        grid=(batch, nb),
        in_specs=[pl.BlockSpec(memory_space=pltpu.SMEM),
                  pl.BlockSpec((rows, qw), cur),
                  pl.BlockSpec((SWA_BLOCK, kw), prev), pl.BlockSpec((rows, kw), cur),
                  pl.BlockSpec((SWA_BLOCK, kw), prev), pl.BlockSpec((rows, kw), cur),
                  _full_spec(bias.shape)],
        out_specs=pl.BlockSpec((rows, qw), cur),
        out_shape=jax.ShapeDtypeStruct((m, qw), BF16),
        compiler_params=_params(("parallel", "parallel")),
        name="sliding_window_attention",
    )(sinks, sq, sk, sk, sv, sv, bias)


def _split_cols(w, sizes):
    offs = np.cumsum((0,) + tuple(sizes))
    return [w[:, offs[j]:offs[j + 1]] for j in range(len(sizes))]


def even_layer_mixer(x2, mod, g_pre, g_post, w_in, conv_w, conv_b, dt_bias, a_log, d_skip, ssd_norm, lam,
                     diff_norm, w_out, diff_bias, layer_idx, batch, seq):
    d = x2.shape[1]
    d_inner = d
    n_ssd_heads = d_inner // SSD_HEAD_DIM
    conv_ch = d_inner + 2 * SSD_GROUPS * SSD_STATE
    wz, wxbc, wdt, wq, wk, wv = _split_cols(w_in, (d_inner, conv_ch, n_ssd_heads, d, d, d))
    wdt = jnp.pad(wdt, ((0, 0), (0, LANES - n_ssd_heads)))
    wq = wq * (DIFF_HEAD_DIM ** -0.5 * LOG2_E)
    weights = [w.astype(BF16) for w in (wz, wxbc, wdt, wq, wk, wv)]
    z, xbc, dt_raw, q, k, v = norm_proj(x2, mod, g_pre, weights, (BF16, BF16, F32, BF16, BF16, BF16), seq)
    y_ssd = ssd_mixer(z, xbc, dt_raw, conv_w, conv_b, dt_bias, a_log, d_skip, ssd_norm, batch, seq)
    lam_init = 0.8 - 0.6 * math.exp(-0.3 * layer_idx)
    y_diff = diff_attention(q, k, v, diff_bias, lam, diff_norm, batch, seq, lam_init)
    w_out = w_out.astype(BF16)
    return out_proj(x2, mod, g_post, [y_ssd, y_diff], [w_out[:d_inner], w_out[d_inner:]], seq)


def odd_layer_mixer(x2, mod, g_pre, g_post, w_in, sinks, w_out, swa_bias, batch, seq):
    ret_qk = RET_HEADS * RET_QK_DIM
    ret_v = RET_HEADS * RET_V_DIM
    swa_q = SWA_HEADS * SWA_HEAD_DIM
    swa_kv = SWA_KV_HEADS * SWA_HEAD_DIM
    wrq, wrk, wrv, wrg, wsq, wsk, wsv = _split_cols(w_in, (ret_qk, ret_qk, ret_v, ret_v, swa_q, swa_kv, swa_kv))
    wsq = wsq * (SWA_HEAD_DIM ** -0.5 * LOG2_E)
    weights = [w.astype(BF16) for w in (wrq, wrk, wrv, wrg, wsq, wsk, wsv)]
    rq, rk, rv, rg, sq, sk, sv = norm_proj(x2, mod, g_pre, weights, (BF16,) * 7, seq)
    y_ret = retention(rq, rk, rv, rg, batch, seq)
    y_swa = sliding_window_attention(sq, sk, sv, sinks, swa_bias, batch, seq)
    w_out = w_out.astype(BF16)
    return out_proj(x2, mod, g_post, [y_ret, y_swa], [w_out[:ret_v], w_out[ret_v:]], seq)


def kernel(x, c, rel_bias, norm_gains, mod_w, mod_b, mlp_w1, mlp_w2, e_w_in, e_conv_w, e_conv_b, e_dt_bias,
           e_A_log, e_D, e_ssd_norm, e_lambda, e_diff_norm, e_w_out, o_w_in, o_sinks, o_w_out):
    batch, seq, d = x.shape
    depth = norm_gains.shape[0]
    assert seq % (2 * ATTN_BLOCK) == 0 and d % LANES == 0
    mods = modulation(c, mod_w, mod_b.reshape(depth * 2, 3 * d))
    mlp_w1_b, mlp_w2_b = mlp_w1.astype(BF16), mlp_w2.astype(BF16)
    far_bucket = REL_BUCKETS - 1
    diff_bias = bias_tiles(rel_bias, _diff_bucket_idx(ATTN_BLOCK), shift_bucket=far_bucket, scale=LOG2_E)
    diff_bias = diff_bias.reshape(rel_bias.shape[1], 2, ATTN_BLOCK, ATTN_BLOCK)
    swa_bias = bias_tiles(rel_bias, _swa_bucket_idx(), scale=LOG2_E)

    x2 = x.reshape(batch * seq, d)
    for layer in range(depth):
        j = layer // 2
        gains = norm_gains[layer]
        if layer % 2 == 0:
            x2 = even_layer_mixer(x2, mods[2 * layer], gains[0], gains[1], e_w_in[j], e_conv_w[j], e_conv_b[j],
                                  e_dt_bias[j], e_A_log[j], e_D[j], e_ssd_norm[j], e_lambda[j], e_diff_norm[j],
                                  e_w_out[j], diff_bias, layer, batch, seq)
        else:
            x2 = odd_layer_mixer(x2, mods[2 * layer], gains[0], gains[1], o_w_in[j], o_sinks[j], o_w_out[j],
                                 swa_bias, batch, seq)
        x2 = mlp(x2, mods[2 * layer + 1], gains[2], gains[3], mlp_w1_b, mlp_w2_b, layer, seq)
    return x2.reshape(batch, seq, d)
```

```python
import functools
import math

import jax
import jax.numpy as jnp
import numpy as np
from jax import lax
from jax.experimental import pallas as pl
from jax.experimental.pallas import tpu as pltpu

EPS = 1e-6
MASK_VALUE = -1e30
LOG2_E = math.log2(math.e)
LANES = 128
SUBLANES = 8
VMEM_LIMIT = 56 * 1024 * 1024

CHUNK = 128
REL_BUCKETS = 32
REL_MAX_DIST = 128
SSD_HEAD_DIM = 64
SSD_GROUPS = 4
SSD_STATE = 128
SSD_CONV = 4
DIFF_HEAD_DIM = 64
RET_HEADS = 4
RET_QK_DIM = 128
RET_V_DIM = 256
ROPE_BASE = 10000.0
SWA_HEADS = 8
SWA_KV_HEADS = 2
SWA_HEAD_DIM = 64
SWA_BLOCK = 128
ATTN_BLOCK = 512

BF16 = jnp.bfloat16
F32 = jnp.float32
NT_DIMS = (((1,), (1,)), ((), ()))


def _params(semantics):
    return pltpu.CompilerParams(dimension_semantics=semantics, vmem_limit_bytes=VMEM_LIMIT)


def _full_spec(shape):
    return pl.BlockSpec(shape, lambda *_: (0,) * len(shape))


def _silu(x):
    h = 0.5 * x
    return h * jnp.tanh(h) + h


def _split3(x):
    hi = x.astype(BF16)
    r1 = x - hi.astype(F32)
    mid = r1.astype(BF16)
    lo = (r1 - mid.astype(F32)).astype(BF16)
    return hi, mid, lo


def _rms(x, eps=EPS):
    return x * lax.rsqrt(jnp.mean(x * x, axis=-1, keepdims=True) + eps)


def _mod_kernel(c_ref, w_ref, b_ref, o_ref):
    c_act = _silu(c_ref[...])
    o_ref[...] = jnp.dot(c_act, w_ref[...], preferred_element_type=F32,
                         precision=lax.Precision.HIGHEST) + b_ref[...]


def modulation(c, mod_w, mod_b):
    b, d = c.shape
    n = mod_b.shape[0]
    per_layer = mod_w.shape[1]
    rows = SUBLANES
    c_pad = jnp.zeros((rows, d), F32).at[:b].set(c)
    tn = d
    out = pl.pallas_call(
        _mod_kernel,
        grid=(n, 3 * d // tn),
        in_specs=[pl.BlockSpec((rows, d), lambda s, j: (0, 0)),
                  pl.BlockSpec((None, None, d, tn), lambda s, j: (s // per_layer, s % per_layer, 0, j)),
                  pl.BlockSpec((None, 1, tn), lambda s, j: (s, 0, j))],
        out_specs=pl.BlockSpec((None, rows, tn), lambda s, j: (s, 0, j)),
        out_shape=jax.ShapeDtypeStruct((n, rows, 3 * d), F32),
        compiler_params=_params(("parallel", "parallel")),
        name="modulation",
    )(c_pad, mod_w, mod_b.reshape(n, 1, 3 * d))
    return out[:, :b].reshape(n, b, 3, d)


def _t5_bucket_np(dist):
    max_exact = REL_BUCKETS // 2
    logd = np.log(np.maximum(dist, 1).astype(np.float32) / np.float32(max_exact))
    large = max_exact + (logd / np.float32(math.log(REL_MAX_DIST / max_exact))
                         * np.float32(REL_BUCKETS - max_exact)).astype(np.int32)
    large = np.minimum(large, REL_BUCKETS - 1)
    return np.where(dist < max_exact, dist, large).astype(np.int32)


def _bias_kernel(rb_ref, idx_ref, o_ref, *, shift_bucket, scale, block_buckets):
    h = pl.program_id(0)
    shift = rb_ref[shift_bucket, h] if shift_bucket is not None else 0.0

    def value(bucket):
        return (rb_ref[bucket, h] - shift) * scale

    for (r0, c0), buckets in block_buckets:
        window = (slice(r0, r0 + LANES), slice(c0, c0 + LANES))
        idx = idx_ref[window]
        acc = jnp.full(idx.shape, MASK_VALUE if -1 in buckets else value(max(buckets)), F32)
        for bucket in buckets:
            if bucket >= 0 and len(buckets) > 1:
                acc = jnp.where(idx == bucket, value(bucket), acc)
        o_ref[window] = acc


def bias_tiles(rel_bias, bucket_idx, shift_bucket=None, scale=1.0):
    n_heads = rel_bias.shape[1]
    r, c = bucket_idx.shape
    block_buckets = tuple(((r0, c0), tuple(int(b) for b in np.unique(bucket_idx[r0:r0 + LANES, c0:c0 + LANES])))
                          for r0 in range(0, r, LANES) for c0 in range(0, c, LANES))
    return pl.pallas_call(
        functools.partial(_bias_kernel, shift_bucket=shift_bucket, scale=scale, block_buckets=block_buckets),
        grid=(n_heads,),
        in_specs=[pl.BlockSpec(memory_space=pltpu.SMEM),
                  pl.BlockSpec((r, c), lambda h: (0, 0))],
        out_specs=pl.BlockSpec((None, r, c), lambda h: (h, 0, 0)),
        out_shape=jax.ShapeDtypeStruct((n_heads, r, c), F32),
        compiler_params=_params(("parallel",)),
        name="bias_tiles",
    )(rel_bias, jnp.asarray(bucket_idx))


def _diff_bucket_idx(t):
    q = np.arange(t)[None, :]
    k = np.arange(t)[:, None]
    diag = q - k
    idx_diag = np.where(diag >= 0, _t5_bucket_np(np.maximum(diag, 0)), -1)
    idx_prev = _t5_bucket_np(diag + t)
    assert (_t5_bucket_np(diag + 2 * t) == REL_BUCKETS - 1).all()
    return np.concatenate([idx_diag, idx_prev], axis=0).astype(np.int32)


def _swa_bucket_idx():
    q = np.arange(SWA_BLOCK)[None, :]
    k = np.arange(2 * SWA_BLOCK)[:, None] - SWA_BLOCK
    dist = q - k
    valid = (dist >= 0) & (dist < SWA_BLOCK)
    return np.where(valid, _t5_bucket_np(np.maximum(dist, 0)), -1).astype(np.int32)


def _norm_proj_kernel(x_ref, mod_ref, g_ref, *refs):
    n = len(refs) // 2
    w_refs, o_refs = refs[:n], refs[n:]
    mod = mod_ref[...]
    h = _rms(x_ref[...]) * g_ref[...] * (1.0 + mod[1:2]) + mod[0:1]
    hb = h.astype(BF16)
    for w_ref, o_ref in zip(w_refs, o_refs):
        o_ref[...] = jnp.dot(hb, w_ref[...], preferred_element_type=F32).astype(o_ref.dtype)


def norm_proj(x2, mod, gain, weights, out_dtypes, seq, tm=512):
    m, d = x2.shape
    per_batch = seq // tm
    in_specs = [pl.BlockSpec((tm, d), lambda i: (i, 0)),
                pl.BlockSpec((None, 3, d), lambda i: (i // per_batch, 0, 0)),
                _full_spec((1, d))]
    in_specs += [_full_spec(w.shape) for w in weights]
    out_specs = [pl.BlockSpec((tm, w.shape[1]), lambda i: (i, 0)) for w in weights]
    out_shape = [jax.ShapeDtypeStruct((m, w.shape[1]), dt) for w, dt in zip(weights, out_dtypes)]
    return pl.pallas_call(
        _norm_proj_kernel,
        grid=(m // tm,),
        in_specs=in_specs,
        out_specs=out_specs,
        out_shape=out_shape,
        compiler_params=_params(("parallel",)),
        name="norm_proj",
    )(x2, mod, gain.reshape(1, d), *weights)


def _out_proj_kernel(x_ref, mod_ref, g_ref, *refs):
    n = (len(refs) - 1) // 2
    y_refs, w_refs, o_ref = refs[:n], refs[n:2 * n], refs[2 * n]
    acc = None
    for y_ref, w_ref in zip(y_refs, w_refs):
        part = jnp.dot(y_ref[...], w_ref[...], preferred_element_type=F32)
        acc = part if acc is None else acc + part
    gate = mod_ref[...][2:3]
    o_ref[...] = x_ref[...] + gate * (_rms(acc) * g_ref[...])


def out_proj(x2, mod, gain, ys, weights, seq, tm=512):
    m, d = x2.shape
    per_batch = seq // tm
    in_specs = [pl.BlockSpec((tm, d), lambda i: (i, 0)),
                pl.BlockSpec((None, 3, d), lambda i: (i // per_batch, 0, 0)),
                _full_spec((1, d))]
    in_specs += [pl.BlockSpec((tm, y.shape[1]), lambda i: (i, 0)) for y in ys]
    in_specs += [_full_spec(w.shape) for w in weights]
    return pl.pallas_call(
        _out_proj_kernel,
        grid=(m // tm,),
        in_specs=in_specs,
        out_specs=pl.BlockSpec((tm, d), lambda i: (i, 0)),
        out_shape=jax.ShapeDtypeStruct((m, d), F32),
        compiler_params=_params(("parallel",)),
        name="out_proj",
    )(x2, mod, gain.reshape(1, d), *ys, *weights)


def _mlp_kernel(x_ref, mod_ref, g_pre_ref, g_post_ref, w1_ref, w2_ref, o_ref, *, ff_chunk):
    x = x_ref[...]
    mod = mod_ref[...]
    h = _rms(x) * g_pre_ref[...] * (1.0 + mod[1:2]) + mod[0:1]
    hb = h.astype(BF16)
    d_ff = w1_ref.shape[1]
    acc = None
    for c0 in range(0, d_ff, ff_chunk):
        a = jnp.dot(hb, w1_ref[:, c0:c0 + ff_chunk], preferred_element_type=F32)
        a = jnp.square(jnp.maximum(a, 0.0)).astype(BF16)
        part = jnp.dot(a, w2_ref[c0:c0 + ff_chunk, :], preferred_element_type=F32)
        acc = part if acc is None else acc + part
    o_ref[...] = x + mod[2:3] * (_rms(acc) * g_post_ref[...])


def mlp(x2, mod, g_pre, g_post, w1, w2, layer, seq, tm=512, ff_chunk=1024):
    m, d = x2.shape
    per_batch = seq // tm
    return pl.pallas_call(
        functools.partial(_mlp_kernel, ff_chunk=ff_chunk),
        grid=(m // tm,),
        in_specs=[pl.BlockSpec((tm, d), lambda i: (i, 0)),
                  pl.BlockSpec((None, 3, d), lambda i: (i // per_batch, 0, 0)),
                  _full_spec((1, d)), _full_spec((1, d)),
                  pl.BlockSpec((None,) + w1.shape[1:], lambda i: (layer, 0, 0)),
                  pl.BlockSpec((None,) + w2.shape[1:], lambda i: (layer, 0, 0))],
        out_specs=pl.BlockSpec((tm, d), lambda i: (i, 0)),
        out_shape=jax.ShapeDtypeStruct((m, d), F32),
        compiler_params=_params(("parallel",)),
        name="mlp",
    )(x2, mod, g_pre.reshape(1, d), g_post.reshape(1, d), w1, w2)


def _ssd_kernel(z_ref, xbc_ref, dt_ref, cw_ref, cb_ref, dtb_ref, alog_ref, dskip_ref, nw_ref, expand_ref,
                shift_ref, o_ref, conv_buf, state_ref, *, d_inner):
    chunk = CHUNK
    rows_per_step = z_ref.shape[0]
    n_state = SSD_STATE
    gn = SSD_GROUPS * n_state
    pair = 2 * SSD_HEAD_DIM
    heads_per_group = d_inner // SSD_HEAD_DIM // SSD_GROUPS
    halo = conv_buf.shape[0] - rows_per_step

    @pl.when(pl.program_id(1) == 0)
    def _():
        state_ref[...] = jnp.zeros_like(state_ref)
        conv_buf[0:halo, :] = jnp.zeros((halo, conv_buf.shape[1]), BF16)

    conv_buf[halo:, :] = xbc_ref[...]

    def chunk_body(r0):
        rows = slice(r0, r0 + chunk)
        cur = xbc_ref[rows, :]
        shifted = jnp.dot(shift_ref[...], conv_buf[r0:r0 + halo + chunk, :], preferred_element_type=F32)
        acc = cb_ref[...] + cw_ref[SSD_CONV - 1:SSD_CONV, :] * cur.astype(F32)
        for tap in range(SSD_CONV - 1):
            acc = acc + cw_ref[tap:tap + 1, :] * shifted[tap * chunk:(tap + 1) * chunk, :]
        xbc = _silu(acc)
        xs = xbc[:, :d_inner]
        b_all = xbc[:, d_inner:d_inner + gn]
        c_all = xbc[:, d_inner + gn:]

        dt_in = dt_ref[rows, :] + dtb_ref[...]
        dt = jnp.maximum(dt_in, 0.0) + jnp.log1p(jnp.exp(-jnp.abs(dt_in)))
        a = dt * (-jnp.exp(alog_ref[...]))
        row = lax.broadcasted_iota(jnp.int32, (chunk, chunk), 0)
        col = lax.broadcasted_iota(jnp.int32, (chunk, chunk), 1)
        causal = row >= col
        tril = jnp.where(causal, 1.0, 0.0).astype(BF16)
        parts = jnp.dot(tril, jnp.concatenate(_split3(a), axis=1), preferred_element_type=F32)
        a_cs = parts[:, :LANES] + parts[:, LANES:2 * LANES] + parts[:, 2 * LANES:]
        a_cs_t = a_cs.T
        lhs = jnp.concatenate([jnp.concatenate(_split3(dt), axis=1), jnp.concatenate(_split3(a_cs), axis=1)], axis=0)
        expanded = jnp.dot(lhs, expand_ref[...], preferred_element_type=F32)
        dt_e, acs_e = expanded[:chunk], expanded[chunk:]
        a_last_e = acs_e[chunk - 1:chunk, :]
        x_dt = xs * dt_e
        w_state = (x_dt * jnp.exp(a_last_e - acs_e)).astype(BF16)
        out_scale = jnp.exp(acs_e)
        chunk_decay = jnp.exp(a_last_e)
        x_dt_b = x_dt.astype(BF16)
        lane = lax.broadcasted_iota(jnp.int32, (chunk, pair), 1)
        first_head = lane < SSD_HEAD_DIM

        n_heads = d_inner // SSD_HEAD_DIM
        cgs = [c_all[:, g * n_state:(g + 1) * n_state].astype(BF16) for g in range(SSD_GROUPS)]
        bgs = [b_all[:, g * n_state:(g + 1) * n_state] for g in range(SSD_GROUPS)]
        cbs = [lax.dot_general(cg, bg.astype(BF16), NT_DIMS, preferred_element_type=F32)
               for cg, bg in zip(cgs, bgs)]
        bg_ts = [bg.T.astype(BF16) for bg in bgs]
        y_blocks = []
        for p in range(n_heads // 2):
            g = p // (heads_per_group // 2)
            sl = slice(p * pair, (p + 1) * pair)
            xp = x_dt_b[:, sl]
            y_diag = None
            for k in range(2):
                hh = 2 * p + k
                decay = jnp.exp(jnp.where(causal, a_cs[:, hh:hh + 1] - a_cs_t[hh:hh + 1, :], MASK_VALUE))
                m = (cbs[g] * decay).astype(BF16)
                xh = jnp.where(first_head if k == 0 else jnp.logical_not(first_head), xp, jnp.zeros_like(xp))
                part = jnp.dot(m, xh, preferred_element_type=F32)
                y_diag = part if y_diag is None else y_diag + part
            st = state_ref[p]
            y_off = jnp.dot(cgs[g], st.astype(BF16), preferred_element_type=F32) * out_scale[:, sl]
            state_ref[p] = st * chunk_decay[:, sl] + jnp.dot(bg_ts[g], w_state[:, sl], preferred_element_type=F32)
            y_blocks.append(y_diag + y_off)
        y = jnp.concatenate(y_blocks, axis=-1) + xs * dskip_ref[...]
        y = y * _silu(z_ref[rows, :].astype(F32))
        o_ref[rows, :] = (_rms(y) * nw_ref[...]).astype(o_ref.dtype)

    for r0 in range(0, rows_per_step, chunk):
        chunk_body(r0)
    conv_buf[0:halo, :] = conv_buf[rows_per_step:rows_per_step + halo, :]


def ssd_mixer(z, xbc, dt_raw, conv_w, conv_b, dt_bias, a_log, d_skip, norm_w, batch, seq, chunks_per_step=4):
    m, d_inner = z.shape
    conv_ch = xbc.shape[1]
    n_heads = d_inner // SSD_HEAD_DIM
    rows = chunks_per_step * CHUNK
    nc = seq // rows
    pad = LANES - n_heads
    expand = np.zeros((LANES, d_inner), np.float32)
    for h in range(n_heads):
        expand[h, h * SSD_HEAD_DIM:(h + 1) * SSD_HEAD_DIM] = 1.0
    expand3 = jnp.asarray(np.tile(expand, (3, 1)), BF16)
    halo = 2 * SUBLANES
    shift = np.zeros(((SSD_CONV - 1) * CHUNK, halo + CHUNK), np.float32)
    for tap in range(SSD_CONV - 1):
        shift[tap * CHUNK + np.arange(CHUNK), halo - (SSD_CONV - 1) + tap + np.arange(CHUNK)] = 1.0
    shift = jnp.asarray(shift, BF16)
    row_spec = lambda width: pl.BlockSpec((rows, width), lambda b, c: (b * nc + c, 0))
    return pl.pallas_call(
        functools.partial(_ssd_kernel, d_inner=d_inner),
        grid=(batch, nc),
        in_specs=[row_spec(d_inner), row_spec(conv_ch), row_spec(LANES),
                  _full_spec((SSD_CONV, conv_ch)), _full_spec((1, conv_ch)),
                  _full_spec((1, LANES)), _full_spec((1, LANES)),
                  _full_spec((1, d_inner)), _full_spec((1, d_inner)), _full_spec(expand3.shape),
                  _full_spec(shift.shape)],
        out_specs=row_spec(d_inner),
        out_shape=jax.ShapeDtypeStruct((m, d_inner), BF16),
        scratch_shapes=[pltpu.VMEM((halo + rows, conv_ch), BF16),
                        pltpu.VMEM((n_heads // 2, SSD_STATE, 2 * SSD_HEAD_DIM), F32)],
        compiler_params=_params(("parallel", "arbitrary")),
        name="ssd_mixer",
    )(z, xbc, dt_raw, conv_w, conv_b.reshape(1, conv_ch),
      jnp.pad(dt_bias, (0, pad)).reshape(1, LANES), jnp.pad(a_log, (0, pad)).reshape(1, LANES),
      jnp.repeat(d_skip, SSD_HEAD_DIM).reshape(1, d_inner), norm_w.reshape(1, d_inner), expand3, shift)


def _diff_attn_kernel(q_ref, k_ref, v_ref, bias_ref, lam_ref, nw_ref, o_ref,
                      vt_ref, s0_ref, s1_ref, cmax0_ref, cmax1_ref, p_ref, m_ref, alpha_ref, acc_ref,
                      *, lam_init):
    s_refs, cmax_refs = (s0_ref, s1_ref), (cmax0_ref, cmax1_ref)
    t = s0_ref.shape[0]
    tq = q_ref.shape[0]
    head_w = v_ref.shape[1]
    ext_rows = vt_ref.shape[1]
    i = pl.program_id(2)
    q_t = q_ref[...].astype(F32).T.astype(BF16)
    dim = lax.broadcasted_iota(jnp.int32, q_t.shape, 0)
    zero = jnp.zeros_like(q_t)
    qq_t = jnp.concatenate([jnp.where(dim < DIFF_HEAD_DIM, q_t, zero),
                            jnp.where(dim >= DIFF_HEAD_DIM, q_t, zero)], axis=1)

    @pl.when(i == 0)
    def _():
        for j in range(vt_ref.shape[0]):
            vt_ref[j, :head_w, :] = v_ref[j * t:(j + 1) * t, :].astype(F32).T.astype(BF16)
            vt_ref[j, head_w:, :] = jnp.ones((ext_rows - head_w, t), BF16)

    m_ref[...] = jnp.full(m_ref.shape, MASK_VALUE, F32)
    acc_ref[...] = jnp.zeros(acc_ref.shape, F32)

    diag_tile, prev_tile = bias_ref[0], bias_ref[1]
    all_cols = ((0, t), (t, t), (tq, t), (tq + t, t))
    near_cols = (all_cols[1::2], all_cols, all_cols)
    near_tiles = ((diag_tile, diag_tile),
                  (diag_tile, prev_tile, diag_tile, prev_tile),
                  (prev_tile, None, prev_tile, None))
    near_blocks = (2 * i + 1, 2 * i, jnp.maximum(2 * i - 1, 0))

    def raw_logits(block, cols=all_cols):
        kb = k_ref[pl.ds(pl.multiple_of(block * t, t), t), :]
        rhs = qq_t if cols is all_cols else jnp.concatenate([qq_t[:, c0:c0 + w] for c0, w in cols], axis=1)
        return jnp.dot(kb, rhs, preferred_element_type=F32)

    def store_logits(pieces, slot, cols=all_cols):
        for piece, (c0, w) in zip(pieces, cols):
            s_refs[slot][:, c0:c0 + w] = piece
            chains = [piece[0:SUBLANES, :], piece[SUBLANES:2 * SUBLANES, :]]
            for n, r in enumerate(range(2 * SUBLANES, t, SUBLANES)):
                chains[n % 2] = jnp.maximum(chains[n % 2], piece[r:r + SUBLANES, :])
            col_max = jnp.maximum(chains[0], chains[1])
            for shift in (4, 2, 1):
                col_max = jnp.maximum(col_max, pltpu.roll(col_max, shift, 0))
            cmax_refs[slot][:, c0:c0 + w] = col_max

    def is_near(visit):
        return isinstance(visit, int) and visit < len(near_blocks)

    def block_of(visit):
        return near_blocks[visit] if is_near(visit) else visit - len(near_blocks)

    def cols_of(visit):
        return near_cols[visit] if is_near(visit) else all_cols

    def logits_piece(visit, slot, n):
        col = cols_of(visit)[n]
        piece = raw_logits(block_of(visit), (col,))
        tile = near_tiles[visit][n] if is_near(visit) else None
        store_logits([piece if tile is None else piece + tile], slot, (col,))

    def softmax_accumulate(visit, slot, col_ranges=((0, 2 * tq),)):
        s_ref = s_refs[slot]
        vt = vt_ref[block_of(visit)]
        packed = 2 * SUBLANES
        for c0, w in col_ranges:
            c = slice(c0, c0 + w)
            m_old = m_ref[:, c]
            m_new = jnp.maximum(m_old, cmax_refs[slot][:, c])
            m_ref[:, c] = m_new
            alpha_ref[:, c] = jnp.exp2(m_old - m_new)
            m_tile = jnp.concatenate([m_new, m_new], axis=0)
            for r in range(0, t, packed):
                p_ref[r:r + packed, c] = jnp.exp2((s_ref[r:r + packed, c] - m_tile).astype(BF16))
            pv = jnp.dot(vt, p_ref[:, c], preferred_element_type=F32)
            alpha = alpha_ref[:, c]
            for r in range(0, ext_rows, SUBLANES):
                acc_ref[r:r + SUBLANES, c] = alpha * acc_ref[r:r + SUBLANES, c] + pv[r:r + SUBLANES, :]

    def pipeline_step(visit, slot):
        stage1, stage2 = cols_of(visit), cols_of(visit - 1)
        for n in range(max(len(stage1), len(stage2))):
            if n < len(stage1):
                logits_piece(visit, slot, n)
            if n < len(stage2):
                softmax_accumulate(visit - 1, 1 - slot, (stage2[n],))

    for n in range(len(near_cols[0])):
        logits_piece(0, 0, n)
    pipeline_step(1, 1)

    @pl.when(i < pl.num_programs(2))
    def _():
        pipeline_step(2, 0)

    @pl.when(i > 0)
    def _():
        pipeline_step(3, 1)
        last = 2 * i + 1

        def pair(k, _):
            pipeline_step(4 + 2 * k, 0)

            @pl.when(5 + 2 * k <= last)
            def _():
                pipeline_step(5 + 2 * k, 1)

            return 0

        lax.fori_loop(0, i - 1, pair, 0)
        softmax_accumulate(last, 1, all_cols)

    lam = lam_ref[...]
    lam_full = (jnp.exp(jnp.sum(lam[0:1] * lam[1:2], axis=-1, keepdims=True))
                - jnp.exp(jnp.sum(lam[2:3] * lam[3:4], axis=-1, keepdims=True)) + lam_init)
    inv_sum = 1.0 / acc_ref[head_w:head_w + SUBLANES, :]
    out_t = acc_ref[:head_w, :] * jnp.tile(inv_sum, (head_w // SUBLANES, 1))
    out = (out_t[:, :tq] - lam_full * out_t[:, tq:]).T
    o_ref[...] = (_rms(out) * nw_ref[...] * (1.0 - lam_init)).astype(o_ref.dtype)


def diff_attention(q, k, v, bias, lam, subln_w, batch, seq, lam_init):
    m, width = q.shape
    head_w = 2 * DIFF_HEAD_DIM
    n_heads = width // head_w
    t = ATTN_BLOCK
    tq = 2 * t
    nq = seq // tq
    cols = 2 * tq
    ext_rows = head_w + 2 * SUBLANES
    return pl.pallas_call(
        functools.partial(_diff_attn_kernel, lam_init=lam_init),
        grid=(batch, n_heads, nq),
        in_specs=[pl.BlockSpec((tq, head_w), lambda b, h, i: (b * nq + i, h)),
                  pl.BlockSpec((seq, head_w), lambda b, h, i: (b, h)),
                  pl.BlockSpec((seq, head_w), lambda b, h, i: (b, h)),
                  pl.BlockSpec((None, 2, t, t), lambda b, h, i: (h, 0, 0, 0)),
                  _full_spec(lam.shape), _full_spec((1, head_w))],
        out_specs=pl.BlockSpec((tq, head_w), lambda b, h, i: (b * nq + i, h)),
        out_shape=jax.ShapeDtypeStruct((m, width), BF16),
        scratch_shapes=[pltpu.VMEM((seq // t, ext_rows, t), BF16),
                        pltpu.VMEM((t, cols), F32), pltpu.VMEM((t, cols), F32),
                        pltpu.VMEM((SUBLANES, cols), F32), pltpu.VMEM((SUBLANES, cols), F32),
                        pltpu.VMEM((t, cols), BF16),
                        pltpu.VMEM((SUBLANES, cols), F32),
                        pltpu.VMEM((SUBLANES, cols), F32),
                        pltpu.VMEM((ext_rows, cols), F32)],
        compiler_params=_params(("parallel", "parallel", "arbitrary")),
        name="diff_attention",
    )(q, k, v, bias, lam, subln_w.reshape(1, head_w))


def _retention_kernel(q_ref, k_ref, v_ref, g_ref, cos_hi_ref, sin_hi_ref, cos_lo_ref, sin_lo_ref, sign_ref,
                      decay_ref, zeta_ref, xi_ref, o_ref, state_ref, *, chunk_decay):
    @pl.when(pl.program_id(1) == 0)
    def _():
        state_ref[...] = jnp.zeros_like(state_ref)

    c_hi, s_hi = cos_hi_ref[...], sin_hi_ref[...]
    c_lo, s_lo = cos_lo_ref[...], sin_lo_ref[...]
    cos_all = c_hi * c_lo - s_hi * s_lo
    sin_all = (s_hi * c_lo + c_hi * s_lo) * sign_ref[...]

    half = RET_QK_DIM // 2
    states = [state_ref[h] for h in range(RET_HEADS)]
    heads = range(RET_HEADS)
    qk_of = lambda h: slice(h * RET_QK_DIM, (h + 1) * RET_QK_DIM)
    v_of = lambda h: slice(h * RET_V_DIM, (h + 1) * RET_V_DIM)
    for r0 in range(0, q_ref.shape[0], CHUNK):
        rows = slice(r0, r0 + CHUNK)
        cos, sin = cos_all[rows, :], sin_all[rows, :]

        def rope(t):
            return t * cos + pltpu.roll(t, half, 1) * sin

        qr = [rope(q_ref[rows, qk_of(h)].astype(F32)) for h in heads]
        kr = [rope(k_ref[rows, qk_of(h)].astype(F32)) * (RET_QK_DIM ** -0.5) for h in heads]
        scores = [lax.dot_general(qr[h].astype(BF16), kr[h].astype(BF16), NT_DIMS,
                                  preferred_element_type=F32) * decay_ref[h] for h in heads]
        cross = [jnp.dot((qr[h] * xi_ref[h]).astype(BF16), states[h].astype(BF16), preferred_element_type=F32)
                 for h in heads]
        inner = [jnp.dot(scores[h].astype(BF16), v_ref[rows, v_of(h)], preferred_element_type=F32) for h in heads]
        kv = [jnp.dot((kr[h].T * zeta_ref[h]).astype(BF16), v_ref[rows, v_of(h)], preferred_element_type=F32)
              for h in heads]
        for h in heads:
            states[h] = states[h] * chunk_decay[h] + kv[h]
            out = _rms(inner[h] + cross[h]) * _silu(g_ref[rows, v_of(h)].astype(F32))
            o_ref[rows, v_of(h)] = out.astype(o_ref.dtype)
    for h in range(RET_HEADS):
        state_ref[h] = states[h]


def retention(rq, rk, rv, rg, batch, seq, chunks_per_step=8):
    m = rq.shape[0]
    rows = chunks_per_step * CHUNK
    nc = seq // rows
    half = RET_QK_DIM // 2
    inv = ROPE_BASE ** (-jnp.arange(half, dtype=F32) / half)
    both_halves = lambda a: jnp.concatenate([a, a], axis=-1)
    ang_hi = (jnp.arange(nc, dtype=F32) * rows)[:, None] * inv[None]
    ang_lo = jnp.arange(rows, dtype=F32)[:, None] * inv[None]
    cos_hi = both_halves(jnp.cos(ang_hi)).reshape(nc, 1, RET_QK_DIM)
    sin_hi = both_halves(jnp.sin(ang_hi)).reshape(nc, 1, RET_QK_DIM)
    cos_lo, sin_lo = both_halves(jnp.cos(ang_lo)), both_halves(jnp.sin(ang_lo))
    sign = jnp.concatenate([-jnp.ones((1, half), F32), jnp.ones((1, half), F32)], axis=-1)
    log_gamma = jnp.log(1.0 - 2.0 ** (-5.0 - jnp.arange(RET_HEADS, dtype=F32)))
    idx = jnp.arange(CHUNK, dtype=F32)
    rel = idx[:, None] - idx[None, :]
    decay = jnp.where(rel >= 0, jnp.exp(jnp.maximum(rel, 0.0)[None] * log_gamma[:, None, None]), 0.0)
    zeta = jnp.exp((CHUNK - 1 - idx)[None] * log_gamma[:, None])[:, None, :]
    xi = jnp.broadcast_to(jnp.exp((idx + 1.0)[None] * log_gamma[:, None])[:, :, None],
                          (RET_HEADS, CHUNK, RET_QK_DIM))
    gamma = 1.0 - 2.0 ** (-5.0 - np.arange(RET_HEADS, dtype=np.float64))
    chunk_decay = tuple(float(g ** CHUNK) for g in gamma)
    row_spec = lambda width: pl.BlockSpec((rows, width), lambda b, c: (b * nc + c, 0))
    return pl.pallas_call(
        functools.partial(_retention_kernel, chunk_decay=chunk_decay),
        grid=(batch, nc),
        in_specs=[row_spec(rq.shape[1]), row_spec(rk.shape[1]), row_spec(rv.shape[1]), row_spec(rg.shape[1]),
                  pl.BlockSpec((None, 1, RET_QK_DIM), lambda b, c: (c, 0, 0)),
                  pl.BlockSpec((None, 1, RET_QK_DIM), lambda b, c: (c, 0, 0)),
                  _full_spec(cos_lo.shape), _full_spec(sin_lo.shape), _full_spec(sign.shape),
                  _full_spec(decay.shape), _full_spec(zeta.shape), _full_spec(xi.shape)],
        out_specs=row_spec(rv.shape[1]),
        out_shape=jax.ShapeDtypeStruct((m, rv.shape[1]), BF16),
        scratch_shapes=[pltpu.VMEM((RET_HEADS, RET_QK_DIM, RET_V_DIM), F32)],
        compiler_params=_params(("parallel", "arbitrary")),
        name="retention",
    )(rq, rk, rv, rg, cos_hi, sin_hi, cos_lo, sin_lo, sign, decay, zeta, xi)


def _swa_kernel(sink_ref, q_ref, kp_ref, kc_ref, vp_ref, vc_ref, bias_ref, o_ref, *, stage_blocks):
    i = pl.program_id(1)
    blk = SWA_BLOCK
    rep = SWA_HEADS // SWA_KV_HEADS
    d = SWA_HEAD_DIM
    cols = rep * blk
    k_all = jnp.concatenate([kp_ref[...], kc_ref[...]], axis=0)
    v_all = jnp.concatenate([vp_ref[...], vc_ref[...]], axis=0)
    key_row = lax.broadcasted_iota(jnp.int32, (2 * blk, cols), 0)
    head_of_lane = lax.broadcasted_iota(jnp.int32, (SUBLANES, cols), 1) // blk
    zero = jnp.zeros((d, blk), F32)
    ones = jnp.ones((2 * SUBLANES, 2 * blk), F32)
    for base in range(0, q_ref.shape[0], stage_blocks * blk):
        units = [(base + n * blk, g) for n in range(stage_blocks) for g in range(SWA_KV_HEADS)]
        logits = []
        for r0, g in units:
            kb = k_all[r0:r0 + 2 * blk]
            q_t = q_ref[r0:r0 + blk, g * rep * d:(g + 1) * rep * d].astype(F32).T
            rhs = jnp.concatenate(
                [jnp.concatenate([q_t[r * d:(r + 1) * d], zero] if g == 0 else [zero, q_t[r * d:(r + 1) * d]],
                                 axis=0) for r in range(rep)], axis=1).astype(BF16)
            s = jnp.dot(kb, rhs, preferred_element_type=F32)
            s = s + jnp.concatenate([bias_ref[g * rep + r] for r in range(rep)], axis=1)
            if r0 == 0:
                s = jnp.where(jnp.logical_and(key_row < blk, i == 0), MASK_VALUE, s)
            sink = jnp.zeros((SUBLANES, cols), F32)
            for r in range(rep):
                sink = jnp.where(head_of_lane == r, sink_ref[g * rep + r] * LOG2_E, sink)
            chains = [s[0:SUBLANES], s[SUBLANES:2 * SUBLANES]]
            for n, r in enumerate(range(2 * SUBLANES, 2 * blk, SUBLANES)):
                chains[n % 2] = jnp.maximum(chains[n % 2], s[r:r + SUBLANES])
            col_max = jnp.maximum(chains[0], chains[1])
            for shift in (4, 2, 1):
                col_max = jnp.maximum(col_max, pltpu.roll(col_max, shift, 0))
            logits.append((s, sink, jnp.maximum(col_max, sink)))
        probs = [jnp.exp2((s - jnp.tile(m, (2 * blk // SUBLANES, 1))).astype(BF16)) for s, _, m in logits]
        outs = []
        v_ts = {base + n * blk: v_all[base + n * blk:base + (n + 2) * blk].astype(F32).T
                for n in range(stage_blocks)}
        for (r0, g), (_, sink, m), p in zip(units, logits, probs):
            v_ext = jnp.concatenate([v_ts[r0][g * d:(g + 1) * d], ones], axis=0).astype(BF16)
            pv = jnp.dot(v_ext, p, preferred_element_type=F32)
            inv = 1.0 / (pv[d:d + SUBLANES] + jnp.exp2(sink - m))
            o_t = pv[:d] * jnp.tile(inv, (d // SUBLANES, 1))
            for j in range(rep // 2):
                pair_t = jnp.concatenate([o_t[:, (2 * j) * blk:(2 * j + 1) * blk],
                                          o_t[:, (2 * j + 1) * blk:(2 * j + 2) * blk]], axis=0)
                outs.append(pair_t.T)
        for n in range(stage_blocks):
            per_block = len(outs) // stage_blocks
            o_ref[base + n * blk:base + (n + 1) * blk, :] = jnp.concatenate(
                outs[n * per_block:(n + 1) * per_block], axis=-1).astype(o_ref.dtype)


def sliding_window_attention(sq, sk, sv, sinks, bias, batch, seq, blocks_per_step=8):
    m, qw = sq.shape
    kw = sk.shape[1]
    rows = blocks_per_step * SWA_BLOCK
    nb = seq // rows
    cur = lambda b, i: (b * nb + i, 0)
    prev = lambda b, i: ((b * nb + i) * blocks_per_step - jnp.minimum(i, 1), 0)
    return pl.pallas_call(
        functools.partial(_swa_kernel, stage_blocks=4),
        grid=(batch, nb),
        in_specs=[pl.BlockSpec(memory_space=pltpu.SMEM),
                  pl.BlockSpec((rows, qw), cur),
                  pl.BlockSpec((SWA_BLOCK, kw), prev), pl.BlockSpec((rows, kw), cur),
                  pl.BlockSpec((SWA_BLOCK, kw), prev), pl.BlockSpec((rows, kw), cur),
                  _full_spec(bias.shape)],
        out_specs=pl.BlockSpec((rows, qw), cur),
        out_shape=jax.ShapeDtypeStruct((m, qw), BF16),
        compiler_params=_params(("parallel", "parallel")),
        name="sliding_window_attention",
    )(sinks, sq, sk, sk, sv, sv, bias)


def _split_cols(w, sizes):
    offs = np.cumsum((0,) + tuple(sizes))
    return [w[:, offs[j]:offs[j + 1]] for j in range(len(sizes))]


def even_layer_mixer(x2, mod, g_pre, g_post, w_in, conv_w, conv_b, dt_bias, a_log, d_skip, ssd_norm, lam,
                     diff_norm, w_out, diff_bias, layer_idx, batch, seq):
    d = x2.shape[1]
    d_inner = d
    n_ssd_heads = d_inner // SSD_HEAD_DIM
    conv_ch = d_inner + 2 * SSD_GROUPS * SSD_STATE
    wz, wxbc, wdt, wq, wk, wv = _split_cols(w_in, (d_inner, conv_ch, n_ssd_heads, d, d, d))
    wdt = jnp.pad(wdt, ((0, 0), (0, LANES - n_ssd_heads)))
    wq = wq * (DIFF_HEAD_DIM ** -0.5 * LOG2_E)
    weights = [w.astype(BF16) for w in (wz, wxbc, wdt, wq, wk, wv)]
    z, xbc, dt_raw, q, k, v = norm_proj(x2, mod, g_pre, weights, (BF16, BF16, F32, BF16, BF16, BF16), seq)
    y_ssd = ssd_mixer(z, xbc, dt_raw, conv_w, conv_b, dt_bias, a_log, d_skip, ssd_norm, batch, seq)
    lam_init = 0.8 - 0.6 * math.exp(-0.3 * layer_idx)
    y_diff = diff_attention(q, k, v, diff_bias, lam, diff_norm, batch, seq, lam_init)
    w_out = w_out.astype(BF16)
    return out_proj(x2, mod, g_post, [y_ssd, y_diff], [w_out[:d_inner], w_out[d_inner:]], seq)


def odd_layer_mixer(x2, mod, g_pre, g_post, w_in, sinks, w_out, swa_bias, batch, seq):
    ret_qk = RET_HEADS * RET_QK_DIM
    ret_v = RET_HEADS * RET_V_DIM
    swa_q = SWA_HEADS * SWA_HEAD_DIM
    swa_kv = SWA_KV_HEADS * SWA_HEAD_DIM
    wrq, wrk, wrv, wrg, wsq, wsk, wsv = _split_cols(w_in, (ret_qk, ret_qk, ret_v, ret_v, swa_q, swa_kv, swa_kv))
    wsq = wsq * (SWA_HEAD_DIM ** -0.5 * LOG2_E)
    weights = [w.astype(BF16) for w in (wrq, wrk, wrv, wrg, wsq, wsk, wsv)]
    rq, rk, rv, rg, sq, sk, sv = norm_proj(x2, mod, g_pre, weights, (BF16,) * 7, seq)
    y_ret = retention(rq, rk, rv, rg, batch, seq)
    y_swa = sliding_window_attention(sq, sk, sv, sinks, swa_bias, batch, seq)
    w_out = w_out.astype(BF16)
    return out_proj(x2, mod, g_post, [y_ret, y_swa], [w_out[:ret_v], w_out[ret_v:]], seq)


def kernel(x, c, rel_bias, norm_gains, mod_w, mod_b, mlp_w1, mlp_w2, e_w_in, e_conv_w, e_conv_b, e_dt_bias,
           e_A_log, e_D, e_ssd_norm, e_lambda, e_diff_norm, e_w_out, o_w_in, o_sinks, o_w_out):
    batch, seq, d = x.shape
    depth = norm_gains.shape[0]
    assert seq % (2 * ATTN_BLOCK) == 0 and d % LANES == 0
    mods = modulation(c, mod_w, mod_b.reshape(depth * 2, 3 * d))
    mlp_w1_b, mlp_w2_b = mlp_w1.astype(BF16), mlp_w2.astype(BF16)
    far_bucket = REL_BUCKETS - 1
    diff_bias = bias_tiles(rel_bias, _diff_bucket_idx(ATTN_BLOCK), shift_bucket=far_bucket, scale=LOG2_E)
    diff_bias = diff_bias.reshape(rel_bias.shape[1], 2, ATTN_BLOCK, ATTN_BLOCK)
    swa_bias = bias_tiles(rel_bias, _swa_bucket_idx(), scale=LOG2_E)

    x2 = x.reshape(batch * seq, d)
    for layer in range(depth):
        j = layer // 2
        gains = norm_gains[layer]
        if layer % 2 == 0:
            x2 = even_layer_mixer(x2, mods[2 * layer], gains[0], gains[1], e_w_in[j], e_conv_w[j], e_conv_b[j],
                                  e_dt_bias[j], e_A_log[j], e_D[j], e_ssd_norm[j], e_lambda[j], e_diff_norm[j],
                                  e_w_out[j], diff_bias, layer, batch, seq)
        else:
            x2 = odd_layer_mixer(x2, mods[2 * layer], gains[0], gains[1], o_w_in[j], o_sinks[j], o_w_out[j],
                                 swa_bias, batch, seq)
        x2 = mlp(x2, mods[2 * layer + 1], gains[2], gains[3], mlp_w1_b, mlp_w2_b, layer, seq)
    return x2.reshape(batch, seq, d)
```

```python
import functools
import math

import jax
import jax.numpy as jnp
import numpy as np
from jax import lax
from jax.experimental import pallas as pl
from jax.experimental.pallas import tpu as pltpu

EPS = 1e-6
MASK_VALUE = -1e30
LOG2_E = math.log2(math.e)
LANES = 128
SUBLANES = 8
VMEM_LIMIT = 56 * 1024 * 1024

CHUNK = 128
REL_BUCKETS = 32
REL_MAX_DIST = 128
SSD_HEAD_DIM = 64
SSD_GROUPS = 4
SSD_STATE = 128
SSD_CONV = 4
DIFF_HEAD_DIM = 64
RET_HEADS = 4
RET_QK_DIM = 128
RET_V_DIM = 256
ROPE_BASE = 10000.0
SWA_HEADS = 8
SWA_KV_HEADS = 2
SWA_HEAD_DIM = 64
SWA_BLOCK = 128
ATTN_BLOCK = 512

BF16 = jnp.bfloat16
F32 = jnp.float32
NT_DIMS = (((1,), (1,)), ((), ()))


def _params(semantics):
    return pltpu.CompilerParams(dimension_semantics=semantics, vmem_limit_bytes=VMEM_LIMIT)


def _full_spec(shape):
    return pl.BlockSpec(shape, lambda *_: (0,) * len(shape))


def _silu(x):
    h = 0.5 * x
    return h * jnp.tanh(h) + h


def _split3(x):
    hi = x.astype(BF16)
    r1 = x - hi.astype(F32)
    mid = r1.astype(BF16)
    lo = (r1 - mid.astype(F32)).astype(BF16)
    return hi, mid, lo


def _rms(x, eps=EPS):
    return x * lax.rsqrt(jnp.mean(x * x, axis=-1, keepdims=True) + eps)


def _mod_kernel(c_ref, w_ref, b_ref, o_ref):
    c_act = _silu(c_ref[...])
    o_ref[...] = jnp.dot(c_act, w_ref[...], preferred_element_type=F32,
                         precision=lax.Precision.HIGHEST) + b_ref[...]


def modulation(c, mod_w, mod_b):
    b, d = c.shape
    n = mod_b.shape[0]
    per_layer = mod_w.shape[1]
    rows = SUBLANES
    c_pad = jnp.zeros((rows, d), F32).at[:b].set(c)
    tn = d
    out = pl.pallas_call(
        _mod_kernel,
        grid=(n, 3 * d // tn),
        in_specs=[pl.BlockSpec((rows, d), lambda s, j: (0, 0)),
                  pl.BlockSpec((None, None, d, tn), lambda s, j: (s // per_layer, s % per_layer, 0, j)),
                  pl.BlockSpec((None, 1, tn), lambda s, j: (s, 0, j))],
        out_specs=pl.BlockSpec((None, rows, tn), lambda s, j: (s, 0, j)),
        out_shape=jax.ShapeDtypeStruct((n, rows, 3 * d), F32),
        compiler_params=_params(("parallel", "parallel")),
        name="modulation",
    )(c_pad, mod_w, mod_b.reshape(n, 1, 3 * d))
    return out[:, :b].reshape(n, b, 3, d)


def _t5_bucket_np(dist):
    max_exact = REL_BUCKETS // 2
    logd = np.log(np.maximum(dist, 1).astype(np.float32) / np.float32(max_exact))
    large = max_exact + (logd / np.float32(math.log(REL_MAX_DIST / max_exact))
                         * np.float32(REL_BUCKETS - max_exact)).astype(np.int32)
    large = np.minimum(large, REL_BUCKETS - 1)
    return np.where(dist < max_exact, dist, large).astype(np.int32)


def _bias_kernel(rb_ref, idx_ref, o_ref, *, shift_bucket, scale, block_buckets):
    h = pl.program_id(0)
    shift = rb_ref[shift_bucket, h] if shift_bucket is not None else 0.0

    def value(bucket):
        return (rb_ref[bucket, h] - shift) * scale

    for (r0, c0), buckets in block_buckets:
        window = (slice(r0, r0 + LANES), slice(c0, c0 + LANES))
        idx = idx_ref[window]
        acc = jnp.full(idx.shape, MASK_VALUE if -1 in buckets else value(max(buckets)), F32)
        for bucket in buckets:
            if bucket >= 0 and len(buckets) > 1:
                acc = jnp.where(idx == bucket, value(bucket), acc)
        o_ref[window] = acc


def bias_tiles(rel_bias, bucket_idx, shift_bucket=None, scale=1.0):
    n_heads = rel_bias.shape[1]
    r, c = bucket_idx.shape
    block_buckets = tuple(((r0, c0), tuple(int(b) for b in np.unique(bucket_idx[r0:r0 + LANES, c0:c0 + LANES])))
                          for r0 in range(0, r, LANES) for c0 in range(0, c, LANES))
    return pl.pallas_call(
        functools.partial(_bias_kernel, shift_bucket=shift_bucket, scale=scale, block_buckets=block_buckets),
        grid=(n_heads,),
        in_specs=[pl.BlockSpec(memory_space=pltpu.SMEM),
                  pl.BlockSpec((r, c), lambda h: (0, 0))],
        out_specs=pl.BlockSpec((None, r, c), lambda h: (h, 0, 0)),
        out_shape=jax.ShapeDtypeStruct((n_heads, r, c), F32),
        compiler_params=_params(("parallel",)),
        name="bias_tiles",
    )(rel_bias, jnp.asarray(bucket_idx))


def _diff_bucket_idx(t):
    q = np.arange(t)[None, :]
    k = np.arange(t)[:, None]
    diag = q - k
    idx_diag = np.where(diag >= 0, _t5_bucket_np(np.maximum(diag, 0)), -1)
    idx_prev = _t5_bucket_np(diag + t)
    assert (_t5_bucket_np(diag + 2 * t) == REL_BUCKETS - 1).all()
    return np.concatenate([idx_diag, idx_prev], axis=0).astype(np.int32)


def _swa_bucket_idx():
    q = np.arange(SWA_BLOCK)[None, :]
    k = np.arange(2 * SWA_BLOCK)[:, None] - SWA_BLOCK
    dist = q - k
    valid = (dist >= 0) & (dist < SWA_BLOCK)
    return np.where(valid, _t5_bucket_np(np.maximum(dist, 0)), -1).astype(np.int32)


def _norm_proj_kernel(x_ref, mod_ref, g_ref, *refs, transposed):
    n = len(refs) // 2
    w_refs, o_refs = refs[:n], refs[n:]
    mod = mod_ref[...]
    h = _rms(x_ref[...]) * g_ref[...] * (1.0 + mod[1:2]) + mod[0:1]
    hb = h.astype(BF16)
    for w_ref, o_ref, flip in zip(w_refs, o_refs, transposed):
        if flip:
            out = lax.dot_general(w_ref[...], hb, NT_DIMS, preferred_element_type=F32)
        else:
            out = jnp.dot(hb, w_ref[...], preferred_element_type=F32)
        o_ref[...] = out.astype(o_ref.dtype)


def norm_proj(x2, mod, gain, weights, out_dtypes, seq, transposed=None, tm=512):
    m, d = x2.shape
    per_batch = seq // tm
    transposed = tuple(transposed or (False,) * len(weights))
    in_specs = [pl.BlockSpec((tm, d), lambda i: (i, 0)),
                pl.BlockSpec((None, 3, d), lambda i: (i // per_batch, 0, 0)),
                _full_spec((1, d))]
    in_specs += [_full_spec(w.shape) for w in weights]
    out_specs = [pl.BlockSpec((w.shape[0], tm), lambda i: (0, i)) if flip
                 else pl.BlockSpec((tm, w.shape[1]), lambda i: (i, 0)) for w, flip in zip(weights, transposed)]
    out_shape = [jax.ShapeDtypeStruct((w.shape[0], m) if flip else (m, w.shape[1]), dt)
                 for w, dt, flip in zip(weights, out_dtypes, transposed)]
    return pl.pallas_call(
        functools.partial(_norm_proj_kernel, transposed=transposed),
        grid=(m // tm,),
        in_specs=in_specs,
        out_specs=out_specs,
        out_shape=out_shape,
        compiler_params=_params(("parallel",)),
        name="norm_proj",
    )(x2, mod, gain.reshape(1, d), *weights)


def _out_proj_kernel(x_ref, mod_ref, g_ref, *refs):
    n = (len(refs) - 1) // 2
    y_refs, w_refs, o_ref = refs[:n], refs[n:2 * n], refs[2 * n]
    acc = None
    for y_ref, w_ref in zip(y_refs, w_refs):
        part = jnp.dot(y_ref[...], w_ref[...], preferred_element_type=F32)
        acc = part if acc is None else acc + part
    gate = mod_ref[...][2:3]
    o_ref[...] = x_ref[...] + gate * (_rms(acc) * g_ref[...])


def out_proj(x2, mod, gain, ys, weights, seq, tm=512):
    m, d = x2.shape
    per_batch = seq // tm
    in_specs = [pl.BlockSpec((tm, d), lambda i: (i, 0)),
                pl.BlockSpec((None, 3, d), lambda i: (i // per_batch, 0, 0)),
                _full_spec((1, d))]
    in_specs += [pl.BlockSpec((tm, y.shape[1]), lambda i: (i, 0)) for y in ys]
    in_specs += [_full_spec(w.shape) for w in weights]
    return pl.pallas_call(
        _out_proj_kernel,
        grid=(m // tm,),
        in_specs=in_specs,
        out_specs=pl.BlockSpec((tm, d), lambda i: (i, 0)),
        out_shape=jax.ShapeDtypeStruct((m, d), F32),
        compiler_params=_params(("parallel",)),
        name="out_proj",
    )(x2, mod, gain.reshape(1, d), *ys, *weights)


def _mlp_kernel(x_ref, mod_ref, g_pre_ref, g_post_ref, w1_ref, w2_ref, o_ref, *, ff_chunk):
    x = x_ref[...]
    mod = mod_ref[...]
    h = _rms(x) * g_pre_ref[...] * (1.0 + mod[1:2]) + mod[0:1]
    hb = h.astype(BF16)
    d_ff = w1_ref.shape[1]
    acc = None
    for c0 in range(0, d_ff, ff_chunk):
        a = jnp.dot(hb, w1_ref[:, c0:c0 + ff_chunk], preferred_element_type=F32)
        a = jnp.square(jnp.maximum(a, 0.0)).astype(BF16)
        part = jnp.dot(a, w2_ref[c0:c0 + ff_chunk, :], preferred_element_type=F32)
        acc = part if acc is None else acc + part
    o_ref[...] = x + mod[2:3] * (_rms(acc) * g_post_ref[...])


def mlp(x2, mod, g_pre, g_post, w1, w2, layer, seq, tm=512, ff_chunk=1024):
    m, d = x2.shape
    per_batch = seq // tm
    return pl.pallas_call(
        functools.partial(_mlp_kernel, ff_chunk=ff_chunk),
        grid=(m // tm,),
        in_specs=[pl.BlockSpec((tm, d), lambda i: (i, 0)),
                  pl.BlockSpec((None, 3, d), lambda i: (i // per_batch, 0, 0)),
                  _full_spec((1, d)), _full_spec((1, d)),
                  pl.BlockSpec((None,) + w1.shape[1:], lambda i: (layer, 0, 0)),
                  pl.BlockSpec((None,) + w2.shape[1:], lambda i: (layer, 0, 0))],
        out_specs=pl.BlockSpec((tm, d), lambda i: (i, 0)),
        out_shape=jax.ShapeDtypeStruct((m, d), F32),
        compiler_params=_params(("parallel",)),
        name="mlp",
    )(x2, mod, g_pre.reshape(1, d), g_post.reshape(1, d), w1, w2)


def _ssd_kernel(z_ref, xbc_ref, dt_ref, cw_ref, cb_ref, dtb_ref, alog_ref, dskip_ref, nw_ref, expand_ref,
                shift_ref, o_ref, conv_buf, state_ref, *, d_inner):
    chunk = CHUNK
    rows_per_step = z_ref.shape[0]
    n_state = SSD_STATE
    gn = SSD_GROUPS * n_state
    pair = 2 * SSD_HEAD_DIM
    heads_per_group = d_inner // SSD_HEAD_DIM // SSD_GROUPS
    halo = conv_buf.shape[0] - rows_per_step

    @pl.when(pl.program_id(1) == 0)
    def _():
        state_ref[...] = jnp.zeros_like(state_ref)
        conv_buf[0:halo, :] = jnp.zeros((halo, conv_buf.shape[1]), BF16)

    conv_buf[halo:, :] = xbc_ref[...]

    def chunk_body(r0):
        rows = slice(r0, r0 + chunk)
        cur = xbc_ref[rows, :]
        shifted = jnp.dot(shift_ref[...], conv_buf[r0:r0 + halo + chunk, :], preferred_element_type=F32)
        acc = cb_ref[...] + cw_ref[SSD_CONV - 1:SSD_CONV, :] * cur.astype(F32)
        for tap in range(SSD_CONV - 1):
            acc = acc + cw_ref[tap:tap + 1, :] * shifted[tap * chunk:(tap + 1) * chunk, :]
        xbc = _silu(acc)
        xs = xbc[:, :d_inner]
        b_all = xbc[:, d_inner:d_inner + gn]
        c_all = xbc[:, d_inner + gn:]

        dt_in = dt_ref[rows, :] + dtb_ref[...]
        dt = jnp.maximum(dt_in, 0.0) + jnp.log1p(jnp.exp(-jnp.abs(dt_in)))
        a = dt * (-jnp.exp(alog_ref[...]))
        row = lax.broadcasted_iota(jnp.int32, (chunk, chunk), 0)
        col = lax.broadcasted_iota(jnp.int32, (chunk, chunk), 1)
        causal = row >= col
        tril = jnp.where(causal, 1.0, 0.0).astype(BF16)
        parts = jnp.dot(tril, jnp.concatenate(_split3(a), axis=1), preferred_element_type=F32)
        a_cs = parts[:, :LANES] + parts[:, LANES:2 * LANES] + parts[:, 2 * LANES:]
        a_cs_t = a_cs.T
        lhs = jnp.concatenate([jnp.concatenate(_split3(dt), axis=1), jnp.concatenate(_split3(a_cs), axis=1)], axis=0)
        expanded = jnp.dot(lhs, expand_ref[...], preferred_element_type=F32)
        dt_e, acs_e = expanded[:chunk], expanded[chunk:]
        a_last_e = acs_e[chunk - 1:chunk, :]
        x_dt = xs * dt_e
        w_state = (x_dt * jnp.exp(a_last_e - acs_e)).astype(BF16)
        out_scale = jnp.exp(acs_e)
        chunk_decay = jnp.exp(a_last_e)
        x_dt_b = x_dt.astype(BF16)
        lane = lax.broadcasted_iota(jnp.int32, (chunk, pair), 1)
        first_head = lane < SSD_HEAD_DIM

        n_heads = d_inner // SSD_HEAD_DIM
        cgs = [c_all[:, g * n_state:(g + 1) * n_state].astype(BF16) for g in range(SSD_GROUPS)]
        bgs = [b_all[:, g * n_state:(g + 1) * n_state] for g in range(SSD_GROUPS)]
        cbs = [lax.dot_general(cg, bg.astype(BF16), NT_DIMS, preferred_element_type=F32)
               for cg, bg in zip(cgs, bgs)]
        bg_ts = [bg.T.astype(BF16) for bg in bgs]
        y_blocks = []
        for p in range(n_heads // 2):
            g = p // (heads_per_group // 2)
            sl = slice(p * pair, (p + 1) * pair)
            xp = x_dt_b[:, sl]
            y_diag = None
            for k in range(2):
                hh = 2 * p + k
                decay = jnp.exp(jnp.where(causal, a_cs[:, hh:hh + 1] - a_cs_t[hh:hh + 1, :], MASK_VALUE))
                m = (cbs[g] * decay).astype(BF16)
                xh = jnp.where(first_head if k == 0 else jnp.logical_not(first_head), xp, jnp.zeros_like(xp))
                part = jnp.dot(m, xh, preferred_element_type=F32)
                y_diag = part if y_diag is None else y_diag + part
            st = state_ref[p]
            y_off = jnp.dot(cgs[g], st.astype(BF16), preferred_element_type=F32) * out_scale[:, sl]
            state_ref[p] = st * chunk_decay[:, sl] + jnp.dot(bg_ts[g], w_state[:, sl], preferred_element_type=F32)
            y_blocks.append(y_diag + y_off)
        y = jnp.concatenate(y_blocks, axis=-1) + xs * dskip_ref[...]
        y = y * _silu(z_ref[rows, :].astype(F32))
        o_ref[rows, :] = (_rms(y) * nw_ref[...]).astype(o_ref.dtype)

    for r0 in range(0, rows_per_step, chunk):
        chunk_body(r0)
    conv_buf[0:halo, :] = conv_buf[rows_per_step:rows_per_step + halo, :]


def ssd_mixer(z, xbc, dt_raw, conv_w, conv_b, dt_bias, a_log, d_skip, norm_w, batch, seq, chunks_per_step=4):
    m, d_inner = z.shape
    conv_ch = xbc.shape[1]
    n_heads = d_inner // SSD_HEAD_DIM
    rows = chunks_per_step * CHUNK
    nc = seq // rows
    pad = LANES - n_heads
    expand = np.zeros((LANES, d_inner), np.float32)
    for h in range(n_heads):
        expand[h, h * SSD_HEAD_DIM:(h + 1) * SSD_HEAD_DIM] = 1.0
    expand3 = jnp.asarray(np.tile(expand, (3, 1)), BF16)
    halo = 2 * SUBLANES
    shift = np.zeros(((SSD_CONV - 1) * CHUNK, halo + CHUNK), np.float32)
    for tap in range(SSD_CONV - 1):
        shift[tap * CHUNK + np.arange(CHUNK), halo - (SSD_CONV - 1) + tap + np.arange(CHUNK)] = 1.0
    shift = jnp.asarray(shift, BF16)
    row_spec = lambda width: pl.BlockSpec((rows, width), lambda b, c: (b * nc + c, 0))
    return pl.pallas_call(
        functools.partial(_ssd_kernel, d_inner=d_inner),
        grid=(batch, nc),
        in_specs=[row_spec(d_inner), row_spec(conv_ch), row_spec(LANES),
                  _full_spec((SSD_CONV, conv_ch)), _full_spec((1, conv_ch)),
                  _full_spec((1, LANES)), _full_spec((1, LANES)),
                  _full_spec((1, d_inner)), _full_spec((1, d_inner)), _full_spec(expand3.shape),
                  _full_spec(shift.shape)],
        out_specs=row_spec(d_inner),
        out_shape=jax.ShapeDtypeStruct((m, d_inner), BF16),
        scratch_shapes=[pltpu.VMEM((halo + rows, conv_ch), BF16),
                        pltpu.VMEM((n_heads // 2, SSD_STATE, 2 * SSD_HEAD_DIM), F32)],
        compiler_params=_params(("parallel", "arbitrary")),
        name="ssd_mixer",
    )(z, xbc, dt_raw, conv_w, conv_b.reshape(1, conv_ch),
      jnp.pad(dt_bias, (0, pad)).reshape(1, LANES), jnp.pad(a_log, (0, pad)).reshape(1, LANES),
      jnp.repeat(d_skip, SSD_HEAD_DIM).reshape(1, d_inner), norm_w.reshape(1, d_inner), expand3, shift)


def _diff_attn_kernel(q_ref, k_ref, v_ref, bias_ref, lam_ref, nw_ref, o_ref,
                      vt_ref, s0_ref, s1_ref, cmax0_ref, cmax1_ref, p_ref, m_ref, alpha_ref, acc_ref,
                      *, lam_init):
    s_refs, cmax_refs = (s0_ref, s1_ref), (cmax0_ref, cmax1_ref)
    t = s0_ref.shape[0]
    tq = q_ref.shape[1]
    head_w = v_ref.shape[1]
    ext_rows = vt_ref.shape[1]
    i = pl.program_id(2)
    q_t = q_ref[...]
    dim = lax.broadcasted_iota(jnp.int32, q_t.shape, 0)
    zero = jnp.zeros_like(q_t)
    qq_t = jnp.concatenate([jnp.where(dim < DIFF_HEAD_DIM, q_t, zero),
                            jnp.where(dim >= DIFF_HEAD_DIM, q_t, zero)], axis=1)

    @pl.when(i == 0)
    def _():
        for j in range(vt_ref.shape[0]):
            vt_ref[j, :head_w, :] = v_ref[j * t:(j + 1) * t, :].astype(F32).T.astype(BF16)
            vt_ref[j, head_w:, :] = jnp.ones((ext_rows - head_w, t), BF16)

    m_ref[...] = jnp.full(m_ref.shape, MASK_VALUE, F32)
    acc_ref[...] = jnp.zeros(acc_ref.shape, F32)

    diag_tile, prev_tile = bias_ref[0], bias_ref[1]
    all_cols = ((0, t), (t, t), (tq, t), (tq + t, t))
    near_cols = (all_cols[1::2], all_cols, all_cols)
    near_tiles = ((diag_tile, diag_tile),
                  (diag_tile, prev_tile, diag_tile, prev_tile),
                  (prev_tile, None, prev_tile, None))
    near_blocks = (2 * i + 1, 2 * i, jnp.maximum(2 * i - 1, 0))

    def raw_logits(block, cols=all_cols):
        kb = k_ref[pl.ds(pl.multiple_of(block * t, t), t), :]
        rhs = qq_t if cols is all_cols else jnp.concatenate([qq_t[:, c0:c0 + w] for c0, w in cols], axis=1)
        return jnp.dot(kb, rhs, preferred_element_type=F32)

    def store_logits(pieces, slot, cols=all_cols):
        for piece, (c0, w) in zip(pieces, cols):
            s_refs[slot][:, c0:c0 + w] = piece
            chains = [piece[0:SUBLANES, :], piece[SUBLANES:2 * SUBLANES, :]]
            for n, r in enumerate(range(2 * SUBLANES, t, SUBLANES)):
                chains[n % 2] = jnp.maximum(chains[n % 2], piece[r:r + SUBLANES, :])
            col_max = jnp.maximum(chains[0], chains[1])
            for shift in (4, 2, 1):
                col_max = jnp.maximum(col_max, pltpu.roll(col_max, shift, 0))
            cmax_refs[slot][:, c0:c0 + w] = col_max

    def is_near(visit):
        return isinstance(visit, int) and visit < len(near_blocks)

    def block_of(visit):
        return near_blocks[visit] if is_near(visit) else visit - len(near_blocks)

    def cols_of(visit):
        return near_cols[visit] if is_near(visit) else all_cols

    def logits_piece(visit, slot, n):
        col = cols_of(visit)[n]
        piece = raw_logits(block_of(visit), (col,))
        tile = near_tiles[visit][n] if is_near(visit) else None
        store_logits([piece if tile is None else piece + tile], slot, (col,))

    def softmax_accumulate(visit, slot, col_ranges=((0, 2 * tq),)):
        s_ref = s_refs[slot]
        vt = vt_ref[block_of(visit)]
        packed = 2 * SUBLANES
        for c0, w in col_ranges:
            c = slice(c0, c0 + w)
            m_old = m_ref[:, c]
            m_new = jnp.maximum(m_old, cmax_refs[slot][:, c])
            m_ref[:, c] = m_new
            alpha_ref[:, c] = jnp.exp2(m_old - m_new)
            m_tile = jnp.concatenate([m_new, m_new], axis=0)
            for r in range(0, t, packed):
                p_ref[r:r + packed, c] = jnp.exp2((s_ref[r:r + packed, c] - m_tile).astype(BF16))
            pv = jnp.dot(vt, p_ref[:, c], preferred_element_type=F32)
            alpha = alpha_ref[:, c]
            for r in range(0, ext_rows, SUBLANES):
                acc_ref[r:r + SUBLANES, c] = alpha * acc_ref[r:r + SUBLANES, c] + pv[r:r + SUBLANES, :]

    def pipeline_step(visit, slot):
        stage1, stage2 = cols_of(visit), cols_of(visit - 1)
        for n in range(max(len(stage1), len(stage2))):
            if n < len(stage1):
                logits_piece(visit, slot, n)
            if n < len(stage2):
                softmax_accumulate(visit - 1, 1 - slot, (stage2[n],))

    for n in range(len(near_cols[0])):
        logits_piece(0, 0, n)
    pipeline_step(1, 1)

    @pl.when(i < pl.num_programs(2))
    def _():
        pipeline_step(2, 0)

    @pl.when(i > 0)
    def _():
        pipeline_step(3, 1)
        last = 2 * i + 1

        def pair(k, _):
            pipeline_step(4 + 2 * k, 0)

            @pl.when(5 + 2 * k <= last)
            def _():
                pipeline_step(5 + 2 * k, 1)

            return 0

        lax.fori_loop(0, i - 1, pair, 0)
        softmax_accumulate(last, 1, all_cols)

    lam = lam_ref[...]
    lam_full = (jnp.exp(jnp.sum(lam[0:1] * lam[1:2], axis=-1, keepdims=True))
                - jnp.exp(jnp.sum(lam[2:3] * lam[3:4], axis=-1, keepdims=True)) + lam_init)
    inv_sum = 1.0 / acc_ref[head_w:head_w + SUBLANES, :]
    out_t = acc_ref[:head_w, :] * jnp.tile(inv_sum, (head_w // SUBLANES, 1))
    out = (out_t[:, :tq] - lam_full * out_t[:, tq:]).T
    o_ref[...] = (_rms(out) * nw_ref[...] * (1.0 - lam_init)).astype(o_ref.dtype)


def diff_attention(q, k, v, bias, lam, subln_w, batch, seq, lam_init):
    m, width = k.shape
    head_w = 2 * DIFF_HEAD_DIM
    n_heads = width // head_w
    t = ATTN_BLOCK
    tq = 2 * t
    nq = seq // tq
    cols = 2 * tq
    ext_rows = head_w + 2 * SUBLANES
    return pl.pallas_call(
        functools.partial(_diff_attn_kernel, lam_init=lam_init),
        grid=(batch, n_heads, nq),
        in_specs=[pl.BlockSpec((head_w, tq), lambda b, h, i: (h, b * nq + i)),
                  pl.BlockSpec((seq, head_w), lambda b, h, i: (b, h)),
                  pl.BlockSpec((seq, head_w), lambda b, h, i: (b, h)),
                  pl.BlockSpec((None, 2, t, t), lambda b, h, i: (h, 0, 0, 0)),
                  _full_spec(lam.shape), _full_spec((1, head_w))],
        out_specs=pl.BlockSpec((tq, head_w), lambda b, h, i: (b * nq + i, h)),
        out_shape=jax.ShapeDtypeStruct((m, width), BF16),
        scratch_shapes=[pltpu.VMEM((seq // t, ext_rows, t), BF16),
                        pltpu.VMEM((t, cols), F32), pltpu.VMEM((t, cols), F32),
                        pltpu.VMEM((SUBLANES, cols), F32), pltpu.VMEM((SUBLANES, cols), F32),
                        pltpu.VMEM((t, cols), BF16),
                        pltpu.VMEM((SUBLANES, cols), F32),
                        pltpu.VMEM((SUBLANES, cols), F32),
                        pltpu.VMEM((ext_rows, cols), F32)],
        compiler_params=_params(("parallel", "parallel", "arbitrary")),
        name="diff_attention",
    )(q, k, v, bias, lam, subln_w.reshape(1, head_w))


def _retention_kernel(q_ref, k_ref, v_ref, g_ref, cos_hi_ref, sin_hi_ref, cos_lo_ref, sin_lo_ref, sign_ref,
                      decay_ref, zeta_ref, xi_ref, o_ref, state_ref, *, chunk_decay):
    @pl.when(pl.program_id(1) == 0)
    def _():
        state_ref[...] = jnp.zeros_like(state_ref)

    c_hi, s_hi = cos_hi_ref[...], sin_hi_ref[...]
    c_lo, s_lo = cos_lo_ref[...], sin_lo_ref[...]
    cos_all = c_hi * c_lo - s_hi * s_lo
    sin_all = (s_hi * c_lo + c_hi * s_lo) * sign_ref[...]

    half = RET_QK_DIM // 2
    states = [state_ref[h] for h in range(RET_HEADS)]
    heads = range(RET_HEADS)
    qk_of = lambda h: slice(h * RET_QK_DIM, (h + 1) * RET_QK_DIM)
    v_of = lambda h: slice(h * RET_V_DIM, (h + 1) * RET_V_DIM)
    for r0 in range(0, q_ref.shape[0], CHUNK):
        rows = slice(r0, r0 + CHUNK)
        cos, sin = cos_all[rows, :], sin_all[rows, :]

        def rope(t):
            return t * cos + pltpu.roll(t, half, 1) * sin

        qr = [rope(q_ref[rows, qk_of(h)].astype(F32)) for h in heads]
        kr = [rope(k_ref[rows, qk_of(h)].astype(F32)) * (RET_QK_DIM ** -0.5) for h in heads]
        scores = [lax.dot_general(qr[h].astype(BF16), kr[h].astype(BF16), NT_DIMS,
                                  preferred_element_type=F32) * decay_ref[h] for h in heads]
        cross = [jnp.dot((qr[h] * xi_ref[h]).astype(BF16), states[h].astype(BF16), preferred_element_type=F32)
                 for h in heads]
        inner = [jnp.dot(scores[h].astype(BF16), v_ref[rows, v_of(h)], preferred_element_type=F32) for h in heads]
        kv = [jnp.dot((kr[h].T * zeta_ref[h]).astype(BF16), v_ref[rows, v_of(h)], preferred_element_type=F32)
              for h in heads]
        for h in heads:
            states[h] = states[h] * chunk_decay[h] + kv[h]
            out = _rms(inner[h] + cross[h]) * _silu(g_ref[rows, v_of(h)].astype(F32))
            o_ref[rows, v_of(h)] = out.astype(o_ref.dtype)
    for h in range(RET_HEADS):
        state_ref[h] = states[h]


def retention(rq, rk, rv, rg, batch, seq, chunks_per_step=8):
    m = rq.shape[0]
    rows = chunks_per_step * CHUNK
    nc = seq // rows
    half = RET_QK_DIM // 2
    inv = ROPE_BASE ** (-jnp.arange(half, dtype=F32) / half)
    both_halves = lambda a: jnp.concatenate([a, a], axis=-1)
    ang_hi = (jnp.arange(nc, dtype=F32) * rows)[:, None] * inv[None]
    ang_lo = jnp.arange(rows, dtype=F32)[:, None] * inv[None]
    cos_hi = both_halves(jnp.cos(ang_hi)).reshape(nc, 1, RET_QK_DIM)
    sin_hi = both_halves(jnp.sin(ang_hi)).reshape(nc, 1, RET_QK_DIM)
    cos_lo, sin_lo = both_halves(jnp.cos(ang_lo)), both_halves(jnp.sin(ang_lo))
    sign = jnp.concatenate([-jnp.ones((1, half), F32), jnp.ones((1, half), F32)], axis=-1)
    log_gamma = jnp.log(1.0 - 2.0 ** (-5.0 - jnp.arange(RET_HEADS, dtype=F32)))
    idx = jnp.arange(CHUNK, dtype=F32)
    rel = idx[:, None] - idx[None, :]
    decay = jnp.where(rel >= 0, jnp.exp(jnp.maximum(rel, 0.0)[None] * log_gamma[:, None, None]), 0.0)
    zeta = jnp.exp((CHUNK - 1 - idx)[None] * log_gamma[:, None])[:, None, :]
    xi = jnp.broadcast_to(jnp.exp((idx + 1.0)[None] * log_gamma[:, None])[:, :, None],
                          (RET_HEADS, CHUNK, RET_QK_DIM))
    gamma = 1.0 - 2.0 ** (-5.0 - np.arange(RET_HEADS, dtype=np.float64))
    chunk_decay = tuple(float(g ** CHUNK) for g in gamma)
    row_spec = lambda width: pl.BlockSpec((rows, width), lambda b, c: (b * nc + c, 0))
    return pl.pallas_call(
        functools.partial(_retention_kernel, chunk_decay=chunk_decay),
        grid=(batch, nc),
        in_specs=[row_spec(rq.shape[1]), row_spec(rk.shape[1]), row_spec(rv.shape[1]), row_spec(rg.shape[1]),
                  pl.BlockSpec((None, 1, RET_QK_DIM), lambda b, c: (c, 0, 0)),
                  pl.BlockSpec((None, 1, RET_QK_DIM), lambda b, c: (c, 0, 0)),
                  _full_spec(cos_lo.shape), _full_spec(sin_lo.shape), _full_spec(sign.shape),
                  _full_spec(decay.shape), _full_spec(zeta.shape), _full_spec(xi.shape)],
        out_specs=row_spec(rv.shape[1]),
        out_shape=jax.ShapeDtypeStruct((m, rv.shape[1]), BF16),
        scratch_shapes=[pltpu.VMEM((RET_HEADS, RET_QK_DIM, RET_V_DIM), F32)],
        compiler_params=_params(("parallel", "arbitrary")),
        name="retention",
    )(rq, rk, rv, rg, cos_hi, sin_hi, cos_lo, sin_lo, sign, decay, zeta, xi)


def _swa_kernel(sink_ref, q_ref, kp_ref, kc_ref, vp_ref, vc_ref, bias_ref, o_ref, *, stage_blocks):
    i = pl.program_id(1)
    blk = SWA_BLOCK
    rep = SWA_HEADS // SWA_KV_HEADS
    d = SWA_HEAD_DIM
    cols = rep * blk
    k_all = jnp.concatenate([kp_ref[...], kc_ref[...]], axis=0)
    v_all = jnp.concatenate([vp_ref[...], vc_ref[...]], axis=0)
    key_row = lax.broadcasted_iota(jnp.int32, (2 * blk, cols), 0)
    head_of_lane = lax.broadcasted_iota(jnp.int32, (SUBLANES, cols), 1) // blk
    zero = jnp.zeros((d, blk), F32)
    ones = jnp.ones((2 * SUBLANES, 2 * blk), F32)
    for base in range(0, q_ref.shape[0], stage_blocks * blk):
        units = [(base + n * blk, g) for n in range(stage_blocks) for g in range(SWA_KV_HEADS)]
        logits = []
        for r0, g in units:
            kb = k_all[r0:r0 + 2 * blk]
            q_t = q_ref[r0:r0 + blk, g * rep * d:(g + 1) * rep * d].astype(F32).T
            rhs = jnp.concatenate(
                [jnp.concatenate([q_t[r * d:(r + 1) * d], zero] if g == 0 else [zero, q_t[r * d:(r + 1) * d]],
                                 axis=0) for r in range(rep)], axis=1).astype(BF16)
            s = jnp.dot(kb, rhs, preferred_element_type=F32)
            s = s + jnp.concatenate([bias_ref[g * rep + r] for r in range(rep)], axis=1)
            if r0 == 0:
                s = jnp.where(jnp.logical_and(key_row < blk, i == 0), MASK_VALUE, s)
            sink = jnp.zeros((SUBLANES, cols), F32)
            for r in range(rep):
                sink = jnp.where(head_of_lane == r, sink_ref[g * rep + r] * LOG2_E, sink)
            chains = [s[0:SUBLANES], s[SUBLANES:2 * SUBLANES]]
            for n, r in enumerate(range(2 * SUBLANES, 2 * blk, SUBLANES)):
                chains[n % 2] = jnp.maximum(chains[n % 2], s[r:r + SUBLANES])
            col_max = jnp.maximum(chains[0], chains[1])
            for shift in (4, 2, 1):
                col_max = jnp.maximum(col_max, pltpu.roll(col_max, shift, 0))
            logits.append((s, sink, jnp.maximum(col_max, sink)))
        probs = [jnp.exp2((s - jnp.tile(m, (2 * blk // SUBLANES, 1))).astype(BF16)) for s, _, m in logits]
        outs = []
        v_ts = {base + n * blk: v_all[base + n * blk:base + (n + 2) * blk].astype(F32).T
                for n in range(stage_blocks)}
        for (r0, g), (_, sink, m), p in zip(units, logits, probs):
            v_ext = jnp.concatenate([v_ts[r0][g * d:(g + 1) * d], ones], axis=0).astype(BF16)
            pv = jnp.dot(v_ext, p, preferred_element_type=F32)
            inv = 1.0 / (pv[d:d + SUBLANES] + jnp.exp2(sink - m))
            o_t = pv[:d] * jnp.tile(inv, (d // SUBLANES, 1))
            for j in range(rep // 2):
                pair_t = jnp.concatenate([o_t[:, (2 * j) * blk:(2 * j + 1) * blk],
                                          o_t[:, (2 * j + 1) * blk:(2 * j + 2) * blk]], axis=0)
                outs.append(pair_t.T)
        for n in range(stage_blocks):
            per_block = len(outs) // stage_blocks
            o_ref[base + n * blk:base + (n + 1) * blk, :] = jnp.concatenate(
                outs[n * per_block:(n + 1) * per_block], axis=-1).astype(o_ref.dtype)


def sliding_window_attention(sq, sk, sv, sinks, bias, batch, seq, blocks_per_step=8):
    m, qw = sq.shape
    kw = sk.shape[1]
    rows = blocks_per_step * SWA_BLOCK
    nb = seq // rows
    cur = lambda b, i: (b * nb + i, 0)
    prev = lambda b, i: ((b * nb + i) * blocks_per_step - jnp.minimum(i, 1), 0)
    return pl.pallas_call(
        functools.partial(_swa_kernel, stage_blocks=4),
        grid=(batch, nb),
        in_specs=[pl.BlockSpec(memory_space=pltpu.SMEM),
                  pl.BlockSpec((rows, qw), cur),
                  pl.BlockSpec((SWA_BLOCK, kw), prev), pl.BlockSpec((rows, kw), cur),
                  pl.BlockSpec((SWA_BLOCK, kw), prev), pl.BlockSpec((rows, kw), cur),
                  _full_spec(bias.shape)],
        out_specs=pl.BlockSpec((rows, qw), cur),
        out_shape=jax.ShapeDtypeStruct((m, qw), BF16),
        compiler_params=_params(("parallel", "parallel")),
        name="sliding_window_attention",
    )(sinks, sq, sk, sk, sv, sv, bias)


def _split_cols(w, sizes):
    offs = np.cumsum((0,) + tuple(sizes))
    return [w[:, offs[j]:offs[j + 1]] for j in range(len(sizes))]


def even_layer_mixer(x2, mod, g_pre, g_post, w_in, conv_w, conv_b, dt_bias, a_log, d_skip, ssd_norm, lam,
                     diff_norm, w_out, diff_bias, layer_idx, batch, seq):
    d = x2.shape[1]
    d_inner = d
    n_ssd_heads = d_inner // SSD_HEAD_DIM
    conv_ch = d_inner + 2 * SSD_GROUPS * SSD_STATE
    wz, wxbc, wdt, wq, wk, wv = _split_cols(w_in, (d_inner, conv_ch, n_ssd_heads, d, d, d))
    wdt = jnp.pad(wdt, ((0, 0), (0, LANES - n_ssd_heads)))
    wq = wq * (DIFF_HEAD_DIM ** -0.5 * LOG2_E)
    weights = [w.astype(BF16) for w in (wz, wxbc, wdt, wq.T, wk, wv)]
    z, xbc, dt_raw, q, k, v = norm_proj(x2, mod, g_pre, weights, (BF16, BF16, F32, BF16, BF16, BF16), seq,
                                        transposed=(False, False, False, True, False, False))
    y_ssd = ssd_mixer(z, xbc, dt_raw, conv_w, conv_b, dt_bias, a_log, d_skip, ssd_norm, batch, seq)
    lam_init = 0.8 - 0.6 * math.exp(-0.3 * layer_idx)
    y_diff = diff_attention(q, k, v, diff_bias, lam, diff_norm, batch, seq, lam_init)
    w_out = w_out.astype(BF16)
    return out_proj(x2, mod, g_post, [y_ssd, y_diff], [w_out[:d_inner], w_out[d_inner:]], seq)


def odd_layer_mixer(x2, mod, g_pre, g_post, w_in, sinks, w_out, swa_bias, batch, seq):
    ret_qk = RET_HEADS * RET_QK_DIM
    ret_v = RET_HEADS * RET_V_DIM
    swa_q = SWA_HEADS * SWA_HEAD_DIM
    swa_kv = SWA_KV_HEADS * SWA_HEAD_DIM
    wrq, wrk, wrv, wrg, wsq, wsk, wsv = _split_cols(w_in, (ret_qk, ret_qk, ret_v, ret_v, swa_q, swa_kv, swa_kv))
    wsq = wsq * (SWA_HEAD_DIM ** -0.5 * LOG2_E)
    weights = [w.astype(BF16) for w in (wrq, wrk, wrv, wrg, wsq, wsk, wsv)]
    rq, rk, rv, rg, sq, sk, sv = norm_proj(x2, mod, g_pre, weights, (BF16,) * 7, seq)
    y_ret = retention(rq, rk, rv, rg, batch, seq)
    y_swa = sliding_window_attention(sq, sk, sv, sinks, swa_bias, batch, seq)
    w_out = w_out.astype(BF16)
    return out_proj(x2, mod, g_post, [y_ret, y_swa], [w_out[:ret_v], w_out[ret_v:]], seq)


def kernel(x, c, rel_bias, norm_gains, mod_w, mod_b, mlp_w1, mlp_w2, e_w_in, e_conv_w, e_conv_b, e_dt_bias,
           e_A_log, e_D, e_ssd_norm, e_lambda, e_diff_norm, e_w_out, o_w_in, o_sinks, o_w_out):
    batch, seq, d = x.shape
    depth = norm_gains.shape[0]
    assert seq % (2 * ATTN_BLOCK) == 0 and d % LANES == 0
    mods = modulation(c, mod_w, mod_b.reshape(depth * 2, 3 * d))
    mlp_w1_b, mlp_w2_b = mlp_w1.astype(BF16), mlp_w2.astype(BF16)
    far_bucket = REL_BUCKETS - 1
    diff_bias = bias_tiles(rel_bias, _diff_bucket_idx(ATTN_BLOCK), shift_bucket=far_bucket, scale=LOG2_E)
    diff_bias = diff_bias.reshape(rel_bias.shape[1], 2, ATTN_BLOCK, ATTN_BLOCK)
    swa_bias = bias_tiles(rel_bias, _swa_bucket_idx(), scale=LOG2_E)

    x2 = x.reshape(batch * seq, d)
    for layer in range(depth):
        j = layer // 2
        gains = norm_gains[layer]
        if layer % 2 == 0:
            x2 = even_layer_mixer(x2, mods[2 * layer], gains[0], gains[1], e_w_in[j], e_conv_w[j], e_conv_b[j],
                                  e_dt_bias[j], e_A_log[j], e_D[j], e_ssd_norm[j], e_lambda[j], e_diff_norm[j],
                                  e_w_out[j], diff_bias, layer, batch, seq)
        else:
            x2 = odd_layer_mixer(x2, mods[2 * layer], gains[0], gains[1], o_w_in[j], o_sinks[j], o_w_out[j],
                                 swa_bias, batch, seq)
        x2 = mlp(x2, mods[2 * layer + 1], gains[2], gains[3], mlp_w1_b, mlp_w2_b, layer, seq)
    return x2.reshape(batch, seq, d)
```

```python
import functools
import math

import jax
import jax.numpy as jnp
import numpy as np
from jax import lax
from jax.experimental import pallas as pl
from jax.experimental.pallas import tpu as pltpu

EPS = 1e-6
MASK_VALUE = -1e30
LOG2_E = math.log2(math.e)
LANES = 128
SUBLANES = 8
VMEM_LIMIT = 56 * 1024 * 1024

CHUNK = 128
REL_BUCKETS = 32
REL_MAX_DIST = 128
SSD_HEAD_DIM = 64
SSD_GROUPS = 4
SSD_STATE = 128
SSD_CONV = 4
DIFF_HEAD_DIM = 64
RET_HEADS = 4
RET_QK_DIM = 128
RET_V_DIM = 256
ROPE_BASE = 10000.0
SWA_HEADS = 8
SWA_KV_HEADS = 2
SWA_HEAD_DIM = 64
SWA_BLOCK = 128
ATTN_BLOCK = 512

BF16 = jnp.bfloat16
F32 = jnp.float32
NT_DIMS = (((1,), (1,)), ((), ()))


def _params(semantics):
    return pltpu.CompilerParams(dimension_semantics=semantics, vmem_limit_bytes=VMEM_LIMIT)


def _full_spec(shape):
    return pl.BlockSpec(shape, lambda *_: (0,) * len(shape))


def _silu(x):
    h = 0.5 * x
    return h * jnp.tanh(h) + h


def _split3(x):
    hi = x.astype(BF16)
    r1 = x - hi.astype(F32)
    mid = r1.astype(BF16)
    lo = (r1 - mid.astype(F32)).astype(BF16)
    return hi, mid, lo


def _rms(x, eps=EPS):
    return x * lax.rsqrt(jnp.mean(x * x, axis=-1, keepdims=True) + eps)


def _mod_kernel(c_ref, w_ref, b_ref, o_ref):
    c_act = _silu(c_ref[...])
    o_ref[...] = jnp.dot(c_act, w_ref[...], preferred_element_type=F32,
                         precision=lax.Precision.HIGHEST) + b_ref[...]


def modulation(c, mod_w, mod_b):
    b, d = c.shape
    n = mod_b.shape[0]
    per_layer = mod_w.shape[1]
    rows = SUBLANES
    c_pad = jnp.zeros((rows, d), F32).at[:b].set(c)
    tn = d
    out = pl.pallas_call(
        _mod_kernel,
        grid=(n, 3 * d // tn),
        in_specs=[pl.BlockSpec((rows, d), lambda s, j: (0, 0)),
                  pl.BlockSpec((None, None, d, tn), lambda s, j: (s // per_layer, s % per_layer, 0, j)),
                  pl.BlockSpec((None, 1, tn), lambda s, j: (s, 0, j))],
        out_specs=pl.BlockSpec((None, rows, tn), lambda s, j: (s, 0, j)),
        out_shape=jax.ShapeDtypeStruct((n, rows, 3 * d), F32),
        compiler_params=_params(("parallel", "parallel")),
        name="modulation",
    )(c_pad, mod_w, mod_b.reshape(n, 1, 3 * d))
    return out[:, :b].reshape(n, b, 3, d)


def _t5_bucket_np(dist):
    max_exact = REL_BUCKETS // 2
    logd = np.log(np.maximum(dist, 1).astype(np.float32) / np.float32(max_exact))
    large = max_exact + (logd / np.float32(math.log(REL_MAX_DIST / max_exact))
                         * np.float32(REL_BUCKETS - max_exact)).astype(np.int32)
    large = np.minimum(large, REL_BUCKETS - 1)
    return np.where(dist < max_exact, dist, large).astype(np.int32)


def _bias_kernel(rb_ref, idx_ref, o_ref, *, shift_bucket, scale, block_buckets):
    h = pl.program_id(0)
    shift = rb_ref[shift_bucket, h] if shift_bucket is not None else 0.0

    def value(bucket):
        return (rb_ref[bucket, h] - shift) * scale

    for (r0, c0), buckets in block_buckets:
        window = (slice(r0, r0 + LANES), slice(c0, c0 + LANES))
        idx = idx_ref[window]
        acc = jnp.full(idx.shape, MASK_VALUE if -1 in buckets else value(max(buckets)), F32)
        for bucket in buckets:
            if bucket >= 0 and len(buckets) > 1:
                acc = jnp.where(idx == bucket, value(bucket), acc)
        o_ref[window] = acc


def bias_tiles(rel_bias, bucket_idx, shift_bucket=None, scale=1.0):
    n_heads = rel_bias.shape[1]
    r, c = bucket_idx.shape
    block_buckets = tuple(((r0, c0), tuple(int(b) for b in np.unique(bucket_idx[r0:r0 + LANES, c0:c0 + LANES])))
                          for r0 in range(0, r, LANES) for c0 in range(0, c, LANES))
    return pl.pallas_call(
        functools.partial(_bias_kernel, shift_bucket=shift_bucket, scale=scale, block_buckets=block_buckets),
        grid=(n_heads,),
        in_specs=[pl.BlockSpec(memory_space=pltpu.SMEM),
                  pl.BlockSpec((r, c), lambda h: (0, 0))],
        out_specs=pl.BlockSpec((None, r, c), lambda h: (h, 0, 0)),
        out_shape=jax.ShapeDtypeStruct((n_heads, r, c), F32),
        compiler_params=_params(("parallel",)),
        name="bias_tiles",
    )(rel_bias, jnp.asarray(bucket_idx))


def _diff_bucket_idx(t):
    q = np.arange(t)[None, :]
    k = np.arange(t)[:, None]
    diag = q - k
    idx_diag = np.where(diag >= 0, _t5_bucket_np(np.maximum(diag, 0)), -1)
    idx_prev = _t5_bucket_np(diag + t)
    assert (_t5_bucket_np(diag + 2 * t) == REL_BUCKETS - 1).all()
    return np.concatenate([idx_diag, idx_prev], axis=0).astype(np.int32)


def _swa_bucket_idx():
    q = np.arange(SWA_BLOCK)[None, :]
    k = np.arange(2 * SWA_BLOCK)[:, None] - SWA_BLOCK
    dist = q - k
    valid = (dist >= 0) & (dist < SWA_BLOCK)
    return np.where(valid, _t5_bucket_np(np.maximum(dist, 0)), -1).astype(np.int32)


def _norm_proj_kernel(x_ref, mod_ref, g_ref, *refs):
    n = len(refs) // 2
    w_refs, o_refs = refs[:n], refs[n:]
    mod = mod_ref[...]
    h = _rms(x_ref[...]) * g_ref[...] * (1.0 + mod[1:2]) + mod[0:1]
    hb = h.astype(BF16)
    for w_ref, o_ref in zip(w_refs, o_refs):
        o_ref[...] = jnp.dot(hb, w_ref[...], preferred_element_type=F32).astype(o_ref.dtype)


def norm_proj(x2, mod, gain, weights, out_dtypes, seq, tm=512):
    m, d = x2.shape
    per_batch = seq // tm
    in_specs = [pl.BlockSpec((tm, d), lambda i: (i, 0)),
                pl.BlockSpec((None, 3, d), lambda i: (i // per_batch, 0, 0)),
                _full_spec((1, d))]
    in_specs += [_full_spec(w.shape) for w in weights]
    out_specs = [pl.BlockSpec((tm, w.shape[1]), lambda i: (i, 0)) for w in weights]
    out_shape = [jax.ShapeDtypeStruct((m, w.shape[1]), dt) for w, dt in zip(weights, out_dtypes)]
    return pl.pallas_call(
        _norm_proj_kernel,
        grid=(m // tm,),
        in_specs=in_specs,
        out_specs=out_specs,
        out_shape=out_shape,
        compiler_params=_params(("parallel",)),
        name="norm_proj",
    )(x2, mod, gain.reshape(1, d), *weights)


def _out_proj_kernel(x_ref, mod_ref, g_ref, *refs):
    n = (len(refs) - 1) // 2
    y_refs, w_refs, o_ref = refs[:n], refs[n:2 * n], refs[2 * n]
    acc = None
    for y_ref, w_ref in zip(y_refs, w_refs):
        part = jnp.dot(y_ref[...], w_ref[...], preferred_element_type=F32)
        acc = part if acc is None else acc + part
    gate = mod_ref[...][2:3]
    o_ref[...] = x_ref[...] + gate * (_rms(acc) * g_ref[...])


def out_proj(x2, mod, gain, ys, weights, seq, tm=512):
    m, d = x2.shape
    per_batch = seq // tm
    in_specs = [pl.BlockSpec((tm, d), lambda i: (i, 0)),
                pl.BlockSpec((None, 3, d), lambda i: (i // per_batch, 0, 0)),
                _full_spec((1, d))]
    in_specs += [pl.BlockSpec((tm, y.shape[1]), lambda i: (i, 0)) for y in ys]
    in_specs += [_full_spec(w.shape) for w in weights]
    return pl.pallas_call(
        _out_proj_kernel,
        grid=(m // tm,),
        in_specs=in_specs,
        out_specs=pl.BlockSpec((tm, d), lambda i: (i, 0)),
        out_shape=jax.ShapeDtypeStruct((m, d), F32),
        compiler_params=_params(("parallel",)),
        name="out_proj",
    )(x2, mod, gain.reshape(1, d), *ys, *weights)


def _mlp_kernel(x_ref, mod_ref, g_pre_ref, g_post_ref, w1_hbm, w2_hbm, o_ref, w1_ref, w2_ref, stage1_ref,
                stage2_ref, sem, *, ff_chunk, layer):
    d_ff = w1_ref.shape[1]
    n_chunks = d_ff // ff_chunk

    def weight_copies(c):
        slot = c % 2
        return (pltpu.make_async_copy(w1_hbm.at[layer, :, pl.ds(c * ff_chunk, ff_chunk)], stage1_ref.at[slot],
                                      sem.at[0, slot]),
                pltpu.make_async_copy(w2_hbm.at[layer, pl.ds(c * ff_chunk, ff_chunk), :], stage2_ref.at[slot],
                                      sem.at[1, slot]))

    def body(load_weights):
        if load_weights:
            for cp in weight_copies(0):
                cp.start()
        x = x_ref[...]
        mod = mod_ref[...]
        h = _rms(x) * g_pre_ref[...] * (1.0 + mod[1:2]) + mod[0:1]
        hb = h.astype(BF16)
        acc = None
        for c in range(n_chunks):
            c0 = c * ff_chunk
            if load_weights:
                if c + 1 < n_chunks:
                    for cp in weight_copies(c + 1):
                        cp.start()
                for cp in weight_copies(c):
                    cp.wait()
                w1_ref[:, c0:c0 + ff_chunk] = stage1_ref[c % 2].astype(BF16)
                w2_ref[c0:c0 + ff_chunk, :] = stage2_ref[c % 2].astype(BF16)
            a = jnp.dot(hb, w1_ref[:, c0:c0 + ff_chunk], preferred_element_type=F32)
            a = jnp.square(jnp.maximum(a, 0.0)).astype(BF16)
            part = jnp.dot(a, w2_ref[c0:c0 + ff_chunk, :], preferred_element_type=F32)
            acc = part if acc is None else acc + part
        o_ref[...] = x + mod[2:3] * (_rms(acc) * g_post_ref[...])

    @pl.when(pl.program_id(0) == 0)
    def _():
        body(True)

    @pl.when(pl.program_id(0) > 0)
    def _():
        body(False)


def mlp(x2, mod, g_pre, g_post, w1, w2, layer, seq, tm=512, ff_chunk=1024):
    m, d = x2.shape
    d_ff = w1.shape[2]
    per_batch = seq // tm
    return pl.pallas_call(
        functools.partial(_mlp_kernel, ff_chunk=ff_chunk, layer=layer),
        grid=(m // tm,),
        in_specs=[pl.BlockSpec((tm, d), lambda i: (i, 0)),
                  pl.BlockSpec((None, 3, d), lambda i: (i // per_batch, 0, 0)),
                  _full_spec((1, d)), _full_spec((1, d)),
                  pl.BlockSpec(memory_space=pl.ANY),
                  pl.BlockSpec(memory_space=pl.ANY)],
        out_specs=pl.BlockSpec((tm, d), lambda i: (i, 0)),
        out_shape=jax.ShapeDtypeStruct((m, d), F32),
        scratch_shapes=[pltpu.VMEM((d, d_ff), BF16), pltpu.VMEM((d_ff, d), BF16),
                        pltpu.VMEM((2, d, ff_chunk), F32), pltpu.VMEM((2, ff_chunk, d), F32),
                        pltpu.SemaphoreType.DMA((2, 2))],
        compiler_params=_params(("arbitrary",)),
        name="mlp",
    )(x2, mod, g_pre.reshape(1, d), g_post.reshape(1, d), w1, w2)


def _ssd_kernel(z_ref, xbc_ref, dt_ref, cw_ref, cb_ref, dtb_ref, alog_ref, dskip_ref, nw_ref, expand_ref,
                shift_ref, o_ref, conv_buf, state_ref, *, d_inner):
    chunk = CHUNK
    rows_per_step = z_ref.shape[0]
    n_state = SSD_STATE
    gn = SSD_GROUPS * n_state
    pair = 2 * SSD_HEAD_DIM
    heads_per_group = d_inner // SSD_HEAD_DIM // SSD_GROUPS
    halo = conv_buf.shape[0] - rows_per_step

    @pl.when(pl.program_id(1) == 0)
    def _():
        state_ref[...] = jnp.zeros_like(state_ref)
        conv_buf[0:halo, :] = jnp.zeros((halo, conv_buf.shape[1]), BF16)

    conv_buf[halo:, :] = xbc_ref[...]

    def chunk_body(r0):
        rows = slice(r0, r0 + chunk)
        cur = xbc_ref[rows, :]
        shifted = jnp.dot(shift_ref[...], conv_buf[r0:r0 + halo + chunk, :], preferred_element_type=F32)
        acc = cb_ref[...] + cw_ref[SSD_CONV - 1:SSD_CONV, :] * cur.astype(F32)
        for tap in range(SSD_CONV - 1):
            acc = acc + cw_ref[tap:tap + 1, :] * shifted[tap * chunk:(tap + 1) * chunk, :]
        xbc = _silu(acc)
        xs = xbc[:, :d_inner]
        b_all = xbc[:, d_inner:d_inner + gn]
        c_all = xbc[:, d_inner + gn:]

        dt_in = dt_ref[rows, :] + dtb_ref[...]
        dt = jnp.maximum(dt_in, 0.0) + jnp.log1p(jnp.exp(-jnp.abs(dt_in)))
        a = dt * (-jnp.exp(alog_ref[...]))
        row = lax.broadcasted_iota(jnp.int32, (chunk, chunk), 0)
        col = lax.broadcasted_iota(jnp.int32, (chunk, chunk), 1)
        causal = row >= col
        tril = jnp.where(causal, 1.0, 0.0).astype(BF16)
        parts = jnp.dot(tril, jnp.concatenate(_split3(a), axis=1), preferred_element_type=F32)
        a_cs = parts[:, :LANES] + parts[:, LANES:2 * LANES] + parts[:, 2 * LANES:]
        a_cs_t = a_cs.T
        lhs = jnp.concatenate([jnp.concatenate(_split3(dt), axis=1), jnp.concatenate(_split3(a_cs), axis=1)], axis=0)
        expanded = jnp.dot(lhs, expand_ref[...], preferred_element_type=F32)
        dt_e, acs_e = expanded[:chunk], expanded[chunk:]
        a_last_e = acs_e[chunk - 1:chunk, :]
        x_dt = xs * dt_e
        w_state = (x_dt * jnp.exp(a_last_e - acs_e)).astype(BF16)
        out_scale = jnp.exp(acs_e)
        chunk_decay = jnp.exp(a_last_e)
        x_dt_b = x_dt.astype(BF16)
        lane = lax.broadcasted_iota(jnp.int32, (chunk, pair), 1)
        first_head = lane < SSD_HEAD_DIM

        n_heads = d_inner // SSD_HEAD_DIM
        cgs = [c_all[:, g * n_state:(g + 1) * n_state].astype(BF16) for g in range(SSD_GROUPS)]
        bgs = [b_all[:, g * n_state:(g + 1) * n_state] for g in range(SSD_GROUPS)]
        cbs = [lax.dot_general(cg, bg.astype(BF16), NT_DIMS, preferred_element_type=F32)
               for cg, bg in zip(cgs, bgs)]
        bg_ts = [bg.T.astype(BF16) for bg in bgs]
        y_blocks = []
        for p in range(n_heads // 2):
            g = p // (heads_per_group // 2)
            sl = slice(p * pair, (p + 1) * pair)
            xp = x_dt_b[:, sl]
            y_diag = None
            for k in range(2):
                hh = 2 * p + k
                decay = jnp.exp(jnp.where(causal, a_cs[:, hh:hh + 1] - a_cs_t[hh:hh + 1, :], MASK_VALUE))
                m = (cbs[g] * decay).astype(BF16)
                xh = jnp.where(first_head if k == 0 else jnp.logical_not(first_head), xp, jnp.zeros_like(xp))
                part = jnp.dot(m, xh, preferred_element_type=F32)
                y_diag = part if y_diag is None else y_diag + part
            st = state_ref[p]
            y_off = jnp.dot(cgs[g], st.astype(BF16), preferred_element_type=F32) * out_scale[:, sl]
            state_ref[p] = st * chunk_decay[:, sl] + jnp.dot(bg_ts[g], w_state[:, sl], preferred_element_type=F32)
            y_blocks.append(y_diag + y_off)
        y = jnp.concatenate(y_blocks, axis=-1) + xs * dskip_ref[...]
        y = y * _silu(z_ref[rows, :].astype(F32))
        o_ref[rows, :] = (_rms(y) * nw_ref[...]).astype(o_ref.dtype)

    for r0 in range(0, rows_per_step, chunk):
        chunk_body(r0)
    conv_buf[0:halo, :] = conv_buf[rows_per_step:rows_per_step + halo, :]


def ssd_mixer(z, xbc, dt_raw, conv_w, conv_b, dt_bias, a_log, d_skip, norm_w, batch, seq, chunks_per_step=4):
    m, d_inner = z.shape
    conv_ch = xbc.shape[1]
    n_heads = d_inner // SSD_HEAD_DIM
    rows = chunks_per_step * CHUNK
    nc = seq // rows
    pad = LANES - n_heads
    expand = np.zeros((LANES, d_inner), np.float32)
    for h in range(n_heads):
        expand[h, h * SSD_HEAD_DIM:(h + 1) * SSD_HEAD_DIM] = 1.0
    expand3 = jnp.asarray(np.tile(expand, (3, 1)), BF16)
    halo = 2 * SUBLANES
    shift = np.zeros(((SSD_CONV - 1) * CHUNK, halo + CHUNK), np.float32)
    for tap in range(SSD_CONV - 1):
        shift[tap * CHUNK + np.arange(CHUNK), halo - (SSD_CONV - 1) + tap + np.arange(CHUNK)] = 1.0
    shift = jnp.asarray(shift, BF16)
    row_spec = lambda width: pl.BlockSpec((rows, width), lambda b, c: (b * nc + c, 0))
    return pl.pallas_call(
        functools.partial(_ssd_kernel, d_inner=d_inner),
        grid=(batch, nc),
        in_specs=[row_spec(d_inner), row_spec(conv_ch), row_spec(LANES),
                  _full_spec((SSD_CONV, conv_ch)), _full_spec((1, conv_ch)),
                  _full_spec((1, LANES)), _full_spec((1, LANES)),
                  _full_spec((1, d_inner)), _full_spec((1, d_inner)), _full_spec(expand3.shape),
                  _full_spec(shift.shape)],
        out_specs=row_spec(d_inner),
        out_shape=jax.ShapeDtypeStruct((m, d_inner), BF16),
        scratch_shapes=[pltpu.VMEM((halo + rows, conv_ch), BF16),
                        pltpu.VMEM((n_heads // 2, SSD_STATE, 2 * SSD_HEAD_DIM), F32)],
        compiler_params=_params(("parallel", "arbitrary")),
        name="ssd_mixer",
    )(z, xbc, dt_raw, conv_w, conv_b.reshape(1, conv_ch),
      jnp.pad(dt_bias, (0, pad)).reshape(1, LANES), jnp.pad(a_log, (0, pad)).reshape(1, LANES),
      jnp.repeat(d_skip, SSD_HEAD_DIM).reshape(1, d_inner), norm_w.reshape(1, d_inner), expand3, shift)


def _diff_attn_kernel(q_ref, k_ref, v_ref, bias_ref, lam_ref, nw_ref, o_ref,
                      vt_ref, s0_ref, s1_ref, cmax0_ref, cmax1_ref, p_ref, m_ref, alpha_ref, acc_ref,
                      *, lam_init):
    s_refs, cmax_refs = (s0_ref, s1_ref), (cmax0_ref, cmax1_ref)
    t = s0_ref.shape[0]
    tq = q_ref.shape[0]
    head_w = v_ref.shape[1]
    ext_rows = vt_ref.shape[1]
    i = pl.program_id(2)
    q_t = q_ref[...].astype(F32).T.astype(BF16)
    dim = lax.broadcasted_iota(jnp.int32, q_t.shape, 0)
    zero = jnp.zeros_like(q_t)
    qq_t = jnp.concatenate([jnp.where(dim < DIFF_HEAD_DIM, q_t, zero),
                            jnp.where(dim >= DIFF_HEAD_DIM, q_t, zero)], axis=1)

    @pl.when(i == 0)
    def _():
        for j in range(vt_ref.shape[0]):
            vt_ref[j, :head_w, :] = v_ref[j * t:(j + 1) * t, :].astype(F32).T.astype(BF16)
            vt_ref[j, head_w:, :] = jnp.ones((ext_rows - head_w, t), BF16)

    m_ref[...] = jnp.full(m_ref.shape, MASK_VALUE, F32)
    acc_ref[...] = jnp.zeros(acc_ref.shape, F32)

    diag_tile, prev_tile = bias_ref[0], bias_ref[1]
    all_cols = ((0, t), (t, t), (tq, t), (tq + t, t))
    near_cols = (all_cols[1::2], all_cols, all_cols)
    near_tiles = ((diag_tile, diag_tile),
                  (diag_tile, prev_tile, diag_tile, prev_tile),
                  (prev_tile, None, prev_tile, None))
    near_blocks = (2 * i + 1, 2 * i, jnp.maximum(2 * i - 1, 0))

    def raw_logits(block, cols=all_cols):
        kb = k_ref[pl.ds(pl.multiple_of(block * t, t), t), :]
        rhs = qq_t if cols is all_cols else jnp.concatenate([qq_t[:, c0:c0 + w] for c0, w in cols], axis=1)
        return jnp.dot(kb, rhs, preferred_element_type=F32)

    def store_logits(pieces, slot, cols=all_cols):
        for piece, (c0, w) in zip(pieces, cols):
            s_refs[slot][:, c0:c0 + w] = piece
            chains = [piece[0:SUBLANES, :], piece[SUBLANES:2 * SUBLANES, :]]
            for n, r in enumerate(range(2 * SUBLANES, t, SUBLANES)):
                chains[n % 2] = jnp.maximum(chains[n % 2], piece[r:r + SUBLANES, :])
            col_max = jnp.maximum(chains[0], chains[1])
            for shift in (4, 2, 1):
                col_max = jnp.maximum(col_max, pltpu.roll(col_max, shift, 0))
            cmax_refs[slot][:, c0:c0 + w] = col_max

    def is_near(visit):
        return isinstance(visit, int) and visit < len(near_blocks)

    def block_of(visit):
        return near_blocks[visit] if is_near(visit) else visit - len(near_blocks)

    def cols_of(visit):
        return near_cols[visit] if is_near(visit) else all_cols

    def logits_piece(visit, slot, n):
        col = cols_of(visit)[n]
        piece = raw_logits(block_of(visit), (col,))
        tile = near_tiles[visit][n] if is_near(visit) else None
        store_logits([piece if tile is None else piece + tile], slot, (col,))

    def softmax_accumulate(visit, slot, col_ranges=((0, 2 * tq),)):
        s_ref = s_refs[slot]
        vt = vt_ref[block_of(visit)]
        packed = 2 * SUBLANES
        for c0, w in col_ranges:
            c = slice(c0, c0 + w)
            m_old = m_ref[:, c]
            m_new = jnp.maximum(m_old, cmax_refs[slot][:, c])
            m_ref[:, c] = m_new
            alpha_ref[:, c] = jnp.exp2(m_old - m_new)
            m_tile = jnp.concatenate([m_new, m_new], axis=0)
            for r in range(0, t, packed):
                p_ref[r:r + packed, c] = jnp.exp2((s_ref[r:r + packed, c] - m_tile).astype(BF16))
            pv = jnp.dot(vt, p_ref[:, c], preferred_element_type=F32)
            alpha = alpha_ref[:, c]
            for r in range(0, ext_rows, SUBLANES):
                acc_ref[r:r + SUBLANES, c] = alpha * acc_ref[r:r + SUBLANES, c] + pv[r:r + SUBLANES, :]

    def pipeline_step(visit, slot):
        stage1, stage2 = cols_of(visit), cols_of(visit - 1)
        for n in range(max(len(stage1), len(stage2))):
            if n < len(stage1):
                logits_piece(visit, slot, n)
            if n < len(stage2):
                softmax_accumulate(visit - 1, 1 - slot, (stage2[n],))

    for n in range(len(near_cols[0])):
        logits_piece(0, 0, n)
    pipeline_step(1, 1)

    @pl.when(i < pl.num_programs(2))
    def _():
        pipeline_step(2, 0)

    @pl.when(i > 0)
    def _():
        pipeline_step(3, 1)
        last = 2 * i + 1

        def pair(k, _):
            pipeline_step(4 + 2 * k, 0)

            @pl.when(5 + 2 * k <= last)
            def _():
                pipeline_step(5 + 2 * k, 1)

            return 0

        lax.fori_loop(0, i - 1, pair, 0)
        softmax_accumulate(last, 1, all_cols)

    lam = lam_ref[...]
    lam_full = (jnp.exp(jnp.sum(lam[0:1] * lam[1:2], axis=-1, keepdims=True))
                - jnp.exp(jnp.sum(lam[2:3] * lam[3:4], axis=-1, keepdims=True)) + lam_init)
    inv_sum = 1.0 / acc_ref[head_w:head_w + SUBLANES, :]
    out_t = acc_ref[:head_w, :] * jnp.tile(inv_sum, (head_w // SUBLANES, 1))
    out = (out_t[:, :tq] - lam_full * out_t[:, tq:]).T
    o_ref[...] = (_rms(out) * nw_ref[...] * (1.0 - lam_init)).astype(o_ref.dtype)


def diff_attention(q, k, v, bias, lam, subln_w, batch, seq, lam_init):
    m, width = q.shape
    head_w = 2 * DIFF_HEAD_DIM
    n_heads = width // head_w
    t = ATTN_BLOCK
    tq = 2 * t
    nq = seq // tq
    cols = 2 * tq
    ext_rows = head_w + 2 * SUBLANES
    return pl.pallas_call(
        functools.partial(_diff_attn_kernel, lam_init=lam_init),
        grid=(batch, n_heads, nq),
        in_specs=[pl.BlockSpec((tq, head_w), lambda b, h, i: (b * nq + i, h)),
                  pl.BlockSpec((seq, head_w), lambda b, h, i: (b, h)),
                  pl.BlockSpec((seq, head_w), lambda b, h, i: (b, h)),
                  pl.BlockSpec((None, 2, t, t), lambda b, h, i: (h, 0, 0, 0)),
                  _full_spec(lam.shape), _full_spec((1, head_w))],
        out_specs=pl.BlockSpec((tq, head_w), lambda b, h, i: (b * nq + i, h)),
        out_shape=jax.ShapeDtypeStruct((m, width), BF16),
        scratch_shapes=[pltpu.VMEM((seq // t, ext_rows, t), BF16),
                        pltpu.VMEM((t, cols), F32), pltpu.VMEM((t, cols), F32),
                        pltpu.VMEM((SUBLANES, cols), F32), pltpu.VMEM((SUBLANES, cols), F32),
                        pltpu.VMEM((t, cols), BF16),
                        pltpu.VMEM((SUBLANES, cols), F32),
                        pltpu.VMEM((SUBLANES, cols), F32),
                        pltpu.VMEM((ext_rows, cols), F32)],
        compiler_params=_params(("parallel", "parallel", "arbitrary")),
        name="diff_attention",
    )(q, k, v, bias, lam, subln_w.reshape(1, head_w))


def _retention_kernel(q_ref, k_ref, v_ref, g_ref, cos_hi_ref, sin_hi_ref, cos_lo_ref, sin_lo_ref, sign_ref,
                      decay_ref, zeta_ref, xi_ref, o_ref, state_ref, *, chunk_decay):
    @pl.when(pl.program_id(1) == 0)
    def _():
        state_ref[...] = jnp.zeros_like(state_ref)

    c_hi, s_hi = cos_hi_ref[...], sin_hi_ref[...]
    c_lo, s_lo = cos_lo_ref[...], sin_lo_ref[...]
    cos_all = c_hi * c_lo - s_hi * s_lo
    sin_all = (s_hi * c_lo + c_hi * s_lo) * sign_ref[...]

    half = RET_QK_DIM // 2
    states = [state_ref[h] for h in range(RET_HEADS)]
    heads = range(RET_HEADS)
    qk_of = lambda h: slice(h * RET_QK_DIM, (h + 1) * RET_QK_DIM)
    v_of = lambda h: slice(h * RET_V_DIM, (h + 1) * RET_V_DIM)
    for r0 in range(0, q_ref.shape[0], CHUNK):
        rows = slice(r0, r0 + CHUNK)
        cos, sin = cos_all[rows, :], sin_all[rows, :]

        def rope(t):
            return t * cos + pltpu.roll(t, half, 1) * sin

        qr = [rope(q_ref[rows, qk_of(h)].astype(F32)) for h in heads]
        kr = [rope(k_ref[rows, qk_of(h)].astype(F32)) * (RET_QK_DIM ** -0.5) for h in heads]
        scores = [lax.dot_general(qr[h].astype(BF16), kr[h].astype(BF16), NT_DIMS,
                                  preferred_element_type=F32) * decay_ref[h] for h in heads]
        cross = [jnp.dot((qr[h] * xi_ref[h]).astype(BF16), states[h].astype(BF16), preferred_element_type=F32)
                 for h in heads]
        inner = [jnp.dot(scores[h].astype(BF16), v_ref[rows, v_of(h)], preferred_element_type=F32) for h in heads]
        kv = [jnp.dot((kr[h].T * zeta_ref[h]).astype(BF16), v_ref[rows, v_of(h)], preferred_element_type=F32)
              for h in heads]
        for h in heads:
            states[h] = states[h] * chunk_decay[h] + kv[h]
            out = _rms(inner[h] + cross[h]) * _silu(g_ref[rows, v_of(h)].astype(F32))
            o_ref[rows, v_of(h)] = out.astype(o_ref.dtype)
    for h in range(RET_HEADS):
        state_ref[h] = states[h]


def retention(rq, rk, rv, rg, batch, seq, chunks_per_step=8):
    m = rq.shape[0]
    rows = chunks_per_step * CHUNK
    nc = seq // rows
    half = RET_QK_DIM // 2
    inv = ROPE_BASE ** (-jnp.arange(half, dtype=F32) / half)
    both_halves = lambda a: jnp.concatenate([a, a], axis=-1)
    ang_hi = (jnp.arange(nc, dtype=F32) * rows)[:, None] * inv[None]
    ang_lo = jnp.arange(rows, dtype=F32)[:, None] * inv[None]
    cos_hi = both_halves(jnp.cos(ang_hi)).reshape(nc, 1, RET_QK_DIM)
    sin_hi = both_halves(jnp.sin(ang_hi)).reshape(nc, 1, RET_QK_DIM)
    cos_lo, sin_lo = both_halves(jnp.cos(ang_lo)), both_halves(jnp.sin(ang_lo))
    sign = jnp.concatenate([-jnp.ones((1, half), F32), jnp.ones((1, half), F32)], axis=-1)
    log_gamma = jnp.log(1.0 - 2.0 ** (-5.0 - jnp.arange(RET_HEADS, dtype=F32)))
    idx = jnp.arange(CHUNK, dtype=F32)
    rel = idx[:, None] - idx[None, :]
    decay = jnp.where(rel >= 0, jnp.exp(jnp.maximum(rel, 0.0)[None] * log_gamma[:, None, None]), 0.0)
    zeta = jnp.exp((CHUNK - 1 - idx)[None] * log_gamma[:, None])[:, None, :]
    xi = jnp.broadcast_to(jnp.exp((idx + 1.0)[None] * log_gamma[:, None])[:, :, None],
                          (RET_HEADS, CHUNK, RET_QK_DIM))
    gamma = 1.0 - 2.0 ** (-5.0 - np.arange(RET_HEADS, dtype=np.float64))
    chunk_decay = tuple(float(g ** CHUNK) for g in gamma)
    row_spec = lambda width: pl.BlockSpec((rows, width), lambda b, c: (b * nc + c, 0))
    return pl.pallas_call(
        functools.partial(_retention_kernel, chunk_decay=chunk_decay),
        grid=(batch, nc),
        in_specs=[row_spec(rq.shape[1]), row_spec(rk.shape[1]), row_spec(rv.shape[1]), row_spec(rg.shape[1]),
                  pl.BlockSpec((None, 1, RET_QK_DIM), lambda b, c: (c, 0, 0)),
                  pl.BlockSpec((None, 1, RET_QK_DIM), lambda b, c: (c, 0, 0)),
                  _full_spec(cos_lo.shape), _full_spec(sin_lo.shape), _full_spec(sign.shape),
                  _full_spec(decay.shape), _full_spec(zeta.shape), _full_spec(xi.shape)],
        out_specs=row_spec(rv.shape[1]),
        out_shape=jax.ShapeDtypeStruct((m, rv.shape[1]), BF16),
        scratch_shapes=[pltpu.VMEM((RET_HEADS, RET_QK_DIM, RET_V_DIM), F32)],
        compiler_params=_params(("parallel", "arbitrary")),
        name="retention",
    )(rq, rk, rv, rg, cos_hi, sin_hi, cos_lo, sin_lo, sign, decay, zeta, xi)


def _swa_kernel(sink_ref, q_ref, kp_ref, kc_ref, vp_ref, vc_ref, bias_ref, o_ref, *, stage_blocks):
    i = pl.program_id(1)
    blk = SWA_BLOCK
    rep = SWA_HEADS // SWA_KV_HEADS
    d = SWA_HEAD_DIM
    cols = rep * blk
    k_all = jnp.concatenate([kp_ref[...], kc_ref[...]], axis=0)
    v_all = jnp.concatenate([vp_ref[...], vc_ref[...]], axis=0)
    key_row = lax.broadcasted_iota(jnp.int32, (2 * blk, cols), 0)
    head_of_lane = lax.broadcasted_iota(jnp.int32, (SUBLANES, cols), 1) // blk
    zero = jnp.zeros((d, blk), F32)
    ones = jnp.ones((2 * SUBLANES, 2 * blk), F32)
    for base in range(0, q_ref.shape[0], stage_blocks * blk):
        units = [(base + n * blk, g) for n in range(stage_blocks) for g in range(SWA_KV_HEADS)]
        logits = []
        for r0, g in units:
            kb = k_all[r0:r0 + 2 * blk]
            q_t = q_ref[r0:r0 + blk, g * rep * d:(g + 1) * rep * d].astype(F32).T
            rhs = jnp.concatenate(
                [jnp.concatenate([q_t[r * d:(r + 1) * d], zero] if g == 0 else [zero, q_t[r * d:(r + 1) * d]],
                                 axis=0) for r in range(rep)], axis=1).astype(BF16)
            s = jnp.dot(kb, rhs, preferred_element_type=F32)
            s = s + jnp.concatenate([bias_ref[g * rep + r] for r in range(rep)], axis=1)
            if r0 == 0:
                s = jnp.where(jnp.logical_and(key_row < blk, i == 0), MASK_VALUE, s)
            sink = jnp.zeros((SUBLANES, cols), F32)
            for r in range(rep):
                sink = jnp.where(head_of_lane == r, sink_ref[g * rep + r] * LOG2_E, sink)
            chains = [s[0:SUBLANES], s[SUBLANES:2 * SUBLANES]]
            for n, r in enumerate(range(2 * SUBLANES, 2 * blk, SUBLANES)):
                chains[n % 2] = jnp.maximum(chains[n % 2], s[r:r + SUBLANES])
            col_max = jnp.maximum(chains[0], chains[1])
            for shift in (4, 2, 1):
                col_max = jnp.maximum(col_max, pltpu.roll(col_max, shift, 0))
            logits.append((s, sink, jnp.maximum(col_max, sink)))
        probs = [jnp.exp2((s - jnp.tile(m, (2 * blk // SUBLANES, 1))).astype(BF16)) for s, _, m in logits]
        outs = []
        v_ts = {base + n * blk: v_all[base + n * blk:base + (n + 2) * blk].astype(F32).T
                for n in range(stage_blocks)}
        for (r0, g), (_, sink, m), p in zip(units, logits, probs):
            v_ext = jnp.concatenate([v_ts[r0][g * d:(g + 1) * d], ones], axis=0).astype(BF16)
            pv = jnp.dot(v_ext, p, preferred_element_type=F32)
            inv = 1.0 / (pv[d:d + SUBLANES] + jnp.exp2(sink - m))
            o_t = pv[:d] * jnp.tile(inv, (d // SUBLANES, 1))
            for j in range(rep // 2):
                pair_t = jnp.concatenate([o_t[:, (2 * j) * blk:(2 * j + 1) * blk],
                                          o_t[:, (2 * j + 1) * blk:(2 * j + 2) * blk]], axis=0)
                outs.append(pair_t.T)
        for n in range(stage_blocks):
            per_block = len(outs) // stage_blocks
            o_ref[base + n * blk:base + (n + 1) * blk, :] = jnp.concatenate(
                outs[n * per_block:(n + 1) * per_block], axis=-1).astype(o_ref.dtype)


def sliding_window_attention(sq, sk, sv, sinks, bias, batch, seq, blocks_per_step=8):
    m, qw = sq.shape
    kw = sk.shape[1]
    rows = blocks_per_step * SWA_BLOCK
    nb = seq // rows
    cur = lambda b, i: (b * nb + i, 0)
    prev = lambda b, i: ((b * nb + i) * blocks_per_step - jnp.minimum(i, 1), 0)
    return pl.pallas_call(
        functools.partial(_swa_kernel, stage_blocks=4),
        grid=(batch, nb),
        in_specs=[pl.BlockSpec(memory_space=pltpu.SMEM),
                  pl.BlockSpec((rows, qw), cur),
                  pl.BlockSpec((SWA_BLOCK, kw), prev), pl.BlockSpec((rows, kw), cur),
                  pl.BlockSpec((SWA_BLOCK, kw), prev), pl.BlockSpec((rows, kw), cur),
                  _full_spec(bias.shape)],
        out_specs=pl.BlockSpec((rows, qw), cur),
        out_shape=jax.ShapeDtypeStruct((m, qw), BF16),
        compiler_params=_params(("parallel", "parallel")),
        name="sliding_window_attention",
    )(sinks, sq, sk, sk, sv, sv, bias)


def _split_cols(w, sizes):
    offs = np.cumsum((0,) + tuple(sizes))
    return [w[:, offs[j]:offs[j + 1]] for j in range(len(sizes))]


def even_layer_mixer(x2, mod, g_pre, g_post, w_in, conv_w, conv_b, dt_bias, a_log, d_skip, ssd_norm, lam,
                     diff_norm, w_out, diff_bias, layer_idx, batch, seq):
    d = x2.shape[1]
    d_inner = d
    n_ssd_heads = d_inner // SSD_HEAD_DIM
    conv_ch = d_inner + 2 * SSD_GROUPS * SSD_STATE
    wz, wxbc, wdt, wq, wk, wv = _split_cols(w_in, (d_inner, conv_ch, n_ssd_heads, d, d, d))
    wdt = jnp.pad(wdt, ((0, 0), (0, LANES - n_ssd_heads)))
    wq = wq * (DIFF_HEAD_DIM ** -0.5 * LOG2_E)
    weights = [w.astype(BF16) for w in (wz, wxbc, wdt, wq, wk, wv)]
    z, xbc, dt_raw, q, k, v = norm_proj(x2, mod, g_pre, weights, (BF16, BF16, F32, BF16, BF16, BF16), seq)
    y_ssd = ssd_mixer(z, xbc, dt_raw, conv_w, conv_b, dt_bias, a_log, d_skip, ssd_norm, batch, seq)
    lam_init = 0.8 - 0.6 * math.exp(-0.3 * layer_idx)
    y_diff = diff_attention(q, k, v, diff_bias, lam, diff_norm, batch, seq, lam_init)
    w_out = w_out.astype(BF16)
    return out_proj(x2, mod, g_post, [y_ssd, y_diff], [w_out[:d_inner], w_out[d_inner:]], seq)


def odd_layer_mixer(x2, mod, g_pre, g_post, w_in, sinks, w_out, swa_bias, batch, seq):
    ret_qk = RET_HEADS * RET_QK_DIM
    ret_v = RET_HEADS * RET_V_DIM
    swa_q = SWA_HEADS * SWA_HEAD_DIM
    swa_kv = SWA_KV_HEADS * SWA_HEAD_DIM
    wrq, wrk, wrv, wrg, wsq, wsk, wsv = _split_cols(w_in, (ret_qk, ret_qk, ret_v, ret_v, swa_q, swa_kv, swa_kv))
    wsq = wsq * (SWA_HEAD_DIM ** -0.5 * LOG2_E)
    weights = [w.astype(BF16) for w in (wrq, wrk, wrv, wrg, wsq, wsk, wsv)]
    rq, rk, rv, rg, sq, sk, sv = norm_proj(x2, mod, g_pre, weights, (BF16,) * 7, seq)
    y_ret = retention(rq, rk, rv, rg, batch, seq)
    y_swa = sliding_window_attention(sq, sk, sv, sinks, swa_bias, batch, seq)
    w_out = w_out.astype(BF16)
    return out_proj(x2, mod, g_post, [y_ret, y_swa], [w_out[:ret_v], w_out[ret_v:]], seq)


def kernel(x, c, rel_bias, norm_gains, mod_w, mod_b, mlp_w1, mlp_w2, e_w_in, e_conv_w, e_conv_b, e_dt_bias,
           e_A_log, e_D, e_ssd_norm, e_lambda, e_diff_norm, e_w_out, o_w_in, o_sinks, o_w_out):
    batch, seq, d = x.shape
    depth = norm_gains.shape[0]
    assert seq % (2 * ATTN_BLOCK) == 0 and d % LANES == 0
    mods = modulation(c, mod_w, mod_b.reshape(depth * 2, 3 * d))
    far_bucket = REL_BUCKETS - 1
    diff_bias = bias_tiles(rel_bias, _diff_bucket_idx(ATTN_BLOCK), shift_bucket=far_bucket, scale=LOG2_E)
    diff_bias = diff_bias.reshape(rel_bias.shape[1], 2, ATTN_BLOCK, ATTN_BLOCK)
    swa_bias = bias_tiles(rel_bias, _swa_bucket_idx(), scale=LOG2_E)

    x2 = x.reshape(batch * seq, d)
    for layer in range(depth):
        j = layer // 2
        gains = norm_gains[layer]
        if layer % 2 == 0:
            x2 = even_layer_mixer(x2, mods[2 * layer], gains[0], gains[1], e_w_in[j], e_conv_w[j], e_conv_b[j],
                                  e_dt_bias[j], e_A_log[j], e_D[j], e_ssd_norm[j], e_lambda[j], e_diff_norm[j],
                                  e_w_out[j], diff_bias, layer, batch, seq)
        else:
            x2 = odd_layer_mixer(x2, mods[2 * layer], gains[0], gains[1], o_w_in[j], o_sinks[j], o_w_out[j],
                                 swa_bias, batch, seq)
        x2 = mlp(x2, mods[2 * layer + 1], gains[2], gains[3], mlp_w1, mlp_w2, layer, seq)
    return x2.reshape(batch, seq, d)
```

```python
import functools
import math

import jax
import jax.numpy as jnp
import numpy as np
from jax import lax
from jax.experimental import pallas as pl
from jax.experimental.pallas import tpu as pltpu

EPS = 1e-6
MASK_VALUE = -1e30
LOG2_E = math.log2(math.e)
LANES = 128
SUBLANES = 8
VMEM_LIMIT = 56 * 1024 * 1024

CHUNK = 128
REL_BUCKETS = 32
REL_MAX_DIST = 128
SSD_HEAD_DIM = 64
SSD_GROUPS = 4
SSD_STATE = 128
SSD_CONV = 4
DIFF_HEAD_DIM = 64
RET_HEADS = 4
RET_QK_DIM = 128
RET_V_DIM = 256
ROPE_BASE = 10000.0
SWA_HEADS = 8
SWA_KV_HEADS = 2
SWA_HEAD_DIM = 64
SWA_BLOCK = 128
ATTN_BLOCK = 512

BF16 = jnp.bfloat16
F32 = jnp.float32
NT_DIMS = (((1,), (1,)), ((), ()))


def _params(semantics):
    return pltpu.CompilerParams(dimension_semantics=semantics, vmem_limit_bytes=VMEM_LIMIT)


def _full_spec(shape):
    return pl.BlockSpec(shape, lambda *_: (0,) * len(shape))


def _silu(x):
    h = 0.5 * x
    return h * jnp.tanh(h) + h


def _split3(x):
    hi = x.astype(BF16)
    r1 = x - hi.astype(F32)
    mid = r1.astype(BF16)
    lo = (r1 - mid.astype(F32)).astype(BF16)
    return hi, mid, lo


def _rms(x, eps=EPS):
    return x * lax.rsqrt(jnp.mean(x * x, axis=-1, keepdims=True) + eps)


def _mod_kernel(ct_ref, w_ref, b_ref, o_ref, *, batch):
    d, tn = w_ref.shape
    o_ref[...] = jnp.zeros(o_ref.shape, F32)
    for bi in range(batch):
        acc = None
        for k0 in range(0, d, LANES):
            col = _silu(ct_ref[k0:k0 + LANES, bi:bi + 1])
            part = (w_ref[k0:k0 + LANES, :] * col).reshape(LANES // SUBLANES, SUBLANES, tn).sum(axis=0)
            acc = part if acc is None else acc + part
        o_ref[bi:bi + 1, :] = jnp.sum(acc, axis=0, keepdims=True) + b_ref[...]


def modulation(c, mod_w, mod_b):
    b, d = c.shape
    n = mod_b.shape[0]
    per_layer = mod_w.shape[1]
    rows = SUBLANES
    c_pad = jnp.zeros((d, LANES), F32).at[:, :b].set(c.T)
    tn = d
    out = pl.pallas_call(
        functools.partial(_mod_kernel, batch=b),
        grid=(n, 3 * d // tn),
        in_specs=[pl.BlockSpec((d, LANES), lambda s, j: (0, 0)),
                  pl.BlockSpec((None, None, d, tn), lambda s, j: (s // per_layer, s % per_layer, 0, j)),
                  pl.BlockSpec((None, 1, tn), lambda s, j: (s, 0, j))],
        out_specs=pl.BlockSpec((None, rows, tn), lambda s, j: (s, 0, j)),
        out_shape=jax.ShapeDtypeStruct((n, rows, 3 * d), F32),
        compiler_params=_params(("parallel", "parallel")),
        name="modulation",
    )(c_pad, mod_w, mod_b.reshape(n, 1, 3 * d))
    return out[:, :b].reshape(n, b, 3, d)


def _t5_bucket_np(dist):
    max_exact = REL_BUCKETS // 2
    logd = np.log(np.maximum(dist, 1).astype(np.float32) / np.float32(max_exact))
    large = max_exact + (logd / np.float32(math.log(REL_MAX_DIST / max_exact))
                         * np.float32(REL_BUCKETS - max_exact)).astype(np.int32)
    large = np.minimum(large, REL_BUCKETS - 1)
    return np.where(dist < max_exact, dist, large).astype(np.int32)


def _bias_kernel(rb_ref, idx_ref, o_ref, *, shift_bucket, scale, block_buckets):
    h = pl.program_id(0)
    shift = rb_ref[shift_bucket, h] if shift_bucket is not None else 0.0

    def value(bucket):
        return (rb_ref[bucket, h] - shift) * scale

    for (r0, c0), buckets in block_buckets:
        window = (slice(r0, r0 + LANES), slice(c0, c0 + LANES))
        idx = idx_ref[window]
        acc = jnp.full(idx.shape, MASK_VALUE if -1 in buckets else value(max(buckets)), F32)
        for bucket in buckets:
            if bucket >= 0 and len(buckets) > 1:
                acc = jnp.where(idx == bucket, value(bucket), acc)
        o_ref[window] = acc


def bias_tiles(rel_bias, bucket_idx, shift_bucket=None, scale=1.0):
    n_heads = rel_bias.shape[1]
    r, c = bucket_idx.shape
    block_buckets = tuple(((r0, c0), tuple(int(b) for b in np.unique(bucket_idx[r0:r0 + LANES, c0:c0 + LANES])))
                          for r0 in range(0, r, LANES) for c0 in range(0, c, LANES))
    return pl.pallas_call(
        functools.partial(_bias_kernel, shift_bucket=shift_bucket, scale=scale, block_buckets=block_buckets),
        grid=(n_heads,),
        in_specs=[pl.BlockSpec(memory_space=pltpu.SMEM),
                  pl.BlockSpec((r, c), lambda h: (0, 0))],
        out_specs=pl.BlockSpec((None, r, c), lambda h: (h, 0, 0)),
        out_shape=jax.ShapeDtypeStruct((n_heads, r, c), F32),
        compiler_params=_params(("parallel",)),
        name="bias_tiles",
    )(rel_bias, jnp.asarray(bucket_idx))


def _diff_bucket_idx(t):
    q = np.arange(t)[None, :]
    k = np.arange(t)[:, None]
    diag = q - k
    idx_diag = np.where(diag >= 0, _t5_bucket_np(np.maximum(diag, 0)), -1)
    idx_prev = _t5_bucket_np(diag + t)
    assert (_t5_bucket_np(diag + 2 * t) == REL_BUCKETS - 1).all()
    return np.concatenate([idx_diag, idx_prev], axis=0).astype(np.int32)


def _swa_bucket_idx():
    q = np.arange(SWA_BLOCK)[None, :]
    k = np.arange(2 * SWA_BLOCK)[:, None] - SWA_BLOCK
    dist = q - k
    valid = (dist >= 0) & (dist < SWA_BLOCK)
    return np.where(valid, _t5_bucket_np(np.maximum(dist, 0)), -1).astype(np.int32)


def _norm_proj_kernel(x_ref, mod_ref, g_ref, *refs):
    n = len(refs) // 2
    w_refs, o_refs = refs[:n], refs[n:]
    mod = mod_ref[...]
    h = _rms(x_ref[...]) * g_ref[...] * (1.0 + mod[1:2]) + mod[0:1]
    hb = h.astype(BF16)
    for w_ref, o_ref in zip(w_refs, o_refs):
        o_ref[...] = jnp.dot(hb, w_ref[...], preferred_element_type=F32).astype(o_ref.dtype)


def norm_proj(x2, mod, gain, weights, out_dtypes, seq, tm=512):
    m, d = x2.shape
    per_batch = seq // tm
    in_specs = [pl.BlockSpec((tm, d), lambda i: (i, 0)),
                pl.BlockSpec((None, 3, d), lambda i: (i // per_batch, 0, 0)),
                _full_spec((1, d))]
    in_specs += [_full_spec(w.shape) for w in weights]
    out_specs = [pl.BlockSpec((tm, w.shape[1]), lambda i: (i, 0)) for w in weights]
    out_shape = [jax.ShapeDtypeStruct((m, w.shape[1]), dt) for w, dt in zip(weights, out_dtypes)]
    return pl.pallas_call(
        _norm_proj_kernel,
        grid=(m // tm,),
        in_specs=in_specs,
        out_specs=out_specs,
        out_shape=out_shape,
        compiler_params=_params(("parallel",)),
        name="norm_proj",
    )(x2, mod, gain.reshape(1, d), *weights)


def _out_proj_kernel(x_ref, mod_ref, g_ref, *refs):
    n = (len(refs) - 1) // 2
    y_refs, w_refs, o_ref = refs[:n], refs[n:2 * n], refs[2 * n]
    acc = None
    for y_ref, w_ref in zip(y_refs, w_refs):
        part = jnp.dot(y_ref[...], w_ref[...], preferred_element_type=F32)
        acc = part if acc is None else acc + part
    gate = mod_ref[...][2:3]
    o_ref[...] = x_ref[...] + gate * (_rms(acc) * g_ref[...])


def out_proj(x2, mod, gain, ys, weights, seq, tm=512):
    m, d = x2.shape
    per_batch = seq // tm
    in_specs = [pl.BlockSpec((tm, d), lambda i: (i, 0)),
                pl.BlockSpec((None, 3, d), lambda i: (i // per_batch, 0, 0)),
                _full_spec((1, d))]
    in_specs += [pl.BlockSpec((tm, y.shape[1]), lambda i: (i, 0)) for y in ys]
    in_specs += [_full_spec(w.shape) for w in weights]
    return pl.pallas_call(
        _out_proj_kernel,
        grid=(m // tm,),
        in_specs=in_specs,
        out_specs=pl.BlockSpec((tm, d), lambda i: (i, 0)),
        out_shape=jax.ShapeDtypeStruct((m, d), F32),
        compiler_params=_params(("parallel",)),
        name="out_proj",
    )(x2, mod, gain.reshape(1, d), *ys, *weights)


def _mlp_kernel(x_ref, mod_ref, g_pre_ref, g_post_ref, w1_hbm, w2_hbm, o_ref, w1_ref, w2_ref, stage1_ref,
                stage2_ref, sem, *, ff_chunk, layer):
    d_ff = w1_ref.shape[1]
    n_chunks = d_ff // ff_chunk

    def weight_copies(c):
        slot = c % 2
        return (pltpu.make_async_copy(w1_hbm.at[layer, :, pl.ds(c * ff_chunk, ff_chunk)], stage1_ref.at[slot],
                                      sem.at[0, slot]),
                pltpu.make_async_copy(w2_hbm.at[layer, pl.ds(c * ff_chunk, ff_chunk), :], stage2_ref.at[slot],
                                      sem.at[1, slot]))

    def body(load_weights):
        if load_weights:
            for cp in weight_copies(0):
                cp.start()
        x = x_ref[...]
        mod = mod_ref[...]
        h = _rms(x) * g_pre_ref[...] * (1.0 + mod[1:2]) + mod[0:1]
        hb = h.astype(BF16)
        acc = None
        for c in range(n_chunks):
            c0 = c * ff_chunk
            if load_weights:
                if c + 1 < n_chunks:
                    for cp in weight_copies(c + 1):
                        cp.start()
                for cp in weight_copies(c):
                    cp.wait()
                w1_ref[:, c0:c0 + ff_chunk] = stage1_ref[c % 2].astype(BF16)
                w2_ref[c0:c0 + ff_chunk, :] = stage2_ref[c % 2].astype(BF16)
            a = jnp.dot(hb, w1_ref[:, c0:c0 + ff_chunk], preferred_element_type=F32)
            a = jnp.square(jnp.maximum(a, 0.0)).astype(BF16)
            part = jnp.dot(a, w2_ref[c0:c0 + ff_chunk, :], preferred_element_type=F32)
            acc = part if acc is None else acc + part
        o_ref[...] = x + mod[2:3] * (_rms(acc) * g_post_ref[...])

    @pl.when(pl.program_id(0) == 0)
    def _():
        body(True)

    @pl.when(pl.program_id(0) > 0)
    def _():
        body(False)


def mlp(x2, mod, g_pre, g_post, w1, w2, layer, seq, tm=512, ff_chunk=1024):
    m, d = x2.shape
    d_ff = w1.shape[2]
    per_batch = seq // tm
    return pl.pallas_call(
        functools.partial(_mlp_kernel, ff_chunk=ff_chunk, layer=layer),
        grid=(m // tm,),
        in_specs=[pl.BlockSpec((tm, d), lambda i: (i, 0)),
                  pl.BlockSpec((None, 3, d), lambda i: (i // per_batch, 0, 0)),
                  _full_spec((1, d)), _full_spec((1, d)),
                  pl.BlockSpec(memory_space=pl.ANY),
                  pl.BlockSpec(memory_space=pl.ANY)],
        out_specs=pl.BlockSpec((tm, d), lambda i: (i, 0)),
        out_shape=jax.ShapeDtypeStruct((m, d), F32),
        scratch_shapes=[pltpu.VMEM((d, d_ff), BF16), pltpu.VMEM((d_ff, d), BF16),
                        pltpu.VMEM((2, d, ff_chunk), F32), pltpu.VMEM((2, ff_chunk, d), F32),
                        pltpu.SemaphoreType.DMA((2, 2))],
        compiler_params=_params(("arbitrary",)),
        name="mlp",
    )(x2, mod, g_pre.reshape(1, d), g_post.reshape(1, d), w1, w2)


def _ssd_kernel(z_ref, xbc_ref, dt_ref, cw_ref, cb_ref, dtb_ref, alog_ref, dskip_ref, nw_ref, expand_ref,
                shift_ref, o_ref, conv_buf, state_ref, *, d_inner):
    chunk = CHUNK
    rows_per_step = z_ref.shape[0]
    n_state = SSD_STATE
    gn = SSD_GROUPS * n_state
    pair = 2 * SSD_HEAD_DIM
    heads_per_group = d_inner // SSD_HEAD_DIM // SSD_GROUPS
    halo = conv_buf.shape[0] - rows_per_step

    @pl.when(pl.program_id(1) == 0)
    def _():
        state_ref[...] = jnp.zeros_like(state_ref)
        conv_buf[0:halo, :] = jnp.zeros((halo, conv_buf.shape[1]), BF16)

    conv_buf[halo:, :] = xbc_ref[...]

    def chunk_body(r0):
        rows = slice(r0, r0 + chunk)
        cur = xbc_ref[rows, :]
        shifted = jnp.dot(shift_ref[...], conv_buf[r0:r0 + halo + chunk, :], preferred_element_type=F32)
        acc = cb_ref[...] + cw_ref[SSD_CONV - 1:SSD_CONV, :] * cur.astype(F32)
        for tap in range(SSD_CONV - 1):
            acc = acc + cw_ref[tap:tap + 1, :] * shifted[tap * chunk:(tap + 1) * chunk, :]
        xbc = _silu(acc)
        xs = xbc[:, :d_inner]
        b_all = xbc[:, d_inner:d_inner + gn]
        c_all = xbc[:, d_inner + gn:]

        dt_in = dt_ref[rows, :] + dtb_ref[...]
        dt = jnp.maximum(dt_in, 0.0) + jnp.log1p(jnp.exp(-jnp.abs(dt_in)))
        a = dt * (-jnp.exp(alog_ref[...]))
        row = lax.broadcasted_iota(jnp.int32, (chunk, chunk), 0)
        col = lax.broadcasted_iota(jnp.int32, (chunk, chunk), 1)
        causal = row >= col
        tril = jnp.where(causal, 1.0, 0.0).astype(BF16)
        parts = jnp.dot(tril, jnp.concatenate(_split3(a), axis=1), preferred_element_type=F32)
        a_cs = parts[:, :LANES] + parts[:, LANES:2 * LANES] + parts[:, 2 * LANES:]
        a_cs_t = a_cs.T
        lhs = jnp.concatenate([jnp.concatenate(_split3(dt), axis=1), jnp.concatenate(_split3(a_cs), axis=1)], axis=0)
        expanded = jnp.dot(lhs, expand_ref[...], preferred_element_type=F32)
        dt_e, acs_e = expanded[:chunk], expanded[chunk:]
        a_last_e = acs_e[chunk - 1:chunk, :]
        x_dt = xs * dt_e
        w_state = (x_dt * jnp.exp(a_last_e - acs_e)).astype(BF16)
        out_scale = jnp.exp(acs_e)
        chunk_decay = jnp.exp(a_last_e)
        x_dt_b = x_dt.astype(BF16)
        lane = lax.broadcasted_iota(jnp.int32, (chunk, pair), 1)
        first_head = lane < SSD_HEAD_DIM

        n_heads = d_inner // SSD_HEAD_DIM
        cgs = [c_all[:, g * n_state:(g + 1) * n_state].astype(BF16) for g in range(SSD_GROUPS)]
        bgs = [b_all[:, g * n_state:(g + 1) * n_state] for g in range(SSD_GROUPS)]
        cbs = [lax.dot_general(cg, bg.astype(BF16), NT_DIMS, preferred_element_type=F32)
               for cg, bg in zip(cgs, bgs)]
        bg_ts = [bg.T.astype(BF16) for bg in bgs]
        y_blocks = []
        for p in range(n_heads // 2):
            g = p // (heads_per_group // 2)
            sl = slice(p * pair, (p + 1) * pair)
            xp = x_dt_b[:, sl]
            y_diag = None
            for k in range(2):
                hh = 2 * p + k
                decay = jnp.exp(jnp.where(causal, a_cs[:, hh:hh + 1] - a_cs_t[hh:hh + 1, :], MASK_VALUE))
                m = (cbs[g] * decay).astype(BF16)
                xh = jnp.where(first_head if k == 0 else jnp.logical_not(first_head), xp, jnp.zeros_like(xp))
                part = jnp.dot(m, xh, preferred_element_type=F32)
                y_diag = part if y_diag is None else y_diag + part
            st = state_ref[p]
            y_off = jnp.dot(cgs[g], st.astype(BF16), preferred_element_type=F32) * out_scale[:, sl]
            state_ref[p] = st * chunk_decay[:, sl] + jnp.dot(bg_ts[g], w_state[:, sl], preferred_element_type=F32)
            y_blocks.append(y_diag + y_off)
        y = jnp.concatenate(y_blocks, axis=-1) + xs * dskip_ref[...]
        y = y * _silu(z_ref[rows, :].astype(F32))
        o_ref[rows, :] = (_rms(y) * nw_ref[...]).astype(o_ref.dtype)

    for r0 in range(0, rows_per_step, chunk):
        chunk_body(r0)
    conv_buf[0:halo, :] = conv_buf[rows_per_step:rows_per_step + halo, :]


def ssd_mixer(z, xbc, dt_raw, conv_w, conv_b, dt_bias, a_log, d_skip, norm_w, batch, seq, chunks_per_step=4):
    m, d_inner = z.shape
    conv_ch = xbc.shape[1]
    n_heads = d_inner // SSD_HEAD_DIM
    rows = chunks_per_step * CHUNK
    nc = seq // rows
    pad = LANES - n_heads
    expand = np.zeros((LANES, d_inner), np.float32)
    for h in range(n_heads):
        expand[h, h * SSD_HEAD_DIM:(h + 1) * SSD_HEAD_DIM] = 1.0
    expand3 = jnp.asarray(np.tile(expand, (3, 1)), BF16)
    halo = 2 * SUBLANES
    shift = np.zeros(((SSD_CONV - 1) * CHUNK, halo + CHUNK), np.float32)
    for tap in range(SSD_CONV - 1):
        shift[tap * CHUNK + np.arange(CHUNK), halo - (SSD_CONV - 1) + tap + np.arange(CHUNK)] = 1.0
    shift = jnp.asarray(shift, BF16)
    row_spec = lambda width: pl.BlockSpec((rows, width), lambda b, c: (b * nc + c, 0))
    return pl.pallas_call(
        functools.partial(_ssd_kernel, d_inner=d_inner),
        grid=(batch, nc),
        in_specs=[row_spec(d_inner), row_spec(conv_ch), row_spec(LANES),
                  _full_spec((SSD_CONV, conv_ch)), _full_spec((1, conv_ch)),
                  _full_spec((1, LANES)), _full_spec((1, LANES)),
                  _full_spec((1, d_inner)), _full_spec((1, d_inner)), _full_spec(expand3.shape),
                  _full_spec(shift.shape)],
        out_specs=row_spec(d_inner),
        out_shape=jax.ShapeDtypeStruct((m, d_inner), BF16),
        scratch_shapes=[pltpu.VMEM((halo + rows, conv_ch), BF16),
                        pltpu.VMEM((n_heads // 2, SSD_STATE, 2 * SSD_HEAD_DIM), F32)],
        compiler_params=_params(("parallel", "arbitrary")),
        name="ssd_mixer",
    )(z, xbc, dt_raw, conv_w, conv_b.reshape(1, conv_ch),
      jnp.pad(dt_bias, (0, pad)).reshape(1, LANES), jnp.pad(a_log, (0, pad)).reshape(1, LANES),
      jnp.repeat(d_skip, SSD_HEAD_DIM).reshape(1, d_inner), norm_w.reshape(1, d_inner), expand3, shift)


def _diff_attn_kernel(q_ref, k_ref, v_ref, bias_ref, lam_ref, nw_ref, o_ref,
                      vt_ref, s0_ref, s1_ref, cmax0_ref, cmax1_ref, p_ref, m_ref, alpha_ref, acc_ref,
                      *, lam_init):
    s_refs, cmax_refs = (s0_ref, s1_ref), (cmax0_ref, cmax1_ref)
    t = s0_ref.shape[0]
    tq = q_ref.shape[0]
    head_w = v_ref.shape[1]
    ext_rows = vt_ref.shape[1]
    i = pl.program_id(2)
    q_t = q_ref[...].astype(F32).T.astype(BF16)
    dim = lax.broadcasted_iota(jnp.int32, q_t.shape, 0)
    zero = jnp.zeros_like(q_t)
    qq_t = jnp.concatenate([jnp.where(dim < DIFF_HEAD_DIM, q_t, zero),
                            jnp.where(dim >= DIFF_HEAD_DIM, q_t, zero)], axis=1)

    @pl.when(i == 0)
    def _():
        for j in range(vt_ref.shape[0]):
            vt_ref[j, :head_w, :] = v_ref[j * t:(j + 1) * t, :].astype(F32).T.astype(BF16)
            vt_ref[j, head_w:, :] = jnp.ones((ext_rows - head_w, t), BF16)

    m_ref[...] = jnp.full(m_ref.shape, MASK_VALUE, F32)
    acc_ref[...] = jnp.zeros(acc_ref.shape, F32)

    diag_tile, prev_tile = bias_ref[0], bias_ref[1]
    all_cols = ((0, t), (t, t), (tq, t), (tq + t, t))
    near_cols = (all_cols[1::2], all_cols, all_cols)
    near_tiles = ((diag_tile, diag_tile),
                  (diag_tile, prev_tile, diag_tile, prev_tile),
                  (prev_tile, None, prev_tile, None))
    near_blocks = (2 * i + 1, 2 * i, jnp.maximum(2 * i - 1, 0))

    def raw_logits(block, cols=all_cols):
        kb = k_ref[pl.ds(pl.multiple_of(block * t, t), t), :]
        rhs = qq_t if cols is all_cols else jnp.concatenate([qq_t[:, c0:c0 + w] for c0, w in cols], axis=1)
        return jnp.dot(kb, rhs, preferred_element_type=F32)

    def store_logits(pieces, slot, cols=all_cols):
        for piece, (c0, w) in zip(pieces, cols):
            s_refs[slot][:, c0:c0 + w] = piece
            chains = [piece[0:SUBLANES, :], piece[SUBLANES:2 * SUBLANES, :]]
            for n, r in enumerate(range(2 * SUBLANES, t, SUBLANES)):
                chains[n % 2] = jnp.maximum(chains[n % 2], piece[r:r + SUBLANES, :])
            col_max = jnp.maximum(chains[0], chains[1])
            for shift in (4, 2, 1):
                col_max = jnp.maximum(col_max, pltpu.roll(col_max, shift, 0))
            cmax_refs[slot][:, c0:c0 + w] = col_max

    def is_near(visit):
        return isinstance(visit, int) and visit < len(near_blocks)

    def block_of(visit):
        return near_blocks[visit] if is_near(visit) else visit - len(near_blocks)

    def cols_of(visit):
        return near_cols[visit] if is_near(visit) else all_cols

    def logits_piece(visit, slot, n):
        col = cols_of(visit)[n]
        piece = raw_logits(block_of(visit), (col,))
        tile = near_tiles[visit][n] if is_near(visit) else None
        store_logits([piece if tile is None else piece + tile], slot, (col,))

    def softmax_accumulate(visit, slot, col_ranges=((0, 2 * tq),)):
        s_ref = s_refs[slot]
        vt = vt_ref[block_of(visit)]
        packed = 2 * SUBLANES
        for c0, w in col_ranges:
            c = slice(c0, c0 + w)
            m_old = m_ref[:, c]
            m_new = jnp.maximum(m_old, cmax_refs[slot][:, c])
            m_ref[:, c] = m_new
            alpha_ref[:, c] = jnp.exp2(m_old - m_new)
            m_tile = jnp.concatenate([m_new, m_new], axis=0)
            for r in range(0, t, packed):
                p_ref[r:r + packed, c] = jnp.exp2((s_ref[r:r + packed, c] - m_tile).astype(BF16))
            pv = jnp.dot(vt, p_ref[:, c], preferred_element_type=F32)
            alpha = alpha_ref[:, c]
            for r in range(0, ext_rows, SUBLANES):
                acc_ref[r:r + SUBLANES, c] = alpha * acc_ref[r:r + SUBLANES, c] + pv[r:r + SUBLANES, :]

    def pipeline_step(visit, slot):
        stage1, stage2 = cols_of(visit), cols_of(visit - 1)
        for n in range(max(len(stage1), len(stage2))):
            if n < len(stage1):
                logits_piece(visit, slot, n)
            if n < len(stage2):
                softmax_accumulate(visit - 1, 1 - slot, (stage2[n],))

    for n in range(len(near_cols[0])):
        logits_piece(0, 0, n)
    pipeline_step(1, 1)

    @pl.when(i < pl.num_programs(2))
    def _():
        pipeline_step(2, 0)

    @pl.when(i > 0)
    def _():
        pipeline_step(3, 1)
        last = 2 * i + 1

        def pair(k, _):
            pipeline_step(4 + 2 * k, 0)

            @pl.when(5 + 2 * k <= last)
            def _():
                pipeline_step(5 + 2 * k, 1)

            return 0

        lax.fori_loop(0, i - 1, pair, 0)
        softmax_accumulate(last, 1, all_cols)

    lam = lam_ref[...]
    lam_full = (jnp.exp(jnp.sum(lam[0:1] * lam[1:2], axis=-1, keepdims=True))
                - jnp.exp(jnp.sum(lam[2:3] * lam[3:4], axis=-1, keepdims=True)) + lam_init)
    inv_sum = 1.0 / acc_ref[head_w:head_w + SUBLANES, :]
    out_t = acc_ref[:head_w, :] * jnp.tile(inv_sum, (head_w // SUBLANES, 1))
    out = (out_t[:, :tq] - lam_full * out_t[:, tq:]).T
    o_ref[...] = (_rms(out) * nw_ref[...] * (1.0 - lam_init)).astype(o_ref.dtype)


def diff_attention(q, k, v, bias, lam, subln_w, batch, seq, lam_init):
    m, width = q.shape
    head_w = 2 * DIFF_HEAD_DIM
    n_heads = width // head_w
    t = ATTN_BLOCK
    tq = 2 * t
    nq = seq // tq
    cols = 2 * tq
    ext_rows = head_w + 2 * SUBLANES
    return pl.pallas_call(
        functools.partial(_diff_attn_kernel, lam_init=lam_init),
        grid=(batch, n_heads, nq),
        in_specs=[pl.BlockSpec((tq, head_w), lambda b, h, i: (b * nq + i, h)),
                  pl.BlockSpec((seq, head_w), lambda b, h, i: (b, h)),
                  pl.BlockSpec((seq, head_w), lambda b, h, i: (b, h)),
                  pl.BlockSpec((None, 2, t, t), lambda b, h, i: (h, 0, 0, 0)),
                  _full_spec(lam.shape), _full_spec((1, head_w))],
        out_specs=pl.BlockSpec((tq, head_w), lambda b, h, i: (b * nq + i, h)),
        out_shape=jax.ShapeDtypeStruct((m, width), BF16),
        scratch_shapes=[pltpu.VMEM((seq // t, ext_rows, t), BF16),
                        pltpu.VMEM((t, cols), F32), pltpu.VMEM((t, cols), F32),
                        pltpu.VMEM((SUBLANES, cols), F32), pltpu.VMEM((SUBLANES, cols), F32),
                        pltpu.VMEM((t, cols), BF16),
                        pltpu.VMEM((SUBLANES, cols), F32),
                        pltpu.VMEM((SUBLANES, cols), F32),
                        pltpu.VMEM((ext_rows, cols), F32)],
        compiler_params=_params(("parallel", "parallel", "arbitrary")),
        name="diff_attention",
    )(q, k, v, bias, lam, subln_w.reshape(1, head_w))


def _retention_kernel(q_ref, k_ref, v_ref, g_ref, cos_hi_ref, sin_hi_ref, cos_lo_ref, sin_lo_ref, sign_ref,
                      decay_ref, zeta_ref, xi_ref, o_ref, state_ref, *, chunk_decay):
    @pl.when(pl.program_id(1) == 0)
    def _():
        state_ref[...] = jnp.zeros_like(state_ref)

    c_hi, s_hi = cos_hi_ref[...], sin_hi_ref[...]
    c_lo, s_lo = cos_lo_ref[...], sin_lo_ref[...]
    cos_all = c_hi * c_lo - s_hi * s_lo
    sin_all = (s_hi * c_lo + c_hi * s_lo) * sign_ref[...]

    half = RET_QK_DIM // 2
    states = [state_ref[h] for h in range(RET_HEADS)]
    heads = range(RET_HEADS)
    qk_of = lambda h: slice(h * RET_QK_DIM, (h + 1) * RET_QK_DIM)
    v_of = lambda h: slice(h * RET_V_DIM, (h + 1) * RET_V_DIM)
    for r0 in range(0, q_ref.shape[0], CHUNK):
        rows = slice(r0, r0 + CHUNK)
        cos, sin = cos_all[rows, :], sin_all[rows, :]

        def rope(t):
            return t * cos + pltpu.roll(t, half, 1) * sin

        qr = [rope(q_ref[rows, qk_of(h)].astype(F32)) for h in heads]
        kr = [rope(k_ref[rows, qk_of(h)].astype(F32)) * (RET_QK_DIM ** -0.5) for h in heads]
        scores = [lax.dot_general(qr[h].astype(BF16), kr[h].astype(BF16), NT_DIMS,
                                  preferred_element_type=F32) * decay_ref[h] for h in heads]
        cross = [jnp.dot((qr[h] * xi_ref[h]).astype(BF16), states[h].astype(BF16), preferred_element_type=F32)
                 for h in heads]
        inner = [jnp.dot(scores[h].astype(BF16), v_ref[rows, v_of(h)], preferred_element_type=F32) for h in heads]
        kv = [jnp.dot((kr[h].T * zeta_ref[h]).astype(BF16), v_ref[rows, v_of(h)], preferred_element_type=F32)
              for h in heads]
        for h in heads:
            states[h] = states[h] * chunk_decay[h] + kv[h]
            out = _rms(inner[h] + cross[h]) * _silu(g_ref[rows, v_of(h)].astype(F32))
            o_ref[rows, v_of(h)] = out.astype(o_ref.dtype)
    for h in range(RET_HEADS):
        state_ref[h] = states[h]


def retention(rq, rk, rv, rg, batch, seq, chunks_per_step=8):
    m = rq.shape[0]
    rows = chunks_per_step * CHUNK
    nc = seq // rows
    half = RET_QK_DIM // 2
    inv = ROPE_BASE ** (-jnp.arange(half, dtype=F32) / half)
    both_halves = lambda a: jnp.concatenate([a, a], axis=-1)
    ang_hi = (jnp.arange(nc, dtype=F32) * rows)[:, None] * inv[None]
    ang_lo = jnp.arange(rows, dtype=F32)[:, None] * inv[None]
    cos_hi = both_halves(jnp.cos(ang_hi)).reshape(nc, 1, RET_QK_DIM)
    sin_hi = both_halves(jnp.sin(ang_hi)).reshape(nc, 1, RET_QK_DIM)
    cos_lo, sin_lo = both_halves(jnp.cos(ang_lo)), both_halves(jnp.sin(ang_lo))
    sign = jnp.concatenate([-jnp.ones((1, half), F32), jnp.ones((1, half), F32)], axis=-1)
    log_gamma = jnp.log(1.0 - 2.0 ** (-5.0 - jnp.arange(RET_HEADS, dtype=F32)))
    idx = jnp.arange(CHUNK, dtype=F32)
    rel = idx[:, None] - idx[None, :]
    decay = jnp.where(rel >= 0, jnp.exp(jnp.maximum(rel, 0.0)[None] * log_gamma[:, None, None]), 0.0)
    zeta = jnp.exp((CHUNK - 1 - idx)[None] * log_gamma[:, None])[:, None, :]
    xi = jnp.broadcast_to(jnp.exp((idx + 1.0)[None] * log_gamma[:, None])[:, :, None],
                          (RET_HEADS, CHUNK, RET_QK_DIM))
    gamma = 1.0 - 2.0 ** (-5.0 - np.arange(RET_HEADS, dtype=np.float64))
    chunk_decay = tuple(float(g ** CHUNK) for g in gamma)
    row_spec = lambda width: pl.BlockSpec((rows, width), lambda b, c: (b * nc + c, 0))
    return pl.pallas_call(
        functools.partial(_retention_kernel, chunk_decay=chunk_decay),
        grid=(batch, nc),
        in_specs=[row_spec(rq.shape[1]), row_spec(rk.shape[1]), row_spec(rv.shape[1]), row_spec(rg.shape[1]),
                  pl.BlockSpec((None, 1, RET_QK_DIM), lambda b, c: (c, 0, 0)),
                  pl.BlockSpec((None, 1, RET_QK_DIM), lambda b, c: (c, 0, 0)),
                  _full_spec(cos_lo.shape), _full_spec(sin_lo.shape), _full_spec(sign.shape),
                  _full_spec(decay.shape), _full_spec(zeta.shape), _full_spec(xi.shape)],
        out_specs=row_spec(rv.shape[1]),
        out_shape=jax.ShapeDtypeStruct((m, rv.shape[1]), BF16),
        scratch_shapes=[pltpu.VMEM((RET_HEADS, RET_QK_DIM, RET_V_DIM), F32)],
        compiler_params=_params(("parallel", "arbitrary")),
        name="retention",
    )(rq, rk, rv, rg, cos_hi, sin_hi, cos_lo, sin_lo, sign, decay, zeta, xi)


def _swa_kernel(sink_ref, q_ref, kp_ref, kc_ref, vp_ref, vc_ref, bias_ref, o_ref, *, stage_blocks):
    i = pl.program_id(1)
    blk = SWA_BLOCK
    rep = SWA_HEADS // SWA_KV_HEADS
    d = SWA_HEAD_DIM
    cols = rep * blk
    k_all = jnp.concatenate([kp_ref[...], kc_ref[...]], axis=0)
    v_all = jnp.concatenate([vp_ref[...], vc_ref[...]], axis=0)
    key_row = lax.broadcasted_iota(jnp.int32, (2 * blk, cols), 0)
    head_of_lane = lax.broadcasted_iota(jnp.int32, (SUBLANES, cols), 1) // blk
    zero = jnp.zeros((d, blk), F32)
    ones = jnp.ones((2 * SUBLANES, 2 * blk), F32)
    for base in range(0, q_ref.shape[0], stage_blocks * blk):
        units = [(base + n * blk, g) for n in range(stage_blocks) for g in range(SWA_KV_HEADS)]
        logits = []
        for r0, g in units:
            kb = k_all[r0:r0 + 2 * blk]
            q_t = q_ref[r0:r0 + blk, g * rep * d:(g + 1) * rep * d].astype(F32).T
            rhs = jnp.concatenate(
                [jnp.concatenate([q_t[r * d:(r + 1) * d], zero] if g == 0 else [zero, q_t[r * d:(r + 1) * d]],
                                 axis=0) for r in range(rep)], axis=1).astype(BF16)
            s = jnp.dot(kb, rhs, preferred_element_type=F32)
            s = s + jnp.concatenate([bias_ref[g * rep + r] for r in range(rep)], axis=1)
            if r0 == 0:
                s = jnp.where(jnp.logical_and(key_row < blk, i == 0), MASK_VALUE, s)
            sink = jnp.zeros((SUBLANES, cols), F32)
            for r in range(rep):
                sink = jnp.where(head_of_lane == r, sink_ref[g * rep + r] * LOG2_E, sink)
            chains = [s[0:SUBLANES], s[SUBLANES:2 * SUBLANES]]
            for n, r in enumerate(range(2 * SUBLANES, 2 * blk, SUBLANES)):
                chains[n % 2] = jnp.maximum(chains[n % 2], s[r:r + SUBLANES])
            col_max = jnp.maximum(chains[0], chains[1])
            for shift in (4, 2, 1):
                col_max = jnp.maximum(col_max, pltpu.roll(col_max, shift, 0))
            logits.append((s, sink, jnp.maximum(col_max, sink)))
        probs = [jnp.exp2((s - jnp.tile(m, (2 * blk // SUBLANES, 1))).astype(BF16)) for s, _, m in logits]
        outs = []
        v_ts = {base + n * blk: v_all[base + n * blk:base + (n + 2) * blk].astype(F32).T
                for n in range(stage_blocks)}
        for (r0, g), (_, sink, m), p in zip(units, logits, probs):
            v_ext = jnp.concatenate([v_ts[r0][g * d:(g + 1) * d], ones], axis=0).astype(BF16)
            pv = jnp.dot(v_ext, p, preferred_element_type=F32)
            inv = 1.0 / (pv[d:d + SUBLANES] + jnp.exp2(sink - m))
            o_t = pv[:d] * jnp.tile(inv, (d // SUBLANES, 1))
            for j in range(rep // 2):
                pair_t = jnp.concatenate([o_t[:, (2 * j) * blk:(2 * j + 1) * blk],
                                          o_t[:, (2 * j + 1) * blk:(2 * j + 2) * blk]], axis=0)
                outs.append(pair_t.T)
        for n in range(stage_blocks):
            per_block = len(outs) // stage_blocks
            o_ref[base + n * blk:base + (n + 1) * blk, :] = jnp.concatenate(
                outs[n * per_block:(n + 1) * per_block], axis=-1).astype(o_ref.dtype)


def sliding_window_attention(sq, sk, sv, sinks, bias, batch, seq, blocks_per_step=8):
    m, qw = sq.shape
    kw = sk.shape[1]
    rows = blocks_per_step * SWA_BLOCK
    nb = seq // rows
    cur = lambda b, i: (b * nb + i, 0)
    prev = lambda b, i: ((b * nb + i) * blocks_per_step - jnp.minimum(i, 1), 0)
    return pl.pallas_call(
        functools.partial(_swa_kernel, stage_blocks=4),
        grid=(batch, nb),
        in_specs=[pl.BlockSpec(memory_space=pltpu.SMEM),
                  pl.BlockSpec((rows, qw), cur),
                  pl.BlockSpec((SWA_BLOCK, kw), prev), pl.BlockSpec((rows, kw), cur),
                  pl.BlockSpec((SWA_BLOCK, kw), prev), pl.BlockSpec((rows, kw), cur),
                  _full_spec(bias.shape)],
        out_specs=pl.BlockSpec((rows, qw), cur),
        out_shape=jax.ShapeDtypeStruct((m, qw), BF16),
        compiler_params=_params(("parallel", "parallel")),
        name="sliding_window_attention",
    )(sinks, sq, sk, sk, sv, sv, bias)


def _split_cols(w, sizes):
    offs = np.cumsum((0,) + tuple(sizes))
    return [w[:, offs[j]:offs[j + 1]] for j in range(len(sizes))]


def even_layer_mixer(x2, mod, g_pre, g_post, w_in, conv_w, conv_b, dt_bias, a_log, d_skip, ssd_norm, lam,
                     diff_norm, w_out, diff_bias, layer_idx, batch, seq):
    d = x2.shape[1]
    d_inner = d
    n_ssd_heads = d_inner // SSD_HEAD_DIM
    conv_ch = d_inner + 2 * SSD_GROUPS * SSD_STATE
    wz, wxbc, wdt, wq, wk, wv = _split_cols(w_in, (d_inner, conv_ch, n_ssd_heads, d, d, d))
    wdt = jnp.pad(wdt, ((0, 0), (0, LANES - n_ssd_heads)))
    wq = wq * (DIFF_HEAD_DIM ** -0.5 * LOG2_E)
    weights = [w.astype(BF16) for w in (wz, wxbc, wdt, wq, wk, wv)]
    z, xbc, dt_raw, q, k, v = norm_proj(x2, mod, g_pre, weights, (BF16, BF16, F32, BF16, BF16, BF16), seq)
    y_ssd = ssd_mixer(z, xbc, dt_raw, conv_w, conv_b, dt_bias, a_log, d_skip, ssd_norm, batch, seq)
    lam_init = 0.8 - 0.6 * math.exp(-0.3 * layer_idx)
    y_diff = diff_attention(q, k, v, diff_bias, lam, diff_norm, batch, seq, lam_init)
    w_out = w_out.astype(BF16)
    return out_proj(x2, mod, g_post, [y_ssd, y_diff], [w_out[:d_inner], w_out[d_inner:]], seq)


def odd_layer_mixer(x2, mod, g_pre, g_post, w_in, sinks, w_out, swa_bias, batch, seq):
    ret_qk = RET_HEADS * RET_QK_DIM
    ret_v = RET_HEADS * RET_V_DIM
    swa_q = SWA_HEADS * SWA_HEAD_DIM
    swa_kv = SWA_KV_HEADS * SWA_HEAD_DIM
    wrq, wrk, wrv, wrg, wsq, wsk, wsv = _split_cols(w_in, (ret_qk, ret_qk, ret_v, ret_v, swa_q, swa_kv, swa_kv))
    wsq = wsq * (SWA_HEAD_DIM ** -0.5 * LOG2_E)
    weights = [w.astype(BF16) for w in (wrq, wrk, wrv, wrg, wsq, wsk, wsv)]
    rq, rk, rv, rg, sq, sk, sv = norm_proj(x2, mod, g_pre, weights, (BF16,) * 7, seq)
    y_ret = retention(rq, rk, rv, rg, batch, seq)
    y_swa = sliding_window_attention(sq, sk, sv, sinks, swa_bias, batch, seq)
    w_out = w_out.astype(BF16)
    return out_proj(x2, mod, g_post, [y_ret, y_swa], [w_out[:ret_v], w_out[ret_v:]], seq)


def kernel(x, c, rel_bias, norm_gains, mod_w, mod_b, mlp_w1, mlp_w2, e_w_in, e_conv_w, e_conv_b, e_dt_bias,
           e_A_log, e_D, e_ssd_norm, e_lambda, e_diff_norm, e_w_out, o_w_in, o_sinks, o_w_out):
    batch, seq, d = x.shape
    depth = norm_gains.shape[0]
    assert seq % (2 * ATTN_BLOCK) == 0 and d % LANES == 0
    mods = modulation(c, mod_w, mod_b.reshape(depth * 2, 3 * d))
    far_bucket = REL_BUCKETS - 1
    diff_bias = bias_tiles(rel_bias, _diff_bucket_idx(ATTN_BLOCK), shift_bucket=far_bucket, scale=LOG2_E)
    diff_bias = diff_bias.reshape(rel_bias.shape[1], 2, ATTN_BLOCK, ATTN_BLOCK)
    swa_bias = bias_tiles(rel_bias, _swa_bucket_idx(), scale=LOG2_E)

    x2 = x.reshape(batch * seq, d)
    for layer in range(depth):
        j = layer // 2
        gains = norm_gains[layer]
        if layer % 2 == 0:
            x2 = even_layer_mixer(x2, mods[2 * layer], gains[0], gains[1], e_w_in[j], e_conv_w[j], e_conv_b[j],
                                  e_dt_bias[j], e_A_log[j], e_D[j], e_ssd_norm[j], e_lambda[j], e_diff_norm[j],
                                  e_w_out[j], diff_bias, layer, batch, seq)
        else:
            x2 = odd_layer_mixer(x2, mods[2 * layer], gains[0], gains[1], o_w_in[j], o_sinks[j], o_w_out[j],
                                 swa_bias, batch, seq)
        x2 = mlp(x2, mods[2 * layer + 1], gains[2], gains[3], mlp_w1, mlp_w2, layer, seq)
    return x2.reshape(batch, seq, d)
```
